```python
import math
import jax
import jax.numpy as jnp
from jax import lax
import numpy as np

D_MODEL = 1024
BATCH = 2
SEQ = 16384
DEPTH = 4

GRID_W = 64
CTX_LEN = 256
N_MIXERS = 3
NORM_EPS = 1e-6

RW_HEAD = 64
RW_H = D_MODEL // RW_HEAD
RW_DECAY_LORA = 64
RW_AAA_LORA = 64
RW_MV_LORA = 32
RW_GATE_LORA = 160
RW_LN_EPS = 64e-5

HY_ORDER = 2
HY_EMB = 33
HY_BANDS = (HY_EMB - 1) // 2
HY_FILTER_W = 64
HY_FAST_DECAY = 0.3
HY_SLOW_DECAY = 1.5
HY_TARGET = 1e-2

HG_DK = 128
HG_H = D_MODEL // HG_DK
HG_DV = D_MODEL // HG_H
HG_CHUNK = 32

FFN_DIM = 2816
N_EXPERTS = 8
TOP_K = 2
EXPERT_DIM = 3584
MOE_BLOCK = 512

N_RWKV = (DEPTH + 2) // 3
N_HYENA = (DEPTH + 1) // 3
N_HGRN = DEPTH // 3
N_DENSE = (DEPTH + 1) // 2
N_MOE = DEPTH // 2

kernel_name = 'hybrid_rwkv7_hyena_hgrn2_moe_dit'


def rmsnorm(x, g, eps=NORM_EPS):
    xf = x.astype(jnp.float32)
    y = xf * lax.rsqrt(jnp.mean(xf * xf, axis=-1, keepdims=True) + eps)
    return (y * g.astype(jnp.float32)).astype(x.dtype)


def shift_seq(h):
    half = h.shape[-1] // 2
    prev = jnp.pad(h[:, :-1, :half], ((0, 0), (1, 0), (0, 0)))
    nxt = jnp.pad(h[:, 1:, half:], ((0, 0), (0, 1), (0, 0)))
    return jnp.concatenate([prev, nxt], axis=-1)


def shift_grid(h):
    b, l, d = h.shape
    rows = l // GRID_W
    g = h.reshape(b, rows, GRID_W, d)
    q = d // 4
    left = jnp.pad(g[:, :, :-1, :q], ((0, 0), (0, 0), (1, 0), (0, 0)))
    right = jnp.pad(g[:, :, 1:, q:2 * q], ((0, 0), (0, 0), (0, 1), (0, 0)))
    up = jnp.pad(g[:, :-1, :, 2 * q:3 * q], ((0, 0), (1, 0), (0, 0), (0, 0)))
    down = jnp.pad(g[:, 1:, :, 3 * q:], ((0, 0), (0, 1), (0, 0), (0, 0)))
    return jnp.concatenate([left, right, up, down], axis=-1).reshape(b, l, d)


def rwkv7_scan(r, decay, k, v, a_vec, b_vec, state0, reverse):
    def step(S, inp):
        r_t, w_t, k_t, v_t, a_t, b_t = inp
        sa = jnp.einsum('bhvk,bhk->bhv', S, a_t)
        S = S * w_t[:, :, None, :] + sa[..., None] * b_t[:, :, None, :] + v_t[..., None] * k_t[:, :, None, :]
        return S, jnp.einsum('bhvk,bhk->bhv', S, r_t)
    xs = tuple(jnp.swapaxes(t, 0, 1) for t in (r, decay, k, v, a_vec, b_vec))
    s_last, ys = lax.scan(step, state0, xs, reverse=reverse)
    return jnp.swapaxes(ys, 0, 1), s_last


def rwkv7_stream(h, shifted, state0, v_first, mu, w_rkv, w_o, w0, w1, w2, a0, a1, a2,
                 g1, g2, k_k, k_a, r_k, ln_w, ln_b, vres):
    f32 = jnp.float32
    B, L, D = h.shape
    xx = shifted - h
    xr, xw, xk, xv, xa, xg = (h + xx * mu[j] for j in range(6))
    r, k, v = jnp.einsum('nbld,nde->nble', jnp.stack([xr, xk, xv]), w_rkv)
    if vres is not None:
        v0, v1, v2 = vres
        v = v + (v_first - v) * jax.nn.sigmoid(v0 + (xv @ v1) @ v2)
    heads = lambda t: t.reshape(B, L, RW_H, RW_HEAD).astype(f32)
    kk = heads(k * k_k)
    kk = kk / jnp.maximum(jnp.linalg.norm(kk, axis=-1, keepdims=True), 1e-12)
    rh, vh, kh = heads(r), heads(v), heads(k)
    kah = k_a.reshape(RW_H, RW_HEAD).astype(f32)
    rkh = r_k.astype(f32)
    ys, bonuses, states = [], [], []
    for d in range(2):
        wl = w0[d] + jnp.tanh(xw @ w1[d]) @ w2[d]
        decay = jnp.exp(-jnp.exp(-jax.nn.softplus(-heads(wl)) - 0.5))
        a = jax.nn.sigmoid(heads(a0[d] + (xa @ a1[d]) @ a2[d]))
        kd = kh * (1.0 + (a - 1.0) * kah)
        yd, sd = rwkv7_scan(rh, decay, kd, vh, -kk, kk * a, state0[d], reverse=(d == 1))
        ys.append(yd)
        bonuses.append(jnp.sum(rh * kd * rkh, axis=-1, keepdims=True) * vh)
        states.append(sd)
    y = ys[0] + ys[1]
    mean = jnp.mean(y, axis=-1, keepdims=True)
    var = jnp.mean(jnp.square(y - mean), axis=-1, keepdims=True)
    y = ((y - mean) * lax.rsqrt(var + RW_LN_EPS)).reshape(B, L, D) * ln_w + ln_b
    y = y + (bonuses[0] + bonuses[1]).reshape(B, L, D)
    g = jax.nn.sigmoid(xg @ g1) @ g2
    out = (y.astype(h.dtype) * g) @ w_o
    return out, jnp.stack(states), v


def hyena_filter(L, w1, b1, w2, b2, w3, b3, w4, freq):
    f32 = jnp.float32
    w1, b1, w2, b2, w3, b3, w4, freq = (p.astype(f32) for p in (w1, b1, w2, b2, w3, b3, w4, freq))
    pos = jnp.arange(L, dtype=f32)
    t = (pos / max(L - 1, 1))[:, None]
    ang = (2.0 * math.pi / L) * pos[:, None] * jnp.linspace(1e-4, HY_BANDS - 1, HY_BANDS, dtype=f32)[None, :]
    z = jnp.concatenate([t, jnp.cos(ang), -jnp.sin(ang)], axis=-1)
    hid = jnp.sin(freq[0] * (z @ w1 + b1))
    hid = jnp.sin(freq[1] * (hid @ w2 + b2))
    hid = jnp.sin(freq[2] * (hid @ w3 + b3))
    filt = (hid @ w4).reshape(L, HY_ORDER, 2, D_MODEL)
    deltas = jnp.linspace(math.log(HY_TARGET) / HY_FAST_DECAY, math.log(HY_TARGET) / HY_SLOW_DECAY,
                          D_MODEL, dtype=f32)
    window = jnp.exp(-t * jnp.abs(deltas))
    return filt * window[:, None, None, :]


def long_conv(u, h_fwd, h_bwd, bias):
    f32 = jnp.float32
    L = u.shape[1]
    kern = jnp.concatenate([h_fwd, jnp.zeros_like(h_fwd[:1]), h_bwd[:0:-1]], axis=0)
    kf = jnp.fft.rfft(kern, axis=0)
    uf = jnp.fft.rfft(u.astype(f32), n=2 * L, axis=1)
    y = jnp.fft.irfft(uf * kf[None], n=2 * L, axis=1)[:, :L]
    return y + u.astype(f32) * bias.astype(f32)


def hyena_stream(h, filt, w_in, b_in, conv_w, conv_b, skip, w_o, b_o):
    proj = h @ w_in + b_in
    prev = jnp.pad(proj[:, :-1], ((0, 0), (1, 0), (0, 0)))
    nxt = jnp.pad(proj[:, 1:], ((0, 0), (0, 1), (0, 0)))
    proj = prev * conv_w[0] + proj * conv_w[1] + nxt * conv_w[2] + conv_b
    x1, x2, v = jnp.split(proj, 3, axis=-1)
    z = long_conv(v, filt[:, 0, 0], filt[:, 0, 1], skip[0]) * x1
    y = long_conv(z, filt[:, 1, 0], filt[:, 1, 1], skip[1]) * x2
    return y.astype(h.dtype) @ w_o + b_o


def gla_chunked(q, k, v, logf, s0):
    B, L, H, DK = q.shape
    DV = v.shape[-1]
    n = L // HG_CHUNK
    def chunks(t):
        return jnp.moveaxis(t.reshape(B, n, HG_CHUNK, H, t.shape[-1]), 3, 1)
    qc, kc, vc, gc = chunks(q), chunks(k), chunks(v), chunks(logf)
    G = jnp.cumsum(gc, axis=3)
    G_ref = G[:, :, :, HG_CHUNK // 2 - 1:HG_CHUNK // 2]
    scores = jnp.einsum('bhntk,bhnsk->bhnts', qc * jnp.exp(G - G_ref), kc * jnp.exp(G_ref - G))
    mask = jnp.tril(jnp.ones((HG_CHUNK, HG_CHUNK), dtype=bool))
    o_intra = jnp.einsum('bhnts,bhnsv->bhntv', jnp.where(mask, scores, 0.0), vc)
    G_last = G[:, :, :, -1:]
    xs = (jnp.moveaxis(qc * jnp.exp(G), 2, 0),
          jnp.moveaxis(kc * jnp.exp(G_last - G), 2, 0),
          jnp.moveaxis(vc, 2, 0),
          jnp.moveaxis(jnp.exp(G_last[:, :, :, 0]), 2, 0))
    def step(S, inp):
        qd, kd, vv, dec = inp
        o = jnp.einsum('bhck,bhkv->bhcv', qd, S)
        S = S * dec[..., None] + jnp.einsum('bhck,bhcv->bhkv', kd, vv)
        return S, o
    s_last, o_inter = lax.scan(step, s0, xs)
    o = o_intra + jnp.moveaxis(o_inter, 0, 2)
    return jnp.moveaxis(o, 1, 3).reshape(B, L, H, DV), s_last


def hgrn2_stream(h, state0, lb, w_in, gn, w_o):
    f32 = jnp.float32
    B, L, D = h.shape
    q, f_fwd, f_bwd, i_in, g = jnp.split(h @ w_in, 5, axis=-1)
    heads = lambda t: t.reshape(B, L, HG_H, -1).astype(f32)
    qh, vh = heads(jax.nn.silu(q)), heads(i_in)
    outs, states = [], []
    for d, f_raw in enumerate((f_fwd, f_bwd)):
        lbd = lb[d].reshape(HG_H, HG_DK).astype(f32)
        fg = lbd + (1.0 - lbd) * jax.nn.sigmoid(heads(f_raw))
        flip = (lambda t: jnp.flip(t, axis=1)) if d == 1 else (lambda t: t)
        od, sd = gla_chunked(flip(qh), flip(1.0 - fg), flip(vh), flip(jnp.log(fg)), state0[d])
        outs.append(flip(od))
        states.append(sd)
    o = outs[0] + outs[1]
    o = o * lax.rsqrt(jnp.mean(o * o, axis=-1, keepdims=True) + NORM_EPS) * gn.astype(f32)
    o = o.reshape(B, L, D).astype(h.dtype) * jax.nn.silu(g)
    return o @ w_o, jnp.stack(states)


def swiglu(h, w13, w2):
    gate, up = jnp.split(h @ w13, 2, axis=-1)
    return (jax.nn.silu(gate) * up) @ w2


def moe_swiglu(h, w_router, w13, w2):
    T, D = h.shape
    A = T * TOP_K
    logits = (h @ w_router).astype(jnp.float32)
    top_val, top_idx = lax.top_k(logits, TOP_K)
    gates = jax.nn.softmax(top_val, axis=-1).reshape(A)
    flat_e = top_idx.reshape(A)
    flat_tok = jnp.repeat(jnp.arange(T, dtype=jnp.int32), TOP_K)
    order = jnp.argsort(flat_e)
    sorted_e, sorted_tok = flat_e[order], flat_tok[order]
    counts = jnp.bincount(flat_e, length=N_EXPERTS)
    starts = jnp.cumsum(counts) - counts
    padded = (counts + MOE_BLOCK - 1) // MOE_BLOCK * MOE_BLOCK
    pends = jnp.cumsum(padded)
    dest = (pends - padded)[sorted_e] + jnp.arange(A, dtype=jnp.int32) - starts[sorted_e]
    n_blocks = (A + N_EXPERTS * (MOE_BLOCK - 1) + MOE_BLOCK - 1) // MOE_BLOCK
    buf = jnp.zeros((n_blocks * MOE_BLOCK, D), h.dtype).at[dest].set(h[sorted_tok])
    block_e = jnp.minimum(jnp.searchsorted(pends, jnp.arange(n_blocks, dtype=jnp.int32) * MOE_BLOCK,
                                           side='right'), N_EXPERTS - 1)
    def expert_block(args):
        xb, e = args
        return swiglu(xb, w13[e], w2[e])
    out = lax.map(expert_block, (buf.reshape(n_blocks, MOE_BLOCK, D), block_e)).reshape(-1, D)
    y = out[dest] * gates[order][:, None].astype(out.dtype)
    return jax.ops.segment_sum(y, sorted_tok, num_segments=T)


def setup_inputs(seed: int = 0) -> dict:
    key = jax.random.key(seed)
    keys = iter(jax.random.split(key, 64))
    f32 = jnp.float32
    D = D_MODEL
    def nrm(shape, scale):
        return jax.random.normal(next(keys), shape, f32) * scale
    def unif(shape, lo, hi):
        return jax.random.uniform(next(keys), shape, f32, lo, hi)
    return {
        'x': nrm((BATCH, SEQ, D), 1.0),
        'c': nrm((BATCH, D), 1.0),
        'ctx': nrm((BATCH, CTX_LEN, D), 1.0),
        'c_ctx': nrm((D,), 1.0),
        'norm_g': 1.0 + nrm((DEPTH, 2, D), 0.1),
        'ada_w': nrm((DEPTH, D, 6 * D), 0.5 * D ** -0.5),
        'ada_b': nrm((DEPTH, 6 * D), 0.02),
        'final_g': 1.0 + nrm((D,), 0.1),
        'rw_mu': unif((N_RWKV, 6, D), 0.0, 1.0),
        'rw_wrkv': nrm((N_RWKV, 3, D, D), D ** -0.5),
        'rw_wo': nrm((N_RWKV, D, D), D ** -0.5),
        'rw_w0': unif((N_RWKV, 2, D), -6.0, -1.0),
        'rw_w1': nrm((N_RWKV, 2, D, RW_DECAY_LORA), D ** -0.5),
        'rw_w2': nrm((N_RWKV, 2, RW_DECAY_LORA, D), 0.1 * RW_DECAY_LORA ** -0.5),
        'rw_a0': nrm((N_RWKV, 2, D), 0.1),
        'rw_a1': nrm((N_RWKV, 2, D, RW_AAA_LORA), D ** -0.5),
        'rw_a2': nrm((N_RWKV, 2, RW_AAA_LORA, D), 0.5 * RW_AAA_LORA ** -0.5),
        'rw_v0': nrm((N_RWKV - 1, D), 0.1),
        'rw_v1': nrm((N_RWKV - 1, D, RW_MV_LORA), D ** -0.5),
        'rw_v2': nrm((N_RWKV - 1, RW_MV_LORA, D), 0.5 * RW_MV_LORA ** -0.5),
        'rw_g1': nrm((N_RWKV, D, RW_GATE_LORA), D ** -0.5),
        'rw_g2': nrm((N_RWKV, RW_GATE_LORA, D), RW_GATE_LORA ** -0.5),
        'rw_kk': 0.85 + nrm((N_RWKV, D), 0.05),
        'rw_ka': 1.0 + nrm((N_RWKV, D), 0.1),
        'rw_rk': nrm((N_RWKV, RW_H, RW_HEAD), 0.1),
        'rw_lnw': 1.0 + nrm((N_RWKV, D), 0.1),
        'rw_lnb': nrm((N_RWKV, D), 0.02),
        'hy_win': nrm((N_HYENA, D, 3 * D), D ** -0.5),
        'hy_bin': nrm((N_HYENA, 3 * D), 0.02),
        'hy_cw': nrm((N_HYENA, 3, 3 * D), 0.5),
        'hy_cb': nrm((N_HYENA, 3 * D), 0.02),
        'hy_fw1': nrm((N_HYENA, HY_EMB, HY_FILTER_W), HY_EMB ** -0.5),
        'hy_fb1': nrm((N_HYENA, HY_FILTER_W), 0.1),
        'hy_fw2': nrm((N_HYENA, HY_FILTER_W, HY_FILTER_W), HY_FILTER_W ** -0.5),
        'hy_fb2': nrm((N_HYENA, HY_FILTER_W), 0.1),
        'hy_fw3': nrm((N_HYENA, HY_FILTER_W, HY_FILTER_W), HY_FILTER_W ** -0.5),
        'hy_fb3': nrm((N_HYENA, HY_FILTER_W), 0.1),
        'hy_fw4': nrm((N_HYENA, HY_FILTER_W, HY_ORDER * 2 * D), 0.05 * HY_FILTER_W ** -0.5),
        'hy_freq': 1.0 + nrm((N_HYENA, 3, HY_FILTER_W), 0.1),
        'hy_skip': nrm((N_HYENA, HY_ORDER, D), 0.5),
        'hy_wo': nrm((N_HYENA, D, D), D ** -0.5),
        'hy_bo': nrm((N_HYENA, D), 0.02),
        'hg_win': nrm((N_HGRN, D, 5 * D), D ** -0.5),
        'hg_lb': nrm((DEPTH, 2, D), 0.1),
        'hg_gn': 1.0 + nrm((N_HGRN, HG_DV), 0.1),
        'hg_wo': nrm((N_HGRN, D, D), D ** -0.5),
        'ffn_w13': nrm((N_DENSE, D, 2 * FFN_DIM), D ** -0.5),
        'ffn_w2': nrm((N_DENSE, FFN_DIM, D), FFN_DIM ** -0.5),
        'moe_router': nrm((N_MOE, D, N_EXPERTS), D ** -0.5),
        'moe_w13': nrm((N_MOE, N_EXPERTS, D, 2 * EXPERT_DIM), D ** -0.5),
        'moe_w2': nrm((N_MOE, N_EXPERTS, EXPERT_DIM, D), EXPERT_DIM ** -0.5),
    }


def reference(x, c, ctx, c_ctx, norm_g, ada_w, ada_b, final_g,
              rw_mu, rw_wrkv, rw_wo, rw_w0, rw_w1, rw_w2, rw_a0, rw_a1, rw_a2,
              rw_v0, rw_v1, rw_v2, rw_g1, rw_g2, rw_kk, rw_ka, rw_rk, rw_lnw, rw_lnb,
              hy_win, hy_bin, hy_cw, hy_cb, hy_fw1, hy_fb1, hy_fw2, hy_fb2, hy_fw3, hy_fb3,
              hy_fw4, hy_freq, hy_skip, hy_wo, hy_bo,
              hg_win, hg_lb, hg_gn, hg_wo,
              ffn_w13, ffn_w2, moe_router, moe_w13, moe_w2):
    f32 = jnp.float32
    B = x.shape[0]
    lat, cx = x, ctx
    v_first = None
    lbc = jnp.cumsum(jax.nn.softmax(hg_lb.astype(f32), axis=0), axis=0)
    lower = lbc - lbc[:1]
    for i in range(DEPTH):
        last = i == DEPTH - 1
        mod_l = jnp.split((jax.nn.silu(c) @ ada_w[i] + ada_b[i])[:, None, :], 6, axis=-1)
        mod_c = jnp.split((jax.nn.silu(c_ctx) @ ada_w[i] + ada_b[i])[None, None, :], 6, axis=-1)
        hl = rmsnorm(lat, norm_g[i, 0]) * (1.0 + mod_l[1]) + mod_l[0]
        hc = rmsnorm(cx, norm_g[i, 0]) * (1.0 + mod_c[1]) + mod_c[0]
        kind, slot = i % N_MIXERS, i // N_MIXERS
        if kind == 0:
            vres = None if slot == 0 else (rw_v0[slot - 1], rw_v1[slot - 1], rw_v2[slot - 1])
            rw = (rw_mu[slot], rw_wrkv[slot], rw_wo[slot], rw_w0[slot], rw_w1[slot], rw_w2[slot],
                  rw_a0[slot], rw_a1[slot], rw_a2[slot], rw_g1[slot], rw_g2[slot], rw_kk[slot],
                  rw_ka[slot], rw_rk[slot], rw_lnw[slot], rw_lnb[slot], vres)
            zero = jnp.zeros((2, B, RW_H, RW_HEAD, RW_HEAD), f32)
            vf_c = None if v_first is None else v_first[0]
            vf_l = None if v_first is None else v_first[1]
            y_c, s_ctx, v_c = rwkv7_stream(hc, shift_seq(hc), zero, vf_c, *rw)
            y_l, _, v_l = rwkv7_stream(hl, shift_grid(hl), s_ctx, vf_l, *rw)
            if slot == 0:
                v_first = (v_c, v_l)
        elif kind == 1:
            fprm = (hy_fw1[slot], hy_fb1[slot], hy_fw2[slot], hy_fb2[slot], hy_fw3[slot],
                    hy_fb3[slot], hy_fw4[slot], hy_freq[slot])
            hprm = (hy_win[slot], hy_bin[slot], hy_cw[slot], hy_cb[slot], hy_skip[slot],
                    hy_wo[slot], hy_bo[slot])
            y_l = hyena_stream(hl, hyena_filter(hl.shape[1], *fprm), *hprm)
            y_c = None if last else hyena_stream(hc, hyena_filter(hc.shape[1], *fprm), *hprm)
        else:
            zero = jnp.zeros((2, B, HG_H, HG_DK, HG_DV), f32)
            gprm = (lower[i], hg_win[slot], hg_gn[slot], hg_wo[slot])
            y_c, s_ctx = hgrn2_stream(hc, zero, *gprm)
            y_l, _ = hgrn2_stream(hl, s_ctx, *gprm)
        lat = lat + mod_l[2] * y_l
        if not last:
            cx = cx + mod_c[2] * y_c
        hl = rmsnorm(lat, norm_g[i, 1]) * (1.0 + mod_l[4]) + mod_l[3]
        if last:
            n_ctx = 0
            tokens = hl.reshape(-1, D_MODEL)
        else:
            hc = rmsnorm(cx, norm_g[i, 1]) * (1.0 + mod_c[4]) + mod_c[3]
            n_ctx = cx.shape[0] * cx.shape[1]
            tokens = jnp.concatenate([hc.reshape(-1, D_MODEL), hl.reshape(-1, D_MODEL)], axis=0)
        if i % 2 == 0:
            out = swiglu(tokens, ffn_w13[i // 2], ffn_w2[i // 2])
        else:
            out = moe_swiglu(tokens, moe_router[i // 2], moe_w13[i // 2], moe_w2[i // 2])
        lat = lat + mod_l[5] * out[n_ctx:].reshape(lat.shape)
        if not last:
            cx = cx + mod_c[5] * out[:n_ctx].reshape(cx.shape)
    return rmsnorm(lat, final_g)
```

```python
import functools
import math

import jax
import jax.numpy as jnp
from jax import lax
from jax.experimental import pallas as pl
from jax.experimental.pallas import tpu as pltpu

F32 = jnp.float32
BF16 = jnp.bfloat16
HIGHEST = lax.Precision.HIGHEST

D_MODEL = 1024
GRID_W = 64
NORM_EPS = 1e-6
RW_HEAD = 64
RW_H = D_MODEL // RW_HEAD
RW_LN_EPS = 64e-5
HY_ORDER = 2
HY_EMB = 33
HY_BANDS = (HY_EMB - 1) // 2
HY_FAST_DECAY = 0.3
HY_SLOW_DECAY = 1.5
HY_TARGET = 1e-2
HG_DK = 128
HG_H = D_MODEL // HG_DK
HG_CHUNK = 32
N_EXPERTS = 8
TOP_K = 2
N_MIXERS = 3

LANES = 128
VMEM_LIMIT = 56 * 1024 * 1024
RW_CHUNK = 64
SCAN_BLOCK = 256
HY_N2 = 128


def _cparams(sem):
    return pltpu.CompilerParams(dimension_semantics=sem, vmem_limit_bytes=VMEM_LIMIT)


def _dot(a, b, prec=None):
    return jnp.dot(a, b, preferred_element_type=F32, precision=prec)


def _dot_nt(a, b, prec=None):
    return lax.dot_general(a, b, (((1,), (1,)), ((), ())), preferred_element_type=F32, precision=prec)


def _mm_kernel(a_ref, b_ref, o_ref, *, prec):
    o_ref[...] = _dot(a_ref[...], b_ref[...], prec)


def _mm(a, b, *, exact=False):
    M, K = a.shape
    N = b.shape[1]
    dt = F32 if exact else BF16
    a = a.astype(dt)
    b = b.astype(dt)
    Mp = -(-M // 8) * 8
    if Mp != M:
        a = jnp.pad(a, ((0, Mp - M), (0, 0)))
    tm = 512 if Mp % 512 == 0 else (256 if Mp % 256 == 0 else Mp)
    tn = 512 if N % 512 == 0 else N
    out = pl.pallas_call(
        functools.partial(_mm_kernel, prec=HIGHEST if exact else None),
        grid=(Mp // tm, N // tn),
        in_specs=[pl.BlockSpec((tm, K), lambda i, j: (i, 0)),
                  pl.BlockSpec((K, tn), lambda i, j: (0, j))],
        out_specs=pl.BlockSpec((tm, tn), lambda i, j: (i, j)),
        out_shape=jax.ShapeDtypeStruct((Mp, N), F32),
        compiler_params=_cparams(("parallel", "parallel")),
        name="mm",
    )(a, b)
    return out[:M] if Mp != M else out


def _mm3(x, w, **kw):
    B, L, K = x.shape
    return _mm(x.reshape(B * L, K), w, **kw).reshape(B, L, w.shape[1])


def _ffn_kernel(x_ref, g_ref, wg_ref, wu_ref, w2_ref, o_ref, acc_ref, *, n_e, n_f):
    e = pl.program_id(1)
    f = pl.program_id(2)

    @pl.when((e == 0) & (f == 0))
    def _zero():
        acc_ref[...] = jnp.zeros_like(acc_ref)

    x = x_ref[...]
    gate = _dot(x, wg_ref[...])
    up = _dot(x, wu_ref[...])
    h = (gate * jax.nn.sigmoid(gate) * up).astype(BF16)
    acc_ref[...] += g_ref[...] * _dot(h, w2_ref[...])

    @pl.when((e == n_e - 1) & (f == n_f - 1))
    def _store():
        o_ref[...] = acc_ref[...]


def _ffn(x, gates, w13, w2):
    T, D = x.shape
    E, _, F2 = w13.shape
    F = F2 // 2
    tm = 512
    tf = 512 if F % 512 == 0 else 256
    n_f = F // tf
    return pl.pallas_call(
        functools.partial(_ffn_kernel, n_e=E, n_f=n_f),
        grid=(T // tm, E, n_f),
        in_specs=[pl.BlockSpec((tm, D), lambda i, e, f: (i, 0)),
                  pl.BlockSpec((None, tm, 1), lambda i, e, f: (e, i, 0)),
                  pl.BlockSpec((None, D, tf), lambda i, e, f: (e, 0, f)),
                  pl.BlockSpec((None, D, tf), lambda i, e, f: (e, 0, f + n_f)),
                  pl.BlockSpec((None, tf, D), lambda i, e, f: (e, f, 0))],
        out_specs=pl.BlockSpec((tm, D), lambda i, e, f: (i, 0)),
        out_shape=jax.ShapeDtypeStruct((T, D), F32),
        scratch_shapes=[pltpu.VMEM((tm, D), F32)],
        compiler_params=_cparams(("parallel", "arbitrary", "arbitrary")),
        name="ffn",
    )(x.astype(BF16), gates, w13, w13, w2)


def _rwkv_scan_kernel(r_ref, lw_ref, k_ref, v_ref, a_ref, b_ref, h0_ref, y_ref, hT_ref, H_scr,
                      *, reverse, nblk, nchunk, prec):
    C = RW_CHUNK
    half = LANES // 2
    i = pl.program_id(2)

    @pl.when(i == 0)
    def _init():
        H_scr[...] = h0_ref[...]

    t_idx = lax.broadcasted_iota(jnp.int32, (C, LANES), 0)
    s_idx = lax.broadcasted_iota(jnp.int32, (C, LANES), 1) & (half - 1)
    tt = lax.broadcasted_iota(jnp.int32, (C, C), 0)
    ss = lax.broadcasted_iota(jnp.int32, (C, C), 1)
    if reverse:
        strict, incl, tri = s_idx > t_idx, s_idx >= t_idx, (ss >= tt).astype(F32)
    else:
        strict, incl, tri = s_idx < t_idx, s_idx <= t_idx, (ss <= tt).astype(F32)
    eye_lp = (s_idx == t_idx).astype(F32)
    lane = lax.broadcasted_iota(jnp.int32, (1, LANES), 1)
    m0 = (lane < half).astype(F32)
    m1 = 1.0 - m0
    rr = lax.broadcasted_iota(jnp.int32, (LANES, LANES), 0)
    cc = lax.broadcasted_iota(jnp.int32, (LANES, LANES), 1)
    mask_bd = ((rr < half) == (cc < half)).astype(F32)

    def bd(x):
        return jnp.concatenate([x * m0, x * m1], axis=0)

    H = H_scr[...]
    order = range(nchunk - 1, -1, -1) if reverse else range(nchunk)
    for c in order:
        sl = slice(c * C, (c + 1) * C)
        r, lw, k, v, a, b = (ref[sl, :] for ref in (r_ref, lw_ref, k_ref, v_ref, a_ref, b_ref))
        G = _dot(tri, lw, HIGHEST)
        eG = jnp.exp(G)
        eGn = jnp.exp(-G)
        rt, at, kt, bt = r * eG, a * jnp.exp(G - lw), k * eGn, b * eGn
        M = _dot_nt(jnp.concatenate([at, rt], axis=0), jnp.concatenate([bd(bt), bd(kt)], axis=0), prec)
        Nm = jnp.where(strict, M[:C, :LANES], 0.0)
        Aak = jnp.where(strict, M[:C, LANES:], 0.0)
        Arb = jnp.where(incl, M[C:, :LANES], 0.0)
        Ark = jnp.where(incl, M[C:, LANES:], 0.0)
        T = eye_lp + Nm
        P = Nm
        for _ in range(int(math.log2(C)) - 1):
            P = _dot(P, bd(P), prec)
            T = T + _dot(T, bd(P), prec)
        bdv = bd(v)
        X = _dot_nt(at, H, prec) + _dot(Aak, bdv, prec)
        U = _dot(T, bd(X), prec)
        y_ref[sl, :] = _dot_nt(rt, H, prec) + _dot(Arb, bd(U), prec) + _dot(Ark, bdv, prec)
        dH = _dot(jnp.concatenate([U, v], axis=0).T, jnp.concatenate([bt, kt], axis=0), prec)
        g_end = G[0:1, :] if reverse else G[C - 1:C, :]
        H = (H + dH * mask_bd) * jnp.exp(g_end)
    H_scr[...] = H

    @pl.when(i == nblk - 1)
    def _fin():
        hT_ref[...] = H


def _pair_states(S):
    B, H, N, _ = S.shape
    S5 = S.reshape(B, H // 2, 2, N, N)
    eye = jnp.eye(2, dtype=S.dtype)
    return jnp.einsum('bpivk,ij->bpivjk', S5, eye).reshape(B, H // 2, 2 * N, 2 * N)


def _unpair_states(Sp):
    B, P, N2, _ = Sp.shape
    N = N2 // 2
    S6 = Sp.reshape(B, P, 2, N, 2, N)
    return jnp.stack([S6[:, :, 0, :, 0, :], S6[:, :, 1, :, 1, :]], axis=2).reshape(B, 2 * P, N, N)


def _rwkv_scan(r, lw, k, v, a, b, state0, reverse, prec=HIGHEST):
    B, L, D = r.shape
    bt = SCAN_BLOCK
    nblk = L // bt
    npair = D // LANES
    blk = (lambda i: nblk - 1 - i) if reverse else (lambda i: i)
    tok = pl.BlockSpec((None, bt, LANES), lambda bb, p, i: (bb, blk(i), p))
    st = pl.BlockSpec((None, None, LANES, LANES), lambda bb, p, i: (bb, p, 0, 0))
    y, hT = pl.pallas_call(
        functools.partial(_rwkv_scan_kernel, reverse=reverse, nblk=nblk, nchunk=bt // RW_CHUNK, prec=prec),
        grid=(B, npair, nblk),
        in_specs=[tok] * 6 + [st],
        out_specs=[tok, st],
        out_shape=[jax.ShapeDtypeStruct((B, L, D), F32),
                   jax.ShapeDtypeStruct((B, npair, LANES, LANES), F32)],
        scratch_shapes=[pltpu.VMEM((LANES, LANES), F32)],
        compiler_params=_cparams(("parallel", "parallel", "arbitrary")),
        name="rwkv_scan_rev" if reverse else "rwkv_scan_fwd",
    )(r, lw, k, v, a, b, _pair_states(state0))
    return y, _unpair_states(hT)


def _gla_kernel(q_ref, k_ref, v_ref, lf_ref, h0_ref, o_ref, hT_ref, H_scr, *, reverse, nblk, nchunk, prec):
    C = HG_CHUNK
    i = pl.program_id(2)

    @pl.when(i == 0)
    def _init():
        H_scr[...] = h0_ref[...]

    tt = lax.broadcasted_iota(jnp.int32, (C, C), 0)
    ss = lax.broadcasted_iota(jnp.int32, (C, C), 1)
    incl = (ss >= tt) if reverse else (ss <= tt)
    tri = incl.astype(F32)
    mid = C // 2 if reverse else C // 2 - 1
    end = 0 if reverse else C - 1

    H = H_scr[...]
    order = range(nchunk - 1, -1, -1) if reverse else range(nchunk)
    for c in order:
        sl = slice(c * C, (c + 1) * C)
        q, k, v, lf = (ref[sl, :] for ref in (q_ref, k_ref, v_ref, lf_ref))
        G = _dot(tri, lf, HIGHEST)
        g_mid = G[mid:mid + 1, :]
        g_end = G[end:end + 1, :]
        scores = jnp.where(incl, _dot_nt(q * jnp.exp(G - g_mid), k * jnp.exp(g_mid - G), prec), 0.0)
        o_ref[sl, :] = _dot(scores, v, prec) + _dot_nt(q * jnp.exp(G), H, prec)
        H = H * jnp.exp(g_end) + _dot(v.T, k * jnp.exp(g_end - G), prec)
    H_scr[...] = H

    @pl.when(i == nblk - 1)
    def _fin():
        hT_ref[...] = H


def _gla_scan(q, k, v, lf, state0, reverse, prec=HIGHEST):
    B, L, D = q.shape
    bt = SCAN_BLOCK
    nblk = L // bt
    nh = D // LANES
    blk = (lambda i: nblk - 1 - i) if reverse else (lambda i: i)
    tok = pl.BlockSpec((None, bt, LANES), lambda bb, p, i: (bb, blk(i), p))
    st = pl.BlockSpec((None, None, LANES, LANES), lambda bb, p, i: (bb, p, 0, 0))
    o, hT = pl.pallas_call(
        functools.partial(_gla_kernel, reverse=reverse, nblk=nblk, nchunk=bt // HG_CHUNK, prec=prec),
        grid=(B, nh, nblk),
        in_specs=[tok] * 4 + [st],
        out_specs=[tok, st],
        out_shape=[jax.ShapeDtypeStruct((B, L, D), F32),
                   jax.ShapeDtypeStruct((B, nh, LANES, LANES), F32)],
        scratch_shapes=[pltpu.VMEM((LANES, LANES), F32)],
        compiler_params=_cparams(("parallel", "parallel", "arbitrary")),
        name="gla_scan_rev" if reverse else "gla_scan_fwd",
    )(q, k, v, lf, jnp.swapaxes(state0, 2, 3))
    return o, jnp.swapaxes(hT, 2, 3)


def _leftmul_kernel(f_ref, x_ref, o_ref):
    o_ref[...] = _dot(f_ref[...], x_ref[...], HIGHEST)


def _leftmul_gate_kernel(f_ref, x_ref, u_ref, s_ref, g_ref, o_ref):
    o_ref[...] = (_dot(f_ref[...], x_ref[...], HIGHEST) + u_ref[...] * s_ref[...]) * g_ref[...]


def _leftmul(f, x, epilogue=None):
    B, K, NC = x.shape
    M = f.shape[0]
    tn = 2048 if NC % 2048 == 0 else NC
    fs = pl.BlockSpec((M, K), lambda b, j: (0, 0))
    xs = pl.BlockSpec((None, K, tn), lambda b, j: (b, 0, j))
    os_ = pl.BlockSpec((None, M, tn), lambda b, j: (b, 0, j))
    if epilogue is None:
        kern, specs, args = _leftmul_kernel, [fs, xs], (f, x)
    else:
        u, skip, gate = epilogue
        kern = _leftmul_gate_kernel
        specs = [fs, xs, os_, pl.BlockSpec((1, tn), lambda b, j: (0, j)), os_]
        args = (f, x, u, skip, gate)
    return pl.pallas_call(
        kern, grid=(B, NC // tn), in_specs=specs, out_specs=os_,
        out_shape=jax.ShapeDtypeStruct((B, M, NC), F32),
        compiler_params=_cparams(("parallel", "parallel")),
        name="dft_outer",
    )(*args)


def _spectral_fwd_kernel(f_ref, y_ref, o_ref):
    o_ref[...] = _dot(f_ref[...], y_ref[...], HIGHEST)


def _spectral_conv_kernel(f_ref, fi_ref, kf_ref, y_ref, o_ref):
    n2 = kf_ref.shape[0] // 2
    z = _dot(f_ref[...], y_ref[...], HIGHEST)
    zr, zi = z[:n2], z[n2:]
    kr, ki = kf_ref[:n2, :], kf_ref[n2:, :]
    p = jnp.concatenate([zr * kr - zi * ki, zr * ki + zi * kr], axis=0)
    o_ref[...] = _dot(fi_ref[...], p, HIGHEST)


def _spectral_mid(y, f_fwd, f_inv=None, kf=None):
    B, K1, R, D = y.shape
    ms = pl.BlockSpec((None, R, R), lambda k1, b: (k1, 0, 0))
    ys = pl.BlockSpec((None, None, R, D), lambda k1, b: (b, k1, 0, 0))
    if kf is None:
        kern, specs, args = _spectral_fwd_kernel, [ms, ys], (f_fwd, y)
    else:
        kern = _spectral_conv_kernel
        specs = [ms, ms, pl.BlockSpec((None, R, D), lambda k1, b: (k1, 0, 0)), ys]
        args = (f_fwd, f_inv, kf, y)
    return pl.pallas_call(
        kern, grid=(K1, B), in_specs=specs, out_specs=ys,
        out_shape=jax.ShapeDtypeStruct((B, K1, R, D), F32),
        compiler_params=_cparams(("parallel", "parallel")),
        name="dft_inner",
    )(*args)


def _dft_tables(L):
    n2 = HY_N2 if L % HY_N2 == 0 and L > HY_N2 * 2 else 1
    N = 2 * L
    n1 = N // n2
    k1h = n1 // 2
    ph = (jnp.arange(k1h, dtype=jnp.int32)[:, None] * 2 + 1) * jnp.arange(k1h, dtype=jnp.int32)[None, :]
    th = (ph % (2 * n1)).astype(F32) * (math.pi / n1)
    f_out = jnp.stack([jnp.cos(th), -jnp.sin(th)], axis=1).reshape(2 * k1h, k1h)
    f_out_inv = (2.0 / N) * f_out.T
    if n2 == 1:
        return n2, f_out, f_out_inv, None, None
    kk = jnp.arange(k1h, dtype=jnp.int32)[:, None, None] + n1 * jnp.arange(n2, dtype=jnp.int32)[None, :, None]
    ph = ((2 * kk + 1) * jnp.arange(n2, dtype=jnp.int32)[None, None, :]) % (2 * N)
    phi = ph.astype(F32) * (math.pi / N)
    c, s = jnp.cos(phi), jnp.sin(phi)
    f_in = jnp.concatenate([jnp.concatenate([c, s], axis=2), jnp.concatenate([-s, c], axis=2)], axis=1)
    ct, st = jnp.swapaxes(c, 1, 2), jnp.swapaxes(s, 1, 2)
    f_in_inv = jnp.concatenate([jnp.concatenate([ct, -st], axis=2), jnp.concatenate([st, ct], axis=2)], axis=1)
    return n2, f_out, f_out_inv, f_in, f_in_inv


def _spectrum(x, tables):
    n2, f_out, _, f_in, _ = tables
    B, L, D = x.shape
    k1h = L // n2
    y = _leftmul(f_out, x.reshape(B, k1h, n2 * D)).reshape(B, k1h, 2 * n2, D)
    return y if f_in is None else _spectral_mid(y, f_in)


def _long_conv_gated(u, kf, skip, gate, tables):
    n2, f_out, f_out_inv, f_in, f_in_inv = tables
    B, L, D = u.shape
    k1h = L // n2
    y = _leftmul(f_out, u.reshape(B, k1h, n2 * D)).reshape(B, k1h, 2 * n2, D)
    if f_in is None:
        yr, yi = y[:, :, 0], y[:, :, 1]
        kr, ki = kf[:, 0], kf[:, 1]
        q = jnp.stack([yr * kr - yi * ki, yr * ki + yi * kr], axis=2)
    else:
        q = _spectral_mid(y, f_in, f_in_inv, kf)
    out = _leftmul(f_out_inv, q.reshape(B, 2 * k1h, n2 * D),
                   epilogue=(u.reshape(B, k1h, n2 * D), jnp.tile(skip, n2)[None, :], gate.reshape(B, k1h, n2 * D)))
    return out.reshape(B, L, D)


def _hyena_filter_spectra(L, w1, b1, w2, b2, w3, b3, w4, freq, tables):
    pos = jnp.arange(L, dtype=F32)
    t = (pos / max(L - 1, 1))[:, None]
    ang = (2.0 * math.pi / L) * pos[:, None] * jnp.linspace(1e-4, HY_BANDS - 1, HY_BANDS, dtype=F32)[None, :]
    z = jnp.concatenate([t, jnp.cos(ang), -jnp.sin(ang)], axis=-1)
    z = jnp.pad(z, ((0, 0), (0, 40 - HY_EMB)))
    w1p = jnp.pad(w1, ((0, 40 - HY_EMB), (0, 0)))
    hid = jnp.sin(freq[0] * (_mm(z, w1p, exact=True) + b1))
    hid = jnp.sin(freq[1] * (_mm(hid, w2, exact=True) + b2))
    hid = jnp.sin(freq[2] * (_mm(hid, w3, exact=True) + b3))
    filt = _mm(hid, w4, exact=True)
    deltas = jnp.linspace(math.log(HY_TARGET) / HY_FAST_DECAY, math.log(HY_TARGET) / HY_SLOW_DECAY,
                          D_MODEL, dtype=F32)
    window = jnp.exp(-t * jnp.abs(deltas))
    filt = filt.reshape(L, HY_ORDER * 2, D_MODEL) * window[:, None, :]
    lag0 = (jnp.arange(L) > 0).astype(F32)[:, None, None]
    dirmask = jnp.array([0.0, 1.0] * HY_ORDER, F32)[None, :, None]
    filt = filt * (1.0 - dirmask * (1.0 - lag0))
    spec = _spectrum(jnp.moveaxis(filt, 1, 0), tables)
    n2 = spec.shape[2] // 2
    out = []
    for o in range(HY_ORDER):
        hf, hb = spec[2 * o], spec[2 * o + 1]
        out.append(jnp.concatenate([hf[:, :n2] + hb[:, :n2], hf[:, n2:] - hb[:, n2:]], axis=1))
    return out


def _hyena_stream(h, fprm, w_in, b_in, conv_w, conv_b, skip, w_o, b_o):
    B, L, D = h.shape
    tables = _dft_tables(L)
    kfs = _hyena_filter_spectra(L, *fprm, tables)
    proj = _mm3(h, w_in) + b_in
    prev = jnp.pad(proj[:, :-1], ((0, 0), (1, 0), (0, 0)))
    nxt = jnp.pad(proj[:, 1:], ((0, 0), (0, 1), (0, 0)))
    proj = prev * conv_w[0] + proj * conv_w[1] + nxt * conv_w[2] + conv_b
    x1, x2, v = jnp.split(proj, 3, axis=-1)
    z = _long_conv_gated(v, kfs[0], skip[0], x1, tables)
    y = _long_conv_gated(z, kfs[1], skip[1], x2, tables)
    return _mm3(y, w_o) + b_o


def _rmsnorm(x, g, eps=NORM_EPS):
    return x * lax.rsqrt(jnp.mean(x * x, axis=-1, keepdims=True) + eps) * g


def _shift_seq(h):
    half = h.shape[-1] // 2
    prev = jnp.pad(h[:, :-1, :half], ((0, 0), (1, 0), (0, 0)))
    nxt = jnp.pad(h[:, 1:, half:], ((0, 0), (0, 1), (0, 0)))
    return jnp.concatenate([prev, nxt], axis=-1)


def _shift_grid(h):
    b, l, d = h.shape
    rows = l // GRID_W
    g = h.reshape(b, rows, GRID_W, d)
    q = d // 4
    left = jnp.pad(g[:, :, :-1, :q], ((0, 0), (0, 0), (1, 0), (0, 0)))
    right = jnp.pad(g[:, :, 1:, q:2 * q], ((0, 0), (0, 0), (0, 1), (0, 0)))
    up = jnp.pad(g[:, :-1, :, 2 * q:3 * q], ((0, 0), (1, 0), (0, 0), (0, 0)))
    down = jnp.pad(g[:, 1:, :, 3 * q:], ((0, 0), (0, 1), (0, 0), (0, 0)))
    return jnp.concatenate([left, right, up, down], axis=-1).reshape(b, l, d)


def _rwkv7_stream(h, shifted, state0, v_first, mu, w_rkv, w_o, w0, w1, w2, a0, a1, a2,
                  g1, g2, k_k, k_a, r_k, ln_w, ln_b, vres):
    B, L, D = h.shape
    xx = shifted - h
    xr, xw, xk, xv, xa, xg = (h + xx * mu[j] for j in range(6))
    r, k, v = _mm3(xr, w_rkv[0]), _mm3(xk, w_rkv[1]), _mm3(xv, w_rkv[2])
    if vres is not None:
        v0, v1, v2 = vres
        v = v + (v_first - v) * jax.nn.sigmoid(v0 + _mm3(_mm3(xv, v1), v2))
    heads = lambda t: t.reshape(B, L, RW_H, RW_HEAD)
    kk = heads(k * k_k)
    kk = (kk / jnp.maximum(jnp.linalg.norm(kk, axis=-1, keepdims=True), 1e-12)).reshape(B, L, D)
    rkf = r_k.reshape(D)
    y = 0.0
    bonus = 0.0
    states = []
    for d in range(2):
        wl = w0[d] + _mm3(jnp.tanh(_mm3(xw, w1[d])), w2[d])
        lw = -jnp.exp(-jax.nn.softplus(-wl) - 0.5)
        a = jax.nn.sigmoid(a0[d] + _mm3(_mm3(xa, a1[d]), a2[d]))
        kd = k * (1.0 + (a - 1.0) * k_a)
        yd, sd = _rwkv_scan(r, lw, kd, v, -kk, kk * a, state0[d], reverse=(d == 1))
        y = y + yd
        bonus = bonus + (jnp.sum(heads(r * kd * rkf), axis=-1, keepdims=True) * heads(v)).reshape(B, L, D)
        states.append(sd)
    yh = heads(y)
    mean = jnp.mean(yh, axis=-1, keepdims=True)
    var = jnp.mean(jnp.square(yh - mean), axis=-1, keepdims=True)
    y = ((yh - mean) * lax.rsqrt(var + RW_LN_EPS)).reshape(B, L, D) * ln_w + ln_b
    y = y + bonus
    g = _mm3(jax.nn.sigmoid(_mm3(xg, g1)), g2)
    return _mm3(y * g, w_o), jnp.stack(states), v


def _hgrn2_stream(h, state0, lb, w_in, gn, w_o):
    B, L, D = h.shape
    q, f_fwd, f_bwd, i_in, g = jnp.split(_mm3(h, w_in), 5, axis=-1)
    q = jax.nn.silu(q)
    o = 0.0
    states = []
    for d, f_raw in enumerate((f_fwd, f_bwd)):
        fg = lb[d] + (1.0 - lb[d]) * jax.nn.sigmoid(f_raw)
        od, sd = _gla_scan(q, 1.0 - fg, i_in, jnp.log(fg), state0[d], reverse=(d == 1))
        o = o + od
        states.append(sd)
    oh = o.reshape(B, L, HG_H, -1)
    oh = oh * lax.rsqrt(jnp.mean(oh * oh, axis=-1, keepdims=True) + NORM_EPS) * gn
    o = oh.reshape(B, L, D) * jax.nn.silu(g)
    return _mm3(o, w_o), jnp.stack(states)


def _route(tokens, w_router):
    logits = _mm(tokens, w_router, exact=True)
    top_val, top_idx = lax.top_k(logits, TOP_K)
    gates = jax.nn.softmax(top_val, axis=-1)
    dense = jnp.sum(jax.nn.one_hot(top_idx, N_EXPERTS, dtype=F32) * gates[..., None], axis=1)
    return dense.T[:, :, None]


def kernel(x, c, ctx, c_ctx, norm_g, ada_w, ada_b, final_g,
           rw_mu, rw_wrkv, rw_wo, rw_w0, rw_w1, rw_w2, rw_a0, rw_a1, rw_a2,
           rw_v0, rw_v1, rw_v2, rw_g1, rw_g2, rw_kk, rw_ka, rw_rk, rw_lnw, rw_lnb,
           hy_win, hy_bin, hy_cw, hy_cb, hy_fw1, hy_fb1, hy_fw2, hy_fb2, hy_fw3, hy_fb3,
           hy_fw4, hy_freq, hy_skip, hy_wo, hy_bo,
           hg_win, hg_lb, hg_gn, hg_wo,
           ffn_w13, ffn_w2, moe_router, moe_w13, moe_w2):
    B = x.shape[0]
    depth = norm_g.shape[0]
    D = D_MODEL
    lat, cx = x, ctx
    v_first = None
    lbc = jnp.cumsum(jax.nn.softmax(hg_lb, axis=0), axis=0)
    lower = lbc - lbc[:1]
    cond = jnp.concatenate([jax.nn.silu(c), jax.nn.silu(c_ctx)[None, :]], axis=0)
    ffn_w13b, ffn_w2b = ffn_w13.astype(BF16), ffn_w2.astype(BF16)
    moe_w13b, moe_w2b = moe_w13.astype(BF16), moe_w2.astype(BF16)
    for i in range(depth):
        last = i == depth - 1
        mod = _mm(cond, ada_w[i]) + ada_b[i]
        mod_l = jnp.split(mod[:B, None, :], 6, axis=-1)
        mod_c = jnp.split(mod[B:, None, :], 6, axis=-1)
        hl = _rmsnorm(lat, norm_g[i, 0]) * (1.0 + mod_l[1]) + mod_l[0]
        hc = _rmsnorm(cx, norm_g[i, 0]) * (1.0 + mod_c[1]) + mod_c[0]
        kind, slot = i % N_MIXERS, i // N_MIXERS
        if kind == 0:
            vres = None if slot == 0 else (rw_v0[slot - 1], rw_v1[slot - 1], rw_v2[slot - 1])
            rw = (rw_mu[slot], rw_wrkv[slot], rw_wo[slot], rw_w0[slot], rw_w1[slot], rw_w2[slot],
                  rw_a0[slot], rw_a1[slot], rw_a2[slot], rw_g1[slot], rw_g2[slot], rw_kk[slot],
                  rw_ka[slot], rw_rk[slot], rw_lnw[slot], rw_lnb[slot], vres)
            zero = jnp.zeros((2, B, RW_H, RW_HEAD, RW_HEAD), F32)
            vf_c = None if v_first is None else v_first[0]
            vf_l = None if v_first is None else v_first[1]
            y_c, s_ctx, v_c = _rwkv7_stream(hc, _shift_seq(hc), zero, vf_c, *rw)
            y_l, _, v_l = _rwkv7_stream(hl, _shift_grid(hl), s_ctx, vf_l, *rw)
            if slot == 0:
                v_first = (v_c, v_l)
        elif kind == 1:
            fprm = (hy_fw1[slot], hy_fb1[slot], hy_fw2[slot], hy_fb2[slot], hy_fw3[slot],
                    hy_fb3[slot], hy_fw4[slot], hy_freq[slot])
            hprm = (hy_win[slot], hy_bin[slot], hy_cw[slot], hy_cb[slot], hy_skip[slot],
                    hy_wo[slot], hy_bo[slot])
            y_l = _hyena_stream(hl, fprm, *hprm)
            y_c = None if last else _hyena_stream(hc, fprm, *hprm)
        else:
            zero = jnp.zeros((2, B, HG_H, HG_DK, D // HG_H), F32)
            gprm = (lower[i], hg_win[slot], hg_gn[slot], hg_wo[slot])
            y_c, s_ctx = _hgrn2_stream(hc, zero, *gprm)
            y_l, _ = _hgrn2_stream(hl, s_ctx, *gprm)
        lat = lat + mod_l[2] * y_l
        if not last:
            cx = cx + mod_c[2] * y_c
        hl = _rmsnorm(lat, norm_g[i, 1]) * (1.0 + mod_l[4]) + mod_l[3]
        streams = [hl.reshape(-1, D)]
        if not last:
            hc = _rmsnorm(cx, norm_g[i, 1]) * (1.0 + mod_c[4]) + mod_c[3]
            streams.append(hc.reshape(-1, D))
        outs = []
        for tokens in streams:
            if i % 2 == 0:
                gates = jnp.ones((1, tokens.shape[0], 1), F32)
                outs.append(_ffn(tokens, gates, ffn_w13b[i // 2][None], ffn_w2b[i // 2][None]))
            else:
                outs.append(_ffn(tokens, _route(tokens, moe_router[i // 2]), moe_w13b[i // 2], moe_w2b[i // 2]))
        lat = lat + mod_l[5] * outs[0].reshape(lat.shape)
        if not last:
            cx = cx + mod_c[5] * outs[1].reshape(cx.shape)
    return _rmsnorm(lat, final_g)
```

```python
import functools
import math

import jax
import jax.numpy as jnp
from jax import lax
from jax.experimental import pallas as pl
from jax.experimental.pallas import tpu as pltpu

F32 = jnp.float32
BF16 = jnp.bfloat16
HIGHEST = lax.Precision.HIGHEST

D_MODEL = 1024
GRID_W = 64
NORM_EPS = 1e-6
RW_HEAD = 64
RW_H = D_MODEL // RW_HEAD
RW_LN_EPS = 64e-5
HY_ORDER = 2
HY_EMB = 33
HY_BANDS = (HY_EMB - 1) // 2
HY_FAST_DECAY = 0.3
HY_SLOW_DECAY = 1.5
HY_TARGET = 1e-2
HG_DK = 128
HG_H = D_MODEL // HG_DK
HG_CHUNK = 32
N_EXPERTS = 8
TOP_K = 2
N_MIXERS = 3

LANES = 128
VMEM_LIMIT = 56 * 1024 * 1024
RW_CHUNK = 64
SCAN_BLOCK = 256
HY_N2 = 128


def _cparams(sem):
    return pltpu.CompilerParams(dimension_semantics=sem, vmem_limit_bytes=VMEM_LIMIT)


def _dot(a, b, prec=None):
    return jnp.dot(a, b, preferred_element_type=F32, precision=prec)


def _dot_nt(a, b, prec=None):
    return lax.dot_general(a, b, (((1,), (1,)), ((), ())), preferred_element_type=F32, precision=prec)


def _split2(x):
    hi = x.astype(BF16)
    lo = (x - hi.astype(F32)).astype(BF16)
    return hi, lo


def _pdot(a, b, mode, nt=False):
    dn = (((1,), (1 if nt else 0,)), ((), ()))
    dg = lambda x, y, p=None: lax.dot_general(x, y, dn, preferred_element_type=F32, precision=p)
    if mode == 'f32':
        return dg(a, b, HIGHEST)
    if mode == 'bf16':
        return dg(a.astype(BF16), b.astype(BF16))
    ah, al = _split2(a)
    bh, bl = _split2(b)
    return dg(ah, bh) + (dg(ah, bl) + dg(al, bh))


def _cumsum_dot(tri, x):
    hi = x.astype(BF16)
    r1 = x - hi.astype(F32)
    mid = r1.astype(BF16)
    lo = (r1 - mid.astype(F32)).astype(BF16)
    n = x.shape[1]
    g = _dot(tri, jnp.concatenate([hi, mid, lo], axis=1))
    return g[:, :n] + (g[:, n:2 * n] + g[:, 2 * n:])


def _mm_kernel(a_ref, b_ref, o_ref, *, prec):
    o_ref[...] = _dot(a_ref[...], b_ref[...], prec)


def _mm(a, b, *, exact=False):
    M, K = a.shape
    N = b.shape[1]
    dt = F32 if exact else BF16
    a = a.astype(dt)
    b = b.astype(dt)
    Mp = -(-M // 8) * 8
    if Mp != M:
        a = jnp.pad(a, ((0, Mp - M), (0, 0)))
    tm = 512 if Mp % 512 == 0 else (256 if Mp % 256 == 0 else Mp)
    tn = 512 if N % 512 == 0 else N
    out = pl.pallas_call(
        functools.partial(_mm_kernel, prec=HIGHEST if exact else None),
        grid=(Mp // tm, N // tn),
        in_specs=[pl.BlockSpec((tm, K), lambda i, j: (i, 0)),
                  pl.BlockSpec((K, tn), lambda i, j: (0, j))],
        out_specs=pl.BlockSpec((tm, tn), lambda i, j: (i, j)),
        out_shape=jax.ShapeDtypeStruct((Mp, N), F32),
        compiler_params=_cparams(("parallel", "parallel")),
        name="mm",
    )(a, b)
    return out[:M] if Mp != M else out


def _mm3(x, w, **kw):
    B, L, K = x.shape
    return _mm(x.reshape(B * L, K), w, **kw).reshape(B, L, w.shape[1])


def _ffn_kernel(x_ref, g_ref, wg_ref, wu_ref, w2_ref, o_ref, acc_ref, *, n_e, n_f):
    e = pl.program_id(1)
    f = pl.program_id(2)

    @pl.when((e == 0) & (f == 0))
    def _zero():
        acc_ref[...] = jnp.zeros_like(acc_ref)

    x = x_ref[...]
    gate = _dot(x, wg_ref[...])
    up = _dot(x, wu_ref[...])
    h = (gate * jax.nn.sigmoid(gate) * up).astype(BF16)
    acc_ref[...] += g_ref[...] * _dot(h, w2_ref[...])

    @pl.when((e == n_e - 1) & (f == n_f - 1))
    def _store():
        o_ref[...] = acc_ref[...]


def _ffn(x, gates, w13, w2):
    T, D = x.shape
    E, _, F2 = w13.shape
    F = F2 // 2
    tm = 512
    tf = 512 if F % 512 == 0 else 256
    n_f = F // tf
    return pl.pallas_call(
        functools.partial(_ffn_kernel, n_e=E, n_f=n_f),
        grid=(T // tm, E, n_f),
        in_specs=[pl.BlockSpec((tm, D), lambda i, e, f: (i, 0)),
                  pl.BlockSpec((None, tm, 1), lambda i, e, f: (e, i, 0)),
                  pl.BlockSpec((None, D, tf), lambda i, e, f: (e, 0, f)),
                  pl.BlockSpec((None, D, tf), lambda i, e, f: (e, 0, f + n_f)),
                  pl.BlockSpec((None, tf, D), lambda i, e, f: (e, f, 0))],
        out_specs=pl.BlockSpec((tm, D), lambda i, e, f: (i, 0)),
        out_shape=jax.ShapeDtypeStruct((T, D), F32),
        scratch_shapes=[pltpu.VMEM((tm, D), F32)],
        compiler_params=_cparams(("parallel", "arbitrary", "arbitrary")),
        name="ffn",
    )(x.astype(BF16), gates, w13, w13, w2)


def _rwkv_scan_kernel(r_ref, lw_ref, k_ref, v_ref, a_ref, b_ref, h0_ref, y_ref, hT_ref, H_scr,
                      *, reverse, nblk, nchunk, npl, modes):
    C = RW_CHUNK
    half = LANES // 2
    m_a, m_inv, m_u, m_st = modes
    i = pl.program_id(2)

    @pl.when(i == 0)
    def _init():
        H_scr[...] = h0_ref[...]

    t_idx = lax.broadcasted_iota(jnp.int32, (C, LANES), 0)
    s_idx = lax.broadcasted_iota(jnp.int32, (C, LANES), 1) & (half - 1)
    tt = lax.broadcasted_iota(jnp.int32, (C, C), 0)
    ss = lax.broadcasted_iota(jnp.int32, (C, C), 1)
    if reverse:
        strict, incl, tri = s_idx > t_idx, s_idx >= t_idx, (ss >= tt).astype(BF16)
    else:
        strict, incl, tri = s_idx < t_idx, s_idx <= t_idx, (ss <= tt).astype(BF16)
    eye_lp = (s_idx == t_idx).astype(F32)
    lane = lax.broadcasted_iota(jnp.int32, (1, LANES), 1)
    m0 = (lane < half).astype(F32)
    m1 = 1.0 - m0
    rr = lax.broadcasted_iota(jnp.int32, (LANES, LANES), 0)
    cc = lax.broadcasted_iota(jnp.int32, (LANES, LANES), 1)
    mask_bd = ((rr < half) == (cc < half)).astype(F32)

    def bd(x):
        return jnp.concatenate([x * m0, x * m1], axis=0)

    order = list(range(nchunk - 1, -1, -1) if reverse else range(nchunk))
    units = [(slice(c * C, (c + 1) * C), slice(q * LANES, (q + 1) * LANES)) for c in order for q in range(npl)]
    ld = lambda ref: [ref[sl, ln] for sl, ln in units]
    r, lw, k, v, a, b = ld(r_ref), ld(lw_ref), ld(k_ref), ld(v_ref), ld(a_ref), ld(b_ref)
    G = [_cumsum_dot(tri, x) for x in lw]
    eG = [jnp.exp(g) for g in G]
    eGn = [jnp.exp(-g) for g in G]
    rt = [x * e for x, e in zip(r, eG)]
    at = [x * jnp.exp(g - l) for x, g, l in zip(a, G, lw)]
    kt = [x * e for x, e in zip(k, eGn)]
    bt = [x * e for x, e in zip(b, eGn)]
    M = [_pdot(jnp.concatenate([x, y], axis=0), jnp.concatenate([bd(z), bd(w)], axis=0), m_a, nt=True)
         for x, y, z, w in zip(at, rt, bt, kt)]
    Nm = [jnp.where(strict, m[:C, :LANES], 0.0) for m in M]
    Aak = [jnp.where(strict, m[:C, LANES:], 0.0) for m in M]
    Arb = [jnp.where(incl, m[C:, :LANES], 0.0) for m in M]
    Ark = [jnp.where(incl, m[C:, LANES:], 0.0) for m in M]
    T = [eye_lp + n for n in Nm]
    P = Nm
    for _ in range(int(math.log2(C)) - 1):
        P = [_pdot(p, bd(p), m_inv) for p in P]
        T = [t + _pdot(t, bd(p), m_inv) for t, p in zip(T, P)]
    bdv = [bd(x) for x in v]
    X0 = [_pdot(x, y, m_st) for x, y in zip(Aak, bdv)]
    WU = [_pdot(t, jnp.concatenate([bd(x), bd(y)], axis=1), m_u) for t, x, y in zip(T, at, X0)]
    W = [x[:, :LANES] for x in WU]
    U0 = [x[:, LANES:] for x in WU]
    RY = [_pdot(x, jnp.concatenate([bd(y), bd(z)], axis=1), m_st) for x, y, z in zip(Arb, W, U0)]
    Rh = [x + y[:, :LANES] for x, y in zip(rt, RY)]
    Y0 = [y[:, LANES:] + _pdot(x, z, m_st) for y, x, z in zip(RY, Ark, bdv)]
    dPhi = [mask_bd * _pdot(x.T, y, m_st) for x, y in zip(W, bt)]
    Psi = [mask_bd * _pdot(jnp.concatenate([x, y], axis=0).T, jnp.concatenate([z, w], axis=0), m_st)
           for x, y, z, w in zip(U0, v, bt, kt)]
    e_end = [jnp.exp(g[0:1, :] if reverse else g[C - 1:C, :]) for g in G]

    Hs = [H_scr[q] for q in range(npl)]
    for u, (sl, ln) in enumerate(units):
        q = u % npl
        H = Hs[q]
        y_ref[sl, ln] = _pdot(Rh[u], H, m_st, nt=True) + Y0[u]
        Hs[q] = (H + _pdot(H, dPhi[u], m_st) + Psi[u]) * e_end[u]
    for q in range(npl):
        H_scr[q] = Hs[q]

    @pl.when(i == nblk - 1)
    def _fin():
        for q in range(npl):
            hT_ref[q] = Hs[q]


def _pair_states(S):
    B, H, N, _ = S.shape
    S5 = S.reshape(B, H // 2, 2, N, N)
    eye = jnp.eye(2, dtype=S.dtype)
    return jnp.einsum('bpivk,ij->bpivjk', S5, eye).reshape(B, H // 2, 2 * N, 2 * N)


def _unpair_states(Sp):
    B, P, N2, _ = Sp.shape
    N = N2 // 2
    S6 = Sp.reshape(B, P, 2, N, 2, N)
    return jnp.stack([S6[:, :, 0, :, 0, :], S6[:, :, 1, :, 1, :]], axis=2).reshape(B, 2 * P, N, N)


RW_MODES = ('bf16', 'bf16', 'bf16', 'bf16')
RW_PAIRS_PER_STEP = 2


def _rwkv_scan(r, lw, k, v, a, b, state0, reverse, modes=RW_MODES):
    B, L, D = r.shape
    bt = SCAN_BLOCK
    nblk = L // bt
    npl = RW_PAIRS_PER_STEP
    npair = D // LANES
    blk = (lambda i: nblk - 1 - i) if reverse else (lambda i: i)
    tok = pl.BlockSpec((None, bt, npl * LANES), lambda bb, p, i: (bb, blk(i), p))
    st = pl.BlockSpec((None, npl, LANES, LANES), lambda bb, p, i: (bb, p, 0, 0))
    y, hT = pl.pallas_call(
        functools.partial(_rwkv_scan_kernel, reverse=reverse, nblk=nblk, nchunk=bt // RW_CHUNK, npl=npl,
                          modes=modes),
        grid=(B, npair // npl, nblk),
        in_specs=[tok] * 6 + [st],
        out_specs=[tok, st],
        out_shape=[jax.ShapeDtypeStruct((B, L, D), F32),
                   jax.ShapeDtypeStruct((B, npair, LANES, LANES), F32)],
        scratch_shapes=[pltpu.VMEM((npl, LANES, LANES), F32)],
        compiler_params=_cparams(("parallel", "parallel", "arbitrary")),
        name="rwkv_scan_rev" if reverse else "rwkv_scan_fwd",
    )(r, lw, k, v, a, b, _pair_states(state0))
    return y, _unpair_states(hT)


def _gla_kernel(q_ref, k_ref, v_ref, lf_ref, h0_ref, o_ref, hT_ref, H_scr, *, reverse, nblk, nchunk, prec):
    C = HG_CHUNK
    i = pl.program_id(2)

    @pl.when(i == 0)
    def _init():
        H_scr[...] = h0_ref[...]

    tt = lax.broadcasted_iota(jnp.int32, (C, C), 0)
    ss = lax.broadcasted_iota(jnp.int32, (C, C), 1)
    incl = (ss >= tt) if reverse else (ss <= tt)
    tri = incl.astype(BF16)
    mid = C // 2 if reverse else C // 2 - 1
    end = 0 if reverse else C - 1

    H = H_scr[...]
    order = range(nchunk - 1, -1, -1) if reverse else range(nchunk)
    for c in order:
        sl = slice(c * C, (c + 1) * C)
        q, k, v, lf = (ref[sl, :] for ref in (q_ref, k_ref, v_ref, lf_ref))
        G = _cumsum_dot(tri, lf)
        g_mid = G[mid:mid + 1, :]
        g_end = G[end:end + 1, :]
        scores = jnp.where(incl, _pdot(q * jnp.exp(G - g_mid), k * jnp.exp(g_mid - G), prec, nt=True), 0.0)
        o_ref[sl, :] = _pdot(scores, v, prec) + _pdot(q * jnp.exp(G), H, prec, nt=True)
        H = H * jnp.exp(g_end) + _pdot(v.T, k * jnp.exp(g_end - G), prec)
    H_scr[...] = H

    @pl.when(i == nblk - 1)
    def _fin():
        hT_ref[...] = H


def _gla_scan(q, k, v, lf, state0, reverse, prec='bf16'):
    B, L, D = q.shape
    bt = SCAN_BLOCK
    nblk = L // bt
    nh = D // LANES
    blk = (lambda i: nblk - 1 - i) if reverse else (lambda i: i)
    tok = pl.BlockSpec((None, bt, LANES), lambda bb, p, i: (bb, blk(i), p))
    st = pl.BlockSpec((None, None, LANES, LANES), lambda bb, p, i: (bb, p, 0, 0))
    o, hT = pl.pallas_call(
        functools.partial(_gla_kernel, reverse=reverse, nblk=nblk, nchunk=bt // HG_CHUNK, prec=prec),
        grid=(B, nh, nblk),
        in_specs=[tok] * 4 + [st],
        out_specs=[tok, st],
        out_shape=[jax.ShapeDtypeStruct((B, L, D), F32),
                   jax.ShapeDtypeStruct((B, nh, LANES, LANES), F32)],
        scratch_shapes=[pltpu.VMEM((LANES, LANES), F32)],
        compiler_params=_cparams(("parallel", "parallel", "arbitrary")),
        name="gla_scan_rev" if reverse else "gla_scan_fwd",
    )(q, k, v, lf, jnp.swapaxes(state0, 2, 3))
    return o, jnp.swapaxes(hT, 2, 3)


def _leftmul_kernel(f_ref, x_ref, o_ref):
    o_ref[...] = _dot(f_ref[...], x_ref[...], HIGHEST)


def _leftmul_gate_kernel(f_ref, x_ref, u_ref, s_ref, g_ref, o_ref):
    o_ref[...] = (_dot(f_ref[...], x_ref[...], HIGHEST) + u_ref[...] * s_ref[...]) * g_ref[...]


def _leftmul(f, x, epilogue=None):
    B, K, NC = x.shape
    M = f.shape[0]
    tn = 2048 if NC % 2048 == 0 else NC
    fs = pl.BlockSpec((M, K), lambda b, j: (0, 0))
    xs = pl.BlockSpec((None, K, tn), lambda b, j: (b, 0, j))
    os_ = pl.BlockSpec((None, M, tn), lambda b, j: (b, 0, j))
    if epilogue is None:
        kern, specs, args = _leftmul_kernel, [fs, xs], (f, x)
    else:
        u, skip, gate = epilogue
        kern = _leftmul_gate_kernel
        specs = [fs, xs, os_, pl.BlockSpec((1, tn), lambda b, j: (0, j)), os_]
        args = (f, x, u, skip, gate)
    return pl.pallas_call(
        kern, grid=(B, NC // tn), in_specs=specs, out_specs=os_,
        out_shape=jax.ShapeDtypeStruct((B, M, NC), F32),
        compiler_params=_cparams(("parallel", "parallel")),
        name="dft_outer",
    )(*args)


def _spectral_fwd_kernel(f_ref, y_ref, o_ref):
    o_ref[...] = _dot(f_ref[...], y_ref[...], HIGHEST)


def _spectral_conv_kernel(f_ref, fi_ref, kf_ref, y_ref, o_ref):
    n2 = kf_ref.shape[0] // 2
    z = _dot(f_ref[...], y_ref[...], HIGHEST)
    zr, zi = z[:n2], z[n2:]
    kr, ki = kf_ref[:n2, :], kf_ref[n2:, :]
    p = jnp.concatenate([zr * kr - zi * ki, zr * ki + zi * kr], axis=0)
    o_ref[...] = _dot(fi_ref[...], p, HIGHEST)


def _spectral_mid(y, f_fwd, f_inv=None, kf=None):
    B, K1, R, D = y.shape
    ms = pl.BlockSpec((None, R, R), lambda k1, b: (k1, 0, 0))
    ys = pl.BlockSpec((None, None, R, D), lambda k1, b: (b, k1, 0, 0))
    if kf is None:
        kern, specs, args = _spectral_fwd_kernel, [ms, ys], (f_fwd, y)
    else:
        kern = _spectral_conv_kernel
        specs = [ms, ms, pl.BlockSpec((None, R, D), lambda k1, b: (k1, 0, 0)), ys]
        args = (f_fwd, f_inv, kf, y)
    return pl.pallas_call(
        kern, grid=(K1, B), in_specs=specs, out_specs=ys,
        out_shape=jax.ShapeDtypeStruct((B, K1, R, D), F32),
        compiler_params=_cparams(("parallel", "parallel")),
        name="dft_inner",
    )(*args)


def _dft_tables(L):
    n2 = HY_N2 if L % HY_N2 == 0 and L > HY_N2 * 2 else 1
    N = 2 * L
    n1 = N // n2
    k1h = n1 // 2
    ph = (jnp.arange(k1h, dtype=jnp.int32)[:, None] * 2 + 1) * jnp.arange(k1h, dtype=jnp.int32)[None, :]
    th = (ph % (2 * n1)).astype(F32) * (math.pi / n1)
    f_out = jnp.stack([jnp.cos(th), -jnp.sin(th)], axis=1).reshape(2 * k1h, k1h)
    f_out_inv = (2.0 / N) * f_out.T
    if n2 == 1:
        return n2, f_out, f_out_inv, None, None
    kk = jnp.arange(k1h, dtype=jnp.int32)[:, None, None] + n1 * jnp.arange(n2, dtype=jnp.int32)[None, :, None]
    ph = ((2 * kk + 1) * jnp.arange(n2, dtype=jnp.int32)[None, None, :]) % (2 * N)
    phi = ph.astype(F32) * (math.pi / N)
    c, s = jnp.cos(phi), jnp.sin(phi)
    f_in = jnp.concatenate([jnp.concatenate([c, s], axis=2), jnp.concatenate([-s, c], axis=2)], axis=1)
    ct, st = jnp.swapaxes(c, 1, 2), jnp.swapaxes(s, 1, 2)
    f_in_inv = jnp.concatenate([jnp.concatenate([ct, -st], axis=2), jnp.concatenate([st, ct], axis=2)], axis=1)
    return n2, f_out, f_out_inv, f_in, f_in_inv


def _spectrum(x, tables):
    n2, f_out, _, f_in, _ = tables
    B, L, D = x.shape
    k1h = L // n2
    y = _leftmul(f_out, x.reshape(B, k1h, n2 * D)).reshape(B, k1h, 2 * n2, D)
    return y if f_in is None else _spectral_mid(y, f_in)


def _long_conv_gated(u, kf, skip, gate, tables):
    n2, f_out, f_out_inv, f_in, f_in_inv = tables
    B, L, D = u.shape
    k1h = L // n2
    y = _leftmul(f_out, u.reshape(B, k1h, n2 * D)).reshape(B, k1h, 2 * n2, D)
    if f_in is None:
        yr, yi = y[:, :, 0], y[:, :, 1]
        kr, ki = kf[:, 0], kf[:, 1]
        q = jnp.stack([yr * kr - yi * ki, yr * ki + yi * kr], axis=2)
    else:
        q = _spectral_mid(y, f_in, f_in_inv, kf)
    out = _leftmul(f_out_inv, q.reshape(B, 2 * k1h, n2 * D),
                   epilogue=(u.reshape(B, k1h, n2 * D), jnp.tile(skip, n2)[None, :], gate.reshape(B, k1h, n2 * D)))
    return out.reshape(B, L, D)


def _hyena_filter_spectra(L, w1, b1, w2, b2, w3, b3, w4, freq, tables):
    pos = jnp.arange(L, dtype=F32)
    t = (pos / max(L - 1, 1))[:, None]
    ang = (2.0 * math.pi / L) * pos[:, None] * jnp.linspace(1e-4, HY_BANDS - 1, HY_BANDS, dtype=F32)[None, :]
    z = jnp.concatenate([t, jnp.cos(ang), -jnp.sin(ang)], axis=-1)
    z = jnp.pad(z, ((0, 0), (0, 40 - HY_EMB)))
    w1p = jnp.pad(w1, ((0, 40 - HY_EMB), (0, 0)))
    hid = jnp.sin(freq[0] * (_mm(z, w1p, exact=True) + b1))
    hid = jnp.sin(freq[1] * (_mm(hid, w2, exact=True) + b2))
    hid = jnp.sin(freq[2] * (_mm(hid, w3, exact=True) + b3))
    filt = _mm(hid, w4, exact=True)
    deltas = jnp.linspace(math.log(HY_TARGET) / HY_FAST_DECAY, math.log(HY_TARGET) / HY_SLOW_DECAY,
                          D_MODEL, dtype=F32)
    window = jnp.exp(-t * jnp.abs(deltas))
    filt = filt.reshape(L, HY_ORDER * 2, D_MODEL) * window[:, None, :]
    lag0 = (jnp.arange(L) > 0).astype(F32)[:, None, None]
    dirmask = jnp.array([0.0, 1.0] * HY_ORDER, F32)[None, :, None]
    filt = filt * (1.0 - dirmask * (1.0 - lag0))
    spec = _spectrum(jnp.moveaxis(filt, 1, 0), tables)
    n2 = spec.shape[2] // 2
    out = []
    for o in range(HY_ORDER):
        hf, hb = spec[2 * o], spec[2 * o + 1]
        out.append(jnp.concatenate([hf[:, :n2] + hb[:, :n2], hf[:, n2:] - hb[:, n2:]], axis=1))
    return out


def _hyena_stream(h, fprm, w_in, b_in, conv_w, conv_b, skip, w_o, b_o):
    B, L, D = h.shape
    tables = _dft_tables(L)
    kfs = _hyena_filter_spectra(L, *fprm, tables)
    proj = _mm3(h, w_in) + b_in
    prev = jnp.pad(proj[:, :-1], ((0, 0), (1, 0), (0, 0)))
    nxt = jnp.pad(proj[:, 1:], ((0, 0), (0, 1), (0, 0)))
    proj = prev * conv_w[0] + proj * conv_w[1] + nxt * conv_w[2] + conv_b
    x1, x2, v = jnp.split(proj, 3, axis=-1)
    z = _long_conv_gated(v, kfs[0], skip[0], x1, tables)
    y = _long_conv_gated(z, kfs[1], skip[1], x2, tables)
    return _mm3(y, w_o) + b_o


def _rmsnorm(x, g, eps=NORM_EPS):
    return x * lax.rsqrt(jnp.mean(x * x, axis=-1, keepdims=True) + eps) * g


def _shift_seq(h):
    half = h.shape[-1] // 2
    prev = jnp.pad(h[:, :-1, :half], ((0, 0), (1, 0), (0, 0)))
    nxt = jnp.pad(h[:, 1:, half:], ((0, 0), (0, 1), (0, 0)))
    return jnp.concatenate([prev, nxt], axis=-1)


def _shift_grid(h):
    b, l, d = h.shape
    rows = l // GRID_W
    g = h.reshape(b, rows, GRID_W, d)
    q = d // 4
    left = jnp.pad(g[:, :, :-1, :q], ((0, 0), (0, 0), (1, 0), (0, 0)))
    right = jnp.pad(g[:, :, 1:, q:2 * q], ((0, 0), (0, 0), (0, 1), (0, 0)))
    up = jnp.pad(g[:, :-1, :, 2 * q:3 * q], ((0, 0), (1, 0), (0, 0), (0, 0)))
    down = jnp.pad(g[:, 1:, :, 3 * q:], ((0, 0), (0, 1), (0, 0), (0, 0)))
    return jnp.concatenate([left, right, up, down], axis=-1).reshape(b, l, d)


def _rwkv7_stream(h, shifted, state0, v_first, mu, w_rkv, w_o, w0, w1, w2, a0, a1, a2,
                  g1, g2, k_k, k_a, r_k, ln_w, ln_b, vres):
    B, L, D = h.shape
    xx = shifted - h
    xr, xw, xk, xv, xa, xg = (h + xx * mu[j] for j in range(6))
    r, k, v = _mm3(xr, w_rkv[0]), _mm3(xk, w_rkv[1]), _mm3(xv, w_rkv[2])
    if vres is not None:
        v0, v1, v2 = vres
        v = v + (v_first - v) * jax.nn.sigmoid(v0 + _mm3(_mm3(xv, v1), v2))
    heads = lambda t: t.reshape(B, L, RW_H, RW_HEAD)
    kk = heads(k * k_k)
    kk = (kk / jnp.maximum(jnp.linalg.norm(kk, axis=-1, keepdims=True), 1e-12)).reshape(B, L, D)
    rkf = r_k.reshape(D)
    y = 0.0
    bonus = 0.0
    states = []
    for d in range(2):
        wl = w0[d] + _mm3(jnp.tanh(_mm3(xw, w1[d])), w2[d])
        lw = -jnp.exp(-jax.nn.softplus(-wl) - 0.5)
        a = jax.nn.sigmoid(a0[d] + _mm3(_mm3(xa, a1[d]), a2[d]))
        kd = k * (1.0 + (a - 1.0) * k_a)
        yd, sd = _rwkv_scan(r, lw, kd, v, -kk, kk * a, state0[d], reverse=(d == 1))
        y = y + yd
        bonus = bonus + (jnp.sum(heads(r * kd * rkf), axis=-1, keepdims=True) * heads(v)).reshape(B, L, D)
        states.append(sd)
    yh = heads(y)
    mean = jnp.mean(yh, axis=-1, keepdims=True)
    var = jnp.mean(jnp.square(yh - mean), axis=-1, keepdims=True)
    y = ((yh - mean) * lax.rsqrt(var + RW_LN_EPS)).reshape(B, L, D) * ln_w + ln_b
    y = y + bonus
    g = _mm3(jax.nn.sigmoid(_mm3(xg, g1)), g2)
    return _mm3(y * g, w_o), jnp.stack(states), v


def _hgrn2_stream(h, state0, lb, w_in, gn, w_o):
    B, L, D = h.shape
    q, f_fwd, f_bwd, i_in, g = jnp.split(_mm3(h, w_in), 5, axis=-1)
    q = jax.nn.silu(q)
    o = 0.0
    states = []
    for d, f_raw in enumerate((f_fwd, f_bwd)):
        fg = lb[d] + (1.0 - lb[d]) * jax.nn.sigmoid(f_raw)
        od, sd = _gla_scan(q, 1.0 - fg, i_in, jnp.log(fg), state0[d], reverse=(d == 1))
        o = o + od
        states.append(sd)
    oh = o.reshape(B, L, HG_H, -1)
    oh = oh * lax.rsqrt(jnp.mean(oh * oh, axis=-1, keepdims=True) + NORM_EPS) * gn
    o = oh.reshape(B, L, D) * jax.nn.silu(g)
    return _mm3(o, w_o), jnp.stack(states)


def _route(tokens, w_router):
    logits = _mm(tokens, w_router, exact=True)
    top_val, top_idx = lax.top_k(logits, TOP_K)
    gates = jax.nn.softmax(top_val, axis=-1)
    dense = jnp.sum(jax.nn.one_hot(top_idx, N_EXPERTS, dtype=F32) * gates[..., None], axis=1)
    return dense.T[:, :, None]


def kernel(x, c, ctx, c_ctx, norm_g, ada_w, ada_b, final_g,
           rw_mu, rw_wrkv, rw_wo, rw_w0, rw_w1, rw_w2, rw_a0, rw_a1, rw_a2,
           rw_v0, rw_v1, rw_v2, rw_g1, rw_g2, rw_kk, rw_ka, rw_rk, rw_lnw, rw_lnb,
           hy_win, hy_bin, hy_cw, hy_cb, hy_fw1, hy_fb1, hy_fw2, hy_fb2, hy_fw3, hy_fb3,
           hy_fw4, hy_freq, hy_skip, hy_wo, hy_bo,
           hg_win, hg_lb, hg_gn, hg_wo,
           ffn_w13, ffn_w2, moe_router, moe_w13, moe_w2):
    B = x.shape[0]
    depth = norm_g.shape[0]
    D = D_MODEL
    lat, cx = x, ctx
    v_first = None
    lbc = jnp.cumsum(jax.nn.softmax(hg_lb, axis=0), axis=0)
    lower = lbc - lbc[:1]
    cond = jnp.concatenate([jax.nn.silu(c), jax.nn.silu(c_ctx)[None, :]], axis=0)
    ffn_w13b, ffn_w2b = ffn_w13.astype(BF16), ffn_w2.astype(BF16)
    moe_w13b, moe_w2b = moe_w13.astype(BF16), moe_w2.astype(BF16)
    for i in range(depth):
        last = i == depth - 1
        mod = _mm(cond, ada_w[i]) + ada_b[i]
        mod_l = jnp.split(mod[:B, None, :], 6, axis=-1)
        mod_c = jnp.split(mod[B:, None, :], 6, axis=-1)
        hl = _rmsnorm(lat, norm_g[i, 0]) * (1.0 + mod_l[1]) + mod_l[0]
        hc = _rmsnorm(cx, norm_g[i, 0]) * (1.0 + mod_c[1]) + mod_c[0]
        kind, slot = i % N_MIXERS, i // N_MIXERS
        if kind == 0:
            vres = None if slot == 0 else (rw_v0[slot - 1], rw_v1[slot - 1], rw_v2[slot - 1])
            rw = (rw_mu[slot], rw_wrkv[slot], rw_wo[slot], rw_w0[slot], rw_w1[slot], rw_w2[slot],
                  rw_a0[slot], rw_a1[slot], rw_a2[slot], rw_g1[slot], rw_g2[slot], rw_kk[slot],
                  rw_ka[slot], rw_rk[slot], rw_lnw[slot], rw_lnb[slot], vres)
            zero = jnp.zeros((2, B, RW_H, RW_HEAD, RW_HEAD), F32)
            vf_c = None if v_first is None else v_first[0]
            vf_l = None if v_first is None else v_first[1]
            y_c, s_ctx, v_c = _rwkv7_stream(hc, _shift_seq(hc), zero, vf_c, *rw)
            y_l, _, v_l = _rwkv7_stream(hl, _shift_grid(hl), s_ctx, vf_l, *rw)
            if slot == 0:
                v_first = (v_c, v_l)
        elif kind == 1:
            fprm = (hy_fw1[slot], hy_fb1[slot], hy_fw2[slot], hy_fb2[slot], hy_fw3[slot],
                    hy_fb3[slot], hy_fw4[slot], hy_freq[slot])
            hprm = (hy_win[slot], hy_bin[slot], hy_cw[slot], hy_cb[slot], hy_skip[slot],
                    hy_wo[slot], hy_bo[slot])
            y_l = _hyena_stream(hl, fprm, *hprm)
            y_c = None if last else _hyena_stream(hc, fprm, *hprm)
        else:
            zero = jnp.zeros((2, B, HG_H, HG_DK, D // HG_H), F32)
            gprm = (lower[i], hg_win[slot], hg_gn[slot], hg_wo[slot])
            y_c, s_ctx = _hgrn2_stream(hc, zero, *gprm)
            y_l, _ = _hgrn2_stream(hl, s_ctx, *gprm)
        lat = lat + mod_l[2] * y_l
        if not last:
            cx = cx + mod_c[2] * y_c
        hl = _rmsnorm(lat, norm_g[i, 1]) * (1.0 + mod_l[4]) + mod_l[3]
        streams = [hl.reshape(-1, D)]
        if not last:
            hc = _rmsnorm(cx, norm_g[i, 1]) * (1.0 + mod_c[4]) + mod_c[3]
            streams.append(hc.reshape(-1, D))
        outs = []
        for tokens in streams:
            if i % 2 == 0:
                gates = jnp.ones((1, tokens.shape[0], 1), F32)
                outs.append(_ffn(tokens, gates, ffn_w13b[i // 2][None], ffn_w2b[i // 2][None]))
            else:
                outs.append(_ffn(tokens, _route(tokens, moe_router[i // 2]), moe_w13b[i // 2], moe_w2b[i // 2]))
        lat = lat + mod_l[5] * outs[0].reshape(lat.shape)
        if not last:
            cx = cx + mod_c[5] * outs[1].reshape(cx.shape)
    return _rmsnorm(lat, final_g)
```

```python
import functools
import math

import jax
import jax.numpy as jnp
from jax import lax
from jax.experimental import pallas as pl
from jax.experimental.pallas import tpu as pltpu

F32 = jnp.float32
BF16 = jnp.bfloat16
HIGHEST = lax.Precision.HIGHEST

D_MODEL = 1024
GRID_W = 64
NORM_EPS = 1e-6
RW_HEAD = 64
RW_H = D_MODEL // RW_HEAD
RW_LN_EPS = 64e-5
HY_ORDER = 2
HY_EMB = 33
HY_BANDS = (HY_EMB - 1) // 2
HY_FAST_DECAY = 0.3
HY_SLOW_DECAY = 1.5
HY_TARGET = 1e-2
HG_DK = 128
HG_H = D_MODEL // HG_DK
HG_CHUNK = 32
N_EXPERTS = 8
TOP_K = 2
N_MIXERS = 3

LANES = 128
VMEM_LIMIT = 56 * 1024 * 1024
RW_CHUNK = 64
SCAN_BLOCK = 256
HY_N2 = 128


def _cparams(sem):
    return pltpu.CompilerParams(dimension_semantics=sem, vmem_limit_bytes=VMEM_LIMIT)


def _dot(a, b, prec=None):
    return jnp.dot(a, b, preferred_element_type=F32, precision=prec)


def _dot_nt(a, b, prec=None):
    return lax.dot_general(a, b, (((1,), (1,)), ((), ())), preferred_element_type=F32, precision=prec)


def _split2(x):
    hi = x.astype(BF16)
    lo = (x - hi.astype(F32)).astype(BF16)
    return hi, lo


def _pdot(a, b, mode, nt=False):
    dn = (((1,), (1 if nt else 0,)), ((), ()))
    dg = lambda x, y, p=None: lax.dot_general(x, y, dn, preferred_element_type=F32, precision=p)
    if mode == 'f32':
        return dg(a, b, HIGHEST)
    if mode == 'bf16':
        return dg(a.astype(BF16), b.astype(BF16))
    ah, al = _split2(a)
    bh, bl = _split2(b)
    return dg(ah, bh) + (dg(ah, bl) + dg(al, bh))


def _cumsum_dot(tri, x):
    hi = x.astype(BF16)
    r1 = x - hi.astype(F32)
    mid = r1.astype(BF16)
    lo = (r1 - mid.astype(F32)).astype(BF16)
    n = x.shape[1]
    g = _dot(tri, jnp.concatenate([hi, mid, lo], axis=1))
    return g[:, :n] + (g[:, n:2 * n] + g[:, 2 * n:])


def _mm_kernel(a_ref, b_ref, o_ref, *, prec):
    o_ref[...] = _dot(a_ref[...], b_ref[...], prec)


def _mm(a, b, *, exact=False):
    M, K = a.shape
    N = b.shape[1]
    dt = F32 if exact else BF16
    a = a.astype(dt)
    b = b.astype(dt)
    Mp = -(-M // 8) * 8
    if Mp != M:
        a = jnp.pad(a, ((0, Mp - M), (0, 0)))
    tm = 512 if Mp % 512 == 0 else (256 if Mp % 256 == 0 else Mp)
    tn = 512 if N % 512 == 0 else N
    out = pl.pallas_call(
        functools.partial(_mm_kernel, prec=HIGHEST if exact else None),
        grid=(Mp // tm, N // tn),
        in_specs=[pl.BlockSpec((tm, K), lambda i, j: (i, 0)),
                  pl.BlockSpec((K, tn), lambda i, j: (0, j))],
        out_specs=pl.BlockSpec((tm, tn), lambda i, j: (i, j)),
        out_shape=jax.ShapeDtypeStruct((Mp, N), F32),
        compiler_params=_cparams(("parallel", "parallel")),
        name="mm",
    )(a, b)
    return out[:M] if Mp != M else out


def _mm3(x, w, **kw):
    B, L, K = x.shape
    return _mm(x.reshape(B * L, K), w, **kw).reshape(B, L, w.shape[1])


def _ffn_kernel(x_ref, g_ref, wg_ref, wu_ref, w2_ref, o_ref, acc_ref, *, n_e, n_f):
    e = pl.program_id(1)
    f = pl.program_id(2)

    @pl.when((e == 0) & (f == 0))
    def _zero():
        acc_ref[...] = jnp.zeros_like(acc_ref)

    x = x_ref[...]
    gate = _dot(x, wg_ref[...])
    up = _dot(x, wu_ref[...])
    h = (gate * jax.nn.sigmoid(gate) * up).astype(BF16)
    acc_ref[...] += g_ref[...] * _dot(h, w2_ref[...])

    @pl.when((e == n_e - 1) & (f == n_f - 1))
    def _store():
        o_ref[...] = acc_ref[...]


def _ffn(x, gates, w13, w2):
    T, D = x.shape
    E, _, F2 = w13.shape
    F = F2 // 2
    tm = 512
    tf = 512 if F % 512 == 0 else 256
    n_f = F // tf
    return pl.pallas_call(
        functools.partial(_ffn_kernel, n_e=E, n_f=n_f),
        grid=(T // tm, E, n_f),
        in_specs=[pl.BlockSpec((tm, D), lambda i, e, f: (i, 0)),
                  pl.BlockSpec((None, tm, 1), lambda i, e, f: (e, i, 0)),
                  pl.BlockSpec((None, D, tf), lambda i, e, f: (e, 0, f)),
                  pl.BlockSpec((None, D, tf), lambda i, e, f: (e, 0, f + n_f)),
                  pl.BlockSpec((None, tf, D), lambda i, e, f: (e, f, 0))],
        out_specs=pl.BlockSpec((tm, D), lambda i, e, f: (i, 0)),
        out_shape=jax.ShapeDtypeStruct((T, D), F32),
        scratch_shapes=[pltpu.VMEM((tm, D), F32)],
        compiler_params=_cparams(("parallel", "arbitrary", "arbitrary")),
        name="ffn",
    )(x.astype(BF16), gates, w13, w13, w2)


MOE_TILE = 2048
MOE_SUB = 256
MOE_SLAB = 512


def _moe_kernel(cnt_ref, x_ref, rrow_ref, rcol_ref, gcol_ref, wg_ref, wu_ref, w2_ref, o_ref, xc_scr, y_scr,
                *, n_f, tm):
    i, e, f = pl.program_id(0), pl.program_id(1), pl.program_id(2)
    sub = MOE_SUB
    n_sub = (cnt_ref[i, e] + (sub - 1)) // sub

    @pl.when((e == 0) & (f == 0))
    def _zero():
        o_ref[...] = jnp.zeros_like(o_ref)

    @pl.when(f == 0)
    def _compact():
        def body(s, carry):
            ridx = lax.broadcasted_iota(jnp.int32, (sub, tm), 0) + s * sub
            onehot = jnp.where(rrow_ref[...] == ridx, 1.0, 0.0).astype(BF16)
            xc_scr[pl.ds(pl.multiple_of(s * sub, sub), sub), :] = _dot(onehot, x_ref[...]).astype(BF16)
            return carry
        lax.fori_loop(0, n_sub, body, 0)

    def expert(s, first):
        rows = pl.ds(pl.multiple_of(s * sub, sub), sub)
        xs = xc_scr[rows, :]
        gate = _dot(xs, wg_ref[...])
        up = _dot(xs, wu_ref[...])
        h = (gate * jax.nn.sigmoid(gate) * up).astype(BF16)
        part = _dot(h, w2_ref[...])
        y_scr[rows, :] = part if first else y_scr[rows, :] + part

    @pl.when(f == 0)
    def _first():
        lax.fori_loop(0, n_sub, lambda s, c: (expert(s, True), c)[1], 0)

    @pl.when(f > 0)
    def _rest():
        lax.fori_loop(0, n_sub, lambda s, c: (expert(s, False), c)[1], 0)

    @pl.when(f == n_f - 1)
    def _scatter():
        def body(s, carry):
            y = y_scr[pl.ds(pl.multiple_of(s * sub, sub), sub), :]
            yh, yl = _split2(y)
            for j in range(tm // MOE_SLAB):
                rows = slice(j * MOE_SLAB, (j + 1) * MOE_SLAB)
                cidx = lax.broadcasted_iota(jnp.int32, (MOE_SLAB, sub), 1) + s * sub
                onehot_t = jnp.where(rcol_ref[rows, :] == cidx, 1.0, 0.0).astype(BF16)
                z = _dot(onehot_t, yh) + _dot(onehot_t, yl)
                o_ref[rows, :] += gcol_ref[rows, :] * z
            return carry
        lax.fori_loop(0, n_sub, body, 0)


def _moe(x, w_router, w13, w2):
    T, D = x.shape
    E, _, F2 = w13.shape
    F = F2 // 2
    tm = min(MOE_TILE, T)
    nt = T // tm
    tf = 512
    n_f = F // tf
    logits = _mm(x, w_router, exact=True)
    top_val, top_idx = lax.top_k(logits, TOP_K)
    gates = jax.nn.softmax(top_val, axis=-1)
    onehot = jax.nn.one_hot(top_idx, E, dtype=F32)
    sel = jnp.sum(onehot, axis=1).astype(jnp.int32).reshape(nt, tm, E)
    gate_dense = jnp.sum(onehot * gates[..., None], axis=1).reshape(nt, tm, E)
    rank = jnp.where(sel > 0, jnp.cumsum(sel, axis=1) - sel, -1)
    rank = jnp.swapaxes(rank, 1, 2)
    counts = jnp.sum(sel, axis=1)
    gcol = jnp.swapaxes(gate_dense, 1, 2)[..., None]
    tile = lambda shape, imap: pl.BlockSpec(shape, imap)
    return pl.pallas_call(
        functools.partial(_moe_kernel, n_f=n_f, tm=tm),
        grid_spec=pltpu.PrefetchScalarGridSpec(
            num_scalar_prefetch=1,
            grid=(nt, E, n_f),
            in_specs=[tile((tm, D), lambda i, e, f, c: (i, 0)),
                      tile((None, None, 1, tm), lambda i, e, f, c: (i, e, 0, 0)),
                      tile((None, None, tm, 1), lambda i, e, f, c: (i, e, 0, 0)),
                      tile((None, None, tm, 1), lambda i, e, f, c: (i, e, 0, 0)),
                      tile((None, D, tf), lambda i, e, f, c: (e, 0, f)),
                      tile((None, D, tf), lambda i, e, f, c: (e, 0, f + n_f)),
                      tile((None, tf, D), lambda i, e, f, c: (e, f, 0))],
            out_specs=tile((tm, D), lambda i, e, f, c: (i, 0)),
            scratch_shapes=[pltpu.VMEM((tm, D), BF16), pltpu.VMEM((tm, D), F32)]),
        out_shape=jax.ShapeDtypeStruct((T, D), F32),
        compiler_params=_cparams(("parallel", "arbitrary", "arbitrary")),
        name="moe",
    )(counts, x.astype(BF16), rank[:, :, None, :], rank[..., None], gcol, w13, w13, w2)


def _rwkv_scan_kernel(r_ref, lw_ref, k_ref, v_ref, a_ref, b_ref, h0_ref, y_ref, hT_ref, H_scr,
                      *, reverse, nblk, nchunk, npl, modes):
    C = RW_CHUNK
    half = LANES // 2
    m_a, m_inv, m_u, m_st = modes
    i = pl.program_id(2)

    @pl.when(i == 0)
    def _init():
        H_scr[...] = h0_ref[...]

    t_idx = lax.broadcasted_iota(jnp.int32, (C, LANES), 0)
    s_idx = lax.broadcasted_iota(jnp.int32, (C, LANES), 1) & (half - 1)
    tt = lax.broadcasted_iota(jnp.int32, (C, C), 0)
    ss = lax.broadcasted_iota(jnp.int32, (C, C), 1)
    if reverse:
        strict, incl, tri = s_idx > t_idx, s_idx >= t_idx, (ss >= tt).astype(BF16)
    else:
        strict, incl, tri = s_idx < t_idx, s_idx <= t_idx, (ss <= tt).astype(BF16)
    eye_lp = (s_idx == t_idx).astype(F32)
    lane = lax.broadcasted_iota(jnp.int32, (1, LANES), 1)
    m0 = (lane < half).astype(F32)
    m1 = 1.0 - m0
    rr = lax.broadcasted_iota(jnp.int32, (LANES, LANES), 0)
    cc = lax.broadcasted_iota(jnp.int32, (LANES, LANES), 1)
    mask_bd = ((rr < half) == (cc < half)).astype(F32)

    def bd(x):
        return jnp.concatenate([x * m0, x * m1], axis=0)

    order = list(range(nchunk - 1, -1, -1) if reverse else range(nchunk))
    units = [(slice(c * C, (c + 1) * C), slice(q * LANES, (q + 1) * LANES)) for c in order for q in range(npl)]
    ld = lambda ref: [ref[sl, ln] for sl, ln in units]
    r, lw, k, v, a, b = ld(r_ref), ld(lw_ref), ld(k_ref), ld(v_ref), ld(a_ref), ld(b_ref)
    G = [_cumsum_dot(tri, x) for x in lw]
    eG = [jnp.exp(g) for g in G]
    eGn = [jnp.exp(-g) for g in G]
    rt = [x * e for x, e in zip(r, eG)]
    at = [x * jnp.exp(g - l) for x, g, l in zip(a, G, lw)]
    kt = [x * e for x, e in zip(k, eGn)]
    bt = [x * e for x, e in zip(b, eGn)]
    M = [_pdot(jnp.concatenate([x, y], axis=0), jnp.concatenate([bd(z), bd(w)], axis=0), m_a, nt=True)
         for x, y, z, w in zip(at, rt, bt, kt)]
    Nm = [jnp.where(strict, m[:C, :LANES], 0.0) for m in M]
    Aak = [jnp.where(strict, m[:C, LANES:], 0.0) for m in M]
    Arb = [jnp.where(incl, m[C:, :LANES], 0.0) for m in M]
    Ark = [jnp.where(incl, m[C:, LANES:], 0.0) for m in M]
    T = [eye_lp + n for n in Nm]
    P = Nm
    for _ in range(int(math.log2(C)) - 1):
        P = [_pdot(p, bd(p), m_inv) for p in P]
        T = [t + _pdot(t, bd(p), m_inv) for t, p in zip(T, P)]
    bdv = [bd(x) for x in v]
    X0 = [_pdot(x, y, m_st) for x, y in zip(Aak, bdv)]
    WU = [_pdot(t, jnp.concatenate([bd(x), bd(y)], axis=1), m_u) for t, x, y in zip(T, at, X0)]
    W = [x[:, :LANES] for x in WU]
    U0 = [x[:, LANES:] for x in WU]
    RY = [_pdot(x, jnp.concatenate([bd(y), bd(z)], axis=1), m_st) for x, y, z in zip(Arb, W, U0)]
    Rh = [x + y[:, :LANES] for x, y in zip(rt, RY)]
    Y0 = [y[:, LANES:] + _pdot(x, z, m_st) for y, x, z in zip(RY, Ark, bdv)]
    dPhi = [mask_bd * _pdot(x.T, y, m_st) for x, y in zip(W, bt)]
    Psi = [mask_bd * _pdot(jnp.concatenate([x, y], axis=0).T, jnp.concatenate([z, w], axis=0), m_st)
           for x, y, z, w in zip(U0, v, bt, kt)]
    e_end = [jnp.exp(g[0:1, :] if reverse else g[C - 1:C, :]) for g in G]

    Hs = [H_scr[q] for q in range(npl)]
    for u, (sl, ln) in enumerate(units):
        q = u % npl
        H = Hs[q]
        y_ref[sl, ln] = _pdot(Rh[u], H, m_st, nt=True) + Y0[u]
        Hs[q] = (H + _pdot(H, dPhi[u], m_st) + Psi[u]) * e_end[u]
    for q in range(npl):
        H_scr[q] = Hs[q]

    @pl.when(i == nblk - 1)
    def _fin():
        for q in range(npl):
            hT_ref[q] = Hs[q]


def _pair_states(S):
    B, H, N, _ = S.shape
    S5 = S.reshape(B, H // 2, 2, N, N)
    eye = jnp.eye(2, dtype=S.dtype)
    return jnp.einsum('bpivk,ij->bpivjk', S5, eye).reshape(B, H // 2, 2 * N, 2 * N)


def _unpair_states(Sp):
    B, P, N2, _ = Sp.shape
    N = N2 // 2
    S6 = Sp.reshape(B, P, 2, N, 2, N)
    return jnp.stack([S6[:, :, 0, :, 0, :], S6[:, :, 1, :, 1, :]], axis=2).reshape(B, 2 * P, N, N)


RW_MODES = ('bf16', 'bf16', 'bf16', 'bf16')
RW_PAIRS_PER_STEP = 2


def _rwkv_scan(r, lw, k, v, a, b, state0, reverse, modes=RW_MODES):
    B, L, D = r.shape
    bt = SCAN_BLOCK
    nblk = L // bt
    npl = RW_PAIRS_PER_STEP
    npair = D // LANES
    blk = (lambda i: nblk - 1 - i) if reverse else (lambda i: i)
    tok = pl.BlockSpec((None, bt, npl * LANES), lambda bb, p, i: (bb, blk(i), p))
    st = pl.BlockSpec((None, npl, LANES, LANES), lambda bb, p, i: (bb, p, 0, 0))
    y, hT = pl.pallas_call(
        functools.partial(_rwkv_scan_kernel, reverse=reverse, nblk=nblk, nchunk=bt // RW_CHUNK, npl=npl,
                          modes=modes),
        grid=(B, npair // npl, nblk),
        in_specs=[tok] * 6 + [st],
        out_specs=[tok, st],
        out_shape=[jax.ShapeDtypeStruct((B, L, D), F32),
                   jax.ShapeDtypeStruct((B, npair, LANES, LANES), F32)],
        scratch_shapes=[pltpu.VMEM((npl, LANES, LANES), F32)],
        compiler_params=_cparams(("parallel", "parallel", "arbitrary")),
        name="rwkv_scan_rev" if reverse else "rwkv_scan_fwd",
    )(r, lw, k, v, a, b, _pair_states(state0))
    return y, _unpair_states(hT)


def _gla_kernel(q_ref, k_ref, v_ref, lf_ref, h0_ref, o_ref, hT_ref, H_scr, *, reverse, nblk, nchunk, prec):
    C = HG_CHUNK
    i = pl.program_id(2)

    @pl.when(i == 0)
    def _init():
        H_scr[...] = h0_ref[...]

    tt = lax.broadcasted_iota(jnp.int32, (C, C), 0)
    ss = lax.broadcasted_iota(jnp.int32, (C, C), 1)
    incl = (ss >= tt) if reverse else (ss <= tt)
    tri = incl.astype(BF16)
    mid = C // 2 if reverse else C // 2 - 1
    end = 0 if reverse else C - 1

    H = H_scr[...]
    order = range(nchunk - 1, -1, -1) if reverse else range(nchunk)
    for c in order:
        sl = slice(c * C, (c + 1) * C)
        q, k, v, lf = (ref[sl, :] for ref in (q_ref, k_ref, v_ref, lf_ref))
        G = _cumsum_dot(tri, lf)
        g_mid = G[mid:mid + 1, :]
        g_end = G[end:end + 1, :]
        scores = jnp.where(incl, _pdot(q * jnp.exp(G - g_mid), k * jnp.exp(g_mid - G), prec, nt=True), 0.0)
        o_ref[sl, :] = _pdot(scores, v, prec) + _pdot(q * jnp.exp(G), H, prec, nt=True)
        H = H * jnp.exp(g_end) + _pdot(v.T, k * jnp.exp(g_end - G), prec)
    H_scr[...] = H

    @pl.when(i == nblk - 1)
    def _fin():
        hT_ref[...] = H


def _gla_scan(q, k, v, lf, state0, reverse, prec='bf16'):
    B, L, D = q.shape
    bt = SCAN_BLOCK
    nblk = L // bt
    nh = D // LANES
    blk = (lambda i: nblk - 1 - i) if reverse else (lambda i: i)
    tok = pl.BlockSpec((None, bt, LANES), lambda bb, p, i: (bb, blk(i), p))
    st = pl.BlockSpec((None, None, LANES, LANES), lambda bb, p, i: (bb, p, 0, 0))
    o, hT = pl.pallas_call(
        functools.partial(_gla_kernel, reverse=reverse, nblk=nblk, nchunk=bt // HG_CHUNK, prec=prec),
        grid=(B, nh, nblk),
        in_specs=[tok] * 4 + [st],
        out_specs=[tok, st],
        out_shape=[jax.ShapeDtypeStruct((B, L, D), F32),
                   jax.ShapeDtypeStruct((B, nh, LANES, LANES), F32)],
        scratch_shapes=[pltpu.VMEM((LANES, LANES), F32)],
        compiler_params=_cparams(("parallel", "parallel", "arbitrary")),
        name="gla_scan_rev" if reverse else "gla_scan_fwd",
    )(q, k, v, lf, jnp.swapaxes(state0, 2, 3))
    return o, jnp.swapaxes(hT, 2, 3)


def _leftmul_kernel(f_ref, x_ref, o_ref, *, nj):
    for j in range(nj):
        o_ref[:, j, :] = _dot(f_ref[...], x_ref[:, j, :], HIGHEST)


def _leftmul_gate_kernel(f_ref, x_ref, u_ref, s_ref, g_ref, o_ref, *, nj):
    for j in range(nj):
        o_ref[:, j, :] = (_dot(f_ref[...], x_ref[:, j, :], HIGHEST) + u_ref[:, j, :] * s_ref[...]) * g_ref[:, j, :]


def _leftmul(f, x, epilogue=None):
    B, K, J, D = x.shape
    M = f.shape[0]
    tj = 8 if J % 8 == 0 else J
    td = 512
    fs = pl.BlockSpec((M, K), lambda b, j, d: (0, 0))
    xs = pl.BlockSpec((None, K, tj, td), lambda b, j, d: (b, 0, j, d))
    os_ = pl.BlockSpec((None, M, tj, td), lambda b, j, d: (b, 0, j, d))
    if epilogue is None:
        kern, specs, args = _leftmul_kernel, [fs, xs], (f, x)
    else:
        u, skip, gate = epilogue
        kern = _leftmul_gate_kernel
        specs = [fs, xs, os_, pl.BlockSpec((1, td), lambda b, j, d: (0, d)), os_]
        args = (f, x, u, skip, gate)
    return pl.pallas_call(
        functools.partial(kern, nj=tj), grid=(B, J // tj, D // td), in_specs=specs, out_specs=os_,
        out_shape=jax.ShapeDtypeStruct((B, M, J, D), F32),
        compiler_params=_cparams(("parallel", "parallel", "parallel")),
        name="dft_outer",
    )(*args)


def _spectral_fwd_kernel(f_ref, y_ref, o_ref):
    o_ref[...] = _dot(f_ref[...], y_ref[...], HIGHEST)


def _spectral_conv_kernel(f_ref, fi_ref, kf_ref, y_ref, o_ref):
    n2 = kf_ref.shape[0] // 2
    z = _dot(f_ref[...], y_ref[...], HIGHEST)
    zr, zi = z[:n2], z[n2:]
    kr, ki = kf_ref[:n2, :], kf_ref[n2:, :]
    p = jnp.concatenate([zr * kr - zi * ki, zr * ki + zi * kr], axis=0)
    o_ref[...] = _dot(fi_ref[...], p, HIGHEST)


def _spectral_mid(y, f_fwd, f_inv=None, kf=None):
    B, K1, R, D = y.shape
    ms = pl.BlockSpec((None, R, R), lambda k1, b: (k1, 0, 0))
    ys = pl.BlockSpec((None, None, R, D), lambda k1, b: (b, k1, 0, 0))
    if kf is None:
        kern, specs, args = _spectral_fwd_kernel, [ms, ys], (f_fwd, y)
    else:
        kern = _spectral_conv_kernel
        specs = [ms, ms, pl.BlockSpec((None, R, D), lambda k1, b: (k1, 0, 0)), ys]
        args = (f_fwd, f_inv, kf, y)
    return pl.pallas_call(
        kern, grid=(K1, B), in_specs=specs, out_specs=ys,
        out_shape=jax.ShapeDtypeStruct((B, K1, R, D), F32),
        compiler_params=_cparams(("parallel", "parallel")),
        name="dft_inner",
    )(*args)


def _dft_tables(L):
    n2 = HY_N2 if L % HY_N2 == 0 and L > HY_N2 * 2 else 1
    N = 2 * L
    n1 = N // n2
    k1h = n1 // 2
    ph = (jnp.arange(k1h, dtype=jnp.int32)[:, None] * 2 + 1) * jnp.arange(k1h, dtype=jnp.int32)[None, :]
    th = (ph % (2 * n1)).astype(F32) * (math.pi / n1)
    f_out = jnp.stack([jnp.cos(th), -jnp.sin(th)], axis=1).reshape(2 * k1h, k1h)
    f_out_inv = (2.0 / N) * f_out.T
    if n2 == 1:
        return n2, f_out, f_out_inv, None, None
    kk = jnp.arange(k1h, dtype=jnp.int32)[:, None, None] + n1 * jnp.arange(n2, dtype=jnp.int32)[None, :, None]
    ph = ((2 * kk + 1) * jnp.arange(n2, dtype=jnp.int32)[None, None, :]) % (2 * N)
    phi = ph.astype(F32) * (math.pi / N)
    c, s = jnp.cos(phi), jnp.sin(phi)
    f_in = jnp.concatenate([jnp.concatenate([c, s], axis=2), jnp.concatenate([-s, c], axis=2)], axis=1)
    ct, st = jnp.swapaxes(c, 1, 2), jnp.swapaxes(s, 1, 2)
    f_in_inv = jnp.concatenate([jnp.concatenate([ct, -st], axis=2), jnp.concatenate([st, ct], axis=2)], axis=1)
    return n2, f_out, f_out_inv, f_in, f_in_inv


def _spectrum(x, tables):
    n2, f_out, _, f_in, _ = tables
    B, L, D = x.shape
    k1h = L // n2
    y = _leftmul(f_out, x.reshape(B, k1h, n2, D)).reshape(B, k1h, 2 * n2, D)
    return y if f_in is None else _spectral_mid(y, f_in)


def _long_conv_gated(u, kf, skip, gate, tables):
    n2, f_out, f_out_inv, f_in, f_in_inv = tables
    B, L, D = u.shape
    k1h = L // n2
    y = _leftmul(f_out, u.reshape(B, k1h, n2, D)).reshape(B, k1h, 2 * n2, D)
    if f_in is None:
        yr, yi = y[:, :, 0], y[:, :, 1]
        kr, ki = kf[:, 0], kf[:, 1]
        q = jnp.stack([yr * kr - yi * ki, yr * ki + yi * kr], axis=2)
    else:
        q = _spectral_mid(y, f_in, f_in_inv, kf)
    out = _leftmul(f_out_inv, q.reshape(B, 2 * k1h, n2, D),
                   epilogue=(u.reshape(B, k1h, n2, D), skip[None, :], gate.reshape(B, k1h, n2, D)))
    return out.reshape(B, L, D)


def _hyena_filter_spectra(L, w1, b1, w2, b2, w3, b3, w4, freq, tables):
    pos = jnp.arange(L, dtype=F32)
    t = (pos / max(L - 1, 1))[:, None]
    ang = (2.0 * math.pi / L) * pos[:, None] * jnp.linspace(1e-4, HY_BANDS - 1, HY_BANDS, dtype=F32)[None, :]
    z = jnp.concatenate([t, jnp.cos(ang), -jnp.sin(ang)], axis=-1)
    z = jnp.pad(z, ((0, 0), (0, 40 - HY_EMB)))
    w1p = jnp.pad(w1, ((0, 40 - HY_EMB), (0, 0)))
    hid = jnp.sin(freq[0] * (_mm(z, w1p, exact=True) + b1))
    hid = jnp.sin(freq[1] * (_mm(hid, w2, exact=True) + b2))
    hid = jnp.sin(freq[2] * (_mm(hid, w3, exact=True) + b3))
    filt = _mm(hid, w4, exact=True)
    deltas = jnp.linspace(math.log(HY_TARGET) / HY_FAST_DECAY, math.log(HY_TARGET) / HY_SLOW_DECAY,
                          D_MODEL, dtype=F32)
    window = jnp.exp(-t * jnp.abs(deltas))
    filt = filt.reshape(L, HY_ORDER * 2, D_MODEL) * window[:, None, :]
    lag0 = (jnp.arange(L) > 0).astype(F32)[:, None, None]
    dirmask = jnp.array([0.0, 1.0] * HY_ORDER, F32)[None, :, None]
    filt = filt * (1.0 - dirmask * (1.0 - lag0))
    spec = _spectrum(jnp.moveaxis(filt, 1, 0), tables)
    n2 = spec.shape[2] // 2
    out = []
    for o in range(HY_ORDER):
        hf, hb = spec[2 * o], spec[2 * o + 1]
        out.append(jnp.concatenate([hf[:, :n2] + hb[:, :n2], hf[:, n2:] - hb[:, n2:]], axis=1))
    return out


def _hyena_stream(h, fprm, w_in, b_in, conv_w, conv_b, skip, w_o, b_o):
    B, L, D = h.shape
    tables = _dft_tables(L)
    kfs = _hyena_filter_spectra(L, *fprm, tables)
    proj = _mm3(h, w_in) + b_in
    prev = jnp.pad(proj[:, :-1], ((0, 0), (1, 0), (0, 0)))
    nxt = jnp.pad(proj[:, 1:], ((0, 0), (0, 1), (0, 0)))
    proj = prev * conv_w[0] + proj * conv_w[1] + nxt * conv_w[2] + conv_b
    x1, x2, v = jnp.split(proj, 3, axis=-1)
    z = _long_conv_gated(v, kfs[0], skip[0], x1, tables)
    y = _long_conv_gated(z, kfs[1], skip[1], x2, tables)
    return _mm3(y, w_o) + b_o


def _rmsnorm(x, g, eps=NORM_EPS):
    return x * lax.rsqrt(jnp.mean(x * x, axis=-1, keepdims=True) + eps) * g


def _shift_seq(h):
    half = h.shape[-1] // 2
    prev = jnp.pad(h[:, :-1, :half], ((0, 0), (1, 0), (0, 0)))
    nxt = jnp.pad(h[:, 1:, half:], ((0, 0), (0, 1), (0, 0)))
    return jnp.concatenate([prev, nxt], axis=-1)


def _shift_grid(h):
    b, l, d = h.shape
    rows = l // GRID_W
    g = h.reshape(b, rows, GRID_W, d)
    q = d // 4
    left = jnp.pad(g[:, :, :-1, :q], ((0, 0), (0, 0), (1, 0), (0, 0)))
    right = jnp.pad(g[:, :, 1:, q:2 * q], ((0, 0), (0, 0), (0, 1), (0, 0)))
    up = jnp.pad(g[:, :-1, :, 2 * q:3 * q], ((0, 0), (1, 0), (0, 0), (0, 0)))
    down = jnp.pad(g[:, 1:, :, 3 * q:], ((0, 0), (0, 1), (0, 0), (0, 0)))
    return jnp.concatenate([left, right, up, down], axis=-1).reshape(b, l, d)


def _rwkv7_stream(h, shifted, state0, v_first, mu, w_rkv, w_o, w0, w1, w2, a0, a1, a2,
                  g1, g2, k_k, k_a, r_k, ln_w, ln_b, vres):
    B, L, D = h.shape
    xx = shifted - h
    xr, xw, xk, xv, xa, xg = (h + xx * mu[j] for j in range(6))
    r, k, v = _mm3(xr, w_rkv[0]), _mm3(xk, w_rkv[1]), _mm3(xv, w_rkv[2])
    if vres is not None:
        v0, v1, v2 = vres
        v = v + (v_first - v) * jax.nn.sigmoid(v0 + _mm3(_mm3(xv, v1), v2))
    heads = lambda t: t.reshape(B, L, RW_H, RW_HEAD)
    kk = heads(k * k_k)
    kk = (kk / jnp.maximum(jnp.linalg.norm(kk, axis=-1, keepdims=True), 1e-12)).reshape(B, L, D)
    rkf = r_k.reshape(D)
    y = 0.0
    bonus = 0.0
    states = []
    for d in range(2):
        wl = w0[d] + _mm3(jnp.tanh(_mm3(xw, w1[d])), w2[d])
        lw = -jnp.exp(-jax.nn.softplus(-wl) - 0.5)
        a = jax.nn.sigmoid(a0[d] + _mm3(_mm3(xa, a1[d]), a2[d]))
        kd = k * (1.0 + (a - 1.0) * k_a)
        yd, sd = _rwkv_scan(r, lw, kd, v, -kk, kk * a, state0[d], reverse=(d == 1))
        y = y + yd
        bonus = bonus + (jnp.sum(heads(r * kd * rkf), axis=-1, keepdims=True) * heads(v)).reshape(B, L, D)
        states.append(sd)
    yh = heads(y)
    mean = jnp.mean(yh, axis=-1, keepdims=True)
    var = jnp.mean(jnp.square(yh - mean), axis=-1, keepdims=True)
    y = ((yh - mean) * lax.rsqrt(var + RW_LN_EPS)).reshape(B, L, D) * ln_w + ln_b
    y = y + bonus
    g = _mm3(jax.nn.sigmoid(_mm3(xg, g1)), g2)
    return _mm3(y * g, w_o), jnp.stack(states), v


def _hgrn2_stream(h, state0, lb, w_in, gn, w_o):
    B, L, D = h.shape
    q, f_fwd, f_bwd, i_in, g = jnp.split(_mm3(h, w_in), 5, axis=-1)
    q = jax.nn.silu(q)
    o = 0.0
    states = []
    for d, f_raw in enumerate((f_fwd, f_bwd)):
        fg = lb[d] + (1.0 - lb[d]) * jax.nn.sigmoid(f_raw)
        od, sd = _gla_scan(q, 1.0 - fg, i_in, jnp.log(fg), state0[d], reverse=(d == 1))
        o = o + od
        states.append(sd)
    oh = o.reshape(B, L, HG_H, -1)
    oh = oh * lax.rsqrt(jnp.mean(oh * oh, axis=-1, keepdims=True) + NORM_EPS) * gn
    o = oh.reshape(B, L, D) * jax.nn.silu(g)
    return _mm3(o, w_o), jnp.stack(states)


def kernel(x, c, ctx, c_ctx, norm_g, ada_w, ada_b, final_g,
           rw_mu, rw_wrkv, rw_wo, rw_w0, rw_w1, rw_w2, rw_a0, rw_a1, rw_a2,
           rw_v0, rw_v1, rw_v2, rw_g1, rw_g2, rw_kk, rw_ka, rw_rk, rw_lnw, rw_lnb,
           hy_win, hy_bin, hy_cw, hy_cb, hy_fw1, hy_fb1, hy_fw2, hy_fb2, hy_fw3, hy_fb3,
           hy_fw4, hy_freq, hy_skip, hy_wo, hy_bo,
           hg_win, hg_lb, hg_gn, hg_wo,
           ffn_w13, ffn_w2, moe_router, moe_w13, moe_w2):
    B = x.shape[0]
    depth = norm_g.shape[0]
    D = D_MODEL
    lat, cx = x, ctx
    v_first = None
    lbc = jnp.cumsum(jax.nn.softmax(hg_lb, axis=0), axis=0)
    lower = lbc - lbc[:1]
    cond = jnp.concatenate([jax.nn.silu(c), jax.nn.silu(c_ctx)[None, :]], axis=0)
    ffn_w13b, ffn_w2b = ffn_w13.astype(BF16), ffn_w2.astype(BF16)
    moe_w13b, moe_w2b = moe_w13.astype(BF16), moe_w2.astype(BF16)
    for i in range(depth):
        last = i == depth - 1
        mod = _mm(cond, ada_w[i]) + ada_b[i]
        mod_l = jnp.split(mod[:B, None, :], 6, axis=-1)
        mod_c = jnp.split(mod[B:, None, :], 6, axis=-1)
        hl = _rmsnorm(lat, norm_g[i, 0]) * (1.0 + mod_l[1]) + mod_l[0]
        hc = _rmsnorm(cx, norm_g[i, 0]) * (1.0 + mod_c[1]) + mod_c[0]
        kind, slot = i % N_MIXERS, i // N_MIXERS
        if kind == 0:
            vres = None if slot == 0 else (rw_v0[slot - 1], rw_v1[slot - 1], rw_v2[slot - 1])
            rw = (rw_mu[slot], rw_wrkv[slot], rw_wo[slot], rw_w0[slot], rw_w1[slot], rw_w2[slot],
                  rw_a0[slot], rw_a1[slot], rw_a2[slot], rw_g1[slot], rw_g2[slot], rw_kk[slot],
                  rw_ka[slot], rw_rk[slot], rw_lnw[slot], rw_lnb[slot], vres)
            zero = jnp.zeros((2, B, RW_H, RW_HEAD, RW_HEAD), F32)
            vf_c = None if v_first is None else v_first[0]
            vf_l = None if v_first is None else v_first[1]
            y_c, s_ctx, v_c = _rwkv7_stream(hc, _shift_seq(hc), zero, vf_c, *rw)
            y_l, _, v_l = _rwkv7_stream(hl, _shift_grid(hl), s_ctx, vf_l, *rw)
            if slot == 0:
                v_first = (v_c, v_l)
        elif kind == 1:
            fprm = (hy_fw1[slot], hy_fb1[slot], hy_fw2[slot], hy_fb2[slot], hy_fw3[slot],
                    hy_fb3[slot], hy_fw4[slot], hy_freq[slot])
            hprm = (hy_win[slot], hy_bin[slot], hy_cw[slot], hy_cb[slot], hy_skip[slot],
                    hy_wo[slot], hy_bo[slot])
            y_l = _hyena_stream(hl, fprm, *hprm)
            y_c = None if last else _hyena_stream(hc, fprm, *hprm)
        else:
            zero = jnp.zeros((2, B, HG_H, HG_DK, D // HG_H), F32)
            gprm = (lower[i], hg_win[slot], hg_gn[slot], hg_wo[slot])
            y_c, s_ctx = _hgrn2_stream(hc, zero, *gprm)
            y_l, _ = _hgrn2_stream(hl, s_ctx, *gprm)
        lat = lat + mod_l[2] * y_l
        if not last:
            cx = cx + mod_c[2] * y_c
        hl = _rmsnorm(lat, norm_g[i, 1]) * (1.0 + mod_l[4]) + mod_l[3]
        streams = [hl.reshape(-1, D)]
        if not last:
            hc = _rmsnorm(cx, norm_g[i, 1]) * (1.0 + mod_c[4]) + mod_c[3]
            streams.append(hc.reshape(-1, D))
        outs = []
        for tokens in streams:
            if i % 2 == 0:
                gates = jnp.ones((1, tokens.shape[0], 1), F32)
                outs.append(_ffn(tokens, gates, ffn_w13b[i // 2][None], ffn_w2b[i // 2][None]))
            else:
                outs.append(_moe(tokens, moe_router[i // 2], moe_w13b[i // 2], moe_w2b[i // 2]))
        lat = lat + mod_l[5] * outs[0].reshape(lat.shape)
        if not last:
            cx = cx + mod_c[5] * outs[1].reshape(cx.shape)
    return _rmsnorm(lat, final_g)
```

```python
import functools
import math

import jax
import jax.numpy as jnp
from jax import lax
from jax.experimental import pallas as pl
from jax.experimental.pallas import tpu as pltpu

F32 = jnp.float32
BF16 = jnp.bfloat16
HIGHEST = lax.Precision.HIGHEST

D_MODEL = 1024
GRID_W = 64
NORM_EPS = 1e-6
RW_HEAD = 64
RW_H = D_MODEL // RW_HEAD
RW_LN_EPS = 64e-5
HY_ORDER = 2
HY_EMB = 33
HY_BANDS = (HY_EMB - 1) // 2
HY_FAST_DECAY = 0.3
HY_SLOW_DECAY = 1.5
HY_TARGET = 1e-2
HG_DK = 128
HG_H = D_MODEL // HG_DK
HG_CHUNK = 32
N_EXPERTS = 8
TOP_K = 2
N_MIXERS = 3

LANES = 128
VMEM_LIMIT = 56 * 1024 * 1024
RW_CHUNK = 64
SCAN_BLOCK = 256
HY_N2 = 128


def _cparams(sem):
    return pltpu.CompilerParams(dimension_semantics=sem, vmem_limit_bytes=VMEM_LIMIT)


def _dot(a, b, prec=None):
    return jnp.dot(a, b, preferred_element_type=F32, precision=prec)


def _dot_nt(a, b, prec=None):
    return lax.dot_general(a, b, (((1,), (1,)), ((), ())), preferred_element_type=F32, precision=prec)


def _split2(x):
    hi = x.astype(BF16)
    lo = (x - hi.astype(F32)).astype(BF16)
    return hi, lo


def _pdot(a, b, mode, nt=False):
    dn = (((1,), (1 if nt else 0,)), ((), ()))
    dg = lambda x, y, p=None: lax.dot_general(x, y, dn, preferred_element_type=F32, precision=p)
    if mode == 'f32':
        return dg(a, b, HIGHEST)
    if mode == 'bf16':
        return dg(a.astype(BF16), b.astype(BF16))
    ah, al = _split2(a)
    bh, bl = _split2(b)
    return dg(ah, bh) + (dg(ah, bl) + dg(al, bh))


def _cumsum_dot(tri, x):
    hi = x.astype(BF16)
    r1 = x - hi.astype(F32)
    mid = r1.astype(BF16)
    lo = (r1 - mid.astype(F32)).astype(BF16)
    n = x.shape[1]
    g = _dot(tri, jnp.concatenate([hi, mid, lo], axis=1))
    return g[:, :n] + (g[:, n:2 * n] + g[:, 2 * n:])


def _mm_kernel(a_ref, b_ref, o_ref, *, prec):
    o_ref[...] = _dot(a_ref[...], b_ref[...], prec)


def _mm(a, b, *, exact=False):
    M, K = a.shape
    N = b.shape[1]
    dt = F32 if exact else BF16
    a = a.astype(dt)
    b = b.astype(dt)
    Mp = -(-M // 8) * 8
    if Mp != M:
        a = jnp.pad(a, ((0, Mp - M), (0, 0)))
    tm = 512 if Mp % 512 == 0 else (256 if Mp % 256 == 0 else Mp)
    tn = 512 if N % 512 == 0 else N
    out = pl.pallas_call(
        functools.partial(_mm_kernel, prec=HIGHEST if exact else None),
        grid=(Mp // tm, N // tn),
        in_specs=[pl.BlockSpec((tm, K), lambda i, j: (i, 0)),
                  pl.BlockSpec((K, tn), lambda i, j: (0, j))],
        out_specs=pl.BlockSpec((tm, tn), lambda i, j: (i, j)),
        out_shape=jax.ShapeDtypeStruct((Mp, N), F32),
        compiler_params=_cparams(("parallel", "parallel")),
        name="mm",
    )(a, b)
    return out[:M] if Mp != M else out


def _mm3(x, w, **kw):
    B, L, K = x.shape
    return _mm(x.reshape(B * L, K), w, **kw).reshape(B, L, w.shape[1])


def _norm_mod(x, vec_ref):
    xn = x * lax.rsqrt(jnp.mean(x * x, axis=-1, keepdims=True) + NORM_EPS)
    return xn * vec_ref[0:1, :] + vec_ref[1:2, :]


def _mod_rows(g, shift, scale, gate, B):
    D = g.shape[-1]
    rows = [jnp.broadcast_to(g * (1.0 + scale[:, 0]), (B, D)), jnp.broadcast_to(shift[:, 0], (B, D)),
            jnp.broadcast_to(gate[:, 0], (B, D))]
    return jnp.stack(rows + [jnp.zeros((B, D), F32)] * 5, axis=1)


def _ffn_kernel(x_ref, vec_ref, wg_ref, wu_ref, w2_ref, o_ref, xn_scr, acc_ref, *, n_f):
    f = pl.program_id(2)

    @pl.when(f == 0)
    def _first():
        xn_scr[...] = _norm_mod(x_ref[...], vec_ref).astype(BF16)
        acc_ref[...] = jnp.zeros_like(acc_ref)

    x = xn_scr[...]
    gate = _dot(x, wg_ref[...])
    up = _dot(x, wu_ref[...])
    h = (gate * jax.nn.sigmoid(gate) * up).astype(BF16)
    acc_ref[...] += _dot(h, w2_ref[...])

    @pl.when(f == n_f - 1)
    def _store():
        o_ref[...] = x_ref[...] + vec_ref[2:3, :] * acc_ref[...]


def _ffn(x, vecs, w13, w2):
    B, L, D = x.shape
    F = w13.shape[1] // 2
    tm = min(512, L)
    tf = 512 if F % 512 == 0 else 256
    n_f = F // tf
    tok = pl.BlockSpec((None, tm, D), lambda b, i, f: (b, i, 0))
    return pl.pallas_call(
        functools.partial(_ffn_kernel, n_f=n_f),
        grid=(B, L // tm, n_f),
        in_specs=[tok,
                  pl.BlockSpec((None, 8, D), lambda b, i, f: (b, 0, 0)),
                  pl.BlockSpec((D, tf), lambda b, i, f: (0, f)),
                  pl.BlockSpec((D, tf), lambda b, i, f: (0, f + n_f)),
                  pl.BlockSpec((tf, D), lambda b, i, f: (f, 0))],
        out_specs=tok,
        out_shape=jax.ShapeDtypeStruct((B, L, D), F32),
        scratch_shapes=[pltpu.VMEM((tm, D), BF16), pltpu.VMEM((tm, D), F32)],
        compiler_params=_cparams(("parallel", "parallel", "arbitrary")),
        name="ffn",
    )(x, vecs, w13, w13, w2)


def _norm_route_kernel(x_ref, vec_ref, wr_ref, xn_ref, lg_ref):
    xn = _norm_mod(x_ref[...], vec_ref)
    xn_ref[...] = xn.astype(BF16)
    lg_ref[...] = _dot(xn, wr_ref[...], HIGHEST)


def _norm_route(x, vecs, w_router):
    B, L, D = x.shape
    E = w_router.shape[1]
    tm = min(512, L)
    tok = pl.BlockSpec((None, tm, D), lambda b, i: (b, i, 0))
    return pl.pallas_call(
        _norm_route_kernel,
        grid=(B, L // tm),
        in_specs=[tok, pl.BlockSpec((None, 8, D), lambda b, i: (b, 0, 0)), pl.BlockSpec((D, E), lambda b, i: (0, 0))],
        out_specs=[tok, pl.BlockSpec((None, tm, E), lambda b, i: (b, i, 0))],
        out_shape=[jax.ShapeDtypeStruct((B, L, D), BF16), jax.ShapeDtypeStruct((B, L, E), F32)],
        compiler_params=_cparams(("parallel", "parallel")),
        name="norm_route",
    )(x, vecs, w_router)


MOE_TILE = 2048
MOE_SUB = 256
MOE_SLAB = 512


def _moe_kernel(cnt_ref, x_ref, rrow_ref, rcol_ref, gcol_ref, wg_ref, wu_ref, w2_ref, o_ref, xc_scr, y_scr,
                *, n_f, tm):
    i, e, f = pl.program_id(0), pl.program_id(1), pl.program_id(2)
    sub = MOE_SUB
    n_sub = (cnt_ref[i, e] + (sub - 1)) // sub

    @pl.when((e == 0) & (f == 0))
    def _zero():
        o_ref[...] = jnp.zeros_like(o_ref)

    @pl.when(f == 0)
    def _compact():
        def body(s, carry):
            ridx = lax.broadcasted_iota(jnp.int32, (sub, tm), 0) + s * sub
            onehot = jnp.where(rrow_ref[...] == ridx, 1.0, 0.0).astype(BF16)
            xc_scr[pl.ds(pl.multiple_of(s * sub, sub), sub), :] = _dot(onehot, x_ref[...]).astype(BF16)
            return carry
        lax.fori_loop(0, n_sub, body, 0)

    def expert(s, first):
        rows = pl.ds(pl.multiple_of(s * sub, sub), sub)
        xs = xc_scr[rows, :]
        gate = _dot(xs, wg_ref[...])
        up = _dot(xs, wu_ref[...])
        h = (gate * jax.nn.sigmoid(gate) * up).astype(BF16)
        part = _dot(h, w2_ref[...])
        y_scr[rows, :] = part if first else y_scr[rows, :] + part

    @pl.when(f == 0)
    def _first():
        lax.fori_loop(0, n_sub, lambda s, c: (expert(s, True), c)[1], 0)

    @pl.when(f > 0)
    def _rest():
        lax.fori_loop(0, n_sub, lambda s, c: (expert(s, False), c)[1], 0)

    @pl.when(f == n_f - 1)
    def _scatter():
        def body(s, carry):
            y = y_scr[pl.ds(pl.multiple_of(s * sub, sub), sub), :]
            yh, yl = _split2(y)
            for j in range(tm // MOE_SLAB):
                rows = slice(j * MOE_SLAB, (j + 1) * MOE_SLAB)
                cidx = lax.broadcasted_iota(jnp.int32, (MOE_SLAB, sub), 1) + s * sub
                onehot_t = jnp.where(rcol_ref[rows, :] == cidx, 1.0, 0.0).astype(BF16)
                z = _dot(onehot_t, yh) + _dot(onehot_t, yl)
                o_ref[rows, :] += gcol_ref[rows, :] * z
            return carry
        lax.fori_loop(0, n_sub, body, 0)


def _moe(x, logits, w13, w2):
    T, D = x.shape
    E, _, F2 = w13.shape
    F = F2 // 2
    tm = min(MOE_TILE, T)
    nt = T // tm
    tf = 512
    n_f = F // tf
    top_val, top_idx = lax.top_k(logits, TOP_K)
    gates = jax.nn.softmax(top_val, axis=-1)
    onehot = jax.nn.one_hot(top_idx, E, dtype=F32)
    sel = jnp.sum(onehot, axis=1).astype(jnp.int32).reshape(nt, tm, E)
    gate_dense = jnp.sum(onehot * gates[..., None], axis=1).reshape(nt, tm, E)
    rank = jnp.where(sel > 0, jnp.cumsum(sel, axis=1) - sel, -1)
    rank = jnp.swapaxes(rank, 1, 2)
    counts = jnp.sum(sel, axis=1)
    gcol = jnp.swapaxes(gate_dense, 1, 2)[..., None]
    tile = lambda shape, imap: pl.BlockSpec(shape, imap)
    return pl.pallas_call(
        functools.partial(_moe_kernel, n_f=n_f, tm=tm),
        grid_spec=pltpu.PrefetchScalarGridSpec(
            num_scalar_prefetch=1,
            grid=(nt, E, n_f),
            in_specs=[tile((tm, D), lambda i, e, f, c: (i, 0)),
                      tile((None, None, 1, tm), lambda i, e, f, c: (i, e, 0, 0)),
                      tile((None, None, tm, 1), lambda i, e, f, c: (i, e, 0, 0)),
                      tile((None, None, tm, 1), lambda i, e, f, c: (i, e, 0, 0)),
                      tile((None, D, tf), lambda i, e, f, c: (e, 0, f)),
                      tile((None, D, tf), lambda i, e, f, c: (e, 0, f + n_f)),
                      tile((None, tf, D), lambda i, e, f, c: (e, f, 0))],
            out_specs=tile((tm, D), lambda i, e, f, c: (i, 0)),
            scratch_shapes=[pltpu.VMEM((tm, D), BF16), pltpu.VMEM((tm, D), F32)]),
        out_shape=jax.ShapeDtypeStruct((T, D), F32),
        compiler_params=_cparams(("parallel", "arbitrary", "arbitrary")),
        name="moe",
    )(counts, x, rank[:, :, None, :], rank[..., None], gcol, w13, w13, w2)


def _rwkv_scan_kernel(r_ref, lw_ref, k_ref, v_ref, a_ref, b_ref, h0_ref, y_ref, hT_ref, H_scr,
                      *, reverse, nblk, nchunk, npl, modes):
    C = RW_CHUNK
    half = LANES // 2
    m_a, m_inv, m_u, m_st = modes
    i = pl.program_id(2)

    @pl.when(i == 0)
    def _init():
        H_scr[...] = h0_ref[...]

    t_idx = lax.broadcasted_iota(jnp.int32, (C, LANES), 0)
    s_idx = lax.broadcasted_iota(jnp.int32, (C, LANES), 1) & (half - 1)
    tt = lax.broadcasted_iota(jnp.int32, (C, C), 0)
    ss = lax.broadcasted_iota(jnp.int32, (C, C), 1)
    if reverse:
        strict, incl, tri = s_idx > t_idx, s_idx >= t_idx, (ss >= tt).astype(BF16)
    else:
        strict, incl, tri = s_idx < t_idx, s_idx <= t_idx, (ss <= tt).astype(BF16)
    eye_lp = (s_idx == t_idx).astype(F32)
    lane = lax.broadcasted_iota(jnp.int32, (1, LANES), 1)
    m0 = (lane < half).astype(F32)
    m1 = 1.0 - m0
    rr = lax.broadcasted_iota(jnp.int32, (LANES, LANES), 0)
    cc = lax.broadcasted_iota(jnp.int32, (LANES, LANES), 1)
    mask_bd = ((rr < half) == (cc < half)).astype(F32)

    def bd(x):
        return jnp.concatenate([x * m0, x * m1], axis=0)

    order = list(range(nchunk - 1, -1, -1) if reverse else range(nchunk))
    units = [(slice(c * C, (c + 1) * C), slice(q * LANES, (q + 1) * LANES)) for c in order for q in range(npl)]
    ld = lambda ref: [ref[sl, ln] for sl, ln in units]
    r, lw, k, v, a, b = ld(r_ref), ld(lw_ref), ld(k_ref), ld(v_ref), ld(a_ref), ld(b_ref)
    G = [_cumsum_dot(tri, x) for x in lw]
    eG = [jnp.exp(g) for g in G]
    eGn = [jnp.exp(-g) for g in G]
    rt = [x * e for x, e in zip(r, eG)]
    at = [x * jnp.exp(g - l) for x, g, l in zip(a, G, lw)]
    kt = [x * e for x, e in zip(k, eGn)]
    bt = [x * e for x, e in zip(b, eGn)]
    M = [_pdot(jnp.concatenate([x, y], axis=0), jnp.concatenate([bd(z), bd(w)], axis=0), m_a, nt=True)
         for x, y, z, w in zip(at, rt, bt, kt)]
    Nm = [jnp.where(strict, m[:C, :LANES], 0.0) for m in M]
    Aak = [jnp.where(strict, m[:C, LANES:], 0.0) for m in M]
    Arb = [jnp.where(incl, m[C:, :LANES], 0.0) for m in M]
    Ark = [jnp.where(incl, m[C:, LANES:], 0.0) for m in M]
    T = None
    for lvl in range(1, int(math.log2(C)) + 1):
        same = (t_idx >> lvl) == (s_idx >> lvl)
        t_hi = ((t_idx >> (lvl - 1)) & 1) == 1
        s_hi = ((s_idx >> (lvl - 1)) & 1) == 1
        off = same & (s_hi & ~t_hi if reverse else t_hi & ~s_hi)
        Noff = [jnp.where(off, n, 0.0) for n in Nm]
        if T is None:
            T = [eye_lp + n for n in Noff]
        else:
            DN = [_pdot(t, bd(n), m_inv) for t, n in zip(T, Noff)]
            T = [t + _pdot(dn, bd(t), m_inv) for t, dn in zip(T, DN)]
    bdv = [bd(x) for x in v]
    X0 = [_pdot(x, y, m_st) for x, y in zip(Aak, bdv)]
    WU = [_pdot(t, jnp.concatenate([bd(x), bd(y)], axis=1), m_u) for t, x, y in zip(T, at, X0)]
    W = [x[:, :LANES] for x in WU]
    U0 = [x[:, LANES:] for x in WU]
    RY = [_pdot(x, jnp.concatenate([bd(y), bd(z)], axis=1), m_st) for x, y, z in zip(Arb, W, U0)]
    Rh = [x + y[:, :LANES] for x, y in zip(rt, RY)]
    Y0 = [y[:, LANES:] + _pdot(x, z, m_st) for y, x, z in zip(RY, Ark, bdv)]
    dPhi = [mask_bd * _pdot(x.T, y, m_st) for x, y in zip(W, bt)]
    Psi = [mask_bd * _pdot(jnp.concatenate([x, y], axis=0).T, jnp.concatenate([z, w], axis=0), m_st)
           for x, y, z, w in zip(U0, v, bt, kt)]
    e_end = [jnp.exp(g[0:1, :] if reverse else g[C - 1:C, :]) for g in G]

    Hs = [H_scr[q] for q in range(npl)]
    for u, (sl, ln) in enumerate(units):
        q = u % npl
        H = Hs[q]
        y_ref[sl, ln] = _pdot(Rh[u], H, m_st, nt=True) + Y0[u]
        Hs[q] = (H + _pdot(H, dPhi[u], m_st) + Psi[u]) * e_end[u]
    for q in range(npl):
        H_scr[q] = Hs[q]

    @pl.when(i == nblk - 1)
    def _fin():
        for q in range(npl):
            hT_ref[q] = Hs[q]


def _pair_states(S):
    B, H, N, _ = S.shape
    S5 = S.reshape(B, H // 2, 2, N, N)
    eye = jnp.eye(2, dtype=S.dtype)
    return jnp.einsum('bpivk,ij->bpivjk', S5, eye).reshape(B, H // 2, 2 * N, 2 * N)


def _unpair_states(Sp):
    B, P, N2, _ = Sp.shape
    N = N2 // 2
    S6 = Sp.reshape(B, P, 2, N, 2, N)
    return jnp.stack([S6[:, :, 0, :, 0, :], S6[:, :, 1, :, 1, :]], axis=2).reshape(B, 2 * P, N, N)


RW_MODES = ('bf16', 'bf16', 'bf16', 'bf16')
RW_PAIRS_PER_STEP = 2


def _rwkv_scan(r, lw, k, v, a, b, state0, reverse, modes=RW_MODES):
    B, L, D = r.shape
    bt = SCAN_BLOCK
    nblk = L // bt
    npl = RW_PAIRS_PER_STEP
    npair = D // LANES
    blk = (lambda i: nblk - 1 - i) if reverse else (lambda i: i)
    tok = pl.BlockSpec((None, bt, npl * LANES), lambda bb, p, i: (bb, blk(i), p))
    st = pl.BlockSpec((None, npl, LANES, LANES), lambda bb, p, i: (bb, p, 0, 0))
    y, hT = pl.pallas_call(
        functools.partial(_rwkv_scan_kernel, reverse=reverse, nblk=nblk, nchunk=bt // RW_CHUNK, npl=npl,
                          modes=modes),
        grid=(B, npair // npl, nblk),
        in_specs=[tok] * 6 + [st],
        out_specs=[tok, st],
        out_shape=[jax.ShapeDtypeStruct((B, L, D), F32),
                   jax.ShapeDtypeStruct((B, npair, LANES, LANES), F32)],
        scratch_shapes=[pltpu.VMEM((npl, LANES, LANES), F32)],
        compiler_params=_cparams(("parallel", "parallel", "arbitrary")),
        name="rwkv_scan_rev" if reverse else "rwkv_scan_fwd",
    )(r, lw, k, v, a, b, _pair_states(state0))
    return y, _unpair_states(hT)


def _gla_kernel(q_ref, k_ref, v_ref, lf_ref, h0_ref, o_ref, hT_ref, H_scr, *, reverse, nblk, nchunk, prec):
    C = HG_CHUNK
    i = pl.program_id(2)

    @pl.when(i == 0)
    def _init():
        H_scr[...] = h0_ref[...]

    tt = lax.broadcasted_iota(jnp.int32, (C, C), 0)
    ss = lax.broadcasted_iota(jnp.int32, (C, C), 1)
    incl = (ss >= tt) if reverse else (ss <= tt)
    tri = incl.astype(BF16)
    mid = C // 2 if reverse else C // 2 - 1
    end = 0 if reverse else C - 1

    H = H_scr[...]
    order = range(nchunk - 1, -1, -1) if reverse else range(nchunk)
    for c in order:
        sl = slice(c * C, (c + 1) * C)
        q, k, v, lf = (ref[sl, :] for ref in (q_ref, k_ref, v_ref, lf_ref))
        G = _cumsum_dot(tri, lf)
        g_mid = G[mid:mid + 1, :]
        g_end = G[end:end + 1, :]
        scores = jnp.where(incl, _pdot(q * jnp.exp(G - g_mid), k * jnp.exp(g_mid - G), prec, nt=True), 0.0)
        o_ref[sl, :] = _pdot(scores, v, prec) + _pdot(q * jnp.exp(G), H, prec, nt=True)
        H = H * jnp.exp(g_end) + _pdot(v.T, k * jnp.exp(g_end - G), prec)
    H_scr[...] = H

    @pl.when(i == nblk - 1)
    def _fin():
        hT_ref[...] = H


def _gla_scan(q, k, v, lf, state0, reverse, prec='bf16'):
    B, L, D = q.shape
    bt = SCAN_BLOCK
    nblk = L // bt
    nh = D // LANES
    blk = (lambda i: nblk - 1 - i) if reverse else (lambda i: i)
    tok = pl.BlockSpec((None, bt, LANES), lambda bb, p, i: (bb, blk(i), p))
    st = pl.BlockSpec((None, None, LANES, LANES), lambda bb, p, i: (bb, p, 0, 0))
    o, hT = pl.pallas_call(
        functools.partial(_gla_kernel, reverse=reverse, nblk=nblk, nchunk=bt // HG_CHUNK, prec=prec),
        grid=(B, nh, nblk),
        in_specs=[tok] * 4 + [st],
        out_specs=[tok, st],
        out_shape=[jax.ShapeDtypeStruct((B, L, D), F32),
                   jax.ShapeDtypeStruct((B, nh, LANES, LANES), F32)],
        scratch_shapes=[pltpu.VMEM((LANES, LANES), F32)],
        compiler_params=_cparams(("parallel", "parallel", "arbitrary")),
        name="gla_scan_rev" if reverse else "gla_scan_fwd",
    )(q, k, v, lf, jnp.swapaxes(state0, 2, 3))
    return o, jnp.swapaxes(hT, 2, 3)


def _leftmul_kernel(f_ref, x_ref, o_ref, *, nj):
    for j in range(nj):
        o_ref[:, j, :] = _dot(f_ref[...], x_ref[:, j, :], HIGHEST)


def _leftmul_gate_kernel(f_ref, x_ref, u_ref, s_ref, g_ref, o_ref, *, nj):
    for j in range(nj):
        o_ref[:, j, :] = (_dot(f_ref[...], x_ref[:, j, :], HIGHEST) + u_ref[:, j, :] * s_ref[...]) * g_ref[:, j, :]


def _leftmul(f, x, epilogue=None):
    B, K, J, D = x.shape
    M = f.shape[0]
    tj = 8 if J % 8 == 0 else J
    td = 512
    fs = pl.BlockSpec((M, K), lambda b, j, d: (0, 0))
    xs = pl.BlockSpec((None, K, tj, td), lambda b, j, d: (b, 0, j, d))
    os_ = pl.BlockSpec((None, M, tj, td), lambda b, j, d: (b, 0, j, d))
    if epilogue is None:
        kern, specs, args = _leftmul_kernel, [fs, xs], (f, x)
    else:
        u, skip, gate = epilogue
        kern = _leftmul_gate_kernel
        specs = [fs, xs, os_, pl.BlockSpec((1, td), lambda b, j, d: (0, d)), os_]
        args = (f, x, u, skip, gate)
    return pl.pallas_call(
        functools.partial(kern, nj=tj), grid=(B, J // tj, D // td), in_specs=specs, out_specs=os_,
        out_shape=jax.ShapeDtypeStruct((B, M, J, D), F32),
        compiler_params=_cparams(("parallel", "parallel", "parallel")),
        name="dft_outer",
    )(*args)


def _spectral_fwd_kernel(f_ref, y_ref, o_ref):
    o_ref[...] = _dot(f_ref[...], y_ref[...], HIGHEST)


def _spectral_conv_kernel(f_ref, fi_ref, kf_ref, y_ref, o_ref):
    n2 = kf_ref.shape[0] // 2
    z = _dot(f_ref[...], y_ref[...], HIGHEST)
    zr, zi = z[:n2], z[n2:]
    kr, ki = kf_ref[:n2, :], kf_ref[n2:, :]
    p = jnp.concatenate([zr * kr - zi * ki, zr * ki + zi * kr], axis=0)
    o_ref[...] = _dot(fi_ref[...], p, HIGHEST)


def _spectral_mid(y, f_fwd, f_inv=None, kf=None):
    B, K1, R, D = y.shape
    ms = pl.BlockSpec((None, R, R), lambda k1, b: (k1, 0, 0))
    ys = pl.BlockSpec((None, None, R, D), lambda k1, b: (b, k1, 0, 0))
    if kf is None:
        kern, specs, args = _spectral_fwd_kernel, [ms, ys], (f_fwd, y)
    else:
        kern = _spectral_conv_kernel
        specs = [ms, ms, pl.BlockSpec((None, R, D), lambda k1, b: (k1, 0, 0)), ys]
        args = (f_fwd, f_inv, kf, y)
    return pl.pallas_call(
        kern, grid=(K1, B), in_specs=specs, out_specs=ys,
        out_shape=jax.ShapeDtypeStruct((B, K1, R, D), F32),
        compiler_params=_cparams(("parallel", "parallel")),
        name="dft_inner",
    )(*args)


def _dft_tables(L):
    n2 = HY_N2 if L % HY_N2 == 0 and L > HY_N2 * 2 else 1
    N = 2 * L
    n1 = N // n2
    k1h = n1 // 2
    ph = (jnp.arange(k1h, dtype=jnp.int32)[:, None] * 2 + 1) * jnp.arange(k1h, dtype=jnp.int32)[None, :]
    th = (ph % (2 * n1)).astype(F32) * (math.pi / n1)
    f_out = jnp.stack([jnp.cos(th), -jnp.sin(th)], axis=1).reshape(2 * k1h, k1h)
    f_out_inv = (2.0 / N) * f_out.T
    if n2 == 1:
        return n2, f_out, f_out_inv, None, None
    kk = jnp.arange(k1h, dtype=jnp.int32)[:, None, None] + n1 * jnp.arange(n2, dtype=jnp.int32)[None, :, None]
    ph = ((2 * kk + 1) * jnp.arange(n2, dtype=jnp.int32)[None, None, :]) % (2 * N)
    phi = ph.astype(F32) * (math.pi / N)
    c, s = jnp.cos(phi), jnp.sin(phi)
    f_in = jnp.concatenate([jnp.concatenate([c, s], axis=2), jnp.concatenate([-s, c], axis=2)], axis=1)
    ct, st = jnp.swapaxes(c, 1, 2), jnp.swapaxes(s, 1, 2)
    f_in_inv = jnp.concatenate([jnp.concatenate([ct, -st], axis=2), jnp.concatenate([st, ct], axis=2)], axis=1)
    return n2, f_out, f_out_inv, f_in, f_in_inv


def _spectrum(x, tables):
    n2, f_out, _, f_in, _ = tables
    B, L, D = x.shape
    k1h = L // n2
    y = _leftmul(f_out, x.reshape(B, k1h, n2, D)).reshape(B, k1h, 2 * n2, D)
    return y if f_in is None else _spectral_mid(y, f_in)


def _long_conv_gated(u, kf, skip, gate, tables):
    n2, f_out, f_out_inv, f_in, f_in_inv = tables
    B, L, D = u.shape
    k1h = L // n2
    y = _leftmul(f_out, u.reshape(B, k1h, n2, D)).reshape(B, k1h, 2 * n2, D)
    if f_in is None:
        yr, yi = y[:, :, 0], y[:, :, 1]
        kr, ki = kf[:, 0], kf[:, 1]
        q = jnp.stack([yr * kr - yi * ki, yr * ki + yi * kr], axis=2)
    else:
        q = _spectral_mid(y, f_in, f_in_inv, kf)
    out = _leftmul(f_out_inv, q.reshape(B, 2 * k1h, n2, D),
                   epilogue=(u.reshape(B, k1h, n2, D), skip[None, :], gate.reshape(B, k1h, n2, D)))
    return out.reshape(B, L, D)


def _hyena_filter_spectra(L, w1, b1, w2, b2, w3, b3, w4, freq, tables):
    pos = jnp.arange(L, dtype=F32)
    t = (pos / max(L - 1, 1))[:, None]
    ang = (2.0 * math.pi / L) * pos[:, None] * jnp.linspace(1e-4, HY_BANDS - 1, HY_BANDS, dtype=F32)[None, :]
    z = jnp.concatenate([t, jnp.cos(ang), -jnp.sin(ang)], axis=-1)
    z = jnp.pad(z, ((0, 0), (0, 40 - HY_EMB)))
    w1p = jnp.pad(w1, ((0, 40 - HY_EMB), (0, 0)))
    hid = jnp.sin(freq[0] * (_mm(z, w1p, exact=True) + b1))
    hid = jnp.sin(freq[1] * (_mm(hid, w2, exact=True) + b2))
    hid = jnp.sin(freq[2] * (_mm(hid, w3, exact=True) + b3))
    filt = _mm(hid, w4, exact=True)
    deltas = jnp.linspace(math.log(HY_TARGET) / HY_FAST_DECAY, math.log(HY_TARGET) / HY_SLOW_DECAY,
                          D_MODEL, dtype=F32)
    window = jnp.exp(-t * jnp.abs(deltas))
    filt = filt.reshape(L, HY_ORDER * 2, D_MODEL) * window[:, None, :]
    lag0 = (jnp.arange(L) > 0).astype(F32)[:, None, None]
    dirmask = jnp.array([0.0, 1.0] * HY_ORDER, F32)[None, :, None]
    filt = filt * (1.0 - dirmask * (1.0 - lag0))
    spec = _spectrum(jnp.moveaxis(filt, 1, 0), tables)
    n2 = spec.shape[2] // 2
    out = []
    for o in range(HY_ORDER):
        hf, hb = spec[2 * o], spec[2 * o + 1]
        out.append(jnp.concatenate([hf[:, :n2] + hb[:, :n2], hf[:, n2:] - hb[:, n2:]], axis=1))
    return out


def _hyena_stream(h, fprm, w_in, b_in, conv_w, conv_b, skip, w_o, b_o):
    B, L, D = h.shape
    tables = _dft_tables(L)
    kfs = _hyena_filter_spectra(L, *fprm, tables)
    proj = _mm3(h, w_in) + b_in
    prev = jnp.pad(proj[:, :-1], ((0, 0), (1, 0), (0, 0)))
    nxt = jnp.pad(proj[:, 1:], ((0, 0), (0, 1), (0, 0)))
    proj = prev * conv_w[0] + proj * conv_w[1] + nxt * conv_w[2] + conv_b
    x1, x2, v = jnp.split(proj, 3, axis=-1)
    z = _long_conv_gated(v, kfs[0], skip[0], x1, tables)
    y = _long_conv_gated(z, kfs[1], skip[1], x2, tables)
    return _mm3(y, w_o) + b_o


def _rmsnorm(x, g, eps=NORM_EPS):
    return x * lax.rsqrt(jnp.mean(x * x, axis=-1, keepdims=True) + eps) * g


def _shift_seq(h):
    half = h.shape[-1] // 2
    prev = jnp.pad(h[:, :-1, :half], ((0, 0), (1, 0), (0, 0)))
    nxt = jnp.pad(h[:, 1:, half:], ((0, 0), (0, 1), (0, 0)))
    return jnp.concatenate([prev, nxt], axis=-1)


def _shift_grid(h):
    b, l, d = h.shape
    rows = l // GRID_W
    g = h.reshape(b, rows, GRID_W, d)
    q = d // 4
    left = jnp.pad(g[:, :, :-1, :q], ((0, 0), (0, 0), (1, 0), (0, 0)))
    right = jnp.pad(g[:, :, 1:, q:2 * q], ((0, 0), (0, 0), (0, 1), (0, 0)))
    up = jnp.pad(g[:, :-1, :, 2 * q:3 * q], ((0, 0), (1, 0), (0, 0), (0, 0)))
    down = jnp.pad(g[:, 1:, :, 3 * q:], ((0, 0), (0, 1), (0, 0), (0, 0)))
    return jnp.concatenate([left, right, up, down], axis=-1).reshape(b, l, d)


RW_TM = 256


def _headsum(x, ones):
    hi, lo = _split2(x)
    cols = []
    for j in range(x.shape[1] // LANES):
        ln = slice(j * LANES, (j + 1) * LANES)
        cols.append(_dot(hi[:, ln], ones) + _dot(lo[:, ln], ones))
    return jnp.concatenate(cols, axis=1)


def _rwkv_pre_kernel(*refs, vres):
    it = iter(refs)
    h_ref, sh_ref = next(it), next(it)
    vf_ref = next(it) if vres else None
    vec_ref, wr_ref, wk_ref, wv_ref, w1_ref, w2_ref, a1_ref, a2_ref, g1_ref, g2_ref = (next(it) for _ in range(10))
    v1_ref, v2_ref = (next(it), next(it)) if vres else (None, None)
    ones_ref = next(it)
    r_o, v_o, nkk_o, lw0_o, lw1_o, kd0_o, kd1_o, b0_o, b1_o, bonus_o, g_o = it
    vec = lambda i: vec_ref[i:i + 1, :]
    h = h_ref[...]
    xx = sh_ref[...] - h
    xr, xw, xk, xv, xa, xg = ((h + xx * vec(j)).astype(BF16) for j in range(6))
    r = _dot(xr, wr_ref[...])
    k = _dot(xk, wk_ref[...])
    v = _dot(xv, wv_ref[...])
    if vres:
        lora = _dot(_dot(xv, v1_ref[...]).astype(BF16), v2_ref[...])
        v = v + (vf_ref[...] - v) * jax.nn.sigmoid(vec(13) + lora)
    ones = ones_ref[...]
    kk = k * vec(10)
    kk = kk / jnp.maximum(jnp.sqrt(_headsum(kk * kk, ones)), 1e-12)
    r_o[...] = r
    v_o[...] = v
    nkk_o[...] = -kk
    bonus = jnp.zeros_like(r)
    for d, (lw_o, kd_o, b_o) in enumerate(((lw0_o, kd0_o, b0_o), (lw1_o, kd1_o, b1_o))):
        wl = vec(6 + d) + _dot(jnp.tanh(_dot(xw, w1_ref[d])).astype(BF16), w2_ref[d])
        lw_o[...] = -jax.nn.sigmoid(wl) * math.exp(-0.5)
        a = jax.nn.sigmoid(vec(8 + d) + _dot(_dot(xa, a1_ref[d]).astype(BF16), a2_ref[d]))
        kd = k * (1.0 + (a - 1.0) * vec(11))
        kd_o[...] = kd
        b_o[...] = kk * a
        bonus = bonus + _headsum(r * kd * vec(12), ones) * v
    bonus_o[...] = bonus
    g_o[...] = _dot(jax.nn.sigmoid(_dot(xg, g1_ref[...])).astype(BF16), g2_ref[...])


def _rwkv_post_kernel(y0_ref, y1_ref, bonus_ref, g_ref, res_ref, mod_ref, ln_ref, wo_ref, ones_ref, o_ref):
    ones = ones_ref[...]
    y = y0_ref[...] + y1_ref[...]
    yc = y - _headsum(y, ones) * (1.0 / RW_HEAD)
    var = _headsum(yc * yc, ones) * (1.0 / RW_HEAD)
    yn = yc * lax.rsqrt(var + RW_LN_EPS) * ln_ref[0:1, :] + ln_ref[1:2, :] + bonus_ref[...]
    out = _dot((yn * g_ref[...]).astype(BF16), wo_ref[...])
    o_ref[...] = res_ref[...] + mod_ref[...] * out


def _full(a):
    nd = a.ndim
    return pl.BlockSpec(a.shape, lambda b, i: (0,) * nd)


def _rwkv7_stream(h, shifted, state0, v_first, resid, mod_gate, mu, w_rkv, w_o, w0, w1, w2, a0, a1, a2,
                  g1, g2, k_k, k_a, r_k, ln_w, ln_b, vres, need_out=True):
    B, L, D = h.shape
    tm = min(RW_TM, L)
    vres_on = vres is not None
    rows = [mu[j] for j in range(6)] + [w0[0], w0[1], a0[0], a0[1], k_k, k_a, r_k.reshape(D)]
    rows.append(vres[0] if vres_on else jnp.zeros((D,), F32))
    vecs = jnp.stack(rows + [jnp.zeros((D,), F32)] * (16 - len(rows)))
    ones = jnp.kron(jnp.eye(LANES // RW_HEAD, dtype=F32), jnp.ones((RW_HEAD, RW_HEAD), F32)).astype(BF16)
    bf = lambda t: t.astype(BF16)
    tok = pl.BlockSpec((None, tm, D), lambda b, i: (b, i, 0))
    args = [h, shifted] + ([v_first] if vres_on else [])
    consts = [vecs, bf(w_rkv[0]), bf(w_rkv[1]), bf(w_rkv[2]), bf(w1), bf(w2), bf(a1), bf(a2), bf(g1), bf(g2)]
    consts += [bf(vres[1]), bf(vres[2])] if vres_on else []
    consts.append(ones)
    outs = pl.pallas_call(
        functools.partial(_rwkv_pre_kernel, vres=vres_on),
        grid=(B, L // tm),
        in_specs=[tok] * len(args) + [_full(c) for c in consts],
        out_specs=[tok] * 11,
        out_shape=[jax.ShapeDtypeStruct((B, L, D), F32)] * 11,
        compiler_params=_cparams(("parallel", "parallel")),
        name="rwkv_pre",
    )(*args, *consts)
    r, v, nkk, lw0, lw1, kd0, kd1, b0, b1, bonus, g = outs
    y0, s0 = _rwkv_scan(r, lw0, kd0, v, nkk, b0, state0[0], reverse=False)
    y1, s1 = _rwkv_scan(r, lw1, kd1, v, nkk, b1, state0[1], reverse=True)
    states = jnp.stack([s0, s1])
    if not need_out:
        return None, states, v
    ln = jnp.stack([ln_w, ln_b] + [jnp.zeros((D,), F32)] * 6)
    wo = bf(w_o)
    new_resid = pl.pallas_call(
        _rwkv_post_kernel,
        grid=(B, L // tm),
        in_specs=[tok] * 5 + [pl.BlockSpec((None, 1, D), lambda b, i: (b, 0, 0)), _full(ln), _full(wo), _full(ones)],
        out_specs=tok,
        out_shape=jax.ShapeDtypeStruct((B, L, D), F32),
        compiler_params=_cparams(("parallel", "parallel")),
        name="rwkv_post",
    )(y0, y1, bonus, g, resid, jnp.broadcast_to(mod_gate, (B, 1, D)), ln, wo, ones)
    return new_resid, states, v


def _hgrn2_stream(h, state0, lb, w_in, gn, w_o):
    B, L, D = h.shape
    q, f_fwd, f_bwd, i_in, g = jnp.split(_mm3(h, w_in), 5, axis=-1)
    q = jax.nn.silu(q)
    o = 0.0
    states = []
    for d, f_raw in enumerate((f_fwd, f_bwd)):
        fg = lb[d] + (1.0 - lb[d]) * jax.nn.sigmoid(f_raw)
        od, sd = _gla_scan(q, 1.0 - fg, i_in, jnp.log(fg), state0[d], reverse=(d == 1))
        o = o + od
        states.append(sd)
    oh = o.reshape(B, L, HG_H, -1)
    oh = oh * lax.rsqrt(jnp.mean(oh * oh, axis=-1, keepdims=True) + NORM_EPS) * gn
    o = oh.reshape(B, L, D) * jax.nn.silu(g)
    return _mm3(o, w_o), jnp.stack(states)


def kernel(x, c, ctx, c_ctx, norm_g, ada_w, ada_b, final_g,
           rw_mu, rw_wrkv, rw_wo, rw_w0, rw_w1, rw_w2, rw_a0, rw_a1, rw_a2,
           rw_v0, rw_v1, rw_v2, rw_g1, rw_g2, rw_kk, rw_ka, rw_rk, rw_lnw, rw_lnb,
           hy_win, hy_bin, hy_cw, hy_cb, hy_fw1, hy_fb1, hy_fw2, hy_fb2, hy_fw3, hy_fb3,
           hy_fw4, hy_freq, hy_skip, hy_wo, hy_bo,
           hg_win, hg_lb, hg_gn, hg_wo,
           ffn_w13, ffn_w2, moe_router, moe_w13, moe_w2):
    B = x.shape[0]
    depth = norm_g.shape[0]
    D = D_MODEL
    lat, cx = x, ctx
    v_first = None
    lbc = jnp.cumsum(jax.nn.softmax(hg_lb, axis=0), axis=0)
    lower = lbc - lbc[:1]
    cond = jnp.concatenate([jax.nn.silu(c), jax.nn.silu(c_ctx)[None, :]], axis=0)
    ffn_w13b, ffn_w2b = ffn_w13.astype(BF16), ffn_w2.astype(BF16)
    moe_w13b, moe_w2b = moe_w13.astype(BF16), moe_w2.astype(BF16)
    for i in range(depth):
        last = i == depth - 1
        mod = _mm(cond, ada_w[i]) + ada_b[i]
        mod_l = jnp.split(mod[:B, None, :], 6, axis=-1)
        mod_c = jnp.split(mod[B:, None, :], 6, axis=-1)
        hl = _rmsnorm(lat, norm_g[i, 0]) * (1.0 + mod_l[1]) + mod_l[0]
        hc = _rmsnorm(cx, norm_g[i, 0]) * (1.0 + mod_c[1]) + mod_c[0]
        kind, slot = i % N_MIXERS, i // N_MIXERS
        if kind == 0:
            vres = None if slot == 0 else (rw_v0[slot - 1], rw_v1[slot - 1], rw_v2[slot - 1])
            rw = (rw_mu[slot], rw_wrkv[slot], rw_wo[slot], rw_w0[slot], rw_w1[slot], rw_w2[slot],
                  rw_a0[slot], rw_a1[slot], rw_a2[slot], rw_g1[slot], rw_g2[slot], rw_kk[slot],
                  rw_ka[slot], rw_rk[slot], rw_lnw[slot], rw_lnb[slot], vres)
            zero = jnp.zeros((2, B, RW_H, RW_HEAD, RW_HEAD), F32)
            vf_c = None if v_first is None else v_first[0]
            vf_l = None if v_first is None else v_first[1]
            cx_new, s_ctx, v_c = _rwkv7_stream(hc, _shift_seq(hc), zero, vf_c, cx, mod_c[2], *rw,
                                               need_out=not last)
            lat, _, v_l = _rwkv7_stream(hl, _shift_grid(hl), s_ctx, vf_l, lat, mod_l[2], *rw)
            if not last:
                cx = cx_new
            if slot == 0:
                v_first = (v_c, v_l)
        elif kind == 1:
            fprm = (hy_fw1[slot], hy_fb1[slot], hy_fw2[slot], hy_fb2[slot], hy_fw3[slot],
                    hy_fb3[slot], hy_fw4[slot], hy_freq[slot])
            hprm = (hy_win[slot], hy_bin[slot], hy_cw[slot], hy_cb[slot], hy_skip[slot],
                    hy_wo[slot], hy_bo[slot])
            y_l = _hyena_stream(hl, fprm, *hprm)
            y_c = None if last else _hyena_stream(hc, fprm, *hprm)
        else:
            zero = jnp.zeros((2, B, HG_H, HG_DK, D // HG_H), F32)
            gprm = (lower[i], hg_win[slot], hg_gn[slot], hg_wo[slot])
            y_c, s_ctx = _hgrn2_stream(hc, zero, *gprm)
            y_l, _ = _hgrn2_stream(hl, s_ctx, *gprm)
        if kind != 0:
            lat = lat + mod_l[2] * y_l
            if not last:
                cx = cx + mod_c[2] * y_c
        def channel_mix(s, mod):
            vecs = _mod_rows(norm_g[i, 1], mod[3], mod[4], mod[5], B)
            if i % 2 == 0:
                return _ffn(s, vecs, ffn_w13b[i // 2], ffn_w2b[i // 2])
            xn, logits = _norm_route(s, vecs, moe_router[i // 2])
            out = _moe(xn.reshape(-1, D), logits.reshape(-1, N_EXPERTS), moe_w13b[i // 2], moe_w2b[i // 2])
            return s + mod[5] * out.reshape(s.shape)

        lat = channel_mix(lat, mod_l)
        if not last:
            cx = channel_mix(cx, mod_c)
    return _rmsnorm(lat, final_g)
```

```python
import functools
import math

import jax
import jax.numpy as jnp
from jax import lax
from jax.experimental import pallas as pl
from jax.experimental.pallas import tpu as pltpu

F32 = jnp.float32
BF16 = jnp.bfloat16
HIGHEST = lax.Precision.HIGHEST

D_MODEL = 1024
GRID_W = 64
NORM_EPS = 1e-6
RW_HEAD = 64
RW_H = D_MODEL // RW_HEAD
RW_LN_EPS = 64e-5
HY_ORDER = 2
HY_EMB = 33
HY_BANDS = (HY_EMB - 1) // 2
HY_FAST_DECAY = 0.3
HY_SLOW_DECAY = 1.5
HY_TARGET = 1e-2
HG_DK = 128
HG_H = D_MODEL // HG_DK
HG_CHUNK = 32
N_EXPERTS = 8
TOP_K = 2
N_MIXERS = 3

LANES = 128
VMEM_LIMIT = 56 * 1024 * 1024
RW_CHUNK = 64
SCAN_BLOCK = 256
HY_N2 = 128


def _cparams(sem):
    return pltpu.CompilerParams(dimension_semantics=sem, vmem_limit_bytes=VMEM_LIMIT)


def _dot(a, b, prec=None):
    return jnp.dot(a, b, preferred_element_type=F32, precision=prec)


def _dot_nt(a, b, prec=None):
    return lax.dot_general(a, b, (((1,), (1,)), ((), ())), preferred_element_type=F32, precision=prec)


def _split2(x):
    hi = x.astype(BF16)
    lo = (x - hi.astype(F32)).astype(BF16)
    return hi, lo


def _pdot(a, b, mode, nt=False):
    dn = (((1,), (1 if nt else 0,)), ((), ()))
    dg = lambda x, y, p=None: lax.dot_general(x, y, dn, preferred_element_type=F32, precision=p)
    if mode == 'f32':
        return dg(a, b, HIGHEST)
    if mode == 'bf16':
        return dg(a.astype(BF16), b.astype(BF16))
    ah, al = _split2(a)
    bh, bl = _split2(b)
    return dg(ah, bh) + (dg(ah, bl) + dg(al, bh))


def _cumsum_dot(tri, x):
    hi = x.astype(BF16)
    r1 = x - hi.astype(F32)
    mid = r1.astype(BF16)
    lo = (r1 - mid.astype(F32)).astype(BF16)
    n = x.shape[1]
    g = _dot(tri, jnp.concatenate([hi, mid, lo], axis=1))
    return g[:, :n] + (g[:, n:2 * n] + g[:, 2 * n:])


def _mm_kernel(a_ref, b_ref, o_ref, *, prec):
    o_ref[...] = _dot(a_ref[...], b_ref[...], prec)


def _mm(a, b, *, exact=False):
    M, K = a.shape
    N = b.shape[1]
    dt = F32 if exact else BF16
    a = a.astype(dt)
    b = b.astype(dt)
    Mp = -(-M // 8) * 8
    if Mp != M:
        a = jnp.pad(a, ((0, Mp - M), (0, 0)))
    tm = 512 if Mp % 512 == 0 else (256 if Mp % 256 == 0 else Mp)
    tn = 512 if N % 512 == 0 else N
    out = pl.pallas_call(
        functools.partial(_mm_kernel, prec=HIGHEST if exact else None),
        grid=(Mp // tm, N // tn),
        in_specs=[pl.BlockSpec((tm, K), lambda i, j: (i, 0)),
                  pl.BlockSpec((K, tn), lambda i, j: (0, j))],
        out_specs=pl.BlockSpec((tm, tn), lambda i, j: (i, j)),
        out_shape=jax.ShapeDtypeStruct((Mp, N), F32),
        compiler_params=_cparams(("parallel", "parallel")),
        name="mm",
    )(a, b)
    return out[:M] if Mp != M else out


def _mm3(x, w, **kw):
    B, L, K = x.shape
    return _mm(x.reshape(B * L, K), w, **kw).reshape(B, L, w.shape[1])


def _norm_mod(x, vec_ref):
    xn = x * lax.rsqrt(jnp.mean(x * x, axis=-1, keepdims=True) + NORM_EPS)
    return xn * vec_ref[0:1, :] + vec_ref[1:2, :]


def _mod_rows(g, shift, scale, gate, B):
    D = g.shape[-1]
    rows = [jnp.broadcast_to(g * (1.0 + scale[:, 0]), (B, D)), jnp.broadcast_to(shift[:, 0], (B, D)),
            jnp.broadcast_to(gate[:, 0], (B, D))]
    return jnp.stack(rows + [jnp.zeros((B, D), F32)] * 5, axis=1)


def _ffn_kernel(x_ref, vec_ref, wg_ref, wu_ref, w2_ref, o_ref, xn_scr, acc_ref, *, n_f):
    f = pl.program_id(2)

    @pl.when(f == 0)
    def _first():
        xn_scr[...] = _norm_mod(x_ref[...], vec_ref).astype(BF16)
        acc_ref[...] = jnp.zeros_like(acc_ref)

    x = xn_scr[...]
    gate = _dot(x, wg_ref[...])
    up = _dot(x, wu_ref[...])
    h = (gate * jax.nn.sigmoid(gate) * up).astype(BF16)
    acc_ref[...] += _dot(h, w2_ref[...])

    @pl.when(f == n_f - 1)
    def _store():
        o_ref[...] = x_ref[...] + vec_ref[2:3, :] * acc_ref[...]


def _ffn(x, vecs, w13, w2):
    B, L, D = x.shape
    F = w13.shape[1] // 2
    tm = min(512, L)
    tf = 512 if F % 512 == 0 else 256
    n_f = F // tf
    tok = pl.BlockSpec((None, tm, D), lambda b, i, f: (b, i, 0))
    return pl.pallas_call(
        functools.partial(_ffn_kernel, n_f=n_f),
        grid=(B, L // tm, n_f),
        in_specs=[tok,
                  pl.BlockSpec((None, 8, D), lambda b, i, f: (b, 0, 0)),
                  pl.BlockSpec((D, tf), lambda b, i, f: (0, f)),
                  pl.BlockSpec((D, tf), lambda b, i, f: (0, f + n_f)),
                  pl.BlockSpec((tf, D), lambda b, i, f: (f, 0))],
        out_specs=tok,
        out_shape=jax.ShapeDtypeStruct((B, L, D), F32),
        scratch_shapes=[pltpu.VMEM((tm, D), BF16), pltpu.VMEM((tm, D), F32)],
        compiler_params=_cparams(("parallel", "parallel", "arbitrary")),
        name="ffn",
    )(x, vecs, w13, w13, w2)


def _norm_route_kernel(x_ref, vec_ref, wr_ref, xn_ref, lg_ref):
    xn = _norm_mod(x_ref[...], vec_ref)
    xn_ref[...] = xn.astype(BF16)
    lg_ref[...] = _dot(xn, wr_ref[...], HIGHEST)


def _norm_route(x, vecs, w_router):
    B, L, D = x.shape
    E = w_router.shape[1]
    tm = min(512, L)
    tok = pl.BlockSpec((None, tm, D), lambda b, i: (b, i, 0))
    return pl.pallas_call(
        _norm_route_kernel,
        grid=(B, L // tm),
        in_specs=[tok, pl.BlockSpec((None, 8, D), lambda b, i: (b, 0, 0)), pl.BlockSpec((D, E), lambda b, i: (0, 0))],
        out_specs=[tok, pl.BlockSpec((None, tm, E), lambda b, i: (b, i, 0))],
        out_shape=[jax.ShapeDtypeStruct((B, L, D), BF16), jax.ShapeDtypeStruct((B, L, E), F32)],
        compiler_params=_cparams(("parallel", "parallel")),
        name="norm_route",
    )(x, vecs, w_router)


MOE_TILE = 2048
MOE_SUB = 256
MOE_SLAB = 512


def _moe_kernel(cnt_ref, x_ref, rrow_ref, rcol_ref, gcol_ref, wg_ref, wu_ref, w2_ref, o_ref, xc_scr, y_scr,
                *, n_f, tm):
    i, e, f = pl.program_id(0), pl.program_id(1), pl.program_id(2)
    sub = MOE_SUB
    n_sub = (cnt_ref[i, e] + (sub - 1)) // sub

    @pl.when((e == 0) & (f == 0))
    def _zero():
        o_ref[...] = jnp.zeros_like(o_ref)

    @pl.when(f == 0)
    def _compact():
        def body(s, carry):
            ridx = lax.broadcasted_iota(jnp.int32, (sub, tm), 0) + s * sub
            onehot = jnp.where(rrow_ref[...] == ridx, 1.0, 0.0).astype(BF16)
            xc_scr[pl.ds(pl.multiple_of(s * sub, sub), sub), :] = _dot(onehot, x_ref[...]).astype(BF16)
            return carry
        lax.fori_loop(0, n_sub, body, 0)

    def expert(s, first):
        rows = pl.ds(pl.multiple_of(s * sub, sub), sub)
        xs = xc_scr[rows, :]
        gate = _dot(xs, wg_ref[...])
        up = _dot(xs, wu_ref[...])
        h = (gate * jax.nn.sigmoid(gate) * up).astype(BF16)
        part = _dot(h, w2_ref[...])
        y_scr[rows, :] = part if first else y_scr[rows, :] + part

    @pl.when(f == 0)
    def _first():
        lax.fori_loop(0, n_sub, lambda s, c: (expert(s, True), c)[1], 0)

    @pl.when(f > 0)
    def _rest():
        lax.fori_loop(0, n_sub, lambda s, c: (expert(s, False), c)[1], 0)

    @pl.when(f == n_f - 1)
    def _scatter():
        def body(s, carry):
            y = y_scr[pl.ds(pl.multiple_of(s * sub, sub), sub), :]
            yh, yl = _split2(y)
            for j in range(tm // MOE_SLAB):
                rows = slice(j * MOE_SLAB, (j + 1) * MOE_SLAB)
                cidx = lax.broadcasted_iota(jnp.int32, (MOE_SLAB, sub), 1) + s * sub
                onehot_t = jnp.where(rcol_ref[rows, :] == cidx, 1.0, 0.0).astype(BF16)
                z = _dot(onehot_t, yh) + _dot(onehot_t, yl)
                o_ref[rows, :] += gcol_ref[rows, :] * z
            return carry
        lax.fori_loop(0, n_sub, body, 0)


def _moe(x, logits, w13, w2):
    T, D = x.shape
    E, _, F2 = w13.shape
    F = F2 // 2
    tm = min(MOE_TILE, T)
    nt = T // tm
    tf = 512
    n_f = F // tf
    top_val, top_idx = lax.top_k(logits, TOP_K)
    gates = jax.nn.softmax(top_val, axis=-1)
    onehot = jax.nn.one_hot(top_idx, E, dtype=F32)
    sel = jnp.sum(onehot, axis=1).astype(jnp.int32).reshape(nt, tm, E)
    gate_dense = jnp.sum(onehot * gates[..., None], axis=1).reshape(nt, tm, E)
    rank = jnp.where(sel > 0, jnp.cumsum(sel, axis=1) - sel, -1)
    rank = jnp.swapaxes(rank, 1, 2)
    counts = jnp.sum(sel, axis=1)
    gcol = jnp.swapaxes(gate_dense, 1, 2)[..., None]
    tile = lambda shape, imap: pl.BlockSpec(shape, imap)
    return pl.pallas_call(
        functools.partial(_moe_kernel, n_f=n_f, tm=tm),
        grid_spec=pltpu.PrefetchScalarGridSpec(
            num_scalar_prefetch=1,
            grid=(nt, E, n_f),
            in_specs=[tile((tm, D), lambda i, e, f, c: (i, 0)),
                      tile((None, None, 1, tm), lambda i, e, f, c: (i, e, 0, 0)),
                      tile((None, None, tm, 1), lambda i, e, f, c: (i, e, 0, 0)),
                      tile((None, None, tm, 1), lambda i, e, f, c: (i, e, 0, 0)),
                      tile((None, D, tf), lambda i, e, f, c: (e, 0, f)),
                      tile((None, D, tf), lambda i, e, f, c: (e, 0, f + n_f)),
                      tile((None, tf, D), lambda i, e, f, c: (e, f, 0))],
            out_specs=tile((tm, D), lambda i, e, f, c: (i, 0)),
            scratch_shapes=[pltpu.VMEM((tm, D), BF16), pltpu.VMEM((tm, D), F32)]),
        out_shape=jax.ShapeDtypeStruct((T, D), F32),
        compiler_params=_cparams(("parallel", "arbitrary", "arbitrary")),
        name="moe",
    )(counts, x, rank[:, :, None, :], rank[..., None], gcol, w13, w13, w2)


def _rwkv_scan_kernel(r_ref, lw_ref, k_ref, v_ref, a_ref, b_ref, h0_ref, y_ref, hT_ref, H_scr,
                      *, reverse, nblk, nchunk, npl, modes):
    C = RW_CHUNK
    half = LANES // 2
    m_a, m_inv, m_u, m_st = modes
    i = pl.program_id(2)

    @pl.when(i == 0)
    def _init():
        H_scr[...] = h0_ref[...]

    t_idx = lax.broadcasted_iota(jnp.int32, (C, LANES), 0)
    s_idx = lax.broadcasted_iota(jnp.int32, (C, LANES), 1) & (half - 1)
    tt = lax.broadcasted_iota(jnp.int32, (C, C), 0)
    ss = lax.broadcasted_iota(jnp.int32, (C, C), 1)
    if reverse:
        strict, incl, tri = s_idx > t_idx, s_idx >= t_idx, (ss >= tt).astype(BF16)
    else:
        strict, incl, tri = s_idx < t_idx, s_idx <= t_idx, (ss <= tt).astype(BF16)
    eye_lp = (s_idx == t_idx).astype(F32)
    lane = lax.broadcasted_iota(jnp.int32, (1, LANES), 1)
    m0 = (lane < half).astype(F32)
    m1 = 1.0 - m0
    rr = lax.broadcasted_iota(jnp.int32, (LANES, LANES), 0)
    cc = lax.broadcasted_iota(jnp.int32, (LANES, LANES), 1)
    mask_bd = ((rr < half) == (cc < half)).astype(F32)

    def bd(x):
        return jnp.concatenate([x * m0, x * m1], axis=0)

    order = list(range(nchunk - 1, -1, -1) if reverse else range(nchunk))
    units = [(slice(c * C, (c + 1) * C), slice(q * LANES, (q + 1) * LANES)) for c in order for q in range(npl)]
    ld = lambda ref: [ref[sl, ln] for sl, ln in units]
    r, lw, k, v, a, b = ld(r_ref), ld(lw_ref), ld(k_ref), ld(v_ref), ld(a_ref), ld(b_ref)
    G = [_cumsum_dot(tri, x) for x in lw]
    eG = [jnp.exp(g) for g in G]
    eGn = [jnp.exp(-g) for g in G]
    rt = [x * e for x, e in zip(r, eG)]
    at = [x * jnp.exp(g - l) for x, g, l in zip(a, G, lw)]
    kt = [x * e for x, e in zip(k, eGn)]
    bt = [x * e for x, e in zip(b, eGn)]
    M = [_pdot(jnp.concatenate([x, y], axis=0), jnp.concatenate([bd(z), bd(w)], axis=0), m_a, nt=True)
         for x, y, z, w in zip(at, rt, bt, kt)]
    Nm = [jnp.where(strict, m[:C, :LANES], 0.0) for m in M]
    Aak = [jnp.where(strict, m[:C, LANES:], 0.0) for m in M]
    Arb = [jnp.where(incl, m[C:, :LANES], 0.0) for m in M]
    Ark = [jnp.where(incl, m[C:, LANES:], 0.0) for m in M]
    T = None
    for lvl in range(1, int(math.log2(C)) + 1):
        same = (t_idx >> lvl) == (s_idx >> lvl)
        t_hi = ((t_idx >> (lvl - 1)) & 1) == 1
        s_hi = ((s_idx >> (lvl - 1)) & 1) == 1
        off = same & (s_hi & ~t_hi if reverse else t_hi & ~s_hi)
        Noff = [jnp.where(off, n, 0.0) for n in Nm]
        if T is None:
            T = [eye_lp + n for n in Noff]
        else:
            DN = [_pdot(t, bd(n), m_inv) for t, n in zip(T, Noff)]
            T = [t + _pdot(dn, bd(t), m_inv) for t, dn in zip(T, DN)]
    bdv = [bd(x) for x in v]
    X0 = [_pdot(x, y, m_st) for x, y in zip(Aak, bdv)]
    WU = [_pdot(t, jnp.concatenate([bd(x), bd(y)], axis=1), m_u) for t, x, y in zip(T, at, X0)]
    W = [x[:, :LANES] for x in WU]
    U0 = [x[:, LANES:] for x in WU]
    RY = [_pdot(x, jnp.concatenate([bd(y), bd(z)], axis=1), m_st) for x, y, z in zip(Arb, W, U0)]
    Rh = [x + y[:, :LANES] for x, y in zip(rt, RY)]
    Y0 = [y[:, LANES:] + _pdot(x, z, m_st) for y, x, z in zip(RY, Ark, bdv)]
    dPhi = [mask_bd * _pdot(x.T, y, m_st) for x, y in zip(W, bt)]
    Psi = [mask_bd * _pdot(jnp.concatenate([x, y], axis=0).T, jnp.concatenate([z, w], axis=0), m_st)
           for x, y, z, w in zip(U0, v, bt, kt)]
    e_end = [jnp.exp(g[0:1, :] if reverse else g[C - 1:C, :]) for g in G]

    Hs = [H_scr[q] for q in range(npl)]
    for u, (sl, ln) in enumerate(units):
        q = u % npl
        H = Hs[q]
        y_ref[sl, ln] = _pdot(Rh[u], H, m_st, nt=True) + Y0[u]
        Hs[q] = (H + _pdot(H, dPhi[u], m_st) + Psi[u]) * e_end[u]
    for q in range(npl):
        H_scr[q] = Hs[q]

    @pl.when(i == nblk - 1)
    def _fin():
        for q in range(npl):
            hT_ref[q] = Hs[q]


def _pair_states(S):
    B, H, N, _ = S.shape
    S5 = S.reshape(B, H // 2, 2, N, N)
    eye = jnp.eye(2, dtype=S.dtype)
    return jnp.einsum('bpivk,ij->bpivjk', S5, eye).reshape(B, H // 2, 2 * N, 2 * N)


def _unpair_states(Sp):
    B, P, N2, _ = Sp.shape
    N = N2 // 2
    S6 = Sp.reshape(B, P, 2, N, 2, N)
    return jnp.stack([S6[:, :, 0, :, 0, :], S6[:, :, 1, :, 1, :]], axis=2).reshape(B, 2 * P, N, N)


RW_MODES = ('bf16', 'bf16', 'bf16', 'bf16')
RW_PAIRS_PER_STEP = 2


def _rwkv_scan(r, lw, k, v, a, b, state0, reverse, modes=RW_MODES):
    B, L, D = r.shape
    bt = SCAN_BLOCK
    nblk = L // bt
    npl = RW_PAIRS_PER_STEP
    npair = D // LANES
    blk = (lambda i: nblk - 1 - i) if reverse else (lambda i: i)
    tok = pl.BlockSpec((None, bt, npl * LANES), lambda bb, p, i: (bb, blk(i), p))
    st = pl.BlockSpec((None, npl, LANES, LANES), lambda bb, p, i: (bb, p, 0, 0))
    y, hT = pl.pallas_call(
        functools.partial(_rwkv_scan_kernel, reverse=reverse, nblk=nblk, nchunk=bt // RW_CHUNK, npl=npl,
                          modes=modes),
        grid=(B, npair // npl, nblk),
        in_specs=[tok] * 6 + [st],
        out_specs=[tok, st],
        out_shape=[jax.ShapeDtypeStruct((B, L, D), F32),
                   jax.ShapeDtypeStruct((B, npair, LANES, LANES), F32)],
        scratch_shapes=[pltpu.VMEM((npl, LANES, LANES), F32)],
        compiler_params=_cparams(("parallel", "parallel", "arbitrary")),
        name="rwkv_scan_rev" if reverse else "rwkv_scan_fwd",
    )(r, lw, k, v, a, b, _pair_states(state0))
    return y, _unpair_states(hT)


GLA_HEADS_PER_STEP = 2


def _gla_kernel(q_ref, f_ref, v_ref, lb_ref, h0_ref, o_ref, hT_ref, H_scr, *, reverse, nblk, npl):
    C = HG_CHUNK
    sh = int(math.log2(C))
    bt = q_ref.shape[0]
    nchunk = bt // C
    i = pl.program_id(2)

    @pl.when(i == 0)
    def _init():
        H_scr[...] = h0_ref[...]

    tt = lax.broadcasted_iota(jnp.int32, (bt, bt), 0)
    ss = lax.broadcasted_iota(jnp.int32, (bt, bt), 1)
    same = (tt >> sh) == (ss >> sh)
    mid = ((tt >> sh) << sh) + (C // 2 if reverse else C // 2 - 1)
    if reverse:
        incl, upto_mid = same & (ss >= tt), same & (ss >= mid)
    else:
        incl, upto_mid = same & (ss <= tt), same & (ss <= mid)
    one = lambda m: jnp.where(m, 1.0, 0.0)
    sums = jnp.concatenate([one(incl), one(incl) - one(upto_mid), one(same) - one(incl)], axis=0).astype(BF16)

    order = list(range(nchunk - 1, -1, -1) if reverse else range(nchunk))
    heads = [slice(p * LANES, (p + 1) * LANES) for p in range(npl)]
    q = [jax.nn.silu(q_ref[:, ln]) for ln in heads]
    fg = [lb_ref[:, ln] + (1.0 - lb_ref[:, ln]) * jax.nn.sigmoid(f_ref[:, ln]) for ln in heads]
    v = [v_ref[:, ln] for ln in heads]
    k = [1.0 - f for f in fg]
    Gs = [_cumsum_dot(sums, jnp.log(f)) for f in fg]
    Gabs = [g[:bt] for g in Gs]
    Grel = [g[bt:2 * bt] for g in Gs]
    Gend = [g[2 * bt:] for g in Gs]
    scores = [jnp.where(incl, _pdot(x * jnp.exp(g), y * jnp.exp(-g), 'bf16', nt=True), 0.0)
              for x, y, g in zip(q, k, Grel)]
    o_intra = [_pdot(s, x, 'bf16') for s, x in zip(scores, v)]
    qa = [x * jnp.exp(g) for x, g in zip(q, Gabs)]
    kend = [x * jnp.exp(g) for x, g in zip(k, Gend)]
    rows = [slice(c * C, (c + 1) * C) for c in range(nchunk)]
    KV = [[_pdot(x[r].T, y[r], 'bf16') for r in rows] for x, y in zip(v, kend)]
    dec = [[jnp.exp(ga[r][0:1, :] + ge[r][0:1, :]) for r in rows] for ga, ge in zip(Gabs, Gend)]

    Hs = [H_scr[p] for p in range(npl)]
    for c in order:
        for p in range(npl):
            o_ref[rows[c], heads[p]] = o_intra[p][rows[c]] + _pdot(qa[p][rows[c]], Hs[p], 'bf16', nt=True)
            Hs[p] = Hs[p] * dec[p][c] + KV[p][c]
    for p in range(npl):
        H_scr[p] = Hs[p]

    @pl.when(i == nblk - 1)
    def _fin():
        for p in range(npl):
            hT_ref[p] = Hs[p]


def _gla_scan(proj, lb, state0, d):
    B, L, D5 = proj.shape
    D = D5 // 5
    reverse = d == 1
    bt = SCAN_BLOCK
    nblk = L // bt
    npl = GLA_HEADS_PER_STEP
    nh = D // LANES
    ncb = nh // npl
    blk = (lambda i: nblk - 1 - i) if reverse else (lambda i: i)
    col = lambda off: pl.BlockSpec((None, bt, npl * LANES), lambda bb, p, i: (bb, blk(i), off * ncb + p))
    st = pl.BlockSpec((None, npl, LANES, LANES), lambda bb, p, i: (bb, p, 0, 0))
    o, hT = pl.pallas_call(
        functools.partial(_gla_kernel, reverse=reverse, nblk=nblk, npl=npl),
        grid=(B, ncb, nblk),
        in_specs=[col(0), col(1 + d), col(3), pl.BlockSpec((1, npl * LANES), lambda bb, p, i: (0, p)), st],
        out_specs=[col(0), st],
        out_shape=[jax.ShapeDtypeStruct((B, L, D), F32),
                   jax.ShapeDtypeStruct((B, nh, LANES, LANES), F32)],
        scratch_shapes=[pltpu.VMEM((npl, LANES, LANES), F32)],
        compiler_params=_cparams(("parallel", "parallel", "arbitrary")),
        name="gla_scan_rev" if reverse else "gla_scan_fwd",
    )(proj, proj, proj, lb[d][None, :], jnp.swapaxes(state0, 2, 3))
    return o, jnp.swapaxes(hT, 2, 3)


def _x3dot(fh, fl, x):
    xh, xl = _split2(x)
    return _dot(fh, xh) + (_dot(fh, xl) + _dot(fl, xh))


def _leftmul_kernel(f_ref, x_ref, o_ref, *, nj):
    for j in range(nj):
        o_ref[:, j, :] = _dot(f_ref[...], x_ref[:, j, :], HIGHEST)


def _leftmul_gate_kernel(f_ref, x_ref, u_ref, s_ref, g_ref, o_ref, *, nj):
    for j in range(nj):
        y = _dot(f_ref[...], x_ref[:, j, :], HIGHEST)
        o_ref[:, j, :] = (y + u_ref[:, j, :] * s_ref[...]) * g_ref[:, j, :]


DFT_TD = 512


def _leftmul(f, x, xoff=0, epilogue=None, D=D_MODEL):
    B, K, J, _ = x.shape
    M = f.shape[0]
    tj = 8 if J % 8 == 0 else J
    td = DFT_TD
    fs = pl.BlockSpec((M, K), lambda b, j, d: (0, 0))
    col = lambda rows, off: pl.BlockSpec((None, rows, tj, td), lambda b, j, d: (b, 0, j, off + d))
    if epilogue is None:
        kern, specs, args = _leftmul_kernel, [fs, col(K, xoff)], (f, x)
    else:
        u, uoff, skip, gate, goff = epilogue
        kern = _leftmul_gate_kernel
        specs = [fs, col(K, xoff), col(M, uoff), pl.BlockSpec((1, td), lambda b, j, d: (0, d)), col(M, goff)]
        args = (f, x, u, skip, gate)
    return pl.pallas_call(
        functools.partial(kern, nj=tj), grid=(B, J // tj, D // td), in_specs=specs, out_specs=col(M, 0),
        out_shape=jax.ShapeDtypeStruct((B, M, J, D), F32),
        compiler_params=_cparams(("parallel", "parallel", "parallel")),
        name="dft_outer",
    )(*args)


def _spectral_fwd_kernel(fh_ref, fl_ref, y_ref, o_ref):
    o_ref[...] = _x3dot(fh_ref[...], fl_ref[...], y_ref[...])


def _spectral_conv_kernel(fh_ref, fl_ref, fih_ref, fil_ref, kf_ref, y_ref, o_ref):
    n2 = kf_ref.shape[0] // 2
    z = _x3dot(fh_ref[...], fl_ref[...], y_ref[...])
    zr, zi = z[:n2], z[n2:]
    kr, ki = kf_ref[:n2, :], kf_ref[n2:, :]
    p = jnp.concatenate([zr * kr - zi * ki, zr * ki + zi * kr], axis=0)
    o_ref[...] = _x3dot(fih_ref[...], fil_ref[...], p)


def _spectral_mid(y, f_fwd, f_inv=None, kf=None):
    B, K1, R, D = y.shape
    ms = pl.BlockSpec((None, R, R), lambda k1, b: (k1, 0, 0))
    ys = pl.BlockSpec((None, None, R, D), lambda k1, b: (b, k1, 0, 0))
    if kf is None:
        kern, specs, args = _spectral_fwd_kernel, [ms, ms, ys], (*_split2(f_fwd), y)
    else:
        kern = _spectral_conv_kernel
        specs = [ms, ms, ms, ms, pl.BlockSpec((None, R, D), lambda k1, b: (k1, 0, 0)), ys]
        args = (*_split2(f_fwd), *_split2(f_inv), kf, y)
    return pl.pallas_call(
        kern, grid=(K1, B), in_specs=specs, out_specs=ys,
        out_shape=jax.ShapeDtypeStruct((B, K1, R, D), F32),
        compiler_params=_cparams(("parallel", "parallel")),
        name="dft_inner",
    )(*args)


def _dft_tables(L):
    n2 = HY_N2 if L % HY_N2 == 0 and L > HY_N2 * 2 else 1
    N = 2 * L
    n1 = N // n2
    k1h = n1 // 2
    ph = (jnp.arange(k1h, dtype=jnp.int32)[:, None] * 2 + 1) * jnp.arange(k1h, dtype=jnp.int32)[None, :]
    th = (ph % (2 * n1)).astype(F32) * (math.pi / n1)
    f_out = jnp.stack([jnp.cos(th), -jnp.sin(th)], axis=1).reshape(2 * k1h, k1h)
    f_out_inv = (2.0 / N) * f_out.T
    if n2 == 1:
        return n2, f_out, f_out_inv, None, None
    kk = jnp.arange(k1h, dtype=jnp.int32)[:, None, None] + n1 * jnp.arange(n2, dtype=jnp.int32)[None, :, None]
    ph = ((2 * kk + 1) * jnp.arange(n2, dtype=jnp.int32)[None, None, :]) % (2 * N)
    phi = ph.astype(F32) * (math.pi / N)
    c, s = jnp.cos(phi), jnp.sin(phi)
    f_in = jnp.concatenate([jnp.concatenate([c, s], axis=2), jnp.concatenate([-s, c], axis=2)], axis=1)
    ct, st = jnp.swapaxes(c, 1, 2), jnp.swapaxes(s, 1, 2)
    f_in_inv = jnp.concatenate([jnp.concatenate([ct, -st], axis=2), jnp.concatenate([st, ct], axis=2)], axis=1)
    return n2, f_out, f_out_inv, f_in, f_in_inv


def _spectrum(x, tables):
    n2, f_out, _, f_in, _ = tables
    B, L, D = x.shape
    k1h = L // n2
    y = _leftmul(f_out, x.reshape(B, k1h, n2, D)).reshape(B, k1h, 2 * n2, D)
    return y if f_in is None else _spectral_mid(y, f_in)


def _long_conv_gated(u, ucol, kf, skip, gate, gcol, tables):
    n2, f_out, f_out_inv, f_in, f_in_inv = tables
    B, L, _ = u.shape
    D = D_MODEL
    k1h = L // n2
    per = D // DFT_TD
    u4 = u.reshape(B, k1h, n2, u.shape[-1])
    y = _leftmul(f_out, u4, xoff=ucol * per).reshape(B, k1h, 2 * n2, D)
    if f_in is None:
        yr, yi = y[:, :, 0], y[:, :, 1]
        kr, ki = kf[:, 0], kf[:, 1]
        q = jnp.stack([yr * kr - yi * ki, yr * ki + yi * kr], axis=2)
    else:
        q = _spectral_mid(y, f_in, f_in_inv, kf)
    out = _leftmul(f_out_inv, q.reshape(B, 2 * k1h, n2, D),
                   epilogue=(u4, ucol * per, skip[None, :], gate.reshape(B, k1h, n2, gate.shape[-1]), gcol * per))
    return out.reshape(B, L, D)


def _conv3_kernel(cur_ref, prev_ref, next_ref, w_ref, o_ref, *, nblk):
    i = pl.program_id(1)
    x = cur_ref[...]
    tm = x.shape[0]
    row = lax.broadcasted_iota(jnp.int32, x.shape, 0)
    before = jnp.where(i > 0, prev_ref[7:8, :], 0.0)
    after = jnp.where(i < nblk - 1, next_ref[0:1, :], 0.0)
    xp = jnp.where(row == 0, before, pltpu.roll(x, 1, axis=0))
    xn = jnp.where(row == tm - 1, after, pltpu.roll(x, tm - 1, axis=0))
    o_ref[...] = xp * w_ref[0:1, :] + x * w_ref[1:2, :] + xn * w_ref[2:3, :] + w_ref[3:4, :]


def _conv3(x, taps, bias):
    B, L, C = x.shape
    tm = min(512, L)
    tc = 512
    nblk = L // tm
    r8 = tm // 8
    w = jnp.concatenate([taps, bias[None, :], jnp.zeros((4, C), F32)], axis=0)
    return pl.pallas_call(
        functools.partial(_conv3_kernel, nblk=nblk),
        grid=(B, nblk, C // tc),
        in_specs=[pl.BlockSpec((None, tm, tc), lambda b, i, j: (b, i, j)),
                  pl.BlockSpec((None, 8, tc), lambda b, i, j: (b, jnp.maximum(i * r8 - 1, 0), j)),
                  pl.BlockSpec((None, 8, tc), lambda b, i, j: (b, jnp.minimum((i + 1) * r8, L // 8 - 1), j)),
                  pl.BlockSpec((8, tc), lambda b, i, j: (0, j))],
        out_specs=pl.BlockSpec((None, tm, tc), lambda b, i, j: (b, i, j)),
        out_shape=jax.ShapeDtypeStruct((B, L, C), F32),
        compiler_params=_cparams(("parallel", "parallel", "parallel")),
        name="conv3",
    )(x, x, x, w)


def _hyena_filter_spectra(L, w1, b1, w2, b2, w3, b3, w4, freq, tables):
    pos = jnp.arange(L, dtype=F32)
    t = (pos / max(L - 1, 1))[:, None]
    ang = (2.0 * math.pi / L) * pos[:, None] * jnp.linspace(1e-4, HY_BANDS - 1, HY_BANDS, dtype=F32)[None, :]
    z = jnp.concatenate([t, jnp.cos(ang), -jnp.sin(ang)], axis=-1)
    z = jnp.pad(z, ((0, 0), (0, 40 - HY_EMB)))
    w1p = jnp.pad(w1, ((0, 40 - HY_EMB), (0, 0)))
    hid = jnp.sin(freq[0] * (_mm(z, w1p, exact=True) + b1))
    hid = jnp.sin(freq[1] * (_mm(hid, w2, exact=True) + b2))
    hid = jnp.sin(freq[2] * (_mm(hid, w3, exact=True) + b3))
    filt = _mm(hid, w4, exact=True)
    deltas = jnp.linspace(math.log(HY_TARGET) / HY_FAST_DECAY, math.log(HY_TARGET) / HY_SLOW_DECAY,
                          D_MODEL, dtype=F32)
    window = jnp.exp(-t * jnp.abs(deltas))
    filt = filt.reshape(L, HY_ORDER * 2, D_MODEL) * window[:, None, :]
    lag0 = (jnp.arange(L) > 0).astype(F32)[:, None, None]
    dirmask = jnp.array([0.0, 1.0] * HY_ORDER, F32)[None, :, None]
    filt = filt * (1.0 - dirmask * (1.0 - lag0))
    spec = _spectrum(jnp.moveaxis(filt, 1, 0), tables)
    n2 = spec.shape[2] // 2
    out = []
    for o in range(HY_ORDER):
        hf, hb = spec[2 * o], spec[2 * o + 1]
        out.append(jnp.concatenate([hf[:, :n2] + hb[:, :n2], hf[:, n2:] - hb[:, n2:]], axis=1))
    return out


def _hyena_stream(resid, vecs, fprm, w_in, b_in, conv_w, conv_b, skip, w_o, b_o):
    B, L, D = resid.shape
    tables = _dft_tables(L)
    kfs = _hyena_filter_spectra(L, *fprm, tables)
    proj = _conv3(_mm_fused(resid, w_in, vecs=vecs, norm=True, bias=b_in), conv_w, conv_b)
    z = _long_conv_gated(proj, 2, kfs[0], skip[0], proj, 0, tables)
    y = _long_conv_gated(z, 0, kfs[1], skip[1], proj, 1, tables)
    return _mm_fused(y, w_o, vecs=vecs, bias=b_o, resid=resid)


def _rmsnorm(x, g, eps=NORM_EPS):
    return x * lax.rsqrt(jnp.mean(x * x, axis=-1, keepdims=True) + eps) * g


def _shift_seq(h):
    half = h.shape[-1] // 2
    prev = jnp.pad(h[:, :-1, :half], ((0, 0), (1, 0), (0, 0)))
    nxt = jnp.pad(h[:, 1:, half:], ((0, 0), (0, 1), (0, 0)))
    return jnp.concatenate([prev, nxt], axis=-1)


def _shift_grid(h):
    b, l, d = h.shape
    rows = l // GRID_W
    g = h.reshape(b, rows, GRID_W, d)
    q = d // 4
    left = jnp.pad(g[:, :, :-1, :q], ((0, 0), (0, 0), (1, 0), (0, 0)))
    right = jnp.pad(g[:, :, 1:, q:2 * q], ((0, 0), (0, 0), (0, 1), (0, 0)))
    up = jnp.pad(g[:, :-1, :, 2 * q:3 * q], ((0, 0), (1, 0), (0, 0), (0, 0)))
    down = jnp.pad(g[:, 1:, :, 3 * q:], ((0, 0), (0, 1), (0, 0), (0, 0)))
    return jnp.concatenate([left, right, up, down], axis=-1).reshape(b, l, d)


RW_TM = 256


def _headsum(x, ones):
    hi, lo = _split2(x)
    cols = []
    for j in range(x.shape[1] // LANES):
        ln = slice(j * LANES, (j + 1) * LANES)
        cols.append(_dot(hi[:, ln], ones) + _dot(lo[:, ln], ones))
    return jnp.concatenate(cols, axis=1)


def _rwkv_pre_kernel(*refs, vres):
    it = iter(refs)
    h_ref, sh_ref = next(it), next(it)
    vf_ref = next(it) if vres else None
    vec_ref, wr_ref, wk_ref, wv_ref, w1_ref, w2_ref, a1_ref, a2_ref, g1_ref, g2_ref = (next(it) for _ in range(10))
    v1_ref, v2_ref = (next(it), next(it)) if vres else (None, None)
    ones_ref = next(it)
    r_o, v_o, nkk_o, lw0_o, lw1_o, kd0_o, kd1_o, b0_o, b1_o, bonus_o, g_o = it
    vec = lambda i: vec_ref[i:i + 1, :]
    h = h_ref[...]
    xx = sh_ref[...] - h
    xr, xw, xk, xv, xa, xg = ((h + xx * vec(j)).astype(BF16) for j in range(6))
    r = _dot(xr, wr_ref[...])
    k = _dot(xk, wk_ref[...])
    v = _dot(xv, wv_ref[...])
    if vres:
        lora = _dot(_dot(xv, v1_ref[...]).astype(BF16), v2_ref[...])
        v = v + (vf_ref[...] - v) * jax.nn.sigmoid(vec(13) + lora)
    ones = ones_ref[...]
    kk = k * vec(10)
    kk = kk / jnp.maximum(jnp.sqrt(_headsum(kk * kk, ones)), 1e-12)
    r_o[...] = r
    v_o[...] = v
    nkk_o[...] = -kk
    bonus = jnp.zeros_like(r)
    for d, (lw_o, kd_o, b_o) in enumerate(((lw0_o, kd0_o, b0_o), (lw1_o, kd1_o, b1_o))):
        wl = vec(6 + d) + _dot(jnp.tanh(_dot(xw, w1_ref[d])).astype(BF16), w2_ref[d])
        lw_o[...] = -jax.nn.sigmoid(wl) * math.exp(-0.5)
        a = jax.nn.sigmoid(vec(8 + d) + _dot(_dot(xa, a1_ref[d]).astype(BF16), a2_ref[d]))
        kd = k * (1.0 + (a - 1.0) * vec(11))
        kd_o[...] = kd
        b_o[...] = kk * a
        bonus = bonus + _headsum(r * kd * vec(12), ones) * v
    bonus_o[...] = bonus
    g_o[...] = _dot(jax.nn.sigmoid(_dot(xg, g1_ref[...])).astype(BF16), g2_ref[...])


def _rwkv_post_kernel(y0_ref, y1_ref, bonus_ref, g_ref, res_ref, mod_ref, ln_ref, wo_ref, ones_ref, o_ref):
    ones = ones_ref[...]
    y = y0_ref[...] + y1_ref[...]
    yc = y - _headsum(y, ones) * (1.0 / RW_HEAD)
    var = _headsum(yc * yc, ones) * (1.0 / RW_HEAD)
    yn = yc * lax.rsqrt(var + RW_LN_EPS) * ln_ref[0:1, :] + ln_ref[1:2, :] + bonus_ref[...]
    out = _dot((yn * g_ref[...]).astype(BF16), wo_ref[...])
    o_ref[...] = res_ref[...] + mod_ref[...] * out


def _full(a):
    nd = a.ndim
    return pl.BlockSpec(a.shape, lambda b, i: (0,) * nd)


def _rwkv7_stream(h, shifted, state0, v_first, resid, mod_gate, mu, w_rkv, w_o, w0, w1, w2, a0, a1, a2,
                  g1, g2, k_k, k_a, r_k, ln_w, ln_b, vres, need_out=True):
    B, L, D = h.shape
    tm = min(RW_TM, L)
    vres_on = vres is not None
    rows = [mu[j] for j in range(6)] + [w0[0], w0[1], a0[0], a0[1], k_k, k_a, r_k.reshape(D)]
    rows.append(vres[0] if vres_on else jnp.zeros((D,), F32))
    vecs = jnp.stack(rows + [jnp.zeros((D,), F32)] * (16 - len(rows)))
    ones = jnp.kron(jnp.eye(LANES // RW_HEAD, dtype=F32), jnp.ones((RW_HEAD, RW_HEAD), F32)).astype(BF16)
    bf = lambda t: t.astype(BF16)
    tok = pl.BlockSpec((None, tm, D), lambda b, i: (b, i, 0))
    args = [h, shifted] + ([v_first] if vres_on else [])
    consts = [vecs, bf(w_rkv[0]), bf(w_rkv[1]), bf(w_rkv[2]), bf(w1), bf(w2), bf(a1), bf(a2), bf(g1), bf(g2)]
    consts += [bf(vres[1]), bf(vres[2])] if vres_on else []
    consts.append(ones)
    outs = pl.pallas_call(
        functools.partial(_rwkv_pre_kernel, vres=vres_on),
        grid=(B, L // tm),
        in_specs=[tok] * len(args) + [_full(c) for c in consts],
        out_specs=[tok] * 11,
        out_shape=[jax.ShapeDtypeStruct((B, L, D), F32)] * 11,
        compiler_params=_cparams(("parallel", "parallel")),
        name="rwkv_pre",
    )(*args, *consts)
    r, v, nkk, lw0, lw1, kd0, kd1, b0, b1, bonus, g = outs
    y0, s0 = _rwkv_scan(r, lw0, kd0, v, nkk, b0, state0[0], reverse=False)
    y1, s1 = _rwkv_scan(r, lw1, kd1, v, nkk, b1, state0[1], reverse=True)
    states = jnp.stack([s0, s1])
    if not need_out:
        return None, states, v
    ln = jnp.stack([ln_w, ln_b] + [jnp.zeros((D,), F32)] * 6)
    wo = bf(w_o)
    new_resid = pl.pallas_call(
        _rwkv_post_kernel,
        grid=(B, L // tm),
        in_specs=[tok] * 5 + [pl.BlockSpec((None, 1, D), lambda b, i: (b, 0, 0)), _full(ln), _full(wo), _full(ones)],
        out_specs=tok,
        out_shape=jax.ShapeDtypeStruct((B, L, D), F32),
        compiler_params=_cparams(("parallel", "parallel")),
        name="rwkv_post",
    )(y0, y1, bonus, g, resid, jnp.broadcast_to(mod_gate, (B, 1, D)), ln, wo, ones)
    return new_resid, states, v


def _mm_fused_kernel(*refs, norm, bias, resid):
    it = iter(refs)
    x_ref = next(it)
    vec_ref = next(it) if (norm or resid) else None
    w_ref = next(it)
    b_ref = next(it) if bias else None
    res_ref = next(it) if resid else None
    o_ref, xs_scr = next(it), next(it)

    @pl.when(pl.program_id(2) == 0)
    def _prep():
        x = x_ref[...]
        xs_scr[...] = (_norm_mod(x, vec_ref) if norm else x).astype(BF16)

    acc = _dot(xs_scr[...], w_ref[...])
    if bias:
        acc = acc + b_ref[...]
    o_ref[...] = res_ref[...] + vec_ref[2:3, :] * acc if resid else acc


def _mm_fused(x, w, vecs=None, norm=False, bias=None, resid=None):
    B, L, K = x.shape
    N = w.shape[1]
    tm = min(512, L)
    tn = 512 if (N % 512 == 0 and resid is None) else N
    use_vec = norm or resid is not None
    specs = [pl.BlockSpec((None, tm, K), lambda b, i, j: (b, i, 0))]
    args = [x]
    if use_vec:
        specs.append(pl.BlockSpec((None, 8, vecs.shape[-1]), lambda b, i, j: (b, 0, 0)))
        args.append(vecs)
    specs.append(pl.BlockSpec((K, tn), lambda b, i, j: (0, j)))
    args.append(w.astype(BF16))
    if bias is not None:
        specs.append(pl.BlockSpec((1, tn), lambda b, i, j: (0, j)))
        args.append(bias[None, :])
    out_spec = pl.BlockSpec((None, tm, tn), lambda b, i, j: (b, i, j))
    if resid is not None:
        assert N == K == vecs.shape[-1] and tn == N
        specs.append(out_spec)
        args.append(resid)
    return pl.pallas_call(
        functools.partial(_mm_fused_kernel, norm=norm, bias=bias is not None, resid=resid is not None),
        grid=(B, L // tm, N // tn),
        in_specs=specs, out_specs=out_spec,
        out_shape=jax.ShapeDtypeStruct((B, L, N), F32),
        scratch_shapes=[pltpu.VMEM((tm, K), BF16)],
        compiler_params=_cparams(("parallel", "parallel", "arbitrary")),
        name="mm_fused",
    )(*args)


def _hgrn_post_kernel(o0_ref, o1_ref, g_ref, res_ref, vec_ref, gn_ref, wo_ref, ones_ref, o_ref):
    o = o0_ref[...] + o1_ref[...]
    ms = _headsum(o * o, ones_ref[...]) * (1.0 / HG_DK)
    on = o * lax.rsqrt(ms + NORM_EPS) * gn_ref[...]
    g = g_ref[...]
    z = (on * (g * jax.nn.sigmoid(g))).astype(BF16)
    o_ref[...] = res_ref[...] + vec_ref[2:3, :] * _dot(z, wo_ref[...])


def _hgrn2_stream(resid, vecs, state0, lb, w_in, gn, w_o, need_out=True):
    B, L, D = resid.shape
    proj = _mm_fused(resid, w_in, vecs=vecs, norm=True)
    o0, s0 = _gla_scan(proj, lb, state0[0], 0)
    o1, s1 = _gla_scan(proj, lb, state0[1], 1)
    states = jnp.stack([s0, s1])
    if not need_out:
        return None, states
    tm = min(RW_TM, L)
    tok = pl.BlockSpec((None, tm, D), lambda b, i: (b, i, 0))
    ones = jnp.ones((LANES, LANES), BF16)
    gn_row = jnp.tile(gn, D // gn.shape[0])[None, :]
    wo = w_o.astype(BF16)
    out = pl.pallas_call(
        _hgrn_post_kernel,
        grid=(B, L // tm),
        in_specs=[tok, tok, pl.BlockSpec((None, tm, D), lambda b, i: (b, i, 4)), tok,
                  pl.BlockSpec((None, 8, D), lambda b, i: (b, 0, 0)), _full(gn_row), _full(wo), _full(ones)],
        out_specs=tok,
        out_shape=jax.ShapeDtypeStruct((B, L, D), F32),
        compiler_params=_cparams(("parallel", "parallel")),
        name="hgrn_post",
    )(o0, o1, proj, resid, vecs, gn_row, wo, ones)
    return out, states


def kernel(x, c, ctx, c_ctx, norm_g, ada_w, ada_b, final_g,
           rw_mu, rw_wrkv, rw_wo, rw_w0, rw_w1, rw_w2, rw_a0, rw_a1, rw_a2,
           rw_v0, rw_v1, rw_v2, rw_g1, rw_g2, rw_kk, rw_ka, rw_rk, rw_lnw, rw_lnb,
           hy_win, hy_bin, hy_cw, hy_cb, hy_fw1, hy_fb1, hy_fw2, hy_fb2, hy_fw3, hy_fb3,
           hy_fw4, hy_freq, hy_skip, hy_wo, hy_bo,
           hg_win, hg_lb, hg_gn, hg_wo,
           ffn_w13, ffn_w2, moe_router, moe_w13, moe_w2):
    B = x.shape[0]
    depth = norm_g.shape[0]
    D = D_MODEL
    lat, cx = x, ctx
    v_first = None
    lbc = jnp.cumsum(jax.nn.softmax(hg_lb, axis=0), axis=0)
    lower = lbc - lbc[:1]
    cond = jnp.concatenate([jax.nn.silu(c), jax.nn.silu(c_ctx)[None, :]], axis=0)
    ffn_w13b, ffn_w2b = ffn_w13.astype(BF16), ffn_w2.astype(BF16)
    moe_w13b, moe_w2b = moe_w13.astype(BF16), moe_w2.astype(BF16)
    for i in range(depth):
        last = i == depth - 1
        mod = _mm(cond, ada_w[i]) + ada_b[i]
        mod_l = jnp.split(mod[:B, None, :], 6, axis=-1)
        mod_c = jnp.split(mod[B:, None, :], 6, axis=-1)
        kind, slot = i % N_MIXERS, i // N_MIXERS
        vec_l = _mod_rows(norm_g[i, 0], mod_l[0], mod_l[1], mod_l[2], B)
        vec_c = _mod_rows(norm_g[i, 0], mod_c[0], mod_c[1], mod_c[2], B)
        if kind == 0:
            hl = _rmsnorm(lat, norm_g[i, 0]) * (1.0 + mod_l[1]) + mod_l[0]
            hc = _rmsnorm(cx, norm_g[i, 0]) * (1.0 + mod_c[1]) + mod_c[0]
            vres = None if slot == 0 else (rw_v0[slot - 1], rw_v1[slot - 1], rw_v2[slot - 1])
            rw = (rw_mu[slot], rw_wrkv[slot], rw_wo[slot], rw_w0[slot], rw_w1[slot], rw_w2[slot],
                  rw_a0[slot], rw_a1[slot], rw_a2[slot], rw_g1[slot], rw_g2[slot], rw_kk[slot],
                  rw_ka[slot], rw_rk[slot], rw_lnw[slot], rw_lnb[slot], vres)
            zero = jnp.zeros((2, B, RW_H, RW_HEAD, RW_HEAD), F32)
            vf_c = None if v_first is None else v_first[0]
            vf_l = None if v_first is None else v_first[1]
            cx_new, s_ctx, v_c = _rwkv7_stream(hc, _shift_seq(hc), zero, vf_c, cx, mod_c[2], *rw,
                                               need_out=not last)
            lat, _, v_l = _rwkv7_stream(hl, _shift_grid(hl), s_ctx, vf_l, lat, mod_l[2], *rw)
            if not last:
                cx = cx_new
            if slot == 0:
                v_first = (v_c, v_l)
        elif kind == 1:
            fprm = (hy_fw1[slot], hy_fb1[slot], hy_fw2[slot], hy_fb2[slot], hy_fw3[slot],
                    hy_fb3[slot], hy_fw4[slot], hy_freq[slot])
            hprm = (hy_win[slot], hy_bin[slot], hy_cw[slot], hy_cb[slot], hy_skip[slot],
                    hy_wo[slot], hy_bo[slot])
            lat = _hyena_stream(lat, vec_l, fprm, *hprm)
            if not last:
                cx = _hyena_stream(cx, vec_c, fprm, *hprm)
        else:
            zero = jnp.zeros((2, B, HG_H, HG_DK, D // HG_H), F32)
            gprm = (lower[i], hg_win[slot], hg_gn[slot], hg_wo[slot])
            cx_new, s_ctx = _hgrn2_stream(cx, vec_c, zero, *gprm, need_out=not last)
            lat, _ = _hgrn2_stream(lat, vec_l, s_ctx, *gprm)
            if not last:
                cx = cx_new

        def channel_mix(s, mod):
            vecs = _mod_rows(norm_g[i, 1], mod[3], mod[4], mod[5], B)
            if i % 2 == 0:
                return _ffn(s, vecs, ffn_w13b[i // 2], ffn_w2b[i // 2])
            xn, logits = _norm_route(s, vecs, moe_router[i // 2])
            out = _moe(xn.reshape(-1, D), logits.reshape(-1, N_EXPERTS), moe_w13b[i // 2], moe_w2b[i // 2])
            return s + mod[5] * out.reshape(s.shape)

        lat = channel_mix(lat, mod_l)
        if not last:
            cx = channel_mix(cx, mod_c)
    return _rmsnorm(lat, final_g)
```

```python
import functools
import math

import jax
import jax.numpy as jnp
from jax import lax
from jax.experimental import pallas as pl
from jax.experimental.pallas import tpu as pltpu

F32 = jnp.float32
BF16 = jnp.bfloat16
HIGHEST = lax.Precision.HIGHEST

D_MODEL = 1024
GRID_W = 64
NORM_EPS = 1e-6
RW_HEAD = 64
RW_H = D_MODEL // RW_HEAD
RW_LN_EPS = 64e-5
HY_ORDER = 2
HY_EMB = 33
HY_BANDS = (HY_EMB - 1) // 2
HY_FAST_DECAY = 0.3
HY_SLOW_DECAY = 1.5
HY_TARGET = 1e-2
HG_DK = 128
HG_H = D_MODEL // HG_DK
HG_CHUNK = 32
N_EXPERTS = 8
TOP_K = 2
N_MIXERS = 3

LANES = 128
VMEM_LIMIT = 56 * 1024 * 1024
RW_CHUNK = 64
SCAN_BLOCK = 256
HY_N2 = 128


def _cparams(sem):
    return pltpu.CompilerParams(dimension_semantics=sem, vmem_limit_bytes=VMEM_LIMIT)


def _dot(a, b, prec=None):
    return jnp.dot(a, b, preferred_element_type=F32, precision=prec)


def _dot_nt(a, b, prec=None):
    return lax.dot_general(a, b, (((1,), (1,)), ((), ())), preferred_element_type=F32, precision=prec)


def _split2(x):
    hi = x.astype(BF16)
    lo = (x - hi.astype(F32)).astype(BF16)
    return hi, lo


def _pdot(a, b, mode, nt=False):
    dn = (((1,), (1 if nt else 0,)), ((), ()))
    dg = lambda x, y, p=None: lax.dot_general(x, y, dn, preferred_element_type=F32, precision=p)
    if mode == 'f32':
        return dg(a, b, HIGHEST)
    if mode == 'bf16':
        return dg(a.astype(BF16), b.astype(BF16))
    ah, al = _split2(a)
    bh, bl = _split2(b)
    return dg(ah, bh) + (dg(ah, bl) + dg(al, bh))


def _cumsum_dot(tri, x):
    hi = x.astype(BF16)
    r1 = x - hi.astype(F32)
    mid = r1.astype(BF16)
    lo = (r1 - mid.astype(F32)).astype(BF16)
    n = x.shape[1]
    g = _dot(tri, jnp.concatenate([hi, mid, lo], axis=1))
    return g[:, :n] + (g[:, n:2 * n] + g[:, 2 * n:])


def _mm_kernel(a_ref, b_ref, o_ref, *, prec):
    o_ref[...] = _dot(a_ref[...], b_ref[...], prec)


def _mm(a, b, *, exact=False):
    M, K = a.shape
    N = b.shape[1]
    dt = F32 if exact else BF16
    a = a.astype(dt)
    b = b.astype(dt)
    Mp = -(-M // 8) * 8
    if Mp != M:
        a = jnp.pad(a, ((0, Mp - M), (0, 0)))
    tm = 512 if Mp % 512 == 0 else (256 if Mp % 256 == 0 else Mp)
    tn = 512 if N % 512 == 0 else N
    out = pl.pallas_call(
        functools.partial(_mm_kernel, prec=HIGHEST if exact else None),
        grid=(Mp // tm, N // tn),
        in_specs=[pl.BlockSpec((tm, K), lambda i, j: (i, 0)),
                  pl.BlockSpec((K, tn), lambda i, j: (0, j))],
        out_specs=pl.BlockSpec((tm, tn), lambda i, j: (i, j)),
        out_shape=jax.ShapeDtypeStruct((Mp, N), F32),
        compiler_params=_cparams(("parallel", "parallel")),
        name="mm",
    )(a, b)
    return out[:M] if Mp != M else out


def _mm3(x, w, **kw):
    B, L, K = x.shape
    return _mm(x.reshape(B * L, K), w, **kw).reshape(B, L, w.shape[1])


def _norm_mod(x, vec_ref):
    xn = x * lax.rsqrt(jnp.mean(x * x, axis=-1, keepdims=True) + NORM_EPS)
    return xn * vec_ref[0:1, :] + vec_ref[1:2, :]


def _mod_rows(g, shift, scale, gate, B):
    D = g.shape[-1]
    rows = [jnp.broadcast_to(g * (1.0 + scale[:, 0]), (B, D)), jnp.broadcast_to(shift[:, 0], (B, D)),
            jnp.broadcast_to(gate[:, 0], (B, D))]
    return jnp.stack(rows + [jnp.zeros((B, D), F32)] * 5, axis=1)


def _ffn_kernel(x_ref, vec_ref, wg_ref, wu_ref, w2_ref, o_ref, xn_scr, acc_ref, *, n_f):
    f = pl.program_id(2)

    @pl.when(f == 0)
    def _first():
        xn_scr[...] = _norm_mod(x_ref[...], vec_ref).astype(BF16)
        acc_ref[...] = jnp.zeros_like(acc_ref)

    x = xn_scr[...]
    gate = _dot(x, wg_ref[...])
    up = _dot(x, wu_ref[...])
    h = (gate * jax.nn.sigmoid(gate) * up).astype(BF16)
    acc_ref[...] += _dot(h, w2_ref[...])

    @pl.when(f == n_f - 1)
    def _store():
        o_ref[...] = x_ref[...] + vec_ref[2:3, :] * acc_ref[...]


def _ffn(x, vecs, w13, w2):
    B, L, D = x.shape
    F = w13.shape[1] // 2
    tm = min(1024, L)
    tf = 512 if F % 512 == 0 else 256
    n_f = F // tf
    tok = pl.BlockSpec((None, tm, D), lambda b, i, f: (b, i, 0))
    return pl.pallas_call(
        functools.partial(_ffn_kernel, n_f=n_f),
        grid=(B, L // tm, n_f),
        in_specs=[tok,
                  pl.BlockSpec((None, 8, D), lambda b, i, f: (b, 0, 0)),
                  pl.BlockSpec((D, tf), lambda b, i, f: (0, f)),
                  pl.BlockSpec((D, tf), lambda b, i, f: (0, f + n_f)),
                  pl.BlockSpec((tf, D), lambda b, i, f: (f, 0))],
        out_specs=tok,
        out_shape=jax.ShapeDtypeStruct((B, L, D), F32),
        scratch_shapes=[pltpu.VMEM((tm, D), BF16), pltpu.VMEM((tm, D), F32)],
        compiler_params=_cparams(("parallel", "parallel", "arbitrary")),
        name="ffn",
    )(x, vecs, w13, w13, w2)


def _norm_route_kernel(x_ref, vec_ref, wr_ref, xn_ref, lg_ref):
    xn = _norm_mod(x_ref[...], vec_ref)
    xn_ref[...] = xn.astype(BF16)
    lg_ref[...] = _dot(xn, wr_ref[...], HIGHEST)


def _norm_route(x, vecs, w_router):
    B, L, D = x.shape
    E = w_router.shape[1]
    tm = min(512, L)
    tok = pl.BlockSpec((None, tm, D), lambda b, i: (b, i, 0))
    return pl.pallas_call(
        _norm_route_kernel,
        grid=(B, L // tm),
        in_specs=[tok, pl.BlockSpec((None, 8, D), lambda b, i: (b, 0, 0)), pl.BlockSpec((D, E), lambda b, i: (0, 0))],
        out_specs=[tok, pl.BlockSpec((None, tm, E), lambda b, i: (b, i, 0))],
        out_shape=[jax.ShapeDtypeStruct((B, L, D), BF16), jax.ShapeDtypeStruct((B, L, E), F32)],
        compiler_params=_cparams(("parallel", "parallel")),
        name="norm_route",
    )(x, vecs, w_router)


MOE_TILE = 2048
MOE_SUB = 256
MOE_SLAB = 512


def _moe_kernel(cnt_ref, x_ref, rrow_ref, rcol_ref, gcol_ref, wg_ref, wu_ref, w2_ref, o_ref, xc_scr, y_scr,
                *, n_f, tm):
    i, e, f = pl.program_id(0), pl.program_id(1), pl.program_id(2)
    sub = MOE_SUB
    slab = min(MOE_SLAB, tm)
    nslab = tm // slab
    n_sub = (cnt_ref[i, e, nslab] + (sub - 1)) // sub

    def touches(s, j):
        return (cnt_ref[i, e, j] < (s + 1) * sub) & (cnt_ref[i, e, j + 1] > s * sub)

    @pl.when((e == 0) & (f == 0))
    def _zero():
        o_ref[...] = jnp.zeros_like(o_ref)

    @pl.when(f == 0)
    def _compact():
        def body(s, carry):
            rows = pl.ds(pl.multiple_of(s * sub, sub), sub)
            y_scr[rows, :] = jnp.zeros((sub, y_scr.shape[1]), F32)
            for j in range(nslab):
                cols = slice(j * slab, (j + 1) * slab)

                @pl.when(touches(s, j))
                def _add():
                    ridx = lax.broadcasted_iota(jnp.int32, (sub, slab), 0) + s * sub
                    onehot = jnp.where(rrow_ref[:, cols] == ridx, 1.0, 0.0).astype(BF16)
                    y_scr[rows, :] += _dot(onehot, x_ref[cols, :])
            xc_scr[rows, :] = y_scr[rows, :].astype(BF16)
            return carry
        lax.fori_loop(0, n_sub, body, 0)

    def expert(s, first):
        rows = pl.ds(pl.multiple_of(s * sub, sub), sub)
        xs = xc_scr[rows, :]
        gate = _dot(xs, wg_ref[...])
        up = _dot(xs, wu_ref[...])
        h = (gate * jax.nn.sigmoid(gate) * up).astype(BF16)
        part = _dot(h, w2_ref[...])
        y_scr[rows, :] = part if first else y_scr[rows, :] + part

    @pl.when(f == 0)
    def _first():
        lax.fori_loop(0, n_sub, lambda s, c: (expert(s, True), c)[1], 0)

    @pl.when(f > 0)
    def _rest():
        lax.fori_loop(0, n_sub, lambda s, c: (expert(s, False), c)[1], 0)

    @pl.when(f == n_f - 1)
    def _scatter():
        def body(s, carry):
            y = y_scr[pl.ds(pl.multiple_of(s * sub, sub), sub), :].astype(BF16)
            for j in range(nslab):
                rows = slice(j * slab, (j + 1) * slab)

                @pl.when(touches(s, j))
                def _add():
                    cidx = lax.broadcasted_iota(jnp.int32, (slab, sub), 1) + s * sub
                    onehot_t = jnp.where(rcol_ref[rows, :] == cidx, 1.0, 0.0).astype(BF16)
                    o_ref[rows, :] += gcol_ref[rows, :] * _dot(onehot_t, y)
            return carry
        lax.fori_loop(0, n_sub, body, 0)


def _moe(x, logits, w13, w2):
    T, D = x.shape
    E, _, F2 = w13.shape
    F = F2 // 2
    tm = min(MOE_TILE, T)
    nt = T // tm
    tf = 512
    n_f = F // tf
    top_val, top_idx = lax.top_k(logits, TOP_K)
    gates = jax.nn.softmax(top_val, axis=-1)
    onehot = jax.nn.one_hot(top_idx, E, dtype=F32)
    sel = jnp.sum(onehot, axis=1).astype(jnp.int32).reshape(nt, tm, E)
    gate_dense = jnp.sum(onehot * gates[..., None], axis=1).reshape(nt, tm, E)
    rank = jnp.where(sel > 0, jnp.cumsum(sel, axis=1) - sel, -1)
    rank = jnp.swapaxes(rank, 1, 2)
    slab = min(MOE_SLAB, tm)
    per_slab = jnp.sum(sel.reshape(nt, tm // slab, slab, E), axis=2)
    counts = jnp.concatenate([jnp.zeros((nt, 1, E), jnp.int32), jnp.cumsum(per_slab, axis=1)], axis=1)
    counts = jnp.swapaxes(counts, 1, 2)
    gcol = jnp.swapaxes(gate_dense, 1, 2)[..., None]
    tile = lambda shape, imap: pl.BlockSpec(shape, imap)
    return pl.pallas_call(
        functools.partial(_moe_kernel, n_f=n_f, tm=tm),
        grid_spec=pltpu.PrefetchScalarGridSpec(
            num_scalar_prefetch=1,
            grid=(nt, E, n_f),
            in_specs=[tile((tm, D), lambda i, e, f, c: (i, 0)),
                      tile((None, None, 1, tm), lambda i, e, f, c: (i, e, 0, 0)),
                      tile((None, None, tm, 1), lambda i, e, f, c: (i, e, 0, 0)),
                      tile((None, None, tm, 1), lambda i, e, f, c: (i, e, 0, 0)),
                      tile((None, D, tf), lambda i, e, f, c: (e, 0, f)),
                      tile((None, D, tf), lambda i, e, f, c: (e, 0, f + n_f)),
                      tile((None, tf, D), lambda i, e, f, c: (e, f, 0))],
            out_specs=tile((tm, D), lambda i, e, f, c: (i, 0)),
            scratch_shapes=[pltpu.VMEM((tm, D), BF16), pltpu.VMEM((tm, D), F32)]),
        out_shape=jax.ShapeDtypeStruct((T, D), F32),
        compiler_params=_cparams(("parallel", "arbitrary", "arbitrary")),
        name="moe",
    )(counts, x, rank[:, :, None, :], rank[..., None], gcol, w13, w13, w2)


def _rwkv_scan_kernel(r_ref, lw_ref, k_ref, v_ref, a_ref, b_ref, h0_ref, y_ref, hT_ref, H_scr,
                      *, reverse, nblk, nchunk, npl, modes):
    C = RW_CHUNK
    half = LANES // 2
    m_a, m_inv, m_u, m_st = modes
    i = pl.program_id(2)

    @pl.when(i == 0)
    def _init():
        H_scr[...] = h0_ref[...]

    t_idx = lax.broadcasted_iota(jnp.int32, (C, LANES), 0)
    s_idx = lax.broadcasted_iota(jnp.int32, (C, LANES), 1) & (half - 1)
    tt = lax.broadcasted_iota(jnp.int32, (C, C), 0)
    ss = lax.broadcasted_iota(jnp.int32, (C, C), 1)
    if reverse:
        strict, incl, tri = s_idx > t_idx, s_idx >= t_idx, (ss >= tt).astype(BF16)
    else:
        strict, incl, tri = s_idx < t_idx, s_idx <= t_idx, (ss <= tt).astype(BF16)
    eye_lp = (s_idx == t_idx).astype(F32)
    lane = lax.broadcasted_iota(jnp.int32, (1, LANES), 1)
    m0 = (lane < half).astype(F32)
    m1 = 1.0 - m0
    rr = lax.broadcasted_iota(jnp.int32, (LANES, LANES), 0)
    cc = lax.broadcasted_iota(jnp.int32, (LANES, LANES), 1)
    mask_bd = ((rr < half) == (cc < half)).astype(F32)

    def bd(x):
        return jnp.concatenate([x * m0, x * m1], axis=0)

    order = list(range(nchunk - 1, -1, -1) if reverse else range(nchunk))
    units = [(slice(c * C, (c + 1) * C), slice(q * LANES, (q + 1) * LANES)) for c in order for q in range(npl)]
    ld = lambda ref: [ref[sl, ln] for sl, ln in units]
    r, lw, k, v, a, b = ld(r_ref), ld(lw_ref), ld(k_ref), ld(v_ref), ld(a_ref), ld(b_ref)
    G = [_cumsum_dot(tri, x) for x in lw]
    eG = [jnp.exp(g) for g in G]
    eGn = [jnp.exp(-g) for g in G]
    rt = [x * e for x, e in zip(r, eG)]
    at = [x * jnp.exp(g - l) for x, g, l in zip(a, G, lw)]
    kt = [x * e for x, e in zip(k, eGn)]
    bt = [x * e for x, e in zip(b, eGn)]
    M = [_pdot(jnp.concatenate([x, y], axis=0), jnp.concatenate([bd(z), bd(w)], axis=0), m_a, nt=True)
         for x, y, z, w in zip(at, rt, bt, kt)]
    Nm = [jnp.where(strict, m[:C, :LANES], 0.0) for m in M]
    Aak = [jnp.where(strict, m[:C, LANES:], 0.0) for m in M]
    Arb = [jnp.where(incl, m[C:, :LANES], 0.0) for m in M]
    Ark = [jnp.where(incl, m[C:, LANES:], 0.0) for m in M]
    T = None
    for lvl in range(1, int(math.log2(C)) + 1):
        same = (t_idx >> lvl) == (s_idx >> lvl)
        t_hi = ((t_idx >> (lvl - 1)) & 1) == 1
        s_hi = ((s_idx >> (lvl - 1)) & 1) == 1
        off = same & (s_hi & ~t_hi if reverse else t_hi & ~s_hi)
        Noff = [jnp.where(off, n, 0.0) for n in Nm]
        if T is None:
            T = [eye_lp + n for n in Noff]
        else:
            DN = [_pdot(t, bd(n), m_inv) for t, n in zip(T, Noff)]
            T = [t + _pdot(dn, bd(t), m_inv) for t, dn in zip(T, DN)]
    bdv = [bd(x) for x in v]
    X0 = [_pdot(x, y, m_st) for x, y in zip(Aak, bdv)]
    WU = [_pdot(t, jnp.concatenate([bd(x), bd(y)], axis=1), m_u) for t, x, y in zip(T, at, X0)]
    W = [x[:, :LANES] for x in WU]
    U0 = [x[:, LANES:] for x in WU]
    RY = [_pdot(x, jnp.concatenate([bd(y), bd(z)], axis=1), m_st) for x, y, z in zip(Arb, W, U0)]
    Rh = [x + y[:, :LANES] for x, y in zip(rt, RY)]
    Y0 = [y[:, LANES:] + _pdot(x, z, m_st) for y, x, z in zip(RY, Ark, bdv)]
    dPhi = [mask_bd * _pdot(x.T, y, m_st) for x, y in zip(W, bt)]
    Psi = [mask_bd * _pdot(jnp.concatenate([x, y], axis=0).T, jnp.concatenate([z, w], axis=0), m_st)
           for x, y, z, w in zip(U0, v, bt, kt)]
    e_end = [jnp.exp(g[0:1, :] if reverse else g[C - 1:C, :]) for g in G]

    Hs = [H_scr[q] for q in range(npl)]
    for u, (sl, ln) in enumerate(units):
        q = u % npl
        H = Hs[q]
        y_ref[sl, ln] = _pdot(Rh[u], H, m_st, nt=True) + Y0[u]
        Hs[q] = (H + _pdot(H, dPhi[u], m_st) + Psi[u]) * e_end[u]
    for q in range(npl):
        H_scr[q] = Hs[q]

    @pl.when(i == nblk - 1)
    def _fin():
        for q in range(npl):
            hT_ref[q] = Hs[q]


def _pair_states(S):
    B, H, N, _ = S.shape
    S5 = S.reshape(B, H // 2, 2, N, N)
    eye = jnp.eye(2, dtype=S.dtype)
    return jnp.einsum('bpivk,ij->bpivjk', S5, eye).reshape(B, H // 2, 2 * N, 2 * N)


def _unpair_states(Sp):
    B, P, N2, _ = Sp.shape
    N = N2 // 2
    S6 = Sp.reshape(B, P, 2, N, 2, N)
    return jnp.stack([S6[:, :, 0, :, 0, :], S6[:, :, 1, :, 1, :]], axis=2).reshape(B, 2 * P, N, N)


RW_MODES = ('bf16', 'bf16', 'bf16', 'bf16')
RW_PAIRS_PER_STEP = 2
RW_BLOCK = 512


def _rwkv_scan(r, lw, k, v, a, b, state0, reverse, modes=RW_MODES):
    B, L, D = r.shape
    bt = min(RW_BLOCK, L)
    nblk = L // bt
    npl = RW_PAIRS_PER_STEP
    npair = D // LANES
    blk = (lambda i: nblk - 1 - i) if reverse else (lambda i: i)
    tok = pl.BlockSpec((None, bt, npl * LANES), lambda bb, p, i: (bb, blk(i), p))
    st = pl.BlockSpec((None, npl, LANES, LANES), lambda bb, p, i: (bb, p, 0, 0))
    y, hT = pl.pallas_call(
        functools.partial(_rwkv_scan_kernel, reverse=reverse, nblk=nblk, nchunk=bt // RW_CHUNK, npl=npl,
                          modes=modes),
        grid=(B, npair // npl, nblk),
        in_specs=[tok] * 6 + [st],
        out_specs=[tok, st],
        out_shape=[jax.ShapeDtypeStruct((B, L, D), F32),
                   jax.ShapeDtypeStruct((B, npair, LANES, LANES), F32)],
        scratch_shapes=[pltpu.VMEM((npl, LANES, LANES), F32)],
        compiler_params=_cparams(("parallel", "parallel", "arbitrary")),
        name="rwkv_scan_rev" if reverse else "rwkv_scan_fwd",
    )(r, lw, k, v, a, b, _pair_states(state0))
    return y, _unpair_states(hT)


GLA_HEADS_PER_STEP = 2


def _gla_kernel(q_ref, f_ref, v_ref, lb_ref, h0_ref, o_ref, hT_ref, H_scr, *, reverse, nblk, npl):
    C = HG_CHUNK
    sh = int(math.log2(C))
    bt = q_ref.shape[0]
    nchunk = bt // C
    i = pl.program_id(2)

    @pl.when(i == 0)
    def _init():
        H_scr[...] = h0_ref[...]

    tt = lax.broadcasted_iota(jnp.int32, (bt, bt), 0)
    ss = lax.broadcasted_iota(jnp.int32, (bt, bt), 1)
    same = (tt >> sh) == (ss >> sh)
    mid = ((tt >> sh) << sh) + (C // 2 if reverse else C // 2 - 1)
    if reverse:
        incl, upto_mid = same & (ss >= tt), same & (ss >= mid)
    else:
        incl, upto_mid = same & (ss <= tt), same & (ss <= mid)
    one = lambda m: jnp.where(m, 1.0, 0.0)
    sums = jnp.concatenate([one(incl), one(incl) - one(upto_mid), one(same) - one(incl)], axis=0).astype(BF16)

    order = list(range(nchunk - 1, -1, -1) if reverse else range(nchunk))
    heads = [slice(p * LANES, (p + 1) * LANES) for p in range(npl)]
    q = [jax.nn.silu(q_ref[:, ln]) for ln in heads]
    fg = [lb_ref[:, ln] + (1.0 - lb_ref[:, ln]) * jax.nn.sigmoid(f_ref[:, ln]) for ln in heads]
    v = [v_ref[:, ln] for ln in heads]
    k = [1.0 - f for f in fg]
    Gs = [_cumsum_dot(sums, jnp.log(f)) for f in fg]
    Gabs = [g[:bt] for g in Gs]
    Grel = [g[bt:2 * bt] for g in Gs]
    Gend = [g[2 * bt:] for g in Gs]
    scores = [jnp.where(incl, _pdot(x * jnp.exp(g), y * jnp.exp(-g), 'bf16', nt=True), 0.0)
              for x, y, g in zip(q, k, Grel)]
    o_intra = [_pdot(s, x, 'bf16') for s, x in zip(scores, v)]
    qa = [x * jnp.exp(g) for x, g in zip(q, Gabs)]
    kend = [x * jnp.exp(g) for x, g in zip(k, Gend)]
    rows = [slice(c * C, (c + 1) * C) for c in range(nchunk)]
    KV = [[_pdot(x[r].T, y[r], 'bf16') for r in rows] for x, y in zip(v, kend)]
    dec = [[jnp.exp(ga[r][0:1, :] + ge[r][0:1, :]) for r in rows] for ga, ge in zip(Gabs, Gend)]

    Hs = [H_scr[p] for p in range(npl)]
    for c in order:
        for p in range(npl):
            o_ref[rows[c], heads[p]] = o_intra[p][rows[c]] + _pdot(qa[p][rows[c]], Hs[p], 'bf16', nt=True)
            Hs[p] = Hs[p] * dec[p][c] + KV[p][c]
    for p in range(npl):
        H_scr[p] = Hs[p]

    @pl.when(i == nblk - 1)
    def _fin():
        for p in range(npl):
            hT_ref[p] = Hs[p]


def _gla_scan(proj, lb, state0, d):
    B, L, D5 = proj.shape
    D = D5 // 5
    reverse = d == 1
    bt = SCAN_BLOCK
    nblk = L // bt
    npl = GLA_HEADS_PER_STEP
    nh = D // LANES
    ncb = nh // npl
    blk = (lambda i: nblk - 1 - i) if reverse else (lambda i: i)
    col = lambda off: pl.BlockSpec((None, bt, npl * LANES), lambda bb, p, i: (bb, blk(i), off * ncb + p))
    st = pl.BlockSpec((None, npl, LANES, LANES), lambda bb, p, i: (bb, p, 0, 0))
    o, hT = pl.pallas_call(
        functools.partial(_gla_kernel, reverse=reverse, nblk=nblk, npl=npl),
        grid=(B, ncb, nblk),
        in_specs=[col(0), col(1 + d), col(3), pl.BlockSpec((1, npl * LANES), lambda bb, p, i: (0, p)), st],
        out_specs=[col(0), st],
        out_shape=[jax.ShapeDtypeStruct((B, L, D), F32),
                   jax.ShapeDtypeStruct((B, nh, LANES, LANES), F32)],
        scratch_shapes=[pltpu.VMEM((npl, LANES, LANES), F32)],
        compiler_params=_cparams(("parallel", "parallel", "arbitrary")),
        name="gla_scan_rev" if reverse else "gla_scan_fwd",
    )(proj, proj, proj, lb[d][None, :], jnp.swapaxes(state0, 2, 3))
    return o, jnp.swapaxes(hT, 2, 3)


def _x3dot(fh, fl, x):
    xh, xl = _split2(x)
    return _dot(fh, xh) + (_dot(fh, xl) + _dot(fl, xh))


def _leftmul_kernel(f_ref, x_ref, o_ref, *, nj):
    for j in range(nj):
        o_ref[:, j, :] = _dot(f_ref[...], x_ref[:, j, :], HIGHEST)


def _leftmul_gate_kernel(f_ref, x_ref, u_ref, s_ref, g_ref, o_ref, *, nj):
    for j in range(nj):
        y = _dot(f_ref[...], x_ref[:, j, :], HIGHEST)
        o_ref[:, j, :] = (y + u_ref[:, j, :] * s_ref[...]) * g_ref[:, j, :]


DFT_TD = 512


def _leftmul(f, x, xoff=0, epilogue=None, D=D_MODEL):
    B, K, J, _ = x.shape
    M = f.shape[0]
    tj = 8 if J % 8 == 0 else J
    td = DFT_TD
    fs = pl.BlockSpec((M, K), lambda b, j, d: (0, 0))
    col = lambda rows, off: pl.BlockSpec((None, rows, tj, td), lambda b, j, d: (b, 0, j, off + d))
    if epilogue is None:
        kern, specs, args = _leftmul_kernel, [fs, col(K, xoff)], (f, x)
    else:
        u, uoff, skip, gate, goff = epilogue
        kern = _leftmul_gate_kernel
        specs = [fs, col(K, xoff), col(M, uoff), pl.BlockSpec((1, td), lambda b, j, d: (0, d)), col(M, goff)]
        args = (f, x, u, skip, gate)
    return pl.pallas_call(
        functools.partial(kern, nj=tj), grid=(B, J // tj, D // td), in_specs=specs, out_specs=col(M, 0),
        out_shape=jax.ShapeDtypeStruct((B, M, J, D), F32),
        compiler_params=_cparams(("parallel", "parallel", "parallel")),
        name="dft_outer",
    )(*args)


def _spectral_fwd_kernel(fh_ref, fl_ref, y_ref, o_ref):
    o_ref[...] = _x3dot(fh_ref[...], fl_ref[...], y_ref[...])


def _spectral_conv_kernel(fh_ref, fl_ref, fih_ref, fil_ref, hf_ref, hb_ref, y_ref, o_ref):
    n2 = hf_ref.shape[0] // 2
    z = _x3dot(fh_ref[...], fl_ref[...], y_ref[...])
    zr, zi = z[:n2], z[n2:]
    kf = _combine_filter_spectrum(hf_ref[...], hb_ref[...])
    kr, ki = kf[:n2], kf[n2:]
    p = jnp.concatenate([zr * kr - zi * ki, zr * ki + zi * kr], axis=0)
    o_ref[...] = _x3dot(fih_ref[...], fil_ref[...], p)


def _spectral_mid(y, f_fwd, f_inv=None, spec=None, order=0):
    B, K1, R, D = y.shape
    ms = pl.BlockSpec((None, R, R), lambda k1, b: (k1, 0, 0))
    ys = pl.BlockSpec((None, None, R, D), lambda k1, b: (b, k1, 0, 0))
    if spec is None:
        kern, specs, args = _spectral_fwd_kernel, [ms, ms, ys], (*_split2(f_fwd), y)
    else:
        kern = _spectral_conv_kernel
        grp = lambda g: pl.BlockSpec((None, R, D), lambda k1, b: (k1, 0, g))
        specs = [ms, ms, ms, ms, grp(2 * order), grp(2 * order + 1), ys]
        args = (*_split2(f_fwd), *_split2(f_inv), spec, spec, y)
    return pl.pallas_call(
        kern, grid=(K1, B), in_specs=specs, out_specs=ys,
        out_shape=jax.ShapeDtypeStruct((B, K1, R, D), F32),
        compiler_params=_cparams(("parallel", "parallel")),
        name="dft_inner",
    )(*args)


def _dft_tables(L):
    n2 = HY_N2 if L % HY_N2 == 0 and L > HY_N2 * 2 else 1
    N = 2 * L
    n1 = N // n2
    k1h = n1 // 2
    ph = (jnp.arange(k1h, dtype=jnp.int32)[:, None] * 2 + 1) * jnp.arange(k1h, dtype=jnp.int32)[None, :]
    th = (ph % (2 * n1)).astype(F32) * (math.pi / n1)
    f_out = jnp.stack([jnp.cos(th), -jnp.sin(th)], axis=1).reshape(2 * k1h, k1h)
    f_out_inv = (2.0 / N) * f_out.T
    if n2 == 1:
        return n2, f_out, f_out_inv, None, None
    kk = jnp.arange(k1h, dtype=jnp.int32)[:, None, None] + n1 * jnp.arange(n2, dtype=jnp.int32)[None, :, None]
    ph = ((2 * kk + 1) * jnp.arange(n2, dtype=jnp.int32)[None, None, :]) % (2 * N)
    phi = ph.astype(F32) * (math.pi / N)
    c, s = jnp.cos(phi), jnp.sin(phi)
    f_in = jnp.concatenate([jnp.concatenate([c, s], axis=2), jnp.concatenate([-s, c], axis=2)], axis=1)
    ct, st = jnp.swapaxes(c, 1, 2), jnp.swapaxes(s, 1, 2)
    f_in_inv = jnp.concatenate([jnp.concatenate([ct, -st], axis=2), jnp.concatenate([st, ct], axis=2)], axis=1)
    return n2, f_out, f_out_inv, f_in, f_in_inv


def _long_conv_gated(u, ucol, spec, order, skip, gate, gcol, tables):
    n2, f_out, f_out_inv, f_in, f_in_inv = tables
    B, L, _ = u.shape
    D = D_MODEL
    k1h = L // n2
    per = D // DFT_TD
    u4 = u.reshape(B, k1h, n2, u.shape[-1])
    y = _leftmul(f_out, u4, xoff=ucol * per).reshape(B, k1h, 2 * n2, D)
    if f_in is None:
        cols = lambda g: spec[..., g * D:(g + 1) * D]
        kf = _combine_filter_spectrum(cols(2 * order), cols(2 * order + 1))
        yr, yi = y[:, :, 0], y[:, :, 1]
        kr, ki = kf[:, 0], kf[:, 1]
        q = jnp.stack([yr * kr - yi * ki, yr * ki + yi * kr], axis=2)
    else:
        q = _spectral_mid(y, f_in, f_in_inv, spec, order)
    out = _leftmul(f_out_inv, q.reshape(B, 2 * k1h, n2, D),
                   epilogue=(u4, ucol * per, skip[None, :], gate.reshape(B, k1h, n2, gate.shape[-1]), gcol * per))
    return out.reshape(B, L, D)


def _conv3_kernel(cur_ref, prev_ref, next_ref, w_ref, o_ref, *, nblk):
    i = pl.program_id(1)
    x = cur_ref[...]
    tm = x.shape[0]
    row = lax.broadcasted_iota(jnp.int32, x.shape, 0)
    before = jnp.where(i > 0, prev_ref[7:8, :], 0.0)
    after = jnp.where(i < nblk - 1, next_ref[0:1, :], 0.0)
    xp = jnp.where(row == 0, before, pltpu.roll(x, 1, axis=0))
    xn = jnp.where(row == tm - 1, after, pltpu.roll(x, tm - 1, axis=0))
    o_ref[...] = xp * w_ref[0:1, :] + x * w_ref[1:2, :] + xn * w_ref[2:3, :] + w_ref[3:4, :]


def _conv3(x, taps, bias):
    B, L, C = x.shape
    tm = min(512, L)
    tc = 512
    nblk = L // tm
    r8 = tm // 8
    w = jnp.concatenate([taps, bias[None, :], jnp.zeros((4, C), F32)], axis=0)
    return pl.pallas_call(
        functools.partial(_conv3_kernel, nblk=nblk),
        grid=(B, nblk, C // tc),
        in_specs=[pl.BlockSpec((None, tm, tc), lambda b, i, j: (b, i, j)),
                  pl.BlockSpec((None, 8, tc), lambda b, i, j: (b, jnp.maximum(i * r8 - 1, 0), j)),
                  pl.BlockSpec((None, 8, tc), lambda b, i, j: (b, jnp.minimum((i + 1) * r8, L // 8 - 1), j)),
                  pl.BlockSpec((8, tc), lambda b, i, j: (0, j))],
        out_specs=pl.BlockSpec((None, tm, tc), lambda b, i, j: (b, i, j)),
        out_shape=jax.ShapeDtypeStruct((B, L, C), F32),
        compiler_params=_cparams(("parallel", "parallel", "parallel")),
        name="conv3",
    )(x, x, x, w)


def _hyena_filter_spectra(L, w1, b1, w2, b2, w3, b3, w4, freq, tables):
    pos = jnp.arange(L, dtype=F32)
    t = (pos / max(L - 1, 1))[:, None]
    ang = (2.0 * math.pi / L) * pos[:, None] * jnp.linspace(1e-4, HY_BANDS - 1, HY_BANDS, dtype=F32)[None, :]
    z = jnp.concatenate([t, jnp.cos(ang), -jnp.sin(ang)], axis=-1)
    z = jnp.pad(z, ((0, 0), (0, 40 - HY_EMB)))
    w1p = jnp.pad(w1, ((0, 40 - HY_EMB), (0, 0)))
    hid = jnp.sin(freq[0] * (_mm(z, w1p, exact=True) + b1))
    hid = jnp.sin(freq[1] * (_mm(hid, w2, exact=True) + b2))
    hid = jnp.sin(freq[2] * (_mm(hid, w3, exact=True) + b3))
    filt = _mm(hid, w4, exact=True)
    deltas = jnp.linspace(math.log(HY_TARGET) / HY_FAST_DECAY, math.log(HY_TARGET) / HY_SLOW_DECAY,
                          D_MODEL, dtype=F32)
    window = jnp.exp(-t * jnp.abs(deltas))
    lag0 = (jnp.arange(L) > 0).astype(F32)[:, None]
    scale = jnp.concatenate([window, window * lag0] * HY_ORDER, axis=1)
    filt = filt * scale
    n2 = tables[0]
    k1h = L // n2
    C = HY_ORDER * 2 * D_MODEL
    y = _leftmul(tables[1], filt.reshape(1, k1h, n2, C), D=C).reshape(1, k1h, 2 * n2, C)
    return (y if tables[3] is None else _spectral_mid(y, tables[3]))[0]


def _combine_filter_spectrum(hf, hb):
    n2 = hf.shape[-2] // 2
    return jnp.concatenate([hf[..., :n2, :] + hb[..., :n2, :], hf[..., n2:, :] - hb[..., n2:, :]], axis=-2)


def _hyena_stream(resid, vecs, fprm, w_in, b_in, conv_w, conv_b, skip, w_o, b_o):
    B, L, D = resid.shape
    tables = _dft_tables(L)
    spec = _hyena_filter_spectra(L, *fprm, tables)
    proj = _conv3(_mm_fused(resid, w_in, vecs=vecs, norm=True, bias=b_in), conv_w, conv_b)
    z = _long_conv_gated(proj, 2, spec, 0, skip[0], proj, 0, tables)
    y = _long_conv_gated(z, 0, spec, 1, skip[1], proj, 1, tables)
    return _mm_fused(y, w_o, vecs=vecs, bias=b_o, resid=resid)


def _rmsnorm(x, g, eps=NORM_EPS):
    return x * lax.rsqrt(jnp.mean(x * x, axis=-1, keepdims=True) + eps) * g


def _shift_seq(h):
    half = h.shape[-1] // 2
    prev = jnp.pad(h[:, :-1, :half], ((0, 0), (1, 0), (0, 0)))
    nxt = jnp.pad(h[:, 1:, half:], ((0, 0), (0, 1), (0, 0)))
    return jnp.concatenate([prev, nxt], axis=-1)


def _shift_grid(h):
    b, l, d = h.shape
    rows = l // GRID_W
    g = h.reshape(b, rows, GRID_W, d)
    q = d // 4
    left = jnp.pad(g[:, :, :-1, :q], ((0, 0), (0, 0), (1, 0), (0, 0)))
    right = jnp.pad(g[:, :, 1:, q:2 * q], ((0, 0), (0, 0), (0, 1), (0, 0)))
    up = jnp.pad(g[:, :-1, :, 2 * q:3 * q], ((0, 0), (1, 0), (0, 0), (0, 0)))
    down = jnp.pad(g[:, 1:, :, 3 * q:], ((0, 0), (0, 1), (0, 0), (0, 0)))
    return jnp.concatenate([left, right, up, down], axis=-1).reshape(b, l, d)


RW_TM = 256


def _headsum(x, ones):
    hi, lo = _split2(x)
    cols = []
    for j in range(x.shape[1] // LANES):
        ln = slice(j * LANES, (j + 1) * LANES)
        cols.append(_dot(hi[:, ln], ones) + _dot(lo[:, ln], ones))
    return jnp.concatenate(cols, axis=1)


def _rwkv_pre_kernel(*refs, vres):
    it = iter(refs)
    h_ref, sh_ref = next(it), next(it)
    vf_ref = next(it) if vres else None
    vec_ref, wr_ref, wk_ref, wv_ref, w1_ref, w2_ref, a1_ref, a2_ref, g1_ref, g2_ref = (next(it) for _ in range(10))
    v1_ref, v2_ref = (next(it), next(it)) if vres else (None, None)
    ones_ref = next(it)
    r_o, v_o, nkk_o, lw0_o, lw1_o, kd0_o, kd1_o, b0_o, b1_o, bonus_o, g_o = it
    vec = lambda i: vec_ref[i:i + 1, :]
    h = h_ref[...]
    xx = sh_ref[...] - h
    xr, xw, xk, xv, xa, xg = ((h + xx * vec(j)).astype(BF16) for j in range(6))
    r = _dot(xr, wr_ref[...])
    k = _dot(xk, wk_ref[...])
    v = _dot(xv, wv_ref[...])
    if vres:
        lora = _dot(_dot(xv, v1_ref[...]).astype(BF16), v2_ref[...])
        v = v + (vf_ref[...] - v) * jax.nn.sigmoid(vec(13) + lora)
    ones = ones_ref[...]
    kk = k * vec(10)
    kk = kk / jnp.maximum(jnp.sqrt(_headsum(kk * kk, ones)), 1e-12)
    r_o[...] = r
    v_o[...] = v
    nkk_o[...] = -kk
    bonus = jnp.zeros_like(r)
    for d, (lw_o, kd_o, b_o) in enumerate(((lw0_o, kd0_o, b0_o), (lw1_o, kd1_o, b1_o))):
        wl = vec(6 + d) + _dot(jnp.tanh(_dot(xw, w1_ref[d])).astype(BF16), w2_ref[d])
        lw_o[...] = -jax.nn.sigmoid(wl) * math.exp(-0.5)
        a = jax.nn.sigmoid(vec(8 + d) + _dot(_dot(xa, a1_ref[d]).astype(BF16), a2_ref[d]))
        kd = k * (1.0 + (a - 1.0) * vec(11))
        kd_o[...] = kd
        b_o[...] = kk * a
        bonus = bonus + _headsum(r * kd * vec(12), ones) * v
    bonus_o[...] = bonus
    g_o[...] = _dot(jax.nn.sigmoid(_dot(xg, g1_ref[...])).astype(BF16), g2_ref[...])


def _rwkv_post_kernel(y0_ref, y1_ref, bonus_ref, g_ref, res_ref, mod_ref, ln_ref, wo_ref, ones_ref, o_ref):
    ones = ones_ref[...]
    y = y0_ref[...] + y1_ref[...]
    yc = y - _headsum(y, ones) * (1.0 / RW_HEAD)
    var = _headsum(yc * yc, ones) * (1.0 / RW_HEAD)
    yn = yc * lax.rsqrt(var + RW_LN_EPS) * ln_ref[0:1, :] + ln_ref[1:2, :] + bonus_ref[...]
    out = _dot((yn * g_ref[...]).astype(BF16), wo_ref[...])
    o_ref[...] = res_ref[...] + mod_ref[...] * out


def _full(a):
    nd = a.ndim
    return pl.BlockSpec(a.shape, lambda b, i: (0,) * nd)


def _rwkv7_stream(h, shifted, state0, v_first, resid, mod_gate, mu, w_rkv, w_o, w0, w1, w2, a0, a1, a2,
                  g1, g2, k_k, k_a, r_k, ln_w, ln_b, vres, need_out=True):
    B, L, D = h.shape
    tm = min(RW_TM, L)
    vres_on = vres is not None
    rows = [mu[j] for j in range(6)] + [w0[0], w0[1], a0[0], a0[1], k_k, k_a, r_k.reshape(D)]
    rows.append(vres[0] if vres_on else jnp.zeros((D,), F32))
    vecs = jnp.stack(rows + [jnp.zeros((D,), F32)] * (16 - len(rows)))
    ones = jnp.kron(jnp.eye(LANES // RW_HEAD, dtype=F32), jnp.ones((RW_HEAD, RW_HEAD), F32)).astype(BF16)
    bf = lambda t: t.astype(BF16)
    tok = pl.BlockSpec((None, tm, D), lambda b, i: (b, i, 0))
    args = [h, shifted] + ([v_first] if vres_on else [])
    consts = [vecs, bf(w_rkv[0]), bf(w_rkv[1]), bf(w_rkv[2]), bf(w1), bf(w2), bf(a1), bf(a2), bf(g1), bf(g2)]
    consts += [bf(vres[1]), bf(vres[2])] if vres_on else []
    consts.append(ones)
    outs = pl.pallas_call(
        functools.partial(_rwkv_pre_kernel, vres=vres_on),
        grid=(B, L // tm),
        in_specs=[tok] * len(args) + [_full(c) for c in consts],
        out_specs=[tok] * 11,
        out_shape=[jax.ShapeDtypeStruct((B, L, D), F32)] * 11,
        compiler_params=_cparams(("parallel", "parallel")),
        name="rwkv_pre",
    )(*args, *consts)
    r, v, nkk, lw0, lw1, kd0, kd1, b0, b1, bonus, g = outs
    y0, s0 = _rwkv_scan(r, lw0, kd0, v, nkk, b0, state0[0], reverse=False)
    y1, s1 = _rwkv_scan(r, lw1, kd1, v, nkk, b1, state0[1], reverse=True)
    states = jnp.stack([s0, s1])
    if not need_out:
        return None, states, v
    ln = jnp.stack([ln_w, ln_b] + [jnp.zeros((D,), F32)] * 6)
    wo = bf(w_o)
    new_resid = pl.pallas_call(
        _rwkv_post_kernel,
        grid=(B, L // tm),
        in_specs=[tok] * 5 + [pl.BlockSpec((None, 1, D), lambda b, i: (b, 0, 0)), _full(ln), _full(wo), _full(ones)],
        out_specs=tok,
        out_shape=jax.ShapeDtypeStruct((B, L, D), F32),
        compiler_params=_cparams(("parallel", "parallel")),
        name="rwkv_post",
    )(y0, y1, bonus, g, resid, jnp.broadcast_to(mod_gate, (B, 1, D)), ln, wo, ones)
    return new_resid, states, v


def _mm_fused_kernel(*refs, norm, bias, resid):
    it = iter(refs)
    x_ref = next(it)
    vec_ref = next(it) if (norm or resid) else None
    w_ref = next(it)
    b_ref = next(it) if bias else None
    res_ref = next(it) if resid else None
    o_ref, xs_scr = next(it), next(it)

    @pl.when(pl.program_id(2) == 0)
    def _prep():
        x = x_ref[...]
        xs_scr[...] = (_norm_mod(x, vec_ref) if norm else x).astype(BF16)

    acc = _dot(xs_scr[...], w_ref[...])
    if bias:
        acc = acc + b_ref[...]
    o_ref[...] = res_ref[...] + vec_ref[2:3, :] * acc if resid else acc


def _mm_fused(x, w, vecs=None, norm=False, bias=None, resid=None):
    B, L, K = x.shape
    N = w.shape[1]
    tm = min(1024, L)
    tn = 512 if (N % 512 == 0 and resid is None) else N
    use_vec = norm or resid is not None
    specs = [pl.BlockSpec((None, tm, K), lambda b, i, j: (b, i, 0))]
    args = [x]
    if use_vec:
        specs.append(pl.BlockSpec((None, 8, vecs.shape[-1]), lambda b, i, j: (b, 0, 0)))
        args.append(vecs)
    specs.append(pl.BlockSpec((K, tn), lambda b, i, j: (0, j)))
    args.append(w.astype(BF16))
    if bias is not None:
        specs.append(pl.BlockSpec((1, tn), lambda b, i, j: (0, j)))
        args.append(bias[None, :])
    out_spec = pl.BlockSpec((None, tm, tn), lambda b, i, j: (b, i, j))
    if resid is not None:
        assert N == K == vecs.shape[-1] and tn == N
        specs.append(out_spec)
        args.append(resid)
    return pl.pallas_call(
        functools.partial(_mm_fused_kernel, norm=norm, bias=bias is not None, resid=resid is not None),
        grid=(B, L // tm, N // tn),
        in_specs=specs, out_specs=out_spec,
        out_shape=jax.ShapeDtypeStruct((B, L, N), F32),
        scratch_shapes=[pltpu.VMEM((tm, K), BF16)],
        compiler_params=_cparams(("parallel", "parallel", "arbitrary")),
        name="mm_fused",
    )(*args)


def _hgrn_post_kernel(o0_ref, o1_ref, g_ref, res_ref, vec_ref, gn_ref, wo_ref, ones_ref, o_ref):
    o = o0_ref[...] + o1_ref[...]
    ms = _headsum(o * o, ones_ref[...]) * (1.0 / HG_DK)
    on = o * lax.rsqrt(ms + NORM_EPS) * gn_ref[...]
    g = g_ref[...]
    z = (on * (g * jax.nn.sigmoid(g))).astype(BF16)
    o_ref[...] = res_ref[...] + vec_ref[2:3, :] * _dot(z, wo_ref[...])


def _hgrn2_stream(resid, vecs, state0, lb, w_in, gn, w_o, need_out=True):
    B, L, D = resid.shape
    proj = _mm_fused(resid, w_in, vecs=vecs, norm=True)
    o0, s0 = _gla_scan(proj, lb, state0[0], 0)
    o1, s1 = _gla_scan(proj, lb, state0[1], 1)
    states = jnp.stack([s0, s1])
    if not need_out:
        return None, states
    tm = min(RW_TM, L)
    tok = pl.BlockSpec((None, tm, D), lambda b, i: (b, i, 0))
    ones = jnp.ones((LANES, LANES), BF16)
    gn_row = jnp.tile(gn, D // gn.shape[0])[None, :]
    wo = w_o.astype(BF16)
    out = pl.pallas_call(
        _hgrn_post_kernel,
        grid=(B, L // tm),
        in_specs=[tok, tok, pl.BlockSpec((None, tm, D), lambda b, i: (b, i, 4)), tok,
                  pl.BlockSpec((None, 8, D), lambda b, i: (b, 0, 0)), _full(gn_row), _full(wo), _full(ones)],
        out_specs=tok,
        out_shape=jax.ShapeDtypeStruct((B, L, D), F32),
        compiler_params=_cparams(("parallel", "parallel")),
        name="hgrn_post",
    )(o0, o1, proj, resid, vecs, gn_row, wo, ones)
    return out, states


def kernel(x, c, ctx, c_ctx, norm_g, ada_w, ada_b, final_g,
           rw_mu, rw_wrkv, rw_wo, rw_w0, rw_w1, rw_w2, rw_a0, rw_a1, rw_a2,
           rw_v0, rw_v1, rw_v2, rw_g1, rw_g2, rw_kk, rw_ka, rw_rk, rw_lnw, rw_lnb,
           hy_win, hy_bin, hy_cw, hy_cb, hy_fw1, hy_fb1, hy_fw2, hy_fb2, hy_fw3, hy_fb3,
           hy_fw4, hy_freq, hy_skip, hy_wo, hy_bo,
           hg_win, hg_lb, hg_gn, hg_wo,
           ffn_w13, ffn_w2, moe_router, moe_w13, moe_w2):
    B = x.shape[0]
    depth = norm_g.shape[0]
    D = D_MODEL
    lat, cx = x, ctx
    v_first = None
    lbc = jnp.cumsum(jax.nn.softmax(hg_lb, axis=0), axis=0)
    lower = lbc - lbc[:1]
    cond = jnp.concatenate([jax.nn.silu(c), jax.nn.silu(c_ctx)[None, :]], axis=0)
    ffn_w13b, ffn_w2b = ffn_w13.astype(BF16), ffn_w2.astype(BF16)
    moe_w13b, moe_w2b = moe_w13.astype(BF16), moe_w2.astype(BF16)
    for i in range(depth):
        last = i == depth - 1
        mod = _mm(cond, ada_w[i]) + ada_b[i]
        mod_l = jnp.split(mod[:B, None, :], 6, axis=-1)
        mod_c = jnp.split(mod[B:, None, :], 6, axis=-1)
        kind, slot = i % N_MIXERS, i // N_MIXERS
        vec_l = _mod_rows(norm_g[i, 0], mod_l[0], mod_l[1], mod_l[2], B)
        vec_c = _mod_rows(norm_g[i, 0], mod_c[0], mod_c[1], mod_c[2], B)
        if kind == 0:
            hl = _rmsnorm(lat, norm_g[i, 0]) * (1.0 + mod_l[1]) + mod_l[0]
            hc = _rmsnorm(cx, norm_g[i, 0]) * (1.0 + mod_c[1]) + mod_c[0]
            vres = None if slot == 0 else (rw_v0[slot - 1], rw_v1[slot - 1], rw_v2[slot - 1])
            rw = (rw_mu[slot], rw_wrkv[slot], rw_wo[slot], rw_w0[slot], rw_w1[slot], rw_w2[slot],
                  rw_a0[slot], rw_a1[slot], rw_a2[slot], rw_g1[slot], rw_g2[slot], rw_kk[slot],
                  rw_ka[slot], rw_rk[slot], rw_lnw[slot], rw_lnb[slot], vres)
            zero = jnp.zeros((2, B, RW_H, RW_HEAD, RW_HEAD), F32)
            vf_c = None if v_first is None else v_first[0]
            vf_l = None if v_first is None else v_first[1]
            cx_new, s_ctx, v_c = _rwkv7_stream(hc, _shift_seq(hc), zero, vf_c, cx, mod_c[2], *rw,
                                               need_out=not last)
            lat, _, v_l = _rwkv7_stream(hl, _shift_grid(hl), s_ctx, vf_l, lat, mod_l[2], *rw)
            if not last:
                cx = cx_new
            if slot == 0:
                v_first = (v_c, v_l)
        elif kind == 1:
            fprm = (hy_fw1[slot], hy_fb1[slot], hy_fw2[slot], hy_fb2[slot], hy_fw3[slot],
                    hy_fb3[slot], hy_fw4[slot], hy_freq[slot])
            hprm = (hy_win[slot], hy_bin[slot], hy_cw[slot], hy_cb[slot], hy_skip[slot],
                    hy_wo[slot], hy_bo[slot])
            lat = _hyena_stream(lat, vec_l, fprm, *hprm)
            if not last:
                cx = _hyena_stream(cx, vec_c, fprm, *hprm)
        else:
            zero = jnp.zeros((2, B, HG_H, HG_DK, D // HG_H), F32)
            gprm = (lower[i], hg_win[slot], hg_gn[slot], hg_wo[slot])
            cx_new, s_ctx = _hgrn2_stream(cx, vec_c, zero, *gprm, need_out=not last)
            lat, _ = _hgrn2_stream(lat, vec_l, s_ctx, *gprm)
            if not last:
                cx = cx_new

        def channel_mix(s, mod):
            vecs = _mod_rows(norm_g[i, 1], mod[3], mod[4], mod[5], B)
            if i % 2 == 0:
                return _ffn(s, vecs, ffn_w13b[i // 2], ffn_w2b[i // 2])
            xn, logits = _norm_route(s, vecs, moe_router[i // 2])
            out = _moe(xn.reshape(-1, D), logits.reshape(-1, N_EXPERTS), moe_w13b[i // 2], moe_w2b[i // 2])
            return s + mod[5] * out.reshape(s.shape)

        lat = channel_mix(lat, mod_l)
        if not last:
            cx = channel_mix(cx, mod_c)
    return _rmsnorm(lat, final_g)
```

```python
import functools
import math

import jax
import jax.numpy as jnp
from jax import lax
from jax.experimental import pallas as pl
from jax.experimental.pallas import tpu as pltpu

F32 = jnp.float32
BF16 = jnp.bfloat16
HIGHEST = lax.Precision.HIGHEST

D_MODEL = 1024
GRID_W = 64
NORM_EPS = 1e-6
RW_HEAD = 64
RW_H = D_MODEL // RW_HEAD
RW_LN_EPS = 64e-5
HY_ORDER = 2
HY_EMB = 33
HY_BANDS = (HY_EMB - 1) // 2
HY_FAST_DECAY = 0.3
HY_SLOW_DECAY = 1.5
HY_TARGET = 1e-2
HG_DK = 128
HG_H = D_MODEL // HG_DK
HG_CHUNK = 32
N_EXPERTS = 8
TOP_K = 2
N_MIXERS = 3

LANES = 128
VMEM_LIMIT = 56 * 1024 * 1024
RW_CHUNK = 64
SCAN_BLOCK = 256
HY_N2 = 128


def _cparams(sem):
    return pltpu.CompilerParams(dimension_semantics=sem, vmem_limit_bytes=VMEM_LIMIT)


def _dot(a, b, prec=None):
    return jnp.dot(a, b, preferred_element_type=F32, precision=prec)


def _dot_nt(a, b, prec=None):
    return lax.dot_general(a, b, (((1,), (1,)), ((), ())), preferred_element_type=F32, precision=prec)


def _split2(x):
    hi = x.astype(BF16)
    lo = (x - hi.astype(F32)).astype(BF16)
    return hi, lo


def _pdot(a, b, mode, nt=False):
    dn = (((1,), (1 if nt else 0,)), ((), ()))
    dg = lambda x, y, p=None: lax.dot_general(x, y, dn, preferred_element_type=F32, precision=p)
    if mode == 'f32':
        return dg(a, b, HIGHEST)
    if mode == 'bf16':
        return dg(a.astype(BF16), b.astype(BF16))
    ah, al = _split2(a)
    bh, bl = _split2(b)
    return dg(ah, bh) + (dg(ah, bl) + dg(al, bh))


def _cumsum_dot(tri, x):
    hi = x.astype(BF16)
    r1 = x - hi.astype(F32)
    mid = r1.astype(BF16)
    lo = (r1 - mid.astype(F32)).astype(BF16)
    n = x.shape[1]
    g = _dot(tri, jnp.concatenate([hi, mid, lo], axis=1))
    return g[:, :n] + (g[:, n:2 * n] + g[:, 2 * n:])


def _mm_kernel(a_ref, b_ref, o_ref, *, prec):
    o_ref[...] = _dot(a_ref[...], b_ref[...], prec)


def _mm(a, b, *, exact=False):
    M, K = a.shape
    N = b.shape[1]
    dt = F32 if exact else BF16
    a = a.astype(dt)
    b = b.astype(dt)
    Mp = -(-M // 8) * 8
    if Mp != M:
        a = jnp.pad(a, ((0, Mp - M), (0, 0)))
    tm = 512 if Mp % 512 == 0 else (256 if Mp % 256 == 0 else Mp)
    tn = 512 if N % 512 == 0 else N
    out = pl.pallas_call(
        functools.partial(_mm_kernel, prec=HIGHEST if exact else None),
        grid=(Mp // tm, N // tn),
        in_specs=[pl.BlockSpec((tm, K), lambda i, j: (i, 0)),
                  pl.BlockSpec((K, tn), lambda i, j: (0, j))],
        out_specs=pl.BlockSpec((tm, tn), lambda i, j: (i, j)),
        out_shape=jax.ShapeDtypeStruct((Mp, N), F32),
        compiler_params=_cparams(("parallel", "parallel")),
        name="mm",
    )(a, b)
    return out[:M] if Mp != M else out


def _mm3(x, w, **kw):
    B, L, K = x.shape
    return _mm(x.reshape(B * L, K), w, **kw).reshape(B, L, w.shape[1])


def _norm_mod(x, vec_ref):
    xn = x * lax.rsqrt(jnp.mean(x * x, axis=-1, keepdims=True) + NORM_EPS)
    return xn * vec_ref[0:1, :] + vec_ref[1:2, :]


def _mod_rows(g, shift, scale, gate, B):
    D = g.shape[-1]
    rows = [jnp.broadcast_to(g * (1.0 + scale[:, 0]), (B, D)), jnp.broadcast_to(shift[:, 0], (B, D)),
            jnp.broadcast_to(gate[:, 0], (B, D))]
    return jnp.stack(rows + [jnp.zeros((B, D), F32)] * 5, axis=1)


def _ffn_kernel(x_ref, vec_ref, wg_ref, wu_ref, w2_ref, o_ref, xn_scr, acc_ref, *, n_f):
    f = pl.program_id(2)

    @pl.when(f == 0)
    def _first():
        xn_scr[...] = _norm_mod(x_ref[...], vec_ref).astype(BF16)
        acc_ref[...] = jnp.zeros_like(acc_ref)

    x = xn_scr[...]
    gate = _dot(x, wg_ref[...])
    up = _dot(x, wu_ref[...])
    h = (gate * jax.nn.sigmoid(gate) * up).astype(BF16)
    acc_ref[...] += _dot(h, w2_ref[...])

    @pl.when(f == n_f - 1)
    def _store():
        o_ref[...] = x_ref[...] + vec_ref[2:3, :] * acc_ref[...]


def _ffn(x, vecs, w13, w2):
    B, L, D = x.shape
    F = w13.shape[1] // 2
    tm = min(1024, L)
    tf = 512 if F % 512 == 0 else 256
    n_f = F // tf
    tok = pl.BlockSpec((None, tm, D), lambda b, i, f: (b, i, 0))
    return pl.pallas_call(
        functools.partial(_ffn_kernel, n_f=n_f),
        grid=(B, L // tm, n_f),
        in_specs=[tok,
                  pl.BlockSpec((None, 8, D), lambda b, i, f: (b, 0, 0)),
                  pl.BlockSpec((D, tf), lambda b, i, f: (0, f)),
                  pl.BlockSpec((D, tf), lambda b, i, f: (0, f + n_f)),
                  pl.BlockSpec((tf, D), lambda b, i, f: (f, 0))],
        out_specs=tok,
        out_shape=jax.ShapeDtypeStruct((B, L, D), F32),
        scratch_shapes=[pltpu.VMEM((tm, D), BF16), pltpu.VMEM((tm, D), F32)],
        compiler_params=_cparams(("parallel", "parallel", "arbitrary")),
        name="ffn",
    )(x, vecs, w13, w13, w2)


def _norm_route_kernel(x_ref, vec_ref, wr_ref, xn_ref, lg_ref):
    xn = _norm_mod(x_ref[...], vec_ref)
    xn_ref[...] = xn.astype(BF16)
    lg_ref[...] = _dot(xn, wr_ref[...], HIGHEST)


def _norm_route(x, vecs, w_router):
    B, L, D = x.shape
    E = w_router.shape[1]
    tm = min(512, L)
    tok = pl.BlockSpec((None, tm, D), lambda b, i: (b, i, 0))
    return pl.pallas_call(
        _norm_route_kernel,
        grid=(B, L // tm),
        in_specs=[tok, pl.BlockSpec((None, 8, D), lambda b, i: (b, 0, 0)), pl.BlockSpec((D, E), lambda b, i: (0, 0))],
        out_specs=[tok, pl.BlockSpec((None, tm, E), lambda b, i: (b, i, 0))],
        out_shape=[jax.ShapeDtypeStruct((B, L, D), BF16), jax.ShapeDtypeStruct((B, L, E), F32)],
        compiler_params=_cparams(("parallel", "parallel")),
        name="norm_route",
    )(x, vecs, w_router)


MOE_TILE = 2048
MOE_SUB = 256
MOE_SLAB = 512


def _moe_kernel(cnt_ref, x_ref, rrow_ref, rcol_ref, gcol_ref, wg_ref, wu_ref, w2_ref, o_ref, xc_scr, y_scr,
                *, n_f, tm):
    i, e, f = pl.program_id(0), pl.program_id(1), pl.program_id(2)
    sub = MOE_SUB
    slab = min(MOE_SLAB, tm)
    nslab = tm // slab
    n_sub = (cnt_ref[i, e, nslab] + (sub - 1)) // sub

    def touches(s, j):
        return (cnt_ref[i, e, j] < (s + 1) * sub) & (cnt_ref[i, e, j + 1] > s * sub)

    @pl.when((e == 0) & (f == 0))
    def _zero():
        o_ref[...] = jnp.zeros_like(o_ref)

    @pl.when(f == 0)
    def _compact():
        def body(s, carry):
            rows = pl.ds(pl.multiple_of(s * sub, sub), sub)
            y_scr[rows, :] = jnp.zeros((sub, y_scr.shape[1]), F32)
            for j in range(nslab):
                cols = slice(j * slab, (j + 1) * slab)

                @pl.when(touches(s, j))
                def _add():
                    ridx = lax.broadcasted_iota(jnp.int32, (sub, slab), 0) + s * sub
                    onehot = jnp.where(rrow_ref[:, cols] == ridx, 1.0, 0.0).astype(BF16)
                    y_scr[rows, :] += _dot(onehot, x_ref[cols, :])
            xc_scr[rows, :] = y_scr[rows, :].astype(BF16)
            return carry
        lax.fori_loop(0, n_sub, body, 0)

    def expert(s, first):
        rows = pl.ds(pl.multiple_of(s * sub, sub), sub)
        xs = xc_scr[rows, :]
        gate = _dot(xs, wg_ref[...])
        up = _dot(xs, wu_ref[...])
        h = (gate * jax.nn.sigmoid(gate) * up).astype(BF16)
        part = _dot(h, w2_ref[...])
        y_scr[rows, :] = part if first else y_scr[rows, :] + part

    @pl.when(f == 0)
    def _first():
        lax.fori_loop(0, n_sub, lambda s, c: (expert(s, True), c)[1], 0)

    @pl.when(f > 0)
    def _rest():
        lax.fori_loop(0, n_sub, lambda s, c: (expert(s, False), c)[1], 0)

    @pl.when(f == n_f - 1)
    def _scatter():
        def body(s, carry):
            y = y_scr[pl.ds(pl.multiple_of(s * sub, sub), sub), :].astype(BF16)
            for j in range(nslab):
                rows = slice(j * slab, (j + 1) * slab)

                @pl.when(touches(s, j))
                def _add():
                    cidx = lax.broadcasted_iota(jnp.int32, (slab, sub), 1) + s * sub
                    onehot_t = jnp.where(rcol_ref[rows, :] == cidx, 1.0, 0.0).astype(BF16)
                    o_ref[rows, :] += gcol_ref[rows, :] * _dot(onehot_t, y)
            return carry
        lax.fori_loop(0, n_sub, body, 0)


def _moe(x, logits, w13, w2):
    T, D = x.shape
    E, _, F2 = w13.shape
    F = F2 // 2
    tm = min(MOE_TILE, T)
    nt = T // tm
    tf = 512
    n_f = F // tf
    top_val, top_idx = lax.top_k(logits, TOP_K)
    gates = jax.nn.softmax(top_val, axis=-1)
    onehot = jax.nn.one_hot(top_idx, E, dtype=F32)
    sel = jnp.sum(onehot, axis=1).astype(jnp.int32).reshape(nt, tm, E)
    gate_dense = jnp.sum(onehot * gates[..., None], axis=1).reshape(nt, tm, E)
    rank = jnp.where(sel > 0, jnp.cumsum(sel, axis=1) - sel, -1)
    rank = jnp.swapaxes(rank, 1, 2)
    slab = min(MOE_SLAB, tm)
    per_slab = jnp.sum(sel.reshape(nt, tm // slab, slab, E), axis=2)
    counts = jnp.concatenate([jnp.zeros((nt, 1, E), jnp.int32), jnp.cumsum(per_slab, axis=1)], axis=1)
    counts = jnp.swapaxes(counts, 1, 2)
    gcol = jnp.swapaxes(gate_dense, 1, 2)[..., None]
    tile = lambda shape, imap: pl.BlockSpec(shape, imap)
    return pl.pallas_call(
        functools.partial(_moe_kernel, n_f=n_f, tm=tm),
        grid_spec=pltpu.PrefetchScalarGridSpec(
            num_scalar_prefetch=1,
            grid=(nt, E, n_f),
            in_specs=[tile((tm, D), lambda i, e, f, c: (i, 0)),
                      tile((None, None, 1, tm), lambda i, e, f, c: (i, e, 0, 0)),
                      tile((None, None, tm, 1), lambda i, e, f, c: (i, e, 0, 0)),
                      tile((None, None, tm, 1), lambda i, e, f, c: (i, e, 0, 0)),
                      tile((None, D, tf), lambda i, e, f, c: (e, 0, f)),
                      tile((None, D, tf), lambda i, e, f, c: (e, 0, f + n_f)),
                      tile((None, tf, D), lambda i, e, f, c: (e, f, 0))],
            out_specs=tile((tm, D), lambda i, e, f, c: (i, 0)),
            scratch_shapes=[pltpu.VMEM((tm, D), BF16), pltpu.VMEM((tm, D), F32)]),
        out_shape=jax.ShapeDtypeStruct((T, D), F32),
        compiler_params=_cparams(("parallel", "arbitrary", "arbitrary")),
        name="moe",
    )(counts, x, rank[:, :, None, :], rank[..., None], gcol, w13, w13, w2)


def _rwkv_scan_kernel(r_ref, lw_ref, k_ref, v_ref, a_ref, b_ref, h0_ref, y_ref, hT_ref, H_scr,
                      *, reverse, nblk, nchunk, npl, modes):
    C = RW_CHUNK
    half = LANES // 2
    m_a, m_inv, m_u, m_st = modes
    i = pl.program_id(2)

    @pl.when(i == 0)
    def _init():
        H_scr[...] = h0_ref[...]

    t_idx = lax.broadcasted_iota(jnp.int32, (C, LANES), 0)
    s_idx = lax.broadcasted_iota(jnp.int32, (C, LANES), 1) & (half - 1)
    tt = lax.broadcasted_iota(jnp.int32, (C, C), 0)
    ss = lax.broadcasted_iota(jnp.int32, (C, C), 1)
    if reverse:
        strict, incl, tri = s_idx > t_idx, s_idx >= t_idx, (ss >= tt).astype(BF16)
    else:
        strict, incl, tri = s_idx < t_idx, s_idx <= t_idx, (ss <= tt).astype(BF16)
    eye_lp = (s_idx == t_idx).astype(F32)
    lane = lax.broadcasted_iota(jnp.int32, (1, LANES), 1)
    m0 = (lane < half).astype(F32)
    m1 = 1.0 - m0
    rr = lax.broadcasted_iota(jnp.int32, (LANES, LANES), 0)
    cc = lax.broadcasted_iota(jnp.int32, (LANES, LANES), 1)
    mask_bd = ((rr < half) == (cc < half)).astype(F32)

    def bd(x):
        return jnp.concatenate([x * m0, x * m1], axis=0)

    order = list(range(nchunk - 1, -1, -1) if reverse else range(nchunk))
    units = [(slice(c * C, (c + 1) * C), slice(q * LANES, (q + 1) * LANES)) for c in order for q in range(npl)]
    ld = lambda ref: [ref[sl, ln] for sl, ln in units]
    r, lw, k, v, a, b = ld(r_ref), ld(lw_ref), ld(k_ref), ld(v_ref), ld(a_ref), ld(b_ref)
    G = [_cumsum_dot(tri, x) for x in lw]
    eG = [jnp.exp(g) for g in G]
    eGn = [jnp.exp(-g) for g in G]
    rt = [x * e for x, e in zip(r, eG)]
    at = [x * jnp.exp(g - l) for x, g, l in zip(a, G, lw)]
    kt = [x * e for x, e in zip(k, eGn)]
    bt = [x * e for x, e in zip(b, eGn)]
    M = [_pdot(jnp.concatenate([x, y], axis=0), jnp.concatenate([bd(z), bd(w)], axis=0), m_a, nt=True)
         for x, y, z, w in zip(at, rt, bt, kt)]
    Nm = [jnp.where(strict, m[:C, :LANES], 0.0) for m in M]
    Aak = [jnp.where(strict, m[:C, LANES:], 0.0) for m in M]
    Arb = [jnp.where(incl, m[C:, :LANES], 0.0) for m in M]
    Ark = [jnp.where(incl, m[C:, LANES:], 0.0) for m in M]
    T = None
    for lvl in range(1, int(math.log2(C)) + 1):
        same = (t_idx >> lvl) == (s_idx >> lvl)
        t_hi = ((t_idx >> (lvl - 1)) & 1) == 1
        s_hi = ((s_idx >> (lvl - 1)) & 1) == 1
        off = same & (s_hi & ~t_hi if reverse else t_hi & ~s_hi)
        Noff = [jnp.where(off, n, 0.0) for n in Nm]
        if T is None:
            T = [eye_lp + n for n in Noff]
        else:
            DN = [_pdot(t, bd(n), m_inv) for t, n in zip(T, Noff)]
            T = [t + _pdot(dn, bd(t), m_inv) for t, dn in zip(T, DN)]
    bdv = [bd(x) for x in v]
    X0 = [_pdot(x, y, m_st) for x, y in zip(Aak, bdv)]
    WU = [_pdot(t, jnp.concatenate([bd(x), bd(y)], axis=1), m_u) for t, x, y in zip(T, at, X0)]
    W = [x[:, :LANES] for x in WU]
    U0 = [x[:, LANES:] for x in WU]
    RY = [_pdot(x, jnp.concatenate([bd(y), bd(z)], axis=1), m_st) for x, y, z in zip(Arb, W, U0)]
    Rh = [x + y[:, :LANES] for x, y in zip(rt, RY)]
    Y0 = [y[:, LANES:] + _pdot(x, z, m_st) for y, x, z in zip(RY, Ark, bdv)]
    dPhi = [mask_bd * _pdot(x.T, y, m_st) for x, y in zip(W, bt)]
    Psi = [mask_bd * _pdot(jnp.concatenate([x, y], axis=0).T, jnp.concatenate([z, w], axis=0), m_st)
           for x, y, z, w in zip(U0, v, bt, kt)]
    e_end = [jnp.exp(g[0:1, :] if reverse else g[C - 1:C, :]) for g in G]

    Hs = [H_scr[q] for q in range(npl)]
    for u, (sl, ln) in enumerate(units):
        q = u % npl
        H = Hs[q]
        y_ref[sl, ln] = _pdot(Rh[u], H, m_st, nt=True) + Y0[u]
        Hs[q] = (H + _pdot(H, dPhi[u], m_st) + Psi[u]) * e_end[u]
    for q in range(npl):
        H_scr[q] = Hs[q]

    @pl.when(i == nblk - 1)
    def _fin():
        for q in range(npl):
            hT_ref[q] = Hs[q]


def _pair_states(S):
    B, H, N, _ = S.shape
    S5 = S.reshape(B, H // 2, 2, N, N)
    eye = jnp.eye(2, dtype=S.dtype)
    return jnp.einsum('bpivk,ij->bpivjk', S5, eye).reshape(B, H // 2, 2 * N, 2 * N)


def _unpair_states(Sp):
    B, P, N2, _ = Sp.shape
    N = N2 // 2
    S6 = Sp.reshape(B, P, 2, N, 2, N)
    return jnp.stack([S6[:, :, 0, :, 0, :], S6[:, :, 1, :, 1, :]], axis=2).reshape(B, 2 * P, N, N)


RW_MODES = ('bf16', 'bf16', 'bf16', 'bf16')
RW_PAIRS_PER_STEP = 2
RW_BLOCK = 512


def _rwkv_scan(r, lw, k, v, a, b, state0, reverse, modes=RW_MODES):
    B, L, D = r.shape
    bt = min(RW_BLOCK, L)
    nblk = L // bt
    npl = RW_PAIRS_PER_STEP
    npair = D // LANES
    blk = (lambda i: nblk - 1 - i) if reverse else (lambda i: i)
    tok = pl.BlockSpec((None, bt, npl * LANES), lambda bb, p, i: (bb, blk(i), p))
    st = pl.BlockSpec((None, npl, LANES, LANES), lambda bb, p, i: (bb, p, 0, 0))
    y, hT = pl.pallas_call(
        functools.partial(_rwkv_scan_kernel, reverse=reverse, nblk=nblk, nchunk=bt // RW_CHUNK, npl=npl,
                          modes=modes),
        grid=(B, npair // npl, nblk),
        in_specs=[tok] * 6 + [st],
        out_specs=[tok, st],
        out_shape=[jax.ShapeDtypeStruct((B, L, D), F32),
                   jax.ShapeDtypeStruct((B, npair, LANES, LANES), F32)],
        scratch_shapes=[pltpu.VMEM((npl, LANES, LANES), F32)],
        compiler_params=_cparams(("parallel", "parallel", "arbitrary")),
        name="rwkv_scan_rev" if reverse else "rwkv_scan_fwd",
    )(r, lw, k, v, a, b, _pair_states(state0))
    return y, _unpair_states(hT)


GLA_HEADS_PER_STEP = 2


def _gla_kernel(q_ref, f_ref, v_ref, lb_ref, h0_ref, o_ref, hT_ref, H_scr, *, reverse, nblk, npl):
    C = HG_CHUNK
    sh = int(math.log2(C))
    bt = q_ref.shape[0]
    nchunk = bt // C
    i = pl.program_id(2)

    @pl.when(i == 0)
    def _init():
        H_scr[...] = h0_ref[...]

    tt = lax.broadcasted_iota(jnp.int32, (bt, bt), 0)
    ss = lax.broadcasted_iota(jnp.int32, (bt, bt), 1)
    same = (tt >> sh) == (ss >> sh)
    mid = ((tt >> sh) << sh) + (C // 2 if reverse else C // 2 - 1)
    if reverse:
        incl, upto_mid = same & (ss >= tt), same & (ss >= mid)
    else:
        incl, upto_mid = same & (ss <= tt), same & (ss <= mid)
    one = lambda m: jnp.where(m, 1.0, 0.0)
    sums = jnp.concatenate([one(incl), one(incl) - one(upto_mid), one(same) - one(incl)], axis=0).astype(BF16)

    order = list(range(nchunk - 1, -1, -1) if reverse else range(nchunk))
    heads = [slice(p * LANES, (p + 1) * LANES) for p in range(npl)]
    q = [jax.nn.silu(q_ref[:, ln]) for ln in heads]
    fg = [lb_ref[:, ln] + (1.0 - lb_ref[:, ln]) * jax.nn.sigmoid(f_ref[:, ln]) for ln in heads]
    v = [v_ref[:, ln] for ln in heads]
    k = [1.0 - f for f in fg]
    Gs = [_cumsum_dot(sums, jnp.log(f)) for f in fg]
    Gabs = [g[:bt] for g in Gs]
    Grel = [g[bt:2 * bt] for g in Gs]
    Gend = [g[2 * bt:] for g in Gs]
    scores = [jnp.where(incl, _pdot(x * jnp.exp(g), y * jnp.exp(-g), 'bf16', nt=True), 0.0)
              for x, y, g in zip(q, k, Grel)]
    o_intra = [_pdot(s, x, 'bf16') for s, x in zip(scores, v)]
    qa = [x * jnp.exp(g) for x, g in zip(q, Gabs)]
    kend = [x * jnp.exp(g) for x, g in zip(k, Gend)]
    rows = [slice(c * C, (c + 1) * C) for c in range(nchunk)]
    KV = [[_pdot(x[r].T, y[r], 'bf16') for r in rows] for x, y in zip(v, kend)]
    dec = [[jnp.exp(ga[r][0:1, :] + ge[r][0:1, :]) for r in rows] for ga, ge in zip(Gabs, Gend)]

    Hs = [H_scr[p] for p in range(npl)]
    for c in order:
        for p in range(npl):
            o_ref[rows[c], heads[p]] = o_intra[p][rows[c]] + _pdot(qa[p][rows[c]], Hs[p], 'bf16', nt=True)
            Hs[p] = Hs[p] * dec[p][c] + KV[p][c]
    for p in range(npl):
        H_scr[p] = Hs[p]

    @pl.when(i == nblk - 1)
    def _fin():
        for p in range(npl):
            hT_ref[p] = Hs[p]


def _gla_scan(proj, lb, state0, d):
    B, L, D5 = proj.shape
    D = D5 // 5
    reverse = d == 1
    bt = SCAN_BLOCK
    nblk = L // bt
    npl = GLA_HEADS_PER_STEP
    nh = D // LANES
    ncb = nh // npl
    blk = (lambda i: nblk - 1 - i) if reverse else (lambda i: i)
    col = lambda off: pl.BlockSpec((None, bt, npl * LANES), lambda bb, p, i: (bb, blk(i), off * ncb + p))
    st = pl.BlockSpec((None, npl, LANES, LANES), lambda bb, p, i: (bb, p, 0, 0))
    o, hT = pl.pallas_call(
        functools.partial(_gla_kernel, reverse=reverse, nblk=nblk, npl=npl),
        grid=(B, ncb, nblk),
        in_specs=[col(0), col(1 + d), col(3), pl.BlockSpec((1, npl * LANES), lambda bb, p, i: (0, p)), st],
        out_specs=[col(0), st],
        out_shape=[jax.ShapeDtypeStruct((B, L, D), F32),
                   jax.ShapeDtypeStruct((B, nh, LANES, LANES), F32)],
        scratch_shapes=[pltpu.VMEM((npl, LANES, LANES), F32)],
        compiler_params=_cparams(("parallel", "parallel", "arbitrary")),
        name="gla_scan_rev" if reverse else "gla_scan_fwd",
    )(proj, proj, proj, lb[d][None, :], jnp.swapaxes(state0, 2, 3))
    return o, jnp.swapaxes(hT, 2, 3)


def _x3dot(fh, fl, x):
    xh, xl = _split2(x)
    return _dot(fh, xh) + (_dot(fh, xl) + _dot(fl, xh))


def _leftmul_kernel(fh_ref, fl_ref, x_ref, o_ref, xs_scr, *, nj):
    for j in range(nj):
        xs_scr[...] = x_ref[:, j, :]
        o_ref[:, j, :] = _x3dot(fh_ref[...], fl_ref[...], xs_scr[...])


def _leftmul_gate_kernel(fh_ref, fl_ref, x_ref, u_ref, s_ref, g_ref, o_ref, xs_scr, *, nj):
    for j in range(nj):
        xs_scr[...] = x_ref[:, j, :]
        y = _x3dot(fh_ref[...], fl_ref[...], xs_scr[...])
        o_ref[:, j, :] = (y + u_ref[:, j, :] * s_ref[...]) * g_ref[:, j, :]


DFT_TD = 512


def _leftmul(f, x, xoff=0, epilogue=None, D=D_MODEL):
    B, K, J, _ = x.shape
    M = f.shape[0]
    tj = 8 if J % 8 == 0 else J
    td = DFT_TD
    fh, fl = _split2(f)
    fs = pl.BlockSpec((M, K), lambda b, j, d: (0, 0))
    col = lambda rows, off: pl.BlockSpec((None, rows, tj, td), lambda b, j, d: (b, 0, j, off + d))
    if epilogue is None:
        kern, specs, args = _leftmul_kernel, [fs, fs, col(K, xoff)], (fh, fl, x)
    else:
        u, uoff, skip, gate, goff = epilogue
        kern = _leftmul_gate_kernel
        specs = [fs, fs, col(K, xoff), col(M, uoff), pl.BlockSpec((1, td), lambda b, j, d: (0, d)), col(M, goff)]
        args = (fh, fl, x, u, skip, gate)
    return pl.pallas_call(
        functools.partial(kern, nj=tj), grid=(B, J // tj, D // td), in_specs=specs, out_specs=col(M, 0),
        out_shape=jax.ShapeDtypeStruct((B, M, J, D), F32),
        scratch_shapes=[pltpu.VMEM((K, td), F32)],
        compiler_params=_cparams(("parallel", "parallel", "parallel")),
        name="dft_outer",
    )(*args)


def _spectral_fwd_kernel(fh_ref, fl_ref, y_ref, o_ref):
    o_ref[...] = _x3dot(fh_ref[...], fl_ref[...], y_ref[...])


def _spectral_conv_kernel(fh_ref, fl_ref, fih_ref, fil_ref, hf_ref, hb_ref, y_ref, o_ref):
    n2 = hf_ref.shape[0] // 2
    z = _x3dot(fh_ref[...], fl_ref[...], y_ref[...])
    zr, zi = z[:n2], z[n2:]
    kf = _combine_filter_spectrum(hf_ref[...], hb_ref[...])
    kr, ki = kf[:n2], kf[n2:]
    p = jnp.concatenate([zr * kr - zi * ki, zr * ki + zi * kr], axis=0)
    o_ref[...] = _x3dot(fih_ref[...], fil_ref[...], p)


def _spectral_mid(y, f_fwd, f_inv=None, spec=None, order=0):
    B, K1, R, D = y.shape
    ms = pl.BlockSpec((None, R, R), lambda k1, b: (k1, 0, 0))
    ys = pl.BlockSpec((None, None, R, D), lambda k1, b: (b, k1, 0, 0))
    if spec is None:
        kern, specs, args = _spectral_fwd_kernel, [ms, ms, ys], (*_split2(f_fwd), y)
    else:
        kern = _spectral_conv_kernel
        grp = lambda g: pl.BlockSpec((None, R, D), lambda k1, b: (k1, 0, g))
        specs = [ms, ms, ms, ms, grp(2 * order), grp(2 * order + 1), ys]
        args = (*_split2(f_fwd), *_split2(f_inv), spec, spec, y)
    return pl.pallas_call(
        kern, grid=(K1, B), in_specs=specs, out_specs=ys,
        out_shape=jax.ShapeDtypeStruct((B, K1, R, D), F32),
        compiler_params=_cparams(("parallel", "parallel")),
        name="dft_inner",
    )(*args)


def _dft_tables(L):
    n2 = HY_N2 if L % HY_N2 == 0 and L > HY_N2 * 2 else 1
    N = 2 * L
    n1 = N // n2
    k1h = n1 // 2
    ph = (jnp.arange(k1h, dtype=jnp.int32)[:, None] * 2 + 1) * jnp.arange(k1h, dtype=jnp.int32)[None, :]
    th = (ph % (2 * n1)).astype(F32) * (math.pi / n1)
    f_out = jnp.stack([jnp.cos(th), -jnp.sin(th)], axis=1).reshape(2 * k1h, k1h)
    f_out_inv = (2.0 / N) * f_out.T
    if n2 == 1:
        return n2, f_out, f_out_inv, None, None
    kk = jnp.arange(k1h, dtype=jnp.int32)[:, None, None] + n1 * jnp.arange(n2, dtype=jnp.int32)[None, :, None]
    ph = ((2 * kk + 1) * jnp.arange(n2, dtype=jnp.int32)[None, None, :]) % (2 * N)
    phi = ph.astype(F32) * (math.pi / N)
    c, s = jnp.cos(phi), jnp.sin(phi)
    f_in = jnp.concatenate([jnp.concatenate([c, s], axis=2), jnp.concatenate([-s, c], axis=2)], axis=1)
    ct, st = jnp.swapaxes(c, 1, 2), jnp.swapaxes(s, 1, 2)
    f_in_inv = jnp.concatenate([jnp.concatenate([ct, -st], axis=2), jnp.concatenate([st, ct], axis=2)], axis=1)
    return n2, f_out, f_out_inv, f_in, f_in_inv


def _long_conv_gated(u, ucol, spec, order, skip, gate, gcol, tables):
    n2, f_out, f_out_inv, f_in, f_in_inv = tables
    B, L, _ = u.shape
    D = D_MODEL
    k1h = L // n2
    per = D // DFT_TD
    u4 = u.reshape(B, k1h, n2, u.shape[-1])
    y = _leftmul(f_out, u4, xoff=ucol * per).reshape(B, k1h, 2 * n2, D)
    if f_in is None:
        cols = lambda g: spec[..., g * D:(g + 1) * D]
        kf = _combine_filter_spectrum(cols(2 * order), cols(2 * order + 1))
        yr, yi = y[:, :, 0], y[:, :, 1]
        kr, ki = kf[:, 0], kf[:, 1]
        q = jnp.stack([yr * kr - yi * ki, yr * ki + yi * kr], axis=2)
    else:
        q = _spectral_mid(y, f_in, f_in_inv, spec, order)
    out = _leftmul(f_out_inv, q.reshape(B, 2 * k1h, n2, D),
                   epilogue=(u4, ucol * per, skip[None, :], gate.reshape(B, k1h, n2, gate.shape[-1]), gcol * per))
    return out.reshape(B, L, D)


def _conv3_kernel(cur_ref, prev_ref, next_ref, w_ref, o_ref, *, nblk):
    i = pl.program_id(1)
    x = cur_ref[...]
    tm = x.shape[0]
    row = lax.broadcasted_iota(jnp.int32, x.shape, 0)
    before = jnp.where(i > 0, prev_ref[7:8, :], 0.0)
    after = jnp.where(i < nblk - 1, next_ref[0:1, :], 0.0)
    xp = jnp.where(row == 0, before, pltpu.roll(x, 1, axis=0))
    xn = jnp.where(row == tm - 1, after, pltpu.roll(x, tm - 1, axis=0))
    o_ref[...] = xp * w_ref[0:1, :] + x * w_ref[1:2, :] + xn * w_ref[2:3, :] + w_ref[3:4, :]


def _conv3(x, taps, bias):
    B, L, C = x.shape
    tm = min(512, L)
    tc = 512
    nblk = L // tm
    r8 = tm // 8
    w = jnp.concatenate([taps, bias[None, :], jnp.zeros((4, C), F32)], axis=0)
    return pl.pallas_call(
        functools.partial(_conv3_kernel, nblk=nblk),
        grid=(B, nblk, C // tc),
        in_specs=[pl.BlockSpec((None, tm, tc), lambda b, i, j: (b, i, j)),
                  pl.BlockSpec((None, 8, tc), lambda b, i, j: (b, jnp.maximum(i * r8 - 1, 0), j)),
                  pl.BlockSpec((None, 8, tc), lambda b, i, j: (b, jnp.minimum((i + 1) * r8, L // 8 - 1), j)),
                  pl.BlockSpec((8, tc), lambda b, i, j: (0, j))],
        out_specs=pl.BlockSpec((None, tm, tc), lambda b, i, j: (b, i, j)),
        out_shape=jax.ShapeDtypeStruct((B, L, C), F32),
        compiler_params=_cparams(("parallel", "parallel", "parallel")),
        name="conv3",
    )(x, x, x, w)


def _hyena_filter_spectra(L, w1, b1, w2, b2, w3, b3, w4, freq, tables):
    pos = jnp.arange(L, dtype=F32)
    t = (pos / max(L - 1, 1))[:, None]
    ang = (2.0 * math.pi / L) * pos[:, None] * jnp.linspace(1e-4, HY_BANDS - 1, HY_BANDS, dtype=F32)[None, :]
    z = jnp.concatenate([t, jnp.cos(ang), -jnp.sin(ang)], axis=-1)
    z = jnp.pad(z, ((0, 0), (0, 40 - HY_EMB)))
    w1p = jnp.pad(w1, ((0, 40 - HY_EMB), (0, 0)))
    hid = jnp.sin(freq[0] * (_mm(z, w1p, exact=True) + b1))
    hid = jnp.sin(freq[1] * (_mm(hid, w2, exact=True) + b2))
    hid = jnp.sin(freq[2] * (_mm(hid, w3, exact=True) + b3))
    filt = _mm(hid, w4, exact=True)
    deltas = jnp.linspace(math.log(HY_TARGET) / HY_FAST_DECAY, math.log(HY_TARGET) / HY_SLOW_DECAY,
                          D_MODEL, dtype=F32)
    window = jnp.exp(-t * jnp.abs(deltas))
    lag0 = (jnp.arange(L) > 0).astype(F32)[:, None]
    scale = jnp.concatenate([window, window * lag0] * HY_ORDER, axis=1)
    filt = filt * scale
    n2 = tables[0]
    k1h = L // n2
    C = HY_ORDER * 2 * D_MODEL
    y = _leftmul(tables[1], filt.reshape(1, k1h, n2, C), D=C).reshape(1, k1h, 2 * n2, C)
    return (y if tables[3] is None else _spectral_mid(y, tables[3]))[0]


def _combine_filter_spectrum(hf, hb):
    n2 = hf.shape[-2] // 2
    return jnp.concatenate([hf[..., :n2, :] + hb[..., :n2, :], hf[..., n2:, :] - hb[..., n2:, :]], axis=-2)


def _hyena_stream(resid, vecs, fprm, w_in, b_in, conv_w, conv_b, skip, w_o, b_o):
    B, L, D = resid.shape
    tables = _dft_tables(L)
    spec = _hyena_filter_spectra(L, *fprm, tables)
    proj = _conv3(_mm_fused(resid, w_in, vecs=vecs, norm=True, bias=b_in), conv_w, conv_b)
    z = _long_conv_gated(proj, 2, spec, 0, skip[0], proj, 0, tables)
    y = _long_conv_gated(z, 0, spec, 1, skip[1], proj, 1, tables)
    return _mm_fused(y, w_o, vecs=vecs, bias=b_o, resid=resid)


def _rmsnorm(x, g, eps=NORM_EPS):
    return x * lax.rsqrt(jnp.mean(x * x, axis=-1, keepdims=True) + eps) * g


RW_TM = 256


def _headsum(x, ones):
    hi, lo = _split2(x)
    cols = []
    for j in range(x.shape[1] // LANES):
        ln = slice(j * LANES, (j + 1) * LANES)
        cols.append(_dot(hi[:, ln], ones) + _dot(lo[:, ln], ones))
    return jnp.concatenate(cols, axis=1)


def _token_shift(h, h_above, h_below, grid_w):
    tm, D = h.shape
    row = lax.broadcasted_iota(jnp.int32, (tm, D), 0)
    lane = lax.broadcasted_iota(jnp.int32, (tm, D), 1)
    before = pltpu.roll(h, 1, axis=0)
    after = pltpu.roll(h, tm - 1, axis=0)
    if grid_w is None:
        return jnp.where(lane < D // 2, jnp.where(row == 0, 0.0, before), jnp.where(row == tm - 1, 0.0, after))
    col = row & (grid_w - 1)
    left = jnp.where(col == 0, 0.0, before)
    right = jnp.where(col == grid_w - 1, 0.0, after)
    up = jnp.concatenate([h_above, h[:tm - grid_w]], axis=0)
    down = jnp.concatenate([h[grid_w:], h_below], axis=0)
    q = D // 4
    return jnp.where(lane < q, left, jnp.where(lane < 2 * q, right, jnp.where(lane < 3 * q, up, down)))


def _rwkv_pre_kernel(*refs, vres, grid_w, nblk):
    it = iter(refs)
    x_ref = next(it)
    xa_ref, xb_ref = (next(it), next(it)) if grid_w else (None, None)
    mod_ref = next(it)
    vf_ref = next(it) if vres else None
    vec_ref, wr_ref, wk_ref, wv_ref, w1_ref, w2_ref, a1_ref, a2_ref, g1_ref, g2_ref = (next(it) for _ in range(10))
    v1_ref, v2_ref = (next(it), next(it)) if vres else (None, None)
    ones_ref = next(it)
    r_o, v_o, nkk_o, lw0_o, lw1_o, kd0_o, kd1_o, b0_o, b1_o, bonus_o, g_o = it
    vec = lambda i: vec_ref[i:i + 1, :]
    i = pl.program_id(1)
    h = _norm_mod(x_ref[...], mod_ref)
    if grid_w:
        h_above = jnp.where(i > 0, _norm_mod(xa_ref[...], mod_ref), 0.0)
        h_below = jnp.where(i < nblk - 1, _norm_mod(xb_ref[...], mod_ref), 0.0)
    else:
        h_above = h_below = None
    xx = _token_shift(h, h_above, h_below, grid_w) - h
    xr, xw, xk, xv, xa, xg = ((h + xx * vec(j)).astype(BF16) for j in range(6))
    r = _dot(xr, wr_ref[...])
    k = _dot(xk, wk_ref[...])
    v = _dot(xv, wv_ref[...])
    if vres:
        lora = _dot(_dot(xv, v1_ref[...]).astype(BF16), v2_ref[...])
        v = v + (vf_ref[...] - v) * jax.nn.sigmoid(vec(13) + lora)
    ones = ones_ref[...]
    kk = k * vec(10)
    kk = kk / jnp.maximum(jnp.sqrt(_headsum(kk * kk, ones)), 1e-12)
    r_o[...] = r
    v_o[...] = v
    nkk_o[...] = -kk
    bonus = jnp.zeros_like(r)
    for d, (lw_o, kd_o, b_o) in enumerate(((lw0_o, kd0_o, b0_o), (lw1_o, kd1_o, b1_o))):
        wl = vec(6 + d) + _dot(jnp.tanh(_dot(xw, w1_ref[d])).astype(BF16), w2_ref[d])
        lw_o[...] = -jax.nn.sigmoid(wl) * math.exp(-0.5)
        a = jax.nn.sigmoid(vec(8 + d) + _dot(_dot(xa, a1_ref[d]).astype(BF16), a2_ref[d]))
        kd = k * (1.0 + (a - 1.0) * vec(11))
        kd_o[...] = kd
        b_o[...] = kk * a
        bonus = bonus + _headsum(r * kd * vec(12), ones) * v
    bonus_o[...] = bonus
    g_o[...] = _dot(jax.nn.sigmoid(_dot(xg, g1_ref[...])).astype(BF16), g2_ref[...])


def _rwkv_post_kernel(y0_ref, y1_ref, bonus_ref, g_ref, res_ref, mod_ref, ln_ref, wo_ref, ones_ref, o_ref):
    ones = ones_ref[...]
    y = y0_ref[...] + y1_ref[...]
    yc = y - _headsum(y, ones) * (1.0 / RW_HEAD)
    var = _headsum(yc * yc, ones) * (1.0 / RW_HEAD)
    yn = yc * lax.rsqrt(var + RW_LN_EPS) * ln_ref[0:1, :] + ln_ref[1:2, :] + bonus_ref[...]
    out = _dot((yn * g_ref[...]).astype(BF16), wo_ref[...])
    o_ref[...] = res_ref[...] + mod_ref[2:3, :] * out


def _full(a):
    nd = a.ndim
    return pl.BlockSpec(a.shape, lambda b, i: (0,) * nd)


def _rwkv7_stream(resid, mods, grid_w, state0, v_first, mu, w_rkv, w_o, w0, w1, w2, a0, a1, a2,
                  g1, g2, k_k, k_a, r_k, ln_w, ln_b, vres, need_out=True):
    B, L, D = resid.shape
    tm = min(RW_TM, L)
    nblk = L // tm
    assert grid_w is not None or nblk == 1
    vres_on = vres is not None
    rows = [mu[j] for j in range(6)] + [w0[0], w0[1], a0[0], a0[1], k_k, k_a, r_k.reshape(D)]
    rows.append(vres[0] if vres_on else jnp.zeros((D,), F32))
    vecs = jnp.stack(rows + [jnp.zeros((D,), F32)] * (16 - len(rows)))
    ones = jnp.kron(jnp.eye(LANES // RW_HEAD, dtype=F32), jnp.ones((RW_HEAD, RW_HEAD), F32)).astype(BF16)
    bf = lambda t: t.astype(BF16)
    tok = pl.BlockSpec((None, tm, D), lambda b, i: (b, i, 0))
    mod_spec = pl.BlockSpec((None, 8, D), lambda b, i: (b, 0, 0))
    args, specs = [resid], [tok]
    if grid_w:
        per = tm // grid_w
        last_row = L // grid_w - 1
        args += [resid, resid]
        specs += [pl.BlockSpec((None, grid_w, D), lambda b, i: (b, jnp.maximum(i * per - 1, 0), 0)),
                  pl.BlockSpec((None, grid_w, D), lambda b, i: (b, jnp.minimum((i + 1) * per, last_row), 0))]
    args.append(mods)
    specs.append(mod_spec)
    if vres_on:
        args.append(v_first)
        specs.append(tok)
    consts = [vecs, bf(w_rkv[0]), bf(w_rkv[1]), bf(w_rkv[2]), bf(w1), bf(w2), bf(a1), bf(a2), bf(g1), bf(g2)]
    consts += [bf(vres[1]), bf(vres[2])] if vres_on else []
    consts.append(ones)
    outs = pl.pallas_call(
        functools.partial(_rwkv_pre_kernel, vres=vres_on, grid_w=grid_w, nblk=nblk),
        grid=(B, nblk),
        in_specs=specs + [_full(c) for c in consts],
        out_specs=[tok] * 11,
        out_shape=[jax.ShapeDtypeStruct((B, L, D), F32)] * 11,
        compiler_params=_cparams(("parallel", "parallel")),
        name="rwkv_pre",
    )(*args, *consts)
    r, v, nkk, lw0, lw1, kd0, kd1, b0, b1, bonus, g = outs
    y0, s0 = _rwkv_scan(r, lw0, kd0, v, nkk, b0, state0[0], reverse=False)
    y1, s1 = _rwkv_scan(r, lw1, kd1, v, nkk, b1, state0[1], reverse=True)
    states = jnp.stack([s0, s1])
    if not need_out:
        return None, states, v
    ln = jnp.stack([ln_w, ln_b] + [jnp.zeros((D,), F32)] * 6)
    wo = bf(w_o)
    new_resid = pl.pallas_call(
        _rwkv_post_kernel,
        grid=(B, L // tm),
        in_specs=[tok] * 5 + [mod_spec, _full(ln), _full(wo), _full(ones)],
        out_specs=tok,
        out_shape=jax.ShapeDtypeStruct((B, L, D), F32),
        compiler_params=_cparams(("parallel", "parallel")),
        name="rwkv_post",
    )(y0, y1, bonus, g, resid, mods, ln, wo, ones)
    return new_resid, states, v


def _mm_fused_kernel(*refs, norm, bias, resid):
    it = iter(refs)
    x_ref = next(it)
    vec_ref = next(it) if (norm or resid) else None
    w_ref = next(it)
    b_ref = next(it) if bias else None
    res_ref = next(it) if resid else None
    o_ref, xs_scr = next(it), next(it)

    @pl.when(pl.program_id(2) == 0)
    def _prep():
        x = x_ref[...]
        xs_scr[...] = (_norm_mod(x, vec_ref) if norm else x).astype(BF16)

    acc = _dot(xs_scr[...], w_ref[...])
    if bias:
        acc = acc + b_ref[...]
    o_ref[...] = res_ref[...] + vec_ref[2:3, :] * acc if resid else acc


def _mm_fused(x, w, vecs=None, norm=False, bias=None, resid=None):
    B, L, K = x.shape
    N = w.shape[1]
    tm = min(1024, L)
    tn = 512 if (N % 512 == 0 and resid is None) else N
    use_vec = norm or resid is not None
    specs = [pl.BlockSpec((None, tm, K), lambda b, i, j: (b, i, 0))]
    args = [x]
    if use_vec:
        specs.append(pl.BlockSpec((None, 8, vecs.shape[-1]), lambda b, i, j: (b, 0, 0)))
        args.append(vecs)
    specs.append(pl.BlockSpec((K, tn), lambda b, i, j: (0, j)))
    args.append(w.astype(BF16))
    if bias is not None:
        specs.append(pl.BlockSpec((1, tn), lambda b, i, j: (0, j)))
        args.append(bias[None, :])
    out_spec = pl.BlockSpec((None, tm, tn), lambda b, i, j: (b, i, j))
    if resid is not None:
        assert N == K == vecs.shape[-1] and tn == N
        specs.append(out_spec)
        args.append(resid)
    return pl.pallas_call(
        functools.partial(_mm_fused_kernel, norm=norm, bias=bias is not None, resid=resid is not None),
        grid=(B, L // tm, N // tn),
        in_specs=specs, out_specs=out_spec,
        out_shape=jax.ShapeDtypeStruct((B, L, N), F32),
        scratch_shapes=[pltpu.VMEM((tm, K), BF16)],
        compiler_params=_cparams(("parallel", "parallel", "arbitrary")),
        name="mm_fused",
    )(*args)


def _hgrn_post_kernel(o0_ref, o1_ref, g_ref, res_ref, vec_ref, gn_ref, wo_ref, ones_ref, o_ref):
    o = o0_ref[...] + o1_ref[...]
    ms = _headsum(o * o, ones_ref[...]) * (1.0 / HG_DK)
    on = o * lax.rsqrt(ms + NORM_EPS) * gn_ref[...]
    g = g_ref[...]
    z = (on * (g * jax.nn.sigmoid(g))).astype(BF16)
    o_ref[...] = res_ref[...] + vec_ref[2:3, :] * _dot(z, wo_ref[...])


def _hgrn2_stream(resid, vecs, state0, lb, w_in, gn, w_o, need_out=True):
    B, L, D = resid.shape
    proj = _mm_fused(resid, w_in, vecs=vecs, norm=True)
    o0, s0 = _gla_scan(proj, lb, state0[0], 0)
    o1, s1 = _gla_scan(proj, lb, state0[1], 1)
    states = jnp.stack([s0, s1])
    if not need_out:
        return None, states
    tm = min(RW_TM, L)
    tok = pl.BlockSpec((None, tm, D), lambda b, i: (b, i, 0))
    ones = jnp.ones((LANES, LANES), BF16)
    gn_row = jnp.tile(gn, D // gn.shape[0])[None, :]
    wo = w_o.astype(BF16)
    out = pl.pallas_call(
        _hgrn_post_kernel,
        grid=(B, L // tm),
        in_specs=[tok, tok, pl.BlockSpec((None, tm, D), lambda b, i: (b, i, 4)), tok,
                  pl.BlockSpec((None, 8, D), lambda b, i: (b, 0, 0)), _full(gn_row), _full(wo), _full(ones)],
        out_specs=tok,
        out_shape=jax.ShapeDtypeStruct((B, L, D), F32),
        compiler_params=_cparams(("parallel", "parallel")),
        name="hgrn_post",
    )(o0, o1, proj, resid, vecs, gn_row, wo, ones)
    return out, states


def kernel(x, c, ctx, c_ctx, norm_g, ada_w, ada_b, final_g,
           rw_mu, rw_wrkv, rw_wo, rw_w0, rw_w1, rw_w2, rw_a0, rw_a1, rw_a2,
           rw_v0, rw_v1, rw_v2, rw_g1, rw_g2, rw_kk, rw_ka, rw_rk, rw_lnw, rw_lnb,
           hy_win, hy_bin, hy_cw, hy_cb, hy_fw1, hy_fb1, hy_fw2, hy_fb2, hy_fw3, hy_fb3,
           hy_fw4, hy_freq, hy_skip, hy_wo, hy_bo,
           hg_win, hg_lb, hg_gn, hg_wo,
           ffn_w13, ffn_w2, moe_router, moe_w13, moe_w2):
    B = x.shape[0]
    depth = norm_g.shape[0]
    D = D_MODEL
    lat, cx = x, ctx
    v_first = None
    lbc = jnp.cumsum(jax.nn.softmax(hg_lb, axis=0), axis=0)
    lower = lbc - lbc[:1]
    cond = jnp.concatenate([jax.nn.silu(c), jax.nn.silu(c_ctx)[None, :]], axis=0)
    ffn_w13b, ffn_w2b = ffn_w13.astype(BF16), ffn_w2.astype(BF16)
    moe_w13b, moe_w2b = moe_w13.astype(BF16), moe_w2.astype(BF16)
    for i in range(depth):
        last = i == depth - 1
        mod = _mm(cond, ada_w[i]) + ada_b[i]
        mod_l = jnp.split(mod[:B, None, :], 6, axis=-1)
        mod_c = jnp.split(mod[B:, None, :], 6, axis=-1)
        kind, slot = i % N_MIXERS, i // N_MIXERS
        vec_l = _mod_rows(norm_g[i, 0], mod_l[0], mod_l[1], mod_l[2], B)
        vec_c = _mod_rows(norm_g[i, 0], mod_c[0], mod_c[1], mod_c[2], B)
        if kind == 0:
            vres = None if slot == 0 else (rw_v0[slot - 1], rw_v1[slot - 1], rw_v2[slot - 1])
            rw = (rw_mu[slot], rw_wrkv[slot], rw_wo[slot], rw_w0[slot], rw_w1[slot], rw_w2[slot],
                  rw_a0[slot], rw_a1[slot], rw_a2[slot], rw_g1[slot], rw_g2[slot], rw_kk[slot],
                  rw_ka[slot], rw_rk[slot], rw_lnw[slot], rw_lnb[slot], vres)
            zero = jnp.zeros((2, B, RW_H, RW_HEAD, RW_HEAD), F32)
            vf_c = None if v_first is None else v_first[0]
            vf_l = None if v_first is None else v_first[1]
            cx_new, s_ctx, v_c = _rwkv7_stream(cx, vec_c, None, zero, vf_c, *rw, need_out=not last)
            lat, _, v_l = _rwkv7_stream(lat, vec_l, GRID_W, s_ctx, vf_l, *rw)
            if not last:
                cx = cx_new
            if slot == 0:
                v_first = (v_c, v_l)
        elif kind == 1:
            fprm = (hy_fw1[slot], hy_fb1[slot], hy_fw2[slot], hy_fb2[slot], hy_fw3[slot],
                    hy_fb3[slot], hy_fw4[slot], hy_freq[slot])
            hprm = (hy_win[slot], hy_bin[slot], hy_cw[slot], hy_cb[slot], hy_skip[slot],
                    hy_wo[slot], hy_bo[slot])
            lat = _hyena_stream(lat, vec_l, fprm, *hprm)
            if not last:
                cx = _hyena_stream(cx, vec_c, fprm, *hprm)
        else:
            zero = jnp.zeros((2, B, HG_H, HG_DK, D // HG_H), F32)
            gprm = (lower[i], hg_win[slot], hg_gn[slot], hg_wo[slot])
            cx_new, s_ctx = _hgrn2_stream(cx, vec_c, zero, *gprm, need_out=not last)
            lat, _ = _hgrn2_stream(lat, vec_l, s_ctx, *gprm)
            if not last:
                cx = cx_new

        def channel_mix(s, mod):
            vecs = _mod_rows(norm_g[i, 1], mod[3], mod[4], mod[5], B)
            if i % 2 == 0:
                return _ffn(s, vecs, ffn_w13b[i // 2], ffn_w2b[i // 2])
            xn, logits = _norm_route(s, vecs, moe_router[i // 2])
            out = _moe(xn.reshape(-1, D), logits.reshape(-1, N_EXPERTS), moe_w13b[i // 2], moe_w2b[i // 2])
            return s + mod[5] * out.reshape(s.shape)

        lat = channel_mix(lat, mod_l)
        if not last:
            cx = channel_mix(cx, mod_c)
    return _rmsnorm(lat, final_g)
```

```python
import functools
import math

import jax
import jax.numpy as jnp
from jax import lax
from jax.experimental import pallas as pl
from jax.experimental.pallas import tpu as pltpu

F32 = jnp.float32
BF16 = jnp.bfloat16
HIGHEST = lax.Precision.HIGHEST

D_MODEL = 1024
GRID_W = 64
NORM_EPS = 1e-6
RW_HEAD = 64
RW_H = D_MODEL // RW_HEAD
RW_LN_EPS = 64e-5
HY_ORDER = 2
HY_EMB = 33
HY_BANDS = (HY_EMB - 1) // 2
HY_FAST_DECAY = 0.3
HY_SLOW_DECAY = 1.5
HY_TARGET = 1e-2
HG_DK = 128
HG_H = D_MODEL // HG_DK
HG_CHUNK = 32
N_EXPERTS = 8
TOP_K = 2
N_MIXERS = 3

LANES = 128
VMEM_LIMIT = 56 * 1024 * 1024
RW_CHUNK = 64
SCAN_BLOCK = 256
HY_N2 = 128


def _cparams(sem):
    return pltpu.CompilerParams(dimension_semantics=sem, vmem_limit_bytes=VMEM_LIMIT)


def _dot(a, b, prec=None):
    return jnp.dot(a, b, preferred_element_type=F32, precision=prec)


def _dot_nt(a, b, prec=None):
    return lax.dot_general(a, b, (((1,), (1,)), ((), ())), preferred_element_type=F32, precision=prec)


def _split2(x):
    hi = x.astype(BF16)
    lo = (x - hi.astype(F32)).astype(BF16)
    return hi, lo


def _pdot(a, b, mode, nt=False):
    dn = (((1,), (1 if nt else 0,)), ((), ()))
    dg = lambda x, y, p=None: lax.dot_general(x, y, dn, preferred_element_type=F32, precision=p)
    if mode == 'f32':
        return dg(a, b, HIGHEST)
    if mode == 'bf16':
        return dg(a.astype(BF16), b.astype(BF16))
    ah, al = _split2(a)
    bh, bl = _split2(b)
    return dg(ah, bh) + (dg(ah, bl) + dg(al, bh))


def _cumsum_dot(tri, x):
    hi = x.astype(BF16)
    r1 = x - hi.astype(F32)
    mid = r1.astype(BF16)
    lo = (r1 - mid.astype(F32)).astype(BF16)
    n = x.shape[1]
    g = _dot(tri, jnp.concatenate([hi, mid, lo], axis=1))
    return g[:, :n] + (g[:, n:2 * n] + g[:, 2 * n:])


def _mm_kernel(a_ref, b_ref, o_ref, *, prec):
    o_ref[...] = _dot(a_ref[...], b_ref[...], prec)


def _mm(a, b, *, exact=False):
    M, K = a.shape
    N = b.shape[1]
    dt = F32 if exact else BF16
    a = a.astype(dt)
    b = b.astype(dt)
    Mp = -(-M // 8) * 8
    if Mp != M:
        a = jnp.pad(a, ((0, Mp - M), (0, 0)))
    tm = 512 if Mp % 512 == 0 else (256 if Mp % 256 == 0 else Mp)
    tn = 512 if N % 512 == 0 else N
    out = pl.pallas_call(
        functools.partial(_mm_kernel, prec=HIGHEST if exact else None),
        grid=(Mp // tm, N // tn),
        in_specs=[pl.BlockSpec((tm, K), lambda i, j: (i, 0)),
                  pl.BlockSpec((K, tn), lambda i, j: (0, j))],
        out_specs=pl.BlockSpec((tm, tn), lambda i, j: (i, j)),
        out_shape=jax.ShapeDtypeStruct((Mp, N), F32),
        compiler_params=_cparams(("parallel", "parallel")),
        name="mm",
    )(a, b)
    return out[:M] if Mp != M else out


def _mm3(x, w, **kw):
    B, L, K = x.shape
    return _mm(x.reshape(B * L, K), w, **kw).reshape(B, L, w.shape[1])


def _norm_mod(x, vec_ref):
    xn = x * lax.rsqrt(jnp.mean(x * x, axis=-1, keepdims=True) + NORM_EPS)
    return xn * vec_ref[0:1, :] + vec_ref[1:2, :]


def _mod_rows(g, shift, scale, gate, B):
    D = g.shape[-1]
    rows = [jnp.broadcast_to(g * (1.0 + scale[:, 0]), (B, D)), jnp.broadcast_to(shift[:, 0], (B, D)),
            jnp.broadcast_to(gate[:, 0], (B, D))]
    return jnp.stack(rows + [jnp.zeros((B, D), F32)] * 5, axis=1)


def _ffn_kernel(x_ref, vec_ref, wg_ref, wu_ref, w2_ref, o_ref, xn_scr, acc_ref, *, n_f):
    f = pl.program_id(2)

    @pl.when(f == 0)
    def _first():
        xn_scr[...] = _norm_mod(x_ref[...], vec_ref).astype(BF16)
        acc_ref[...] = jnp.zeros_like(acc_ref)

    x = xn_scr[...]
    gate = _dot(x, wg_ref[...])
    up = _dot(x, wu_ref[...])
    h = (gate * jax.nn.sigmoid(gate) * up).astype(BF16)
    acc_ref[...] += _dot(h, w2_ref[...])

    @pl.when(f == n_f - 1)
    def _store():
        o_ref[...] = x_ref[...] + vec_ref[2:3, :] * acc_ref[...]


def _ffn(x, vecs, w13, w2):
    B, L, D = x.shape
    F = w13.shape[1] // 2
    tm = min(1024, L)
    tf = 512 if F % 512 == 0 else 256
    n_f = F // tf
    tok = pl.BlockSpec((None, tm, D), lambda b, i, f: (b, i, 0))
    return pl.pallas_call(
        functools.partial(_ffn_kernel, n_f=n_f),
        grid=(B, L // tm, n_f),
        in_specs=[tok,
                  pl.BlockSpec((None, 8, D), lambda b, i, f: (b, 0, 0)),
                  pl.BlockSpec((D, tf), lambda b, i, f: (0, f)),
                  pl.BlockSpec((D, tf), lambda b, i, f: (0, f + n_f)),
                  pl.BlockSpec((tf, D), lambda b, i, f: (f, 0))],
        out_specs=tok,
        out_shape=jax.ShapeDtypeStruct((B, L, D), F32),
        scratch_shapes=[pltpu.VMEM((tm, D), BF16), pltpu.VMEM((tm, D), F32)],
        compiler_params=_cparams(("parallel", "parallel", "arbitrary")),
        name="ffn",
    )(x, vecs, w13, w13, w2)


def _norm_route_kernel(x_ref, vec_ref, wr_ref, xn_ref, lg_ref):
    xn = _norm_mod(x_ref[...], vec_ref)
    xn_ref[...] = xn.astype(BF16)
    lg_ref[...] = _dot(xn, wr_ref[...], HIGHEST)


def _norm_route(x, vecs, w_router):
    B, L, D = x.shape
    E = w_router.shape[1]
    tm = min(512, L)
    tok = pl.BlockSpec((None, tm, D), lambda b, i: (b, i, 0))
    return pl.pallas_call(
        _norm_route_kernel,
        grid=(B, L // tm),
        in_specs=[tok, pl.BlockSpec((None, 8, D), lambda b, i: (b, 0, 0)), pl.BlockSpec((D, E), lambda b, i: (0, 0))],
        out_specs=[tok, pl.BlockSpec((None, tm, E), lambda b, i: (b, i, 0))],
        out_shape=[jax.ShapeDtypeStruct((B, L, D), BF16), jax.ShapeDtypeStruct((B, L, E), F32)],
        compiler_params=_cparams(("parallel", "parallel")),
        name="norm_route",
    )(x, vecs, w_router)


MOE_TILE = 2048
MOE_SUB = 192
MOE_SLAB = 512


def _moe_kernel(cnt_ref, x_ref, rrow_ref, rcol_ref, gcol_ref, wg_ref, wu_ref, w2_ref, o_ref, xc_scr, y_scr,
                *, n_f, tm):
    i, e, f = pl.program_id(0), pl.program_id(1), pl.program_id(2)
    sub = MOE_SUB
    slab = min(MOE_SLAB, tm)
    nslab = tm // slab
    n_sub = (cnt_ref[i, e, nslab] + (sub - 1)) // sub

    def touches(s, j):
        return (cnt_ref[i, e, j] < (s + 1) * sub) & (cnt_ref[i, e, j + 1] > s * sub)

    @pl.when((e == 0) & (f == 0))
    def _zero():
        o_ref[...] = jnp.zeros_like(o_ref)

    @pl.when(f == 0)
    def _compact():
        def body(s, carry):
            rows = pl.ds(pl.multiple_of(s * sub, sub), sub)
            y_scr[rows, :] = jnp.zeros((sub, y_scr.shape[1]), F32)
            for j in range(nslab):
                cols = slice(j * slab, (j + 1) * slab)

                @pl.when(touches(s, j))
                def _add():
                    ridx = lax.broadcasted_iota(jnp.int32, (sub, slab), 0) + s * sub
                    onehot = jnp.where(rrow_ref[:, cols] == ridx, 1.0, 0.0).astype(BF16)
                    y_scr[rows, :] += _dot(onehot, x_ref[cols, :])
            xc_scr[rows, :] = y_scr[rows, :].astype(BF16)
            return carry
        lax.fori_loop(0, n_sub, body, 0)

    def expert(s, first):
        rows = pl.ds(pl.multiple_of(s * sub, sub), sub)
        xs = xc_scr[rows, :]
        gate = _dot(xs, wg_ref[...])
        up = _dot(xs, wu_ref[...])
        h = (gate * jax.nn.sigmoid(gate) * up).astype(BF16)
        part = _dot(h, w2_ref[...])
        y_scr[rows, :] = part if first else y_scr[rows, :] + part

    @pl.when(f == 0)
    def _first():
        lax.fori_loop(0, n_sub, lambda s, c: (expert(s, True), c)[1], 0)

    @pl.when(f > 0)
    def _rest():
        lax.fori_loop(0, n_sub, lambda s, c: (expert(s, False), c)[1], 0)

    @pl.when(f == n_f - 1)
    def _scatter():
        def body(s, carry):
            y = y_scr[pl.ds(pl.multiple_of(s * sub, sub), sub), :].astype(BF16)
            for j in range(nslab):
                rows = slice(j * slab, (j + 1) * slab)

                @pl.when(touches(s, j))
                def _add():
                    cidx = lax.broadcasted_iota(jnp.int32, (slab, sub), 1) + s * sub
                    onehot_t = jnp.where(rcol_ref[rows, :] == cidx, 1.0, 0.0).astype(BF16)
                    o_ref[rows, :] += gcol_ref[rows, :] * _dot(onehot_t, y)
            return carry
        lax.fori_loop(0, n_sub, body, 0)


def _moe(x, logits, w13, w2):
    T, D = x.shape
    E, _, F2 = w13.shape
    F = F2 // 2
    tm = min(MOE_TILE, T)
    nt = T // tm
    tf = 512
    n_f = F // tf
    top_val, top_idx = lax.top_k(logits, TOP_K)
    gates = jax.nn.softmax(top_val, axis=-1)
    onehot = jax.nn.one_hot(top_idx, E, dtype=F32)
    sel = jnp.sum(onehot, axis=1).astype(jnp.int32).reshape(nt, tm, E)
    gate_dense = jnp.sum(onehot * gates[..., None], axis=1).reshape(nt, tm, E)
    rank = jnp.where(sel > 0, jnp.cumsum(sel, axis=1) - sel, -1)
    rank = jnp.swapaxes(rank, 1, 2)
    slab = min(MOE_SLAB, tm)
    per_slab = jnp.sum(sel.reshape(nt, tm // slab, slab, E), axis=2)
    counts = jnp.concatenate([jnp.zeros((nt, 1, E), jnp.int32), jnp.cumsum(per_slab, axis=1)], axis=1)
    counts = jnp.swapaxes(counts, 1, 2)
    gcol = jnp.swapaxes(gate_dense, 1, 2)[..., None]
    tile = lambda shape, imap: pl.BlockSpec(shape, imap)
    return pl.pallas_call(
        functools.partial(_moe_kernel, n_f=n_f, tm=tm),
        grid_spec=pltpu.PrefetchScalarGridSpec(
            num_scalar_prefetch=1,
            grid=(nt, E, n_f),
            in_specs=[tile((tm, D), lambda i, e, f, c: (i, 0)),
                      tile((None, None, 1, tm), lambda i, e, f, c: (i, e, 0, 0)),
                      tile((None, None, tm, 1), lambda i, e, f, c: (i, e, 0, 0)),
                      tile((None, None, tm, 1), lambda i, e, f, c: (i, e, 0, 0)),
                      tile((None, D, tf), lambda i, e, f, c: (e, 0, f)),
                      tile((None, D, tf), lambda i, e, f, c: (e, 0, f + n_f)),
                      tile((None, tf, D), lambda i, e, f, c: (e, f, 0))],
            out_specs=tile((tm, D), lambda i, e, f, c: (i, 0)),
            scratch_shapes=[pltpu.VMEM((-(-tm // MOE_SUB) * MOE_SUB, D), BF16),
                            pltpu.VMEM((-(-tm // MOE_SUB) * MOE_SUB, D), F32)]),
        out_shape=jax.ShapeDtypeStruct((T, D), F32),
        compiler_params=_cparams(("parallel", "arbitrary", "arbitrary")),
        name="moe",
    )(counts, x, rank[:, :, None, :], rank[..., None], gcol, w13, w13, w2)


def _rwkv_scan_kernel(r_ref, lw_ref, k_ref, v_ref, a_ref, b_ref, h0_ref, y_ref, hT_ref, H_scr,
                      *, reverse, nblk, nchunk, npl, modes):
    C = RW_CHUNK
    half = LANES // 2
    m_a, m_inv, m_u, m_st = modes
    i = pl.program_id(2)

    @pl.when(i == 0)
    def _init():
        H_scr[...] = h0_ref[...]

    t_idx = lax.broadcasted_iota(jnp.int32, (C, LANES), 0)
    s_idx = lax.broadcasted_iota(jnp.int32, (C, LANES), 1) & (half - 1)
    tt = lax.broadcasted_iota(jnp.int32, (C, C), 0)
    ss = lax.broadcasted_iota(jnp.int32, (C, C), 1)
    if reverse:
        strict, incl, tri = s_idx > t_idx, s_idx >= t_idx, (ss >= tt).astype(BF16)
    else:
        strict, incl, tri = s_idx < t_idx, s_idx <= t_idx, (ss <= tt).astype(BF16)
    eye_lp = (s_idx == t_idx).astype(F32)
    lane = lax.broadcasted_iota(jnp.int32, (1, LANES), 1)
    m0 = (lane < half).astype(F32)
    m1 = 1.0 - m0
    rr = lax.broadcasted_iota(jnp.int32, (LANES, LANES), 0)
    cc = lax.broadcasted_iota(jnp.int32, (LANES, LANES), 1)
    mask_bd = ((rr < half) == (cc < half)).astype(F32)

    def bd(x):
        return jnp.concatenate([x * m0, x * m1], axis=0)

    order = list(range(nchunk - 1, -1, -1) if reverse else range(nchunk))
    units = [(slice(c * C, (c + 1) * C), slice(q * LANES, (q + 1) * LANES)) for c in order for q in range(npl)]
    ld = lambda ref: [ref[sl, ln] for sl, ln in units]
    r, lw, k, v, a, b = ld(r_ref), ld(lw_ref), ld(k_ref), ld(v_ref), ld(a_ref), ld(b_ref)
    G = [_cumsum_dot(tri, x) for x in lw]
    eG = [jnp.exp(g) for g in G]
    eGn = [jnp.exp(-g) for g in G]
    rt = [x * e for x, e in zip(r, eG)]
    at = [x * jnp.exp(g - l) for x, g, l in zip(a, G, lw)]
    kt = [x * e for x, e in zip(k, eGn)]
    bt = [x * e for x, e in zip(b, eGn)]
    M = [_pdot(jnp.concatenate([x, y], axis=0), jnp.concatenate([bd(z), bd(w)], axis=0), m_a, nt=True)
         for x, y, z, w in zip(at, rt, bt, kt)]
    Nm = [jnp.where(strict, m[:C, :LANES], 0.0) for m in M]
    Aak = [jnp.where(strict, m[:C, LANES:], 0.0) for m in M]
    Arb = [jnp.where(incl, m[C:, :LANES], 0.0) for m in M]
    Ark = [jnp.where(incl, m[C:, LANES:], 0.0) for m in M]
    T = None
    for lvl in range(1, int(math.log2(C)) + 1):
        same = (t_idx >> lvl) == (s_idx >> lvl)
        t_hi = ((t_idx >> (lvl - 1)) & 1) == 1
        s_hi = ((s_idx >> (lvl - 1)) & 1) == 1
        off = same & (s_hi & ~t_hi if reverse else t_hi & ~s_hi)
        Noff = [jnp.where(off, n, 0.0) for n in Nm]
        if T is None:
            T = [eye_lp + n for n in Noff]
        else:
            DN = [_pdot(t, bd(n), m_inv) for t, n in zip(T, Noff)]
            T = [t + _pdot(dn, bd(t), m_inv) for t, dn in zip(T, DN)]
    bdv = [bd(x) for x in v]
    X0 = [_pdot(x, y, m_st) for x, y in zip(Aak, bdv)]
    WU = [_pdot(t, jnp.concatenate([bd(x), bd(y)], axis=1), m_u) for t, x, y in zip(T, at, X0)]
    W = [x[:, :LANES] for x in WU]
    U0 = [x[:, LANES:] for x in WU]
    RY = [_pdot(x, jnp.concatenate([bd(y), bd(z)], axis=1), m_st) for x, y, z in zip(Arb, W, U0)]
    Rh = [x + y[:, :LANES] for x, y in zip(rt, RY)]
    Y0 = [y[:, LANES:] + _pdot(x, z, m_st) for y, x, z in zip(RY, Ark, bdv)]
    dPhi = [mask_bd * _pdot(x.T, y, m_st) for x, y in zip(W, bt)]
    Psi = [mask_bd * _pdot(jnp.concatenate([x, y], axis=0).T, jnp.concatenate([z, w], axis=0), m_st)
           for x, y, z, w in zip(U0, v, bt, kt)]
    e_end = [jnp.exp(g[0:1, :] if reverse else g[C - 1:C, :]) for g in G]

    Hs = [H_scr[q] for q in range(npl)]
    for u, (sl, ln) in enumerate(units):
        q = u % npl
        H = Hs[q]
        y_ref[sl, ln] = _pdot(Rh[u], H, m_st, nt=True) + Y0[u]
        Hs[q] = (H + _pdot(H, dPhi[u], m_st) + Psi[u]) * e_end[u]
    for q in range(npl):
        H_scr[q] = Hs[q]

    @pl.when(i == nblk - 1)
    def _fin():
        for q in range(npl):
            hT_ref[q] = Hs[q]


def _pair_states(S):
    B, H, N, _ = S.shape
    S5 = S.reshape(B, H // 2, 2, N, N)
    eye = jnp.eye(2, dtype=S.dtype)
    return jnp.einsum('bpivk,ij->bpivjk', S5, eye).reshape(B, H // 2, 2 * N, 2 * N)


def _unpair_states(Sp):
    B, P, N2, _ = Sp.shape
    N = N2 // 2
    S6 = Sp.reshape(B, P, 2, N, 2, N)
    return jnp.stack([S6[:, :, 0, :, 0, :], S6[:, :, 1, :, 1, :]], axis=2).reshape(B, 2 * P, N, N)


RW_MODES = ('bf16', 'bf16', 'bf16', 'bf16')
RW_PAIRS_PER_STEP = 2
RW_BLOCK = 512


def _rwkv_scan(r, lw, k, v, a, b, state0, reverse, modes=RW_MODES):
    B, L, D = r.shape
    bt = min(RW_BLOCK, L)
    nblk = L // bt
    npl = RW_PAIRS_PER_STEP
    npair = D // LANES
    blk = (lambda i: nblk - 1 - i) if reverse else (lambda i: i)
    tok = pl.BlockSpec((None, bt, npl * LANES), lambda bb, p, i: (bb, blk(i), p))
    st = pl.BlockSpec((None, npl, LANES, LANES), lambda bb, p, i: (bb, p, 0, 0))
    y, hT = pl.pallas_call(
        functools.partial(_rwkv_scan_kernel, reverse=reverse, nblk=nblk, nchunk=bt // RW_CHUNK, npl=npl,
                          modes=modes),
        grid=(B, npair // npl, nblk),
        in_specs=[tok] * 6 + [st],
        out_specs=[tok, st],
        out_shape=[jax.ShapeDtypeStruct((B, L, D), F32),
                   jax.ShapeDtypeStruct((B, npair, LANES, LANES), F32)],
        scratch_shapes=[pltpu.VMEM((npl, LANES, LANES), F32)],
        compiler_params=_cparams(("parallel", "parallel", "arbitrary")),
        name="rwkv_scan_rev" if reverse else "rwkv_scan_fwd",
    )(r, lw, k, v, a, b, _pair_states(state0))
    return y, _unpair_states(hT)


GLA_HEADS_PER_STEP = 2


def _gla_kernel(q_ref, f_ref, v_ref, lb_ref, h0_ref, o_ref, hT_ref, H_scr, *, reverse, nblk, npl):
    C = HG_CHUNK
    sh = int(math.log2(C))
    bt = q_ref.shape[0]
    nchunk = bt // C
    i = pl.program_id(2)

    @pl.when(i == 0)
    def _init():
        H_scr[...] = h0_ref[...]

    tt = lax.broadcasted_iota(jnp.int32, (bt, bt), 0)
    ss = lax.broadcasted_iota(jnp.int32, (bt, bt), 1)
    same = (tt >> sh) == (ss >> sh)
    mid = ((tt >> sh) << sh) + (C // 2 if reverse else C // 2 - 1)
    if reverse:
        incl, upto_mid = same & (ss >= tt), same & (ss >= mid)
    else:
        incl, upto_mid = same & (ss <= tt), same & (ss <= mid)
    one = lambda m: jnp.where(m, 1.0, 0.0)
    sums = jnp.concatenate([one(incl), one(incl) - one(upto_mid), one(same) - one(incl)], axis=0).astype(BF16)

    order = list(range(nchunk - 1, -1, -1) if reverse else range(nchunk))
    heads = [slice(p * LANES, (p + 1) * LANES) for p in range(npl)]
    q = [jax.nn.silu(q_ref[:, ln]) for ln in heads]
    fg = [lb_ref[:, ln] + (1.0 - lb_ref[:, ln]) * jax.nn.sigmoid(f_ref[:, ln]) for ln in heads]
    v = [v_ref[:, ln] for ln in heads]
    k = [1.0 - f for f in fg]
    Gs = [_cumsum_dot(sums, jnp.log(f)) for f in fg]
    Gabs = [g[:bt] for g in Gs]
    Grel = [g[bt:2 * bt] for g in Gs]
    Gend = [g[2 * bt:] for g in Gs]
    scores = [jnp.where(incl, _pdot(x * jnp.exp(g), y * jnp.exp(-g), 'bf16', nt=True), 0.0)
              for x, y, g in zip(q, k, Grel)]
    o_intra = [_pdot(s, x, 'bf16') for s, x in zip(scores, v)]
    qa = [x * jnp.exp(g) for x, g in zip(q, Gabs)]
    kend = [x * jnp.exp(g) for x, g in zip(k, Gend)]
    rows = [slice(c * C, (c + 1) * C) for c in range(nchunk)]
    KV = [[_pdot(x[r].T, y[r], 'bf16') for r in rows] for x, y in zip(v, kend)]
    dec = [[jnp.exp(ga[r][0:1, :] + ge[r][0:1, :]) for r in rows] for ga, ge in zip(Gabs, Gend)]

    Hs = [H_scr[p] for p in range(npl)]
    for c in order:
        for p in range(npl):
            o_ref[rows[c], heads[p]] = o_intra[p][rows[c]] + _pdot(qa[p][rows[c]], Hs[p], 'bf16', nt=True)
            Hs[p] = Hs[p] * dec[p][c] + KV[p][c]
    for p in range(npl):
        H_scr[p] = Hs[p]

    @pl.when(i == nblk - 1)
    def _fin():
        for p in range(npl):
            hT_ref[p] = Hs[p]


def _gla_scan(proj, lb, state0, d):
    B, L, D5 = proj.shape
    D = D5 // 5
    reverse = d == 1
    bt = SCAN_BLOCK
    nblk = L // bt
    npl = GLA_HEADS_PER_STEP
    nh = D // LANES
    ncb = nh // npl
    blk = (lambda i: nblk - 1 - i) if reverse else (lambda i: i)
    col = lambda off: pl.BlockSpec((None, bt, npl * LANES), lambda bb, p, i: (bb, blk(i), off * ncb + p))
    st = pl.BlockSpec((None, npl, LANES, LANES), lambda bb, p, i: (bb, p, 0, 0))
    o, hT = pl.pallas_call(
        functools.partial(_gla_kernel, reverse=reverse, nblk=nblk, npl=npl),
        grid=(B, ncb, nblk),
        in_specs=[col(0), col(1 + d), col(3), pl.BlockSpec((1, npl * LANES), lambda bb, p, i: (0, p)), st],
        out_specs=[col(0), st],
        out_shape=[jax.ShapeDtypeStruct((B, L, D), F32),
                   jax.ShapeDtypeStruct((B, nh, LANES, LANES), F32)],
        scratch_shapes=[pltpu.VMEM((npl, LANES, LANES), F32)],
        compiler_params=_cparams(("parallel", "parallel", "arbitrary")),
        name="gla_scan_rev" if reverse else "gla_scan_fwd",
    )(proj, proj, proj, lb[d][None, :], jnp.swapaxes(state0, 2, 3))
    return o, jnp.swapaxes(hT, 2, 3)


def _x3dot(fh, fl, x):
    xh, xl = _split2(x)
    return _dot(fh, xh) + (_dot(fh, xl) + _dot(fl, xh))


def _leftmul_kernel(fh_ref, fl_ref, x_ref, o_ref, xs_scr, *, nj):
    for j in range(nj):
        xs_scr[...] = x_ref[:, j, :]
        o_ref[:, j, :] = _x3dot(fh_ref[...], fl_ref[...], xs_scr[...])


def _leftmul_gate_kernel(fh_ref, fl_ref, x_ref, u_ref, s_ref, g_ref, o_ref, xs_scr, *, nj):
    for j in range(nj):
        xs_scr[...] = x_ref[:, j, :]
        y = _x3dot(fh_ref[...], fl_ref[...], xs_scr[...])
        o_ref[:, j, :] = (y + u_ref[:, j, :] * s_ref[...]) * g_ref[:, j, :]


DFT_TD = 512


def _leftmul(f, x, xoff=0, epilogue=None, D=D_MODEL):
    B, K, J, _ = x.shape
    M = f.shape[0]
    tj = 8 if J % 8 == 0 else J
    td = DFT_TD
    fh, fl = _split2(f)
    fs = pl.BlockSpec((M, K), lambda b, j, d: (0, 0))
    col = lambda rows, off: pl.BlockSpec((None, rows, tj, td), lambda b, j, d: (b, 0, j, off + d))
    if epilogue is None:
        kern, specs, args = _leftmul_kernel, [fs, fs, col(K, xoff)], (fh, fl, x)
    else:
        u, uoff, skip, gate, goff = epilogue
        kern = _leftmul_gate_kernel
        specs = [fs, fs, col(K, xoff), col(M, uoff), pl.BlockSpec((1, td), lambda b, j, d: (0, d)), col(M, goff)]
        args = (fh, fl, x, u, skip, gate)
    return pl.pallas_call(
        functools.partial(kern, nj=tj), grid=(B, J // tj, D // td), in_specs=specs, out_specs=col(M, 0),
        out_shape=jax.ShapeDtypeStruct((B, M, J, D), F32),
        scratch_shapes=[pltpu.VMEM((K, td), F32)],
        compiler_params=_cparams(("parallel", "parallel", "parallel")),
        name="dft_outer",
    )(*args)


def _spectral_fwd_kernel(fh_ref, fl_ref, y_ref, o_ref):
    o_ref[...] = _x3dot(fh_ref[...], fl_ref[...], y_ref[...])


def _spectral_conv_kernel(fh_ref, fl_ref, fih_ref, fil_ref, hf_ref, hb_ref, y_ref, o_ref):
    n2 = hf_ref.shape[0] // 2
    z = _x3dot(fh_ref[...], fl_ref[...], y_ref[...])
    zr, zi = z[:n2], z[n2:]
    kf = _combine_filter_spectrum(hf_ref[...], hb_ref[...])
    kr, ki = kf[:n2], kf[n2:]
    p = jnp.concatenate([zr * kr - zi * ki, zr * ki + zi * kr], axis=0)
    o_ref[...] = _x3dot(fih_ref[...], fil_ref[...], p)


def _spectral_mid(y, f_fwd, f_inv=None, spec=None, order=0):
    B, K1, R, D = y.shape
    ms = pl.BlockSpec((None, R, R), lambda k1, b: (k1, 0, 0))
    ys = pl.BlockSpec((None, None, R, D), lambda k1, b: (b, k1, 0, 0))
    if spec is None:
        kern, specs, args = _spectral_fwd_kernel, [ms, ms, ys], (*_split2(f_fwd), y)
    else:
        kern = _spectral_conv_kernel
        grp = lambda g: pl.BlockSpec((None, R, D), lambda k1, b: (k1, 0, g))
        specs = [ms, ms, ms, ms, grp(2 * order), grp(2 * order + 1), ys]
        args = (*_split2(f_fwd), *_split2(f_inv), spec, spec, y)
    return pl.pallas_call(
        kern, grid=(K1, B), in_specs=specs, out_specs=ys,
        out_shape=jax.ShapeDtypeStruct((B, K1, R, D), F32),
        compiler_params=_cparams(("parallel", "parallel")),
        name="dft_inner",
    )(*args)


def _dft_tables(L):
    n2 = HY_N2 if L % HY_N2 == 0 and L > HY_N2 * 2 else 1
    N = 2 * L
    n1 = N // n2
    k1h = n1 // 2
    ph = (jnp.arange(k1h, dtype=jnp.int32)[:, None] * 2 + 1) * jnp.arange(k1h, dtype=jnp.int32)[None, :]
    th = (ph % (2 * n1)).astype(F32) * (math.pi / n1)
    f_out = jnp.stack([jnp.cos(th), -jnp.sin(th)], axis=1).reshape(2 * k1h, k1h)
    f_out_inv = (2.0 / N) * f_out.T
    if n2 == 1:
        return n2, f_out, f_out_inv, None, None
    kk = jnp.arange(k1h, dtype=jnp.int32)[:, None, None] + n1 * jnp.arange(n2, dtype=jnp.int32)[None, :, None]
    ph = ((2 * kk + 1) * jnp.arange(n2, dtype=jnp.int32)[None, None, :]) % (2 * N)
    phi = ph.astype(F32) * (math.pi / N)
    c, s = jnp.cos(phi), jnp.sin(phi)
    f_in = jnp.concatenate([jnp.concatenate([c, s], axis=2), jnp.concatenate([-s, c], axis=2)], axis=1)
    ct, st = jnp.swapaxes(c, 1, 2), jnp.swapaxes(s, 1, 2)
    f_in_inv = jnp.concatenate([jnp.concatenate([ct, -st], axis=2), jnp.concatenate([st, ct], axis=2)], axis=1)
    return n2, f_out, f_out_inv, f_in, f_in_inv


def _long_conv_gated(u, ucol, spec, order, skip, gate, gcol, tables):
    n2, f_out, f_out_inv, f_in, f_in_inv = tables
    B, L, _ = u.shape
    D = D_MODEL
    k1h = L // n2
    per = D // DFT_TD
    u4 = u.reshape(B, k1h, n2, u.shape[-1])
    y = _leftmul(f_out, u4, xoff=ucol * per).reshape(B, k1h, 2 * n2, D)
    if f_in is None:
        cols = lambda g: spec[..., g * D:(g + 1) * D]
        kf = _combine_filter_spectrum(cols(2 * order), cols(2 * order + 1))
        yr, yi = y[:, :, 0], y[:, :, 1]
        kr, ki = kf[:, 0], kf[:, 1]
        q = jnp.stack([yr * kr - yi * ki, yr * ki + yi * kr], axis=2)
    else:
        q = _spectral_mid(y, f_in, f_in_inv, spec, order)
    out = _leftmul(f_out_inv, q.reshape(B, 2 * k1h, n2, D),
                   epilogue=(u4, ucol * per, skip[None, :], gate.reshape(B, k1h, n2, gate.shape[-1]), gcol * per))
    return out.reshape(B, L, D)


def _conv3_kernel(cur_ref, prev_ref, next_ref, w_ref, o_ref, *, nblk):
    i = pl.program_id(1)
    x = cur_ref[...]
    tm = x.shape[0]
    row = lax.broadcasted_iota(jnp.int32, x.shape, 0)
    before = jnp.where(i > 0, prev_ref[7:8, :], 0.0)
    after = jnp.where(i < nblk - 1, next_ref[0:1, :], 0.0)
    xp = jnp.where(row == 0, before, pltpu.roll(x, 1, axis=0))
    xn = jnp.where(row == tm - 1, after, pltpu.roll(x, tm - 1, axis=0))
    o_ref[...] = xp * w_ref[0:1, :] + x * w_ref[1:2, :] + xn * w_ref[2:3, :] + w_ref[3:4, :]


def _conv3(x, taps, bias):
    B, L, C = x.shape
    tm = min(512, L)
    tc = 512
    nblk = L // tm
    r8 = tm // 8
    w = jnp.concatenate([taps, bias[None, :], jnp.zeros((4, C), F32)], axis=0)
    return pl.pallas_call(
        functools.partial(_conv3_kernel, nblk=nblk),
        grid=(B, nblk, C // tc),
        in_specs=[pl.BlockSpec((None, tm, tc), lambda b, i, j: (b, i, j)),
                  pl.BlockSpec((None, 8, tc), lambda b, i, j: (b, jnp.maximum(i * r8 - 1, 0), j)),
                  pl.BlockSpec((None, 8, tc), lambda b, i, j: (b, jnp.minimum((i + 1) * r8, L // 8 - 1), j)),
                  pl.BlockSpec((8, tc), lambda b, i, j: (0, j))],
        out_specs=pl.BlockSpec((None, tm, tc), lambda b, i, j: (b, i, j)),
        out_shape=jax.ShapeDtypeStruct((B, L, C), F32),
        compiler_params=_cparams(("parallel", "parallel", "parallel")),
        name="conv3",
    )(x, x, x, w)


def _hyena_filter_spectra(L, w1, b1, w2, b2, w3, b3, w4, freq, tables):
    pos = jnp.arange(L, dtype=F32)
    t = (pos / max(L - 1, 1))[:, None]
    ang = (2.0 * math.pi / L) * pos[:, None] * jnp.linspace(1e-4, HY_BANDS - 1, HY_BANDS, dtype=F32)[None, :]
    z = jnp.concatenate([t, jnp.cos(ang), -jnp.sin(ang)], axis=-1)
    z = jnp.pad(z, ((0, 0), (0, 40 - HY_EMB)))
    w1p = jnp.pad(w1, ((0, 40 - HY_EMB), (0, 0)))
    hid = jnp.sin(freq[0] * (_mm(z, w1p, exact=True) + b1))
    hid = jnp.sin(freq[1] * (_mm(hid, w2, exact=True) + b2))
    hid = jnp.sin(freq[2] * (_mm(hid, w3, exact=True) + b3))
    filt = _mm(hid, w4, exact=True)
    deltas = jnp.linspace(math.log(HY_TARGET) / HY_FAST_DECAY, math.log(HY_TARGET) / HY_SLOW_DECAY,
                          D_MODEL, dtype=F32)
    window = jnp.exp(-t * jnp.abs(deltas))
    lag0 = (jnp.arange(L) > 0).astype(F32)[:, None]
    scale = jnp.concatenate([window, window * lag0] * HY_ORDER, axis=1)
    filt = filt * scale
    n2 = tables[0]
    k1h = L // n2
    C = HY_ORDER * 2 * D_MODEL
    y = _leftmul(tables[1], filt.reshape(1, k1h, n2, C), D=C).reshape(1, k1h, 2 * n2, C)
    return (y if tables[3] is None else _spectral_mid(y, tables[3]))[0]


def _combine_filter_spectrum(hf, hb):
    n2 = hf.shape[-2] // 2
    return jnp.concatenate([hf[..., :n2, :] + hb[..., :n2, :], hf[..., n2:, :] - hb[..., n2:, :]], axis=-2)


def _hyena_stream(resid, vecs, fprm, w_in, b_in, conv_w, conv_b, skip, w_o, b_o):
    B, L, D = resid.shape
    tables = _dft_tables(L)
    spec = _hyena_filter_spectra(L, *fprm, tables)
    proj = _conv3(_mm_fused(resid, w_in, vecs=vecs, norm=True, bias=b_in), conv_w, conv_b)
    z = _long_conv_gated(proj, 2, spec, 0, skip[0], proj, 0, tables)
    y = _long_conv_gated(z, 0, spec, 1, skip[1], proj, 1, tables)
    return _mm_fused(y, w_o, vecs=vecs, bias=b_o, resid=resid)


def _rmsnorm(x, g, eps=NORM_EPS):
    return x * lax.rsqrt(jnp.mean(x * x, axis=-1, keepdims=True) + eps) * g


RW_TM = 256


def _headsum(x, ones):
    hi, lo = _split2(x)
    cols = []
    for j in range(x.shape[1] // LANES):
        ln = slice(j * LANES, (j + 1) * LANES)
        cols.append(_dot(hi[:, ln], ones) + _dot(lo[:, ln], ones))
    return jnp.concatenate(cols, axis=1)


def _token_shift(h, h_above, h_below, grid_w):
    tm, D = h.shape
    row = lax.broadcasted_iota(jnp.int32, (tm, D), 0)
    lane = lax.broadcasted_iota(jnp.int32, (tm, D), 1)
    before = pltpu.roll(h, 1, axis=0)
    after = pltpu.roll(h, tm - 1, axis=0)
    if grid_w is None:
        return jnp.where(lane < D // 2, jnp.where(row == 0, 0.0, before), jnp.where(row == tm - 1, 0.0, after))
    col = row & (grid_w - 1)
    left = jnp.where(col == 0, 0.0, before)
    right = jnp.where(col == grid_w - 1, 0.0, after)
    up = jnp.concatenate([h_above, h[:tm - grid_w]], axis=0)
    down = jnp.concatenate([h[grid_w:], h_below], axis=0)
    q = D // 4
    return jnp.where(lane < q, left, jnp.where(lane < 2 * q, right, jnp.where(lane < 3 * q, up, down)))


def _rwkv_pre_kernel(*refs, vres, grid_w, nblk):
    it = iter(refs)
    x_ref = next(it)
    xa_ref, xb_ref = (next(it), next(it)) if grid_w else (None, None)
    mod_ref = next(it)
    vf_ref = next(it) if vres else None
    vec_ref, wr_ref, wk_ref, wv_ref, w1_ref, w2_ref, a1_ref, a2_ref, g1_ref, g2_ref = (next(it) for _ in range(10))
    v1_ref, v2_ref = (next(it), next(it)) if vres else (None, None)
    ones_ref = next(it)
    r_o, v_o, nkk_o, lw0_o, lw1_o, kd0_o, kd1_o, b0_o, b1_o, bonus_o, g_o = it
    vec = lambda i: vec_ref[i:i + 1, :]
    i = pl.program_id(1)
    h = _norm_mod(x_ref[...], mod_ref)
    if grid_w:
        h_above = jnp.where(i > 0, _norm_mod(xa_ref[...], mod_ref), 0.0)
        h_below = jnp.where(i < nblk - 1, _norm_mod(xb_ref[...], mod_ref), 0.0)
    else:
        h_above = h_below = None
    xx = _token_shift(h, h_above, h_below, grid_w) - h
    xr, xw, xk, xv, xa, xg = ((h + xx * vec(j)).astype(BF16) for j in range(6))
    r = _dot(xr, wr_ref[...])
    k = _dot(xk, wk_ref[...])
    v = _dot(xv, wv_ref[...])
    if vres:
        lora = _dot(_dot(xv, v1_ref[...]).astype(BF16), v2_ref[...])
        v = v + (vf_ref[...] - v) * jax.nn.sigmoid(vec(13) + lora)
    ones = ones_ref[...]
    kk = k * vec(10)
    kk = kk / jnp.maximum(jnp.sqrt(_headsum(kk * kk, ones)), 1e-12)
    r_o[...] = r
    v_o[...] = v
    nkk_o[...] = -kk
    bonus = jnp.zeros_like(r)
    for d, (lw_o, kd_o, b_o) in enumerate(((lw0_o, kd0_o, b0_o), (lw1_o, kd1_o, b1_o))):
        wl = vec(6 + d) + _dot(jnp.tanh(_dot(xw, w1_ref[d])).astype(BF16), w2_ref[d])
        lw_o[...] = -jax.nn.sigmoid(wl) * math.exp(-0.5)
        a = jax.nn.sigmoid(vec(8 + d) + _dot(_dot(xa, a1_ref[d]).astype(BF16), a2_ref[d]))
        kd = k * (1.0 + (a - 1.0) * vec(11))
        kd_o[...] = kd
        b_o[...] = kk * a
        bonus = bonus + _headsum(r * kd * vec(12), ones) * v
    bonus_o[...] = bonus
    g_o[...] = _dot(jax.nn.sigmoid(_dot(xg, g1_ref[...])).astype(BF16), g2_ref[...])


def _rwkv_post_kernel(y0_ref, y1_ref, bonus_ref, g_ref, res_ref, mod_ref, ln_ref, wo_ref, ones_ref, o_ref):
    ones = ones_ref[...]
    y = y0_ref[...] + y1_ref[...]
    yc = y - _headsum(y, ones) * (1.0 / RW_HEAD)
    var = _headsum(yc * yc, ones) * (1.0 / RW_HEAD)
    yn = yc * lax.rsqrt(var + RW_LN_EPS) * ln_ref[0:1, :] + ln_ref[1:2, :] + bonus_ref[...]
    out = _dot((yn * g_ref[...]).astype(BF16), wo_ref[...])
    o_ref[...] = res_ref[...] + mod_ref[2:3, :] * out


def _full(a):
    nd = a.ndim
    return pl.BlockSpec(a.shape, lambda b, i: (0,) * nd)


def _rwkv7_stream(resid, mods, grid_w, state0, v_first, mu, w_rkv, w_o, w0, w1, w2, a0, a1, a2,
                  g1, g2, k_k, k_a, r_k, ln_w, ln_b, vres, need_out=True):
    B, L, D = resid.shape
    tm = min(RW_TM, L)
    nblk = L // tm
    assert grid_w is not None or nblk == 1
    vres_on = vres is not None
    rows = [mu[j] for j in range(6)] + [w0[0], w0[1], a0[0], a0[1], k_k, k_a, r_k.reshape(D)]
    rows.append(vres[0] if vres_on else jnp.zeros((D,), F32))
    vecs = jnp.stack(rows + [jnp.zeros((D,), F32)] * (16 - len(rows)))
    ones = jnp.kron(jnp.eye(LANES // RW_HEAD, dtype=F32), jnp.ones((RW_HEAD, RW_HEAD), F32)).astype(BF16)
    bf = lambda t: t.astype(BF16)
    tok = pl.BlockSpec((None, tm, D), lambda b, i: (b, i, 0))
    mod_spec = pl.BlockSpec((None, 8, D), lambda b, i: (b, 0, 0))
    args, specs = [resid], [tok]
    if grid_w:
        per = tm // grid_w
        last_row = L // grid_w - 1
        args += [resid, resid]
        specs += [pl.BlockSpec((None, grid_w, D), lambda b, i: (b, jnp.maximum(i * per - 1, 0), 0)),
                  pl.BlockSpec((None, grid_w, D), lambda b, i: (b, jnp.minimum((i + 1) * per, last_row), 0))]
    args.append(mods)
    specs.append(mod_spec)
    if vres_on:
        args.append(v_first)
        specs.append(tok)
    consts = [vecs, bf(w_rkv[0]), bf(w_rkv[1]), bf(w_rkv[2]), bf(w1), bf(w2), bf(a1), bf(a2), bf(g1), bf(g2)]
    consts += [bf(vres[1]), bf(vres[2])] if vres_on else []
    consts.append(ones)
    outs = pl.pallas_call(
        functools.partial(_rwkv_pre_kernel, vres=vres_on, grid_w=grid_w, nblk=nblk),
        grid=(B, nblk),
        in_specs=specs + [_full(c) for c in consts],
        out_specs=[tok] * 11,
        out_shape=[jax.ShapeDtypeStruct((B, L, D), F32)] * 11,
        compiler_params=_cparams(("parallel", "parallel")),
        name="rwkv_pre",
    )(*args, *consts)
    r, v, nkk, lw0, lw1, kd0, kd1, b0, b1, bonus, g = outs
    y0, s0 = _rwkv_scan(r, lw0, kd0, v, nkk, b0, state0[0], reverse=False)
    y1, s1 = _rwkv_scan(r, lw1, kd1, v, nkk, b1, state0[1], reverse=True)
    states = jnp.stack([s0, s1])
    if not need_out:
        return None, states, v
    ln = jnp.stack([ln_w, ln_b] + [jnp.zeros((D,), F32)] * 6)
    wo = bf(w_o)
    new_resid = pl.pallas_call(
        _rwkv_post_kernel,
        grid=(B, L // tm),
        in_specs=[tok] * 5 + [mod_spec, _full(ln), _full(wo), _full(ones)],
        out_specs=tok,
        out_shape=jax.ShapeDtypeStruct((B, L, D), F32),
        compiler_params=_cparams(("parallel", "parallel")),
        name="rwkv_post",
    )(y0, y1, bonus, g, resid, mods, ln, wo, ones)
    return new_resid, states, v


def _mm_fused_kernel(*refs, norm, bias, resid):
    it = iter(refs)
    x_ref = next(it)
    vec_ref = next(it) if (norm or resid) else None
    w_ref = next(it)
    b_ref = next(it) if bias else None
    res_ref = next(it) if resid else None
    o_ref, xs_scr = next(it), next(it)

    @pl.when(pl.program_id(2) == 0)
    def _prep():
        x = x_ref[...]
        xs_scr[...] = (_norm_mod(x, vec_ref) if norm else x).astype(BF16)

    acc = _dot(xs_scr[...], w_ref[...])
    if bias:
        acc = acc + b_ref[...]
    o_ref[...] = res_ref[...] + vec_ref[2:3, :] * acc if resid else acc


def _mm_fused(x, w, vecs=None, norm=False, bias=None, resid=None):
    B, L, K = x.shape
    N = w.shape[1]
    tm = min(1024, L)
    tn = 512 if (N % 512 == 0 and resid is None) else N
    use_vec = norm or resid is not None
    specs = [pl.BlockSpec((None, tm, K), lambda b, i, j: (b, i, 0))]
    args = [x]
    if use_vec:
        specs.append(pl.BlockSpec((None, 8, vecs.shape[-1]), lambda b, i, j: (b, 0, 0)))
        args.append(vecs)
    specs.append(pl.BlockSpec((K, tn), lambda b, i, j: (0, j)))
    args.append(w.astype(BF16))
    if bias is not None:
        specs.append(pl.BlockSpec((1, tn), lambda b, i, j: (0, j)))
        args.append(bias[None, :])
    out_spec = pl.BlockSpec((None, tm, tn), lambda b, i, j: (b, i, j))
    if resid is not None:
        assert N == K == vecs.shape[-1] and tn == N
        specs.append(out_spec)
        args.append(resid)
    return pl.pallas_call(
        functools.partial(_mm_fused_kernel, norm=norm, bias=bias is not None, resid=resid is not None),
        grid=(B, L // tm, N // tn),
        in_specs=specs, out_specs=out_spec,
        out_shape=jax.ShapeDtypeStruct((B, L, N), F32),
        scratch_shapes=[pltpu.VMEM((tm, K), BF16)],
        compiler_params=_cparams(("parallel", "parallel", "arbitrary")),
        name="mm_fused",
    )(*args)


def _hgrn_post_kernel(o0_ref, o1_ref, g_ref, res_ref, vec_ref, gn_ref, wo_ref, ones_ref, o_ref):
    o = o0_ref[...] + o1_ref[...]
    ms = _headsum(o * o, ones_ref[...]) * (1.0 / HG_DK)
    on = o * lax.rsqrt(ms + NORM_EPS) * gn_ref[...]
    g = g_ref[...]
    z = (on * (g * jax.nn.sigmoid(g))).astype(BF16)
    o_ref[...] = res_ref[...] + vec_ref[2:3, :] * _dot(z, wo_ref[...])


def _hgrn2_stream(resid, vecs, state0, lb, w_in, gn, w_o, need_out=True):
    B, L, D = resid.shape
    proj = _mm_fused(resid, w_in, vecs=vecs, norm=True)
    o0, s0 = _gla_scan(proj, lb, state0[0], 0)
    o1, s1 = _gla_scan(proj, lb, state0[1], 1)
    states = jnp.stack([s0, s1])
    if not need_out:
        return None, states
    tm = min(RW_TM, L)
    tok = pl.BlockSpec((None, tm, D), lambda b, i: (b, i, 0))
    ones = jnp.ones((LANES, LANES), BF16)
    gn_row = jnp.tile(gn, D // gn.shape[0])[None, :]
    wo = w_o.astype(BF16)
    out = pl.pallas_call(
        _hgrn_post_kernel,
        grid=(B, L // tm),
        in_specs=[tok, tok, pl.BlockSpec((None, tm, D), lambda b, i: (b, i, 4)), tok,
                  pl.BlockSpec((None, 8, D), lambda b, i: (b, 0, 0)), _full(gn_row), _full(wo), _full(ones)],
        out_specs=tok,
        out_shape=jax.ShapeDtypeStruct((B, L, D), F32),
        compiler_params=_cparams(("parallel", "parallel")),
        name="hgrn_post",
    )(o0, o1, proj, resid, vecs, gn_row, wo, ones)
    return out, states


def kernel(x, c, ctx, c_ctx, norm_g, ada_w, ada_b, final_g,
           rw_mu, rw_wrkv, rw_wo, rw_w0, rw_w1, rw_w2, rw_a0, rw_a1, rw_a2,
           rw_v0, rw_v1, rw_v2, rw_g1, rw_g2, rw_kk, rw_ka, rw_rk, rw_lnw, rw_lnb,
           hy_win, hy_bin, hy_cw, hy_cb, hy_fw1, hy_fb1, hy_fw2, hy_fb2, hy_fw3, hy_fb3,
           hy_fw4, hy_freq, hy_skip, hy_wo, hy_bo,
           hg_win, hg_lb, hg_gn, hg_wo,
           ffn_w13, ffn_w2, moe_router, moe_w13, moe_w2):
    B = x.shape[0]
    depth = norm_g.shape[0]
    D = D_MODEL
    lat, cx = x, ctx
    v_first = None
    lbc = jnp.cumsum(jax.nn.softmax(hg_lb, axis=0), axis=0)
    lower = lbc - lbc[:1]
    cond = jnp.concatenate([jax.nn.silu(c), jax.nn.silu(c_ctx)[None, :]], axis=0)
    for i in range(depth):
        last = i == depth - 1
        if i % 2 == 0:
            mix_w13, mix_w2 = ffn_w13[i // 2].astype(BF16), ffn_w2[i // 2].astype(BF16)
        else:
            mix_w13, mix_w2 = moe_w13[i // 2].astype(BF16), moe_w2[i // 2].astype(BF16)
        mod = _mm(cond, ada_w[i]) + ada_b[i]
        mod_l = jnp.split(mod[:B, None, :], 6, axis=-1)
        mod_c = jnp.split(mod[B:, None, :], 6, axis=-1)
        kind, slot = i % N_MIXERS, i // N_MIXERS
        vec_l = _mod_rows(norm_g[i, 0], mod_l[0], mod_l[1], mod_l[2], B)
        vec_c = _mod_rows(norm_g[i, 0], mod_c[0], mod_c[1], mod_c[2], B)
        if kind == 0:
            vres = None if slot == 0 else (rw_v0[slot - 1], rw_v1[slot - 1], rw_v2[slot - 1])
            rw = (rw_mu[slot], rw_wrkv[slot], rw_wo[slot], rw_w0[slot], rw_w1[slot], rw_w2[slot],
                  rw_a0[slot], rw_a1[slot], rw_a2[slot], rw_g1[slot], rw_g2[slot], rw_kk[slot],
                  rw_ka[slot], rw_rk[slot], rw_lnw[slot], rw_lnb[slot], vres)
            zero = jnp.zeros((2, B, RW_H, RW_HEAD, RW_HEAD), F32)
            vf_c = None if v_first is None else v_first[0]
            vf_l = None if v_first is None else v_first[1]
            cx_new, s_ctx, v_c = _rwkv7_stream(cx, vec_c, None, zero, vf_c, *rw, need_out=not last)
            lat, _, v_l = _rwkv7_stream(lat, vec_l, GRID_W, s_ctx, vf_l, *rw)
            if not last:
                cx = cx_new
            if slot == 0:
                v_first = (v_c, v_l)
        elif kind == 1:
            fprm = (hy_fw1[slot], hy_fb1[slot], hy_fw2[slot], hy_fb2[slot], hy_fw3[slot],
                    hy_fb3[slot], hy_fw4[slot], hy_freq[slot])
            hprm = (hy_win[slot], hy_bin[slot], hy_cw[slot], hy_cb[slot], hy_skip[slot],
                    hy_wo[slot], hy_bo[slot])
            lat = _hyena_stream(lat, vec_l, fprm, *hprm)
            if not last:
                cx = _hyena_stream(cx, vec_c, fprm, *hprm)
        else:
            zero = jnp.zeros((2, B, HG_H, HG_DK, D // HG_H), F32)
            gprm = (lower[i], hg_win[slot], hg_gn[slot], hg_wo[slot])
            cx_new, s_ctx = _hgrn2_stream(cx, vec_c, zero, *gprm, need_out=not last)
            lat, _ = _hgrn2_stream(lat, vec_l, s_ctx, *gprm)
            if not last:
                cx = cx_new

        def channel_mix(s, mod):
            vecs = _mod_rows(norm_g[i, 1], mod[3], mod[4], mod[5], B)
            if i % 2 == 0:
                return _ffn(s, vecs, mix_w13, mix_w2)
            xn, logits = _norm_route(s, vecs, moe_router[i // 2])
            out = _moe(xn.reshape(-1, D), logits.reshape(-1, N_EXPERTS), mix_w13, mix_w2)
            return s + mod[5] * out.reshape(s.shape)

        lat = channel_mix(lat, mod_l)
        if not last:
            cx = channel_mix(cx, mod_c)
    return _rmsnorm(lat, final_g)
```

```python
import functools
import math

import jax
import jax.numpy as jnp
from jax import lax
from jax.experimental import pallas as pl
from jax.experimental.pallas import tpu as pltpu

F32 = jnp.float32
BF16 = jnp.bfloat16
HIGHEST = lax.Precision.HIGHEST

D_MODEL = 1024
GRID_W = 64
NORM_EPS = 1e-6
RW_HEAD = 64
RW_H = D_MODEL // RW_HEAD
RW_LN_EPS = 64e-5
HY_ORDER = 2
HY_EMB = 33
HY_BANDS = (HY_EMB - 1) // 2
HY_FAST_DECAY = 0.3
HY_SLOW_DECAY = 1.5
HY_TARGET = 1e-2
HG_DK = 128
HG_H = D_MODEL // HG_DK
HG_CHUNK = 32
N_EXPERTS = 8
TOP_K = 2
N_MIXERS = 3

LANES = 128
VMEM_LIMIT = 56 * 1024 * 1024

MM_TM, MM_TN = 1024, 512
FFN_TF = 512
NORM_TM = 512
CONV_TC = 512
RW_TM = 256
RW_CHUNK = 64
RW_BLOCK = 512
RW_PAIRS_PER_STEP = 2
GLA_BLOCK = 256
GLA_HEADS_PER_STEP = 2
MOE_TILE = 2048
MOE_SUB = 256
MOE_SLAB = 512
HY_N2 = 128
DFT_TD = 512


def _cparams(sem):
    return pltpu.CompilerParams(dimension_semantics=sem, vmem_limit_bytes=VMEM_LIMIT)


def _dot(a, b, prec=None):
    return jnp.dot(a, b, preferred_element_type=F32, precision=prec)


def _split2(x):
    hi = x.astype(BF16)
    lo = (x - hi.astype(F32)).astype(BF16)
    return hi, lo


def _bdot(a, b, nt=False):
    dn = (((1,), (1 if nt else 0,)), ((), ()))
    return lax.dot_general(a.astype(BF16), b.astype(BF16), dn, preferred_element_type=F32)


def _x3dot(fh, fl, x):
    xh, xl = _split2(x)
    return _dot(fh, xh) + (_dot(fh, xl) + _dot(fl, xh))


def _cumsum_dot(tri, x):
    hi = x.astype(BF16)
    r1 = x - hi.astype(F32)
    mid = r1.astype(BF16)
    lo = (r1 - mid.astype(F32)).astype(BF16)
    n = x.shape[1]
    g = _dot(tri, jnp.concatenate([hi, mid, lo], axis=1))
    return g[:, :n] + (g[:, n:2 * n] + g[:, 2 * n:])


def _mm_kernel(a_ref, b_ref, o_ref, *, exact):
    if exact:
        o_ref[...] = _x3dot(*_split2(a_ref[...]), b_ref[...])
    else:
        o_ref[...] = _dot(a_ref[...], b_ref[...])


def _mm(a, b, *, exact=False):
    M, K = a.shape
    N = b.shape[1]
    dt = F32 if exact else BF16
    a = a.astype(dt)
    b = b.astype(dt)
    Mp = -(-M // 8) * 8
    if Mp != M:
        a = jnp.pad(a, ((0, Mp - M), (0, 0)))
    tm = MM_TM // 2 if Mp % (MM_TM // 2) == 0 else Mp
    tn = MM_TN if N % MM_TN == 0 else N
    out = pl.pallas_call(
        functools.partial(_mm_kernel, exact=exact),
        grid=(Mp // tm, N // tn),
        in_specs=[pl.BlockSpec((tm, K), lambda i, j: (i, 0)),
                  pl.BlockSpec((K, tn), lambda i, j: (0, j))],
        out_specs=pl.BlockSpec((tm, tn), lambda i, j: (i, j)),
        out_shape=jax.ShapeDtypeStruct((Mp, N), F32),
        compiler_params=_cparams(("parallel", "parallel")),
        name="mm",
    )(a, b)
    return out[:M] if Mp != M else out


def _norm_mod(x, vec_ref):
    xn = x * lax.rsqrt(jnp.mean(x * x, axis=-1, keepdims=True) + NORM_EPS)
    return xn * vec_ref[0:1, :] + vec_ref[1:2, :]


def _mod_rows(g, shift, scale, gate, B):
    D = g.shape[-1]
    rows = [jnp.broadcast_to(g * (1.0 + scale[:, 0]), (B, D)), jnp.broadcast_to(shift[:, 0], (B, D)),
            jnp.broadcast_to(gate[:, 0], (B, D))]
    return jnp.stack(rows + [jnp.zeros((B, D), F32)] * 5, axis=1)


def _ffn_kernel(x_ref, vec_ref, wg_ref, wu_ref, w2_ref, o_ref, xn_scr, acc_ref, *, n_f):
    f = pl.program_id(2)

    @pl.when(f == 0)
    def _first():
        xn_scr[...] = _norm_mod(x_ref[...], vec_ref).astype(BF16)
        acc_ref[...] = jnp.zeros_like(acc_ref)

    x = xn_scr[...]
    gate = _dot(x, wg_ref[...])
    up = _dot(x, wu_ref[...])
    h = (gate * jax.nn.sigmoid(gate) * up).astype(BF16)
    acc_ref[...] += _dot(h, w2_ref[...])

    @pl.when(f == n_f - 1)
    def _store():
        o_ref[...] = x_ref[...] + vec_ref[2:3, :] * acc_ref[...]


def _ffn(x, vecs, w13, w2):
    B, L, D = x.shape
    F = w13.shape[1] // 2
    tm = min(MM_TM, L)
    tf = FFN_TF if F % FFN_TF == 0 else FFN_TF // 2
    n_f = F // tf
    tok = pl.BlockSpec((None, tm, D), lambda b, i, f: (b, i, 0))
    return pl.pallas_call(
        functools.partial(_ffn_kernel, n_f=n_f),
        grid=(B, L // tm, n_f),
        in_specs=[tok,
                  pl.BlockSpec((None, 8, D), lambda b, i, f: (b, 0, 0)),
                  pl.BlockSpec((D, tf), lambda b, i, f: (0, f)),
                  pl.BlockSpec((D, tf), lambda b, i, f: (0, f + n_f)),
                  pl.BlockSpec((tf, D), lambda b, i, f: (f, 0))],
        out_specs=tok,
        out_shape=jax.ShapeDtypeStruct((B, L, D), F32),
        scratch_shapes=[pltpu.VMEM((tm, D), BF16), pltpu.VMEM((tm, D), F32)],
        compiler_params=_cparams(("parallel", "parallel", "arbitrary")),
        name="ffn",
    )(x, vecs, w13, w13, w2)


def _norm_route_kernel(x_ref, vec_ref, wr_ref, xn_ref, lg_ref):
    xn = _norm_mod(x_ref[...], vec_ref)
    xn_ref[...] = xn.astype(BF16)
    lg_ref[...] = _dot(xn, wr_ref[...], HIGHEST)


def _norm_route(x, vecs, w_router):
    B, L, D = x.shape
    E = w_router.shape[1]
    tm = min(NORM_TM, L)
    tok = pl.BlockSpec((None, tm, D), lambda b, i: (b, i, 0))
    return pl.pallas_call(
        _norm_route_kernel,
        grid=(B, L // tm),
        in_specs=[tok, pl.BlockSpec((None, 8, D), lambda b, i: (b, 0, 0)), pl.BlockSpec((D, E), lambda b, i: (0, 0))],
        out_specs=[tok, pl.BlockSpec((None, tm, E), lambda b, i: (b, i, 0))],
        out_shape=[jax.ShapeDtypeStruct((B, L, D), BF16), jax.ShapeDtypeStruct((B, L, E), F32)],
        compiler_params=_cparams(("parallel", "parallel")),
        name="norm_route",
    )(x, vecs, w_router)


def _gated_add_kernel(s_ref, y_ref, vec_ref, *rest, final):
    o_ref = rest[-1]
    z = s_ref[...] + vec_ref[2:3, :] * y_ref[...]
    if final:
        z = z * lax.rsqrt(jnp.mean(z * z, axis=-1, keepdims=True) + NORM_EPS) * rest[0][...]
    o_ref[...] = z


def _gated_add(s, y, vecs, out_gain=None):
    B, L, D = s.shape
    tm = min(NORM_TM, L)
    tok = pl.BlockSpec((None, tm, D), lambda b, i: (b, i, 0))
    final = out_gain is not None
    specs = [tok, tok, pl.BlockSpec((None, 8, D), lambda b, i: (b, 0, 0))]
    args = [s, y, vecs]
    if final:
        specs.append(pl.BlockSpec((1, D), lambda b, i: (0, 0)))
        args.append(out_gain[None, :])
    return pl.pallas_call(
        functools.partial(_gated_add_kernel, final=final),
        grid=(B, L // tm), in_specs=specs, out_specs=tok,
        out_shape=jax.ShapeDtypeStruct((B, L, D), F32),
        compiler_params=_cparams(("parallel", "parallel")),
        name="gated_add",
    )(*args)


def _moe_kernel(cnt_ref, x_ref, rrow_ref, rcol_ref, gcol_ref, wg_ref, wu_ref, w2_ref, o_ref, xc_scr, y_scr,
                *, n_f, tm):
    i, e, f = pl.program_id(0), pl.program_id(1), pl.program_id(2)
    sub = MOE_SUB
    slab = min(MOE_SLAB, tm)
    nslab = tm // slab
    n_sub = (cnt_ref[i, e, nslab] + (sub - 1)) // sub

    def touches(s, j):
        return (cnt_ref[i, e, j] < (s + 1) * sub) & (cnt_ref[i, e, j + 1] > s * sub)

    @pl.when((e == 0) & (f == 0))
    def _zero():
        o_ref[...] = jnp.zeros_like(o_ref)

    @pl.when(f == 0)
    def _compact():
        def body(s, carry):
            rows = pl.ds(pl.multiple_of(s * sub, sub), sub)
            y_scr[rows, :] = jnp.zeros((sub, y_scr.shape[1]), F32)
            for j in range(nslab):
                cols = slice(j * slab, (j + 1) * slab)

                @pl.when(touches(s, j))
                def _add():
                    ridx = lax.broadcasted_iota(jnp.int32, (sub, slab), 0) + s * sub
                    onehot = jnp.where(rrow_ref[:, cols] == ridx, 1.0, 0.0).astype(BF16)
                    y_scr[rows, :] += _dot(onehot, x_ref[cols, :])
            xc_scr[rows, :] = y_scr[rows, :].astype(BF16)
            return carry
        lax.fori_loop(0, n_sub, body, 0)

    def expert(s, first):
        rows = pl.ds(pl.multiple_of(s * sub, sub), sub)
        xs = xc_scr[rows, :]
        gate = _dot(xs, wg_ref[...])
        up = _dot(xs, wu_ref[...])
        h = (gate * jax.nn.sigmoid(gate) * up).astype(BF16)
        part = _dot(h, w2_ref[...])
        y_scr[rows, :] = part if first else y_scr[rows, :] + part

    @pl.when(f == 0)
    def _first():
        lax.fori_loop(0, n_sub, lambda s, c: (expert(s, True), c)[1], 0)

    @pl.when(f > 0)
    def _rest():
        lax.fori_loop(0, n_sub, lambda s, c: (expert(s, False), c)[1], 0)

    @pl.when(f == n_f - 1)
    def _scatter():
        def body(s, carry):
            y = y_scr[pl.ds(pl.multiple_of(s * sub, sub), sub), :].astype(BF16)
            for j in range(nslab):
                rows = slice(j * slab, (j + 1) * slab)

                @pl.when(touches(s, j))
                def _add():
                    cidx = lax.broadcasted_iota(jnp.int32, (slab, sub), 1) + s * sub
                    onehot_t = jnp.where(rcol_ref[rows, :] == cidx, 1.0, 0.0).astype(BF16)
                    o_ref[rows, :] += gcol_ref[rows, :] * _dot(onehot_t, y)
            return carry
        lax.fori_loop(0, n_sub, body, 0)


def _moe(x, logits, w13, w2):
    T, D = x.shape
    E, _, F2 = w13.shape
    F = F2 // 2
    tm = min(MOE_TILE, T)
    nt = T // tm
    tf = FFN_TF
    n_f = F // tf
    top_val, top_idx = lax.top_k(logits, TOP_K)
    gates = jax.nn.softmax(top_val, axis=-1)
    onehot = jax.nn.one_hot(top_idx, E, dtype=F32)
    sel = jnp.sum(onehot, axis=1).astype(jnp.int32).reshape(nt, tm, E)
    gate_dense = jnp.sum(onehot * gates[..., None], axis=1).reshape(nt, tm, E)
    rank = jnp.where(sel > 0, jnp.cumsum(sel, axis=1) - sel, -1)
    rank = jnp.swapaxes(rank, 1, 2)
    slab = min(MOE_SLAB, tm)
    per_slab = jnp.sum(sel.reshape(nt, tm // slab, slab, E), axis=2)
    counts = jnp.concatenate([jnp.zeros((nt, 1, E), jnp.int32), jnp.cumsum(per_slab, axis=1)], axis=1)
    counts = jnp.swapaxes(counts, 1, 2)
    gcol = jnp.swapaxes(gate_dense, 1, 2)[..., None]
    tile = lambda shape, imap: pl.BlockSpec(shape, imap)
    return pl.pallas_call(
        functools.partial(_moe_kernel, n_f=n_f, tm=tm),
        grid_spec=pltpu.PrefetchScalarGridSpec(
            num_scalar_prefetch=1,
            grid=(nt, E, n_f),
            in_specs=[tile((tm, D), lambda i, e, f, c: (i, 0)),
                      tile((None, None, 1, tm), lambda i, e, f, c: (i, e, 0, 0)),
                      tile((None, None, tm, 1), lambda i, e, f, c: (i, e, 0, 0)),
                      tile((None, None, tm, 1), lambda i, e, f, c: (i, e, 0, 0)),
                      tile((None, D, tf), lambda i, e, f, c: (e, 0, f)),
                      tile((None, D, tf), lambda i, e, f, c: (e, 0, f + n_f)),
                      tile((None, tf, D), lambda i, e, f, c: (e, f, 0))],
            out_specs=tile((tm, D), lambda i, e, f, c: (i, 0)),
            scratch_shapes=[pltpu.VMEM((-(-tm // MOE_SUB) * MOE_SUB, D), BF16),
                            pltpu.VMEM((-(-tm // MOE_SUB) * MOE_SUB, D), F32)]),
        out_shape=jax.ShapeDtypeStruct((T, D), F32),
        compiler_params=_cparams(("parallel", "arbitrary", "arbitrary")),
        name="moe",
    )(counts, x, rank[:, :, None, :], rank[..., None], gcol, w13, w13, w2)


def _rwkv_scan_kernel(r_ref, lw_ref, k_ref, v_ref, a_ref, b_ref, h0_ref, y_ref, hT_ref, H_scr,
                      *, reverse, nblk, nchunk, npl):
    C = RW_CHUNK
    half = LANES // 2
    i = pl.program_id(2)

    @pl.when(i == 0)
    def _init():
        H_scr[...] = h0_ref[...]

    t_idx = lax.broadcasted_iota(jnp.int32, (C, LANES), 0)
    s_idx = lax.broadcasted_iota(jnp.int32, (C, LANES), 1) & (half - 1)
    tt = lax.broadcasted_iota(jnp.int32, (C, C), 0)
    ss = lax.broadcasted_iota(jnp.int32, (C, C), 1)
    if reverse:
        strict, incl, tri = s_idx > t_idx, s_idx >= t_idx, (ss >= tt).astype(BF16)
    else:
        strict, incl, tri = s_idx < t_idx, s_idx <= t_idx, (ss <= tt).astype(BF16)
    eye_lp = (s_idx == t_idx).astype(F32)
    lane = lax.broadcasted_iota(jnp.int32, (1, LANES), 1)
    m0 = (lane < half).astype(F32)
    m1 = 1.0 - m0
    rr = lax.broadcasted_iota(jnp.int32, (LANES, LANES), 0)
    cc = lax.broadcasted_iota(jnp.int32, (LANES, LANES), 1)
    mask_bd = ((rr < half) == (cc < half)).astype(F32)

    def bd(x):
        return jnp.concatenate([x * m0, x * m1], axis=0)

    order = list(range(nchunk - 1, -1, -1) if reverse else range(nchunk))
    units = [(slice(c * C, (c + 1) * C), slice(q * LANES, (q + 1) * LANES)) for c in order for q in range(npl)]
    ld = lambda ref: [ref[sl, ln] for sl, ln in units]
    r, lw, k, v, a, b = ld(r_ref), ld(lw_ref), ld(k_ref), ld(v_ref), ld(a_ref), ld(b_ref)
    G = [_cumsum_dot(tri, x) for x in lw]
    eG = [jnp.exp(g) for g in G]
    eGn = [jnp.exp(-g) for g in G]
    rt = [x * e for x, e in zip(r, eG)]
    at = [x * jnp.exp(g - l) for x, g, l in zip(a, G, lw)]
    kt = [x * e for x, e in zip(k, eGn)]
    bt = [x * e for x, e in zip(b, eGn)]
    M = [_bdot(jnp.concatenate([x, y], axis=0), jnp.concatenate([bd(z), bd(w)], axis=0), nt=True)
         for x, y, z, w in zip(at, rt, bt, kt)]
    Nm = [jnp.where(strict, m[:C, :LANES], 0.0) for m in M]
    Aak = [jnp.where(strict, m[:C, LANES:], 0.0) for m in M]
    Arb = [jnp.where(incl, m[C:, :LANES], 0.0) for m in M]
    Ark = [jnp.where(incl, m[C:, LANES:], 0.0) for m in M]
    T = None
    for lvl in range(1, int(math.log2(C)) + 1):
        same = (t_idx >> lvl) == (s_idx >> lvl)
        t_hi = ((t_idx >> (lvl - 1)) & 1) == 1
        s_hi = ((s_idx >> (lvl - 1)) & 1) == 1
        off = same & (s_hi & ~t_hi if reverse else t_hi & ~s_hi)
        Noff = [jnp.where(off, n, 0.0) for n in Nm]
        if T is None:
            T = [eye_lp + n for n in Noff]
        else:
            DN = [_bdot(t, bd(n)) for t, n in zip(T, Noff)]
            T = [t + _bdot(dn, bd(t)) for t, dn in zip(T, DN)]
    bdv = [bd(x) for x in v]
    X0 = [_bdot(x, y) for x, y in zip(Aak, bdv)]
    WU = [_bdot(t, jnp.concatenate([bd(x), bd(y)], axis=1)) for t, x, y in zip(T, at, X0)]
    W = [x[:, :LANES] for x in WU]
    U0 = [x[:, LANES:] for x in WU]
    RY = [_bdot(x, jnp.concatenate([bd(y), bd(z)], axis=1)) for x, y, z in zip(Arb, W, U0)]
    Rh = [x + y[:, :LANES] for x, y in zip(rt, RY)]
    Y0 = [y[:, LANES:] + _bdot(x, z) for y, x, z in zip(RY, Ark, bdv)]
    dPhi = [mask_bd * _bdot(x.T, y) for x, y in zip(W, bt)]
    Psi = [mask_bd * _bdot(jnp.concatenate([x, y], axis=0).T, jnp.concatenate([z, w], axis=0))
           for x, y, z, w in zip(U0, v, bt, kt)]
    e_end = [jnp.exp(g[0:1, :] if reverse else g[C - 1:C, :]) for g in G]

    Hs = [H_scr[q] for q in range(npl)]
    for u, (sl, ln) in enumerate(units):
        q = u % npl
        H = Hs[q]
        y_ref[sl, ln] = _bdot(Rh[u], H, nt=True) + Y0[u]
        Hs[q] = (H + _bdot(H, dPhi[u]) + Psi[u]) * e_end[u]
    for q in range(npl):
        H_scr[q] = Hs[q]

    @pl.when(i == nblk - 1)
    def _fin():
        for q in range(npl):
            hT_ref[q] = Hs[q]


def _pair_states(S):
    B, H, N, _ = S.shape
    S5 = S.reshape(B, H // 2, 2, N, N)
    eye = jnp.eye(2, dtype=S.dtype)
    return jnp.einsum('bpivk,ij->bpivjk', S5, eye).reshape(B, H // 2, 2 * N, 2 * N)


def _unpair_states(Sp):
    B, P, N2, _ = Sp.shape
    N = N2 // 2
    S6 = Sp.reshape(B, P, 2, N, 2, N)
    return jnp.stack([S6[:, :, 0, :, 0, :], S6[:, :, 1, :, 1, :]], axis=2).reshape(B, 2 * P, N, N)


def _rwkv_scan(r, lw, k, v, a, b, state0, reverse):
    B, L, D = r.shape
    bt = min(RW_BLOCK, L)
    nblk = L // bt
    npl = RW_PAIRS_PER_STEP
    npair = D // LANES
    blk = (lambda i: nblk - 1 - i) if reverse else (lambda i: i)
    tok = pl.BlockSpec((None, bt, npl * LANES), lambda bb, p, i: (bb, blk(i), p))
    st = pl.BlockSpec((None, npl, LANES, LANES), lambda bb, p, i: (bb, p, 0, 0))
    y, hT = pl.pallas_call(
        functools.partial(_rwkv_scan_kernel, reverse=reverse, nblk=nblk, nchunk=bt // RW_CHUNK, npl=npl),
        grid=(B, npair // npl, nblk),
        in_specs=[tok] * 6 + [st],
        out_specs=[tok, st],
        out_shape=[jax.ShapeDtypeStruct((B, L, D), F32),
                   jax.ShapeDtypeStruct((B, npair, LANES, LANES), F32)],
        scratch_shapes=[pltpu.VMEM((npl, LANES, LANES), F32)],
        compiler_params=_cparams(("parallel", "parallel", "arbitrary")),
        name="rwkv_scan_rev" if reverse else "rwkv_scan_fwd",
    )(r, lw, k, v, a, b, _pair_states(state0))
    return y, _unpair_states(hT)


def _gla_kernel(q_ref, f_ref, v_ref, lb_ref, h0_ref, o_ref, hT_ref, H_scr, *, reverse, nblk, npl):
    C = HG_CHUNK
    sh = int(math.log2(C))
    bt = q_ref.shape[0]
    nchunk = bt // C
    i = pl.program_id(2)

    @pl.when(i == 0)
    def _init():
        H_scr[...] = h0_ref[...]

    tt = lax.broadcasted_iota(jnp.int32, (bt, bt), 0)
    ss = lax.broadcasted_iota(jnp.int32, (bt, bt), 1)
    same = (tt >> sh) == (ss >> sh)
    mid = ((tt >> sh) << sh) + (C // 2 if reverse else C // 2 - 1)
    if reverse:
        incl, upto_mid = same & (ss >= tt), same & (ss >= mid)
    else:
        incl, upto_mid = same & (ss <= tt), same & (ss <= mid)
    one = lambda m: jnp.where(m, 1.0, 0.0)
    sums = jnp.concatenate([one(incl), one(incl) - one(upto_mid), one(same) - one(incl)], axis=0).astype(BF16)

    order = list(range(nchunk - 1, -1, -1) if reverse else range(nchunk))
    heads = [slice(p * LANES, (p + 1) * LANES) for p in range(npl)]
    q = [jax.nn.silu(q_ref[:, ln]) for ln in heads]
    fg = [lb_ref[:, ln] + (1.0 - lb_ref[:, ln]) * jax.nn.sigmoid(f_ref[:, ln]) for ln in heads]
    v = [v_ref[:, ln] for ln in heads]
    k = [1.0 - f for f in fg]
    Gs = [_cumsum_dot(sums, jnp.log(f)) for f in fg]
    Gabs = [g[:bt] for g in Gs]
    Grel = [g[bt:2 * bt] for g in Gs]
    Gend = [g[2 * bt:] for g in Gs]
    scores = [jnp.where(incl, _bdot(x * jnp.exp(g), y * jnp.exp(-g), nt=True), 0.0)
              for x, y, g in zip(q, k, Grel)]
    o_intra = [_bdot(s, x) for s, x in zip(scores, v)]
    qa = [x * jnp.exp(g) for x, g in zip(q, Gabs)]
    kend = [x * jnp.exp(g) for x, g in zip(k, Gend)]
    rows = [slice(c * C, (c + 1) * C) for c in range(nchunk)]
    KV = [[_bdot(x[r].T, y[r]) for r in rows] for x, y in zip(v, kend)]
    dec = [[jnp.exp(ga[r][0:1, :] + ge[r][0:1, :]) for r in rows] for ga, ge in zip(Gabs, Gend)]

    Hs = [H_scr[p] for p in range(npl)]
    for c in order:
        for p in range(npl):
            o_ref[rows[c], heads[p]] = o_intra[p][rows[c]] + _bdot(qa[p][rows[c]], Hs[p], nt=True)
            Hs[p] = Hs[p] * dec[p][c] + KV[p][c]
    for p in range(npl):
        H_scr[p] = Hs[p]

    @pl.when(i == nblk - 1)
    def _fin():
        for p in range(npl):
            hT_ref[p] = Hs[p]


def _gla_scan(proj, lb, state0, d):
    B, L, D5 = proj.shape
    D = D5 // 5
    reverse = d == 1
    bt = GLA_BLOCK
    nblk = L // bt
    npl = GLA_HEADS_PER_STEP
    nh = D // LANES
    ncb = nh // npl
    blk = (lambda i: nblk - 1 - i) if reverse else (lambda i: i)
    col = lambda off: pl.BlockSpec((None, bt, npl * LANES), lambda bb, p, i: (bb, blk(i), off * ncb + p))
    st = pl.BlockSpec((None, npl, LANES, LANES), lambda bb, p, i: (bb, p, 0, 0))
    o, hT = pl.pallas_call(
        functools.partial(_gla_kernel, reverse=reverse, nblk=nblk, npl=npl),
        grid=(B, ncb, nblk),
        in_specs=[col(0), col(1 + d), col(3), pl.BlockSpec((1, npl * LANES), lambda bb, p, i: (0, p)), st],
        out_specs=[col(0), st],
        out_shape=[jax.ShapeDtypeStruct((B, L, D), F32),
                   jax.ShapeDtypeStruct((B, nh, LANES, LANES), F32)],
        scratch_shapes=[pltpu.VMEM((npl, LANES, LANES), F32)],
        compiler_params=_cparams(("parallel", "parallel", "arbitrary")),
        name="gla_scan_rev" if reverse else "gla_scan_fwd",
    )(proj, proj, proj, lb[d][None, :], jnp.swapaxes(state0, 2, 3))
    return o, jnp.swapaxes(hT, 2, 3)


def _leftmul_kernel(fh_ref, fl_ref, x_ref, o_ref, xs_scr, *, nj):
    for j in range(nj):
        xs_scr[...] = x_ref[:, j, :]
        o_ref[:, j, :] = _x3dot(fh_ref[...], fl_ref[...], xs_scr[...])


def _leftmul_gate_kernel(fh_ref, fl_ref, x_ref, u_ref, s_ref, g_ref, o_ref, xs_scr, *, nj):
    for j in range(nj):
        xs_scr[...] = x_ref[:, j, :]
        y = _x3dot(fh_ref[...], fl_ref[...], xs_scr[...])
        o_ref[:, j, :] = (y + u_ref[:, j, :] * s_ref[...]) * g_ref[:, j, :]


def _leftmul(f, x, xoff=0, epilogue=None, D=D_MODEL):
    B, K, J, _ = x.shape
    M = f.shape[0]
    tj = 8 if J % 8 == 0 else J
    td = DFT_TD
    fh, fl = _split2(f)
    fs = pl.BlockSpec((M, K), lambda b, j, d: (0, 0))
    col = lambda rows, off: pl.BlockSpec((None, rows, tj, td), lambda b, j, d: (b, 0, j, off + d))
    if epilogue is None:
        kern, specs, args = _leftmul_kernel, [fs, fs, col(K, xoff)], (fh, fl, x)
    else:
        u, uoff, skip, gate, goff = epilogue
        kern = _leftmul_gate_kernel
        specs = [fs, fs, col(K, xoff), col(M, uoff), pl.BlockSpec((1, td), lambda b, j, d: (0, d)), col(M, goff)]
        args = (fh, fl, x, u, skip, gate)
    return pl.pallas_call(
        functools.partial(kern, nj=tj), grid=(B, J // tj, D // td), in_specs=specs, out_specs=col(M, 0),
        out_shape=jax.ShapeDtypeStruct((B, M, J, D), F32),
        scratch_shapes=[pltpu.VMEM((K, td), F32)],
        compiler_params=_cparams(("parallel", "parallel", "parallel")),
        name="dft_outer",
    )(*args)


def _spectral_fwd_kernel(fh_ref, fl_ref, y_ref, o_ref):
    o_ref[...] = _x3dot(fh_ref[...], fl_ref[...], y_ref[...])


def _spectral_conv_kernel(fh_ref, fl_ref, fih_ref, fil_ref, hf_ref, hb_ref, y_ref, o_ref):
    n2 = hf_ref.shape[0] // 2
    z = _x3dot(fh_ref[...], fl_ref[...], y_ref[...])
    zr, zi = z[:n2], z[n2:]
    kf = _combine_filter_spectrum(hf_ref[...], hb_ref[...])
    kr, ki = kf[:n2], kf[n2:]
    p = jnp.concatenate([zr * kr - zi * ki, zr * ki + zi * kr], axis=0)
    o_ref[...] = _x3dot(fih_ref[...], fil_ref[...], p)


def _spectral_mid(y, f_fwd, f_inv=None, spec=None, order=0):
    B, K1, R, D = y.shape
    ms = pl.BlockSpec((None, R, R), lambda k1, b: (k1, 0, 0))
    ys = pl.BlockSpec((None, None, R, D), lambda k1, b: (b, k1, 0, 0))
    if spec is None:
        kern, specs, args = _spectral_fwd_kernel, [ms, ms, ys], (*_split2(f_fwd), y)
    else:
        kern = _spectral_conv_kernel
        grp = lambda g: pl.BlockSpec((None, R, D), lambda k1, b: (k1, 0, g))
        specs = [ms, ms, ms, ms, grp(2 * order), grp(2 * order + 1), ys]
        args = (*_split2(f_fwd), *_split2(f_inv), spec, spec, y)
    return pl.pallas_call(
        kern, grid=(K1, B), in_specs=specs, out_specs=ys,
        out_shape=jax.ShapeDtypeStruct((B, K1, R, D), F32),
        compiler_params=_cparams(("parallel", "parallel")),
        name="dft_inner",
    )(*args)


def _dft_tables(L):
    n2 = HY_N2 if L % HY_N2 == 0 and L > HY_N2 * 2 else 1
    N = 2 * L
    n1 = N // n2
    k1h = n1 // 2
    ph = (jnp.arange(k1h, dtype=jnp.int32)[:, None] * 2 + 1) * jnp.arange(k1h, dtype=jnp.int32)[None, :]
    th = (ph % (2 * n1)).astype(F32) * (math.pi / n1)
    f_out = jnp.stack([jnp.cos(th), -jnp.sin(th)], axis=1).reshape(2 * k1h, k1h)
    f_out_inv = (2.0 / N) * f_out.T
    if n2 == 1:
        return n2, f_out, f_out_inv, None, None
    kk = jnp.arange(k1h, dtype=jnp.int32)[:, None, None] + n1 * jnp.arange(n2, dtype=jnp.int32)[None, :, None]
    ph = ((2 * kk + 1) * jnp.arange(n2, dtype=jnp.int32)[None, None, :]) % (2 * N)
    phi = ph.astype(F32) * (math.pi / N)
    c, s = jnp.cos(phi), jnp.sin(phi)
    f_in = jnp.concatenate([jnp.concatenate([c, s], axis=2), jnp.concatenate([-s, c], axis=2)], axis=1)
    ct, st = jnp.swapaxes(c, 1, 2), jnp.swapaxes(s, 1, 2)
    f_in_inv = jnp.concatenate([jnp.concatenate([ct, -st], axis=2), jnp.concatenate([st, ct], axis=2)], axis=1)
    return n2, f_out, f_out_inv, f_in, f_in_inv


def _long_conv_gated(u, ucol, spec, order, skip, gate, gcol, tables):
    n2, f_out, f_out_inv, f_in, f_in_inv = tables
    B, L, _ = u.shape
    D = D_MODEL
    k1h = L // n2
    per = D // DFT_TD
    u4 = u.reshape(B, k1h, n2, u.shape[-1])
    y = _leftmul(f_out, u4, xoff=ucol * per).reshape(B, k1h, 2 * n2, D)
    if f_in is None:
        cols = lambda g: spec[..., g * D:(g + 1) * D]
        kf = _combine_filter_spectrum(cols(2 * order), cols(2 * order + 1))
        yr, yi = y[:, :, 0], y[:, :, 1]
        kr, ki = kf[:, 0], kf[:, 1]
        q = jnp.stack([yr * kr - yi * ki, yr * ki + yi * kr], axis=2)
    else:
        q = _spectral_mid(y, f_in, f_in_inv, spec, order)
    out = _leftmul(f_out_inv, q.reshape(B, 2 * k1h, n2, D),
                   epilogue=(u4, ucol * per, skip[None, :], gate.reshape(B, k1h, n2, gate.shape[-1]), gcol * per))
    return out.reshape(B, L, D)


def _conv3_kernel(cur_ref, prev_ref, next_ref, w_ref, o_ref, *, nblk):
    i = pl.program_id(1)
    x = cur_ref[...]
    tm = x.shape[0]
    row = lax.broadcasted_iota(jnp.int32, x.shape, 0)
    before = jnp.where(i > 0, prev_ref[7:8, :], 0.0)
    after = jnp.where(i < nblk - 1, next_ref[0:1, :], 0.0)
    xp = jnp.where(row == 0, before, pltpu.roll(x, 1, axis=0))
    xn = jnp.where(row == tm - 1, after, pltpu.roll(x, tm - 1, axis=0))
    o_ref[...] = xp * w_ref[0:1, :] + x * w_ref[1:2, :] + xn * w_ref[2:3, :] + w_ref[3:4, :]


def _conv3(x, taps, bias):
    B, L, C = x.shape
    tm = min(NORM_TM, L)
    tc = CONV_TC
    nblk = L // tm
    r8 = tm // 8
    w = jnp.concatenate([taps, bias[None, :], jnp.zeros((4, C), F32)], axis=0)
    return pl.pallas_call(
        functools.partial(_conv3_kernel, nblk=nblk),
        grid=(B, nblk, C // tc),
        in_specs=[pl.BlockSpec((None, tm, tc), lambda b, i, j: (b, i, j)),
                  pl.BlockSpec((None, 8, tc), lambda b, i, j: (b, jnp.maximum(i * r8 - 1, 0), j)),
                  pl.BlockSpec((None, 8, tc), lambda b, i, j: (b, jnp.minimum((i + 1) * r8, L // 8 - 1), j)),
                  pl.BlockSpec((8, tc), lambda b, i, j: (0, j))],
        out_specs=pl.BlockSpec((None, tm, tc), lambda b, i, j: (b, i, j)),
        out_shape=jax.ShapeDtypeStruct((B, L, C), F32),
        compiler_params=_cparams(("parallel", "parallel", "parallel")),
        name="conv3",
    )(x, x, x, w)


def _hyena_filter_spectra(L, w1, b1, w2, b2, w3, b3, w4, freq, tables):
    pos = jnp.arange(L, dtype=F32)
    t = (pos / max(L - 1, 1))[:, None]
    ang = (2.0 * math.pi / L) * pos[:, None] * jnp.linspace(1e-4, HY_BANDS - 1, HY_BANDS, dtype=F32)[None, :]
    z = jnp.concatenate([t, jnp.cos(ang), -jnp.sin(ang)], axis=-1)
    z = jnp.pad(z, ((0, 0), (0, 40 - HY_EMB)))
    w1p = jnp.pad(w1, ((0, 40 - HY_EMB), (0, 0)))
    hid = jnp.sin(freq[0] * (_mm(z, w1p, exact=True) + b1))
    hid = jnp.sin(freq[1] * (_mm(hid, w2, exact=True) + b2))
    hid = jnp.sin(freq[2] * (_mm(hid, w3, exact=True) + b3))
    filt = _mm(hid, w4, exact=True)
    deltas = jnp.linspace(math.log(HY_TARGET) / HY_FAST_DECAY, math.log(HY_TARGET) / HY_SLOW_DECAY,
                          D_MODEL, dtype=F32)
    window = jnp.exp(-t * jnp.abs(deltas))
    lag0 = (jnp.arange(L) > 0).astype(F32)[:, None]
    scale = jnp.concatenate([window, window * lag0] * HY_ORDER, axis=1)
    filt = filt * scale
    n2 = tables[0]
    k1h = L // n2
    C = HY_ORDER * 2 * D_MODEL
    y = _leftmul(tables[1], filt.reshape(1, k1h, n2, C), D=C).reshape(1, k1h, 2 * n2, C)
    return (y if tables[3] is None else _spectral_mid(y, tables[3]))[0]


def _combine_filter_spectrum(hf, hb):
    n2 = hf.shape[-2] // 2
    return jnp.concatenate([hf[..., :n2, :] + hb[..., :n2, :], hf[..., n2:, :] - hb[..., n2:, :]], axis=-2)


def _hyena_stream(resid, vecs, fprm, w_in, b_in, conv_w, conv_b, skip, w_o, b_o):
    B, L, D = resid.shape
    tables = _dft_tables(L)
    spec = _hyena_filter_spectra(L, *fprm, tables)
    proj = _conv3(_mm_fused(resid, w_in, vecs=vecs, norm=True, bias=b_in), conv_w, conv_b)
    z = _long_conv_gated(proj, 2, spec, 0, skip[0], proj, 0, tables)
    y = _long_conv_gated(z, 0, spec, 1, skip[1], proj, 1, tables)
    return _mm_fused(y, w_o, vecs=vecs, bias=b_o, resid=resid)


def _headsum(x, ones):
    hi, lo = _split2(x)
    cols = []
    for j in range(x.shape[1] // LANES):
        ln = slice(j * LANES, (j + 1) * LANES)
        cols.append(_dot(hi[:, ln], ones) + _dot(lo[:, ln], ones))
    return jnp.concatenate(cols, axis=1)


def _token_shift(h, h_above, h_below, grid_w):
    tm, D = h.shape
    row = lax.broadcasted_iota(jnp.int32, (tm, D), 0)
    lane = lax.broadcasted_iota(jnp.int32, (tm, D), 1)
    before = pltpu.roll(h, 1, axis=0)
    after = pltpu.roll(h, tm - 1, axis=0)
    if grid_w is None:
        return jnp.where(lane < D // 2, jnp.where(row == 0, 0.0, before), jnp.where(row == tm - 1, 0.0, after))
    col = row & (grid_w - 1)
    left = jnp.where(col == 0, 0.0, before)
    right = jnp.where(col == grid_w - 1, 0.0, after)
    up = jnp.concatenate([h_above, h[:tm - grid_w]], axis=0)
    down = jnp.concatenate([h[grid_w:], h_below], axis=0)
    q = D // 4
    return jnp.where(lane < q, left, jnp.where(lane < 2 * q, right, jnp.where(lane < 3 * q, up, down)))


def _rwkv_pre_kernel(*refs, vres, grid_w, nblk):
    it = iter(refs)
    x_ref = next(it)
    xa_ref, xb_ref = (next(it), next(it)) if grid_w else (None, None)
    mod_ref = next(it)
    vf_ref = next(it) if vres else None
    vec_ref, wr_ref, wk_ref, wv_ref, w1_ref, w2_ref, a1_ref, a2_ref, g1_ref, g2_ref = (next(it) for _ in range(10))
    v1_ref, v2_ref = (next(it), next(it)) if vres else (None, None)
    ones_ref = next(it)
    r_o, v_o, nkk_o, lw0_o, lw1_o, kd0_o, kd1_o, b0_o, b1_o, bonus_o, g_o = it
    vec = lambda i: vec_ref[i:i + 1, :]
    i = pl.program_id(1)
    h = _norm_mod(x_ref[...], mod_ref)
    if grid_w:
        h_above = jnp.where(i > 0, _norm_mod(xa_ref[...], mod_ref), 0.0)
        h_below = jnp.where(i < nblk - 1, _norm_mod(xb_ref[...], mod_ref), 0.0)
    else:
        h_above = h_below = None
    xx = _token_shift(h, h_above, h_below, grid_w) - h
    xr, xw, xk, xv, xa, xg = ((h + xx * vec(j)).astype(BF16) for j in range(6))
    r = _dot(xr, wr_ref[...])
    k = _dot(xk, wk_ref[...])
    v = _dot(xv, wv_ref[...])
    if vres:
        lora = _dot(_dot(xv, v1_ref[...]).astype(BF16), v2_ref[...])
        v = v + (vf_ref[...] - v) * jax.nn.sigmoid(vec(13) + lora)
    ones = ones_ref[...]
    kk = k * vec(10)
    kk = kk / jnp.maximum(jnp.sqrt(_headsum(kk * kk, ones)), 1e-12)
    r_o[...] = r
    v_o[...] = v
    nkk_o[...] = -kk
    bonus = jnp.zeros_like(r)
    for d, (lw_o, kd_o, b_o) in enumerate(((lw0_o, kd0_o, b0_o), (lw1_o, kd1_o, b1_o))):
        wl = vec(6 + d) + _dot(jnp.tanh(_dot(xw, w1_ref[d])).astype(BF16), w2_ref[d])
        lw_o[...] = -jax.nn.sigmoid(wl) * math.exp(-0.5)
        a = jax.nn.sigmoid(vec(8 + d) + _dot(_dot(xa, a1_ref[d]).astype(BF16), a2_ref[d]))
        kd = k * (1.0 + (a - 1.0) * vec(11))
        kd_o[...] = kd
        b_o[...] = kk * a
        bonus = bonus + _headsum(r * kd * vec(12), ones) * v
    bonus_o[...] = bonus
    g_o[...] = _dot(jax.nn.sigmoid(_dot(xg, g1_ref[...])).astype(BF16), g2_ref[...])


def _rwkv_post_kernel(y0_ref, y1_ref, bonus_ref, g_ref, res_ref, mod_ref, ln_ref, wo_ref, ones_ref, o_ref):
    ones = ones_ref[...]
    y = y0_ref[...] + y1_ref[...]
    yc = y - _headsum(y, ones) * (1.0 / RW_HEAD)
    var = _headsum(yc * yc, ones) * (1.0 / RW_HEAD)
    yn = yc * lax.rsqrt(var + RW_LN_EPS) * ln_ref[0:1, :] + ln_ref[1:2, :] + bonus_ref[...]
    out = _dot((yn * g_ref[...]).astype(BF16), wo_ref[...])
    o_ref[...] = res_ref[...] + mod_ref[2:3, :] * out


def _full(a):
    nd = a.ndim
    return pl.BlockSpec(a.shape, lambda b, i: (0,) * nd)


def _rwkv7_stream(resid, mods, grid_w, state0, v_first, mu, w_rkv, w_o, w0, w1, w2, a0, a1, a2,
                  g1, g2, k_k, k_a, r_k, ln_w, ln_b, vres, need_out=True):
    B, L, D = resid.shape
    tm = min(RW_TM, L)
    nblk = L // tm
    assert grid_w is not None or nblk == 1
    vres_on = vres is not None
    rows = [mu[j] for j in range(6)] + [w0[0], w0[1], a0[0], a0[1], k_k, k_a, r_k.reshape(D)]
    rows.append(vres[0] if vres_on else jnp.zeros((D,), F32))
    vecs = jnp.stack(rows + [jnp.zeros((D,), F32)] * (16 - len(rows)))
    ones = jnp.kron(jnp.eye(LANES // RW_HEAD, dtype=F32), jnp.ones((RW_HEAD, RW_HEAD), F32)).astype(BF16)
    bf = lambda t: t.astype(BF16)
    tok = pl.BlockSpec((None, tm, D), lambda b, i: (b, i, 0))
    mod_spec = pl.BlockSpec((None, 8, D), lambda b, i: (b, 0, 0))
    args, specs = [resid], [tok]
    if grid_w:
        per = tm // grid_w
        last_row = L // grid_w - 1
        args += [resid, resid]
        specs += [pl.BlockSpec((None, grid_w, D), lambda b, i: (b, jnp.maximum(i * per - 1, 0), 0)),
                  pl.BlockSpec((None, grid_w, D), lambda b, i: (b, jnp.minimum((i + 1) * per, last_row), 0))]
    args.append(mods)
    specs.append(mod_spec)
    if vres_on:
        args.append(v_first)
        specs.append(tok)
    consts = [vecs, bf(w_rkv[0]), bf(w_rkv[1]), bf(w_rkv[2]), bf(w1), bf(w2), bf(a1), bf(a2), bf(g1), bf(g2)]
    consts += [bf(vres[1]), bf(vres[2])] if vres_on else []
    consts.append(ones)
    outs = pl.pallas_call(
        functools.partial(_rwkv_pre_kernel, vres=vres_on, grid_w=grid_w, nblk=nblk),
        grid=(B, nblk),
        in_specs=specs + [_full(c) for c in consts],
        out_specs=[tok] * 11,
        out_shape=[jax.ShapeDtypeStruct((B, L, D), F32)] * 11,
        compiler_params=_cparams(("parallel", "parallel")),
        name="rwkv_pre",
    )(*args, *consts)
    r, v, nkk, lw0, lw1, kd0, kd1, b0, b1, bonus, g = outs
    y0, s0 = _rwkv_scan(r, lw0, kd0, v, nkk, b0, state0[0], reverse=False)
    y1, s1 = _rwkv_scan(r, lw1, kd1, v, nkk, b1, state0[1], reverse=True)
    states = jnp.stack([s0, s1])
    if not need_out:
        return None, states, v
    ln = jnp.stack([ln_w, ln_b] + [jnp.zeros((D,), F32)] * 6)
    wo = bf(w_o)
    new_resid = pl.pallas_call(
        _rwkv_post_kernel,
        grid=(B, L // tm),
        in_specs=[tok] * 5 + [mod_spec, _full(ln), _full(wo), _full(ones)],
        out_specs=tok,
        out_shape=jax.ShapeDtypeStruct((B, L, D), F32),
        compiler_params=_cparams(("parallel", "parallel")),
        name="rwkv_post",
    )(y0, y1, bonus, g, resid, mods, ln, wo, ones)
    return new_resid, states, v


def _mm_fused_kernel(*refs, norm, bias, resid):
    it = iter(refs)
    x_ref = next(it)
    vec_ref = next(it) if (norm or resid) else None
    w_ref = next(it)
    b_ref = next(it) if bias else None
    res_ref = next(it) if resid else None
    o_ref, xs_scr = next(it), next(it)

    @pl.when(pl.program_id(2) == 0)
    def _prep():
        x = x_ref[...]
        xs_scr[...] = (_norm_mod(x, vec_ref) if norm else x).astype(BF16)

    acc = _dot(xs_scr[...], w_ref[...])
    if bias:
        acc = acc + b_ref[...]
    o_ref[...] = res_ref[...] + vec_ref[2:3, :] * acc if resid else acc


def _mm_fused(x, w, vecs=None, norm=False, bias=None, resid=None):
    B, L, K = x.shape
    N = w.shape[1]
    tm = min(MM_TM, L)
    tn = MM_TN if (N % MM_TN == 0 and resid is None) else N
    use_vec = norm or resid is not None
    specs = [pl.BlockSpec((None, tm, K), lambda b, i, j: (b, i, 0))]
    args = [x]
    if use_vec:
        specs.append(pl.BlockSpec((None, 8, vecs.shape[-1]), lambda b, i, j: (b, 0, 0)))
        args.append(vecs)
    specs.append(pl.BlockSpec((K, tn), lambda b, i, j: (0, j)))
    args.append(w.astype(BF16))
    if bias is not None:
        specs.append(pl.BlockSpec((1, tn), lambda b, i, j: (0, j)))
        args.append(bias[None, :])
    out_spec = pl.BlockSpec((None, tm, tn), lambda b, i, j: (b, i, j))
    if resid is not None:
        assert N == K == vecs.shape[-1] and tn == N
        specs.append(out_spec)
        args.append(resid)
    return pl.pallas_call(
        functools.partial(_mm_fused_kernel, norm=norm, bias=bias is not None, resid=resid is not None),
        grid=(B, L // tm, N // tn),
        in_specs=specs, out_specs=out_spec,
        out_shape=jax.ShapeDtypeStruct((B, L, N), F32),
        scratch_shapes=[pltpu.VMEM((tm, K), BF16)],
        compiler_params=_cparams(("parallel", "parallel", "arbitrary")),
        name="mm_fused",
    )(*args)


def _hgrn_post_kernel(o0_ref, o1_ref, g_ref, res_ref, vec_ref, gn_ref, wo_ref, ones_ref, o_ref):
    o = o0_ref[...] + o1_ref[...]
    ms = _headsum(o * o, ones_ref[...]) * (1.0 / HG_DK)
    on = o * lax.rsqrt(ms + NORM_EPS) * gn_ref[...]
    g = g_ref[...]
    z = (on * (g * jax.nn.sigmoid(g))).astype(BF16)
    o_ref[...] = res_ref[...] + vec_ref[2:3, :] * _dot(z, wo_ref[...])


def _hgrn2_stream(resid, vecs, state0, lb, w_in, gn, w_o, need_out=True):
    B, L, D = resid.shape
    proj = _mm_fused(resid, w_in, vecs=vecs, norm=True)
    o0, s0 = _gla_scan(proj, lb, state0[0], 0)
    o1, s1 = _gla_scan(proj, lb, state0[1], 1)
    states = jnp.stack([s0, s1])
    if not need_out:
        return None, states
    tm = min(RW_TM, L)
    tok = pl.BlockSpec((None, tm, D), lambda b, i: (b, i, 0))
    ones = jnp.ones((LANES, LANES), BF16)
    gn_row = jnp.tile(gn, D // gn.shape[0])[None, :]
    wo = w_o.astype(BF16)
    out = pl.pallas_call(
        _hgrn_post_kernel,
        grid=(B, L // tm),
        in_specs=[tok, tok, pl.BlockSpec((None, tm, D), lambda b, i: (b, i, 4)), tok,
                  pl.BlockSpec((None, 8, D), lambda b, i: (b, 0, 0)), _full(gn_row), _full(wo), _full(ones)],
        out_specs=tok,
        out_shape=jax.ShapeDtypeStruct((B, L, D), F32),
        compiler_params=_cparams(("parallel", "parallel")),
        name="hgrn_post",
    )(o0, o1, proj, resid, vecs, gn_row, wo, ones)
    return out, states


def kernel(x, c, ctx, c_ctx, norm_g, ada_w, ada_b, final_g,
           rw_mu, rw_wrkv, rw_wo, rw_w0, rw_w1, rw_w2, rw_a0, rw_a1, rw_a2,
           rw_v0, rw_v1, rw_v2, rw_g1, rw_g2, rw_kk, rw_ka, rw_rk, rw_lnw, rw_lnb,
           hy_win, hy_bin, hy_cw, hy_cb, hy_fw1, hy_fb1, hy_fw2, hy_fb2, hy_fw3, hy_fb3,
           hy_fw4, hy_freq, hy_skip, hy_wo, hy_bo,
           hg_win, hg_lb, hg_gn, hg_wo,
           ffn_w13, ffn_w2, moe_router, moe_w13, moe_w2):
    B = x.shape[0]
    depth = norm_g.shape[0]
    D = D_MODEL
    lat, cx = x, ctx
    v_first = None
    lbc = jnp.cumsum(jax.nn.softmax(hg_lb, axis=0), axis=0)
    lower = lbc - lbc[:1]
    cond = jnp.concatenate([jax.nn.silu(c), jax.nn.silu(c_ctx)[None, :]], axis=0)
    for i in range(depth):
        last = i == depth - 1
        if i % 2 == 0:
            mix_w13, mix_w2 = ffn_w13[i // 2].astype(BF16), ffn_w2[i // 2].astype(BF16)
        else:
            mix_w13, mix_w2 = moe_w13[i // 2].astype(BF16), moe_w2[i // 2].astype(BF16)
        mod = _mm(cond, ada_w[i]) + ada_b[i]
        mod_l = jnp.split(mod[:B, None, :], 6, axis=-1)
        mod_c = jnp.split(mod[B:, None, :], 6, axis=-1)
        kind, slot = i % N_MIXERS, i // N_MIXERS
        vec_l = _mod_rows(norm_g[i, 0], mod_l[0], mod_l[1], mod_l[2], B)
        vec_c = _mod_rows(norm_g[i, 0], mod_c[0], mod_c[1], mod_c[2], B)
        if kind == 0:
            vres = None if slot == 0 else (rw_v0[slot - 1], rw_v1[slot - 1], rw_v2[slot - 1])
            rw = (rw_mu[slot], rw_wrkv[slot], rw_wo[slot], rw_w0[slot], rw_w1[slot], rw_w2[slot],
                  rw_a0[slot], rw_a1[slot], rw_a2[slot], rw_g1[slot], rw_g2[slot], rw_kk[slot],
                  rw_ka[slot], rw_rk[slot], rw_lnw[slot], rw_lnb[slot], vres)
            zero = jnp.zeros((2, B, RW_H, RW_HEAD, RW_HEAD), F32)
            vf_c = None if v_first is None else v_first[0]
            vf_l = None if v_first is None else v_first[1]
            cx_new, s_ctx, v_c = _rwkv7_stream(cx, vec_c, None, zero, vf_c, *rw, need_out=not last)
            lat, _, v_l = _rwkv7_stream(lat, vec_l, GRID_W, s_ctx, vf_l, *rw)
            if not last:
                cx = cx_new
            if slot == 0:
                v_first = (v_c, v_l)
        elif kind == 1:
            fprm = (hy_fw1[slot], hy_fb1[slot], hy_fw2[slot], hy_fb2[slot], hy_fw3[slot],
                    hy_fb3[slot], hy_fw4[slot], hy_freq[slot])
            hprm = (hy_win[slot], hy_bin[slot], hy_cw[slot], hy_cb[slot], hy_skip[slot],
                    hy_wo[slot], hy_bo[slot])
            lat = _hyena_stream(lat, vec_l, fprm, *hprm)
            if not last:
                cx = _hyena_stream(cx, vec_c, fprm, *hprm)
        else:
            zero = jnp.zeros((2, B, HG_H, HG_DK, D // HG_H), F32)
            gprm = (lower[i], hg_win[slot], hg_gn[slot], hg_wo[slot])
            cx_new, s_ctx = _hgrn2_stream(cx, vec_c, zero, *gprm, need_out=not last)
            lat, _ = _hgrn2_stream(lat, vec_l, s_ctx, *gprm)
            if not last:
                cx = cx_new

        def channel_mix(s, mod, out_gain=None):
            vecs = _mod_rows(norm_g[i, 1], mod[3], mod[4], mod[5], B)
            if i % 2 == 0:
                return _ffn(s, vecs, mix_w13, mix_w2)
            xn, logits = _norm_route(s, vecs, moe_router[i // 2])
            out = _moe(xn.reshape(-1, D), logits.reshape(-1, N_EXPERTS), mix_w13, mix_w2)
            return _gated_add(s, out.reshape(s.shape), vecs, out_gain)

        assert depth % 2 == 0
        lat = channel_mix(lat, mod_l, final_g if last else None)
        if not last:
            cx = channel_mix(cx, mod_c)
    return lat
```

```python
import functools
import math

import jax
import jax.numpy as jnp
from jax import lax
from jax.experimental import pallas as pl
from jax.experimental.pallas import tpu as pltpu

F32 = jnp.float32
BF16 = jnp.bfloat16
HIGHEST = lax.Precision.HIGHEST

D_MODEL = 1024
GRID_W = 64
NORM_EPS = 1e-6
RW_HEAD = 64
RW_H = D_MODEL // RW_HEAD
RW_LN_EPS = 64e-5
HY_ORDER = 2
HY_EMB = 33
HY_BANDS = (HY_EMB - 1) // 2
HY_FAST_DECAY = 0.3
HY_SLOW_DECAY = 1.5
HY_TARGET = 1e-2
HG_DK = 128
HG_H = D_MODEL // HG_DK
HG_CHUNK = 32
N_EXPERTS = 8
TOP_K = 2
N_MIXERS = 3

LANES = 128
VMEM_LIMIT = 56 * 1024 * 1024

MM_TM, MM_TN = 1024, 512
FFN_TF = 512
NORM_TM = 512
CONV_TC = 512
RW_TM = 256
RW_CHUNK = 64
RW_BLOCK = 512
RW_PAIRS_PER_STEP = 4
GLA_BLOCK = 256
GLA_HEADS_PER_STEP = 8
MOE_TILE = 2048
MOE_SUB = 256
MOE_SLAB = 512
HY_N2 = 128
DFT_TD = 512


def _cparams(sem):
    return pltpu.CompilerParams(dimension_semantics=sem, vmem_limit_bytes=VMEM_LIMIT)


def _dot(a, b, prec=None):
    return jnp.dot(a, b, preferred_element_type=F32, precision=prec)


def _split2(x):
    hi = x.astype(BF16)
    lo = (x - hi.astype(F32)).astype(BF16)
    return hi, lo


def _bdot(a, b, nt=False):
    dn = (((1,), (1 if nt else 0,)), ((), ()))
    return lax.dot_general(a.astype(BF16), b.astype(BF16), dn, preferred_element_type=F32)


def _x3dot(fh, fl, x):
    xh, xl = _split2(x)
    return _dot(fh, xh) + (_dot(fh, xl) + _dot(fl, xh))


def _cumsum_dot(tri, x):
    hi = x.astype(BF16)
    r1 = x - hi.astype(F32)
    mid = r1.astype(BF16)
    lo = (r1 - mid.astype(F32)).astype(BF16)
    n = x.shape[1]
    g = _dot(tri, jnp.concatenate([hi, mid, lo], axis=1))
    return g[:, :n] + (g[:, n:2 * n] + g[:, 2 * n:])


def _mm_kernel(a_ref, b_ref, o_ref, *, exact):
    if exact:
        o_ref[...] = _x3dot(*_split2(a_ref[...]), b_ref[...])
    else:
        o_ref[...] = _dot(a_ref[...], b_ref[...])


def _mm(a, b, *, exact=False):
    M, K = a.shape
    N = b.shape[1]
    dt = F32 if exact else BF16
    a = a.astype(dt)
    b = b.astype(dt)
    Mp = -(-M // 8) * 8
    if Mp != M:
        a = jnp.pad(a, ((0, Mp - M), (0, 0)))
    tm = MM_TM // 2 if Mp % (MM_TM // 2) == 0 else Mp
    tn = MM_TN if N % MM_TN == 0 else N
    out = pl.pallas_call(
        functools.partial(_mm_kernel, exact=exact),
        grid=(Mp // tm, N // tn),
        in_specs=[pl.BlockSpec((tm, K), lambda i, j: (i, 0)),
                  pl.BlockSpec((K, tn), lambda i, j: (0, j))],
        out_specs=pl.BlockSpec((tm, tn), lambda i, j: (i, j)),
        out_shape=jax.ShapeDtypeStruct((Mp, N), F32),
        compiler_params=_cparams(("parallel", "parallel")),
        name="mm",
    )(a, b)
    return out[:M] if Mp != M else out


def _norm_mod(x, vec_ref):
    xn = x * lax.rsqrt(jnp.mean(x * x, axis=-1, keepdims=True) + NORM_EPS)
    return xn * vec_ref[0:1, :] + vec_ref[1:2, :]


def _mod_rows(g, shift, scale, gate, B):
    D = g.shape[-1]
    rows = [jnp.broadcast_to(g * (1.0 + scale[:, 0]), (B, D)), jnp.broadcast_to(shift[:, 0], (B, D)),
            jnp.broadcast_to(gate[:, 0], (B, D))]
    return jnp.stack(rows + [jnp.zeros((B, D), F32)] * 5, axis=1)


def _ffn_kernel(x_ref, vec_ref, wg_ref, wu_ref, w2_ref, o_ref, xn_scr, acc_ref, *, n_f):
    f = pl.program_id(2)

    @pl.when(f == 0)
    def _first():
        xn_scr[...] = _norm_mod(x_ref[...], vec_ref).astype(BF16)
        acc_ref[...] = jnp.zeros_like(acc_ref)

    x = xn_scr[...]
    gate = _dot(x, wg_ref[...])
    up = _dot(x, wu_ref[...])
    h = (gate * jax.nn.sigmoid(gate) * up).astype(BF16)
    acc_ref[...] += _dot(h, w2_ref[...])

    @pl.when(f == n_f - 1)
    def _store():
        o_ref[...] = x_ref[...] + vec_ref[2:3, :] * acc_ref[...]


def _ffn(x, vecs, w13, w2):
    B, L, D = x.shape
    F = w13.shape[1] // 2
    tm = min(MM_TM, L)
    tf = FFN_TF if F % FFN_TF == 0 else FFN_TF // 2
    n_f = F // tf
    tok = pl.BlockSpec((None, tm, D), lambda b, i, f: (b, i, 0))
    return pl.pallas_call(
        functools.partial(_ffn_kernel, n_f=n_f),
        grid=(B, L // tm, n_f),
        in_specs=[tok,
                  pl.BlockSpec((None, 8, D), lambda b, i, f: (b, 0, 0)),
                  pl.BlockSpec((D, tf), lambda b, i, f: (0, f)),
                  pl.BlockSpec((D, tf), lambda b, i, f: (0, f + n_f)),
                  pl.BlockSpec((tf, D), lambda b, i, f: (f, 0))],
        out_specs=tok,
        out_shape=jax.ShapeDtypeStruct((B, L, D), F32),
        scratch_shapes=[pltpu.VMEM((tm, D), BF16), pltpu.VMEM((tm, D), F32)],
        compiler_params=_cparams(("parallel", "parallel", "arbitrary")),
        name="ffn",
    )(x, vecs, w13, w13, w2)


def _norm_route_kernel(x_ref, vec_ref, wr_ref, xn_ref, lg_ref):
    xn = _norm_mod(x_ref[...], vec_ref)
    xn_ref[...] = xn.astype(BF16)
    lg_ref[...] = _dot(xn, wr_ref[...], HIGHEST)


def _norm_route(x, vecs, w_router):
    B, L, D = x.shape
    E = w_router.shape[1]
    tm = min(NORM_TM, L)
    tok = pl.BlockSpec((None, tm, D), lambda b, i: (b, i, 0))
    return pl.pallas_call(
        _norm_route_kernel,
        grid=(B, L // tm),
        in_specs=[tok, pl.BlockSpec((None, 8, D), lambda b, i: (b, 0, 0)), pl.BlockSpec((D, E), lambda b, i: (0, 0))],
        out_specs=[tok, pl.BlockSpec((None, tm, E), lambda b, i: (b, i, 0))],
        out_shape=[jax.ShapeDtypeStruct((B, L, D), BF16), jax.ShapeDtypeStruct((B, L, E), F32)],
        compiler_params=_cparams(("parallel", "parallel")),
        name="norm_route",
    )(x, vecs, w_router)


def _gated_add_kernel(s_ref, y_ref, vec_ref, *rest, final):
    o_ref = rest[-1]
    z = s_ref[...] + vec_ref[2:3, :] * y_ref[...]
    if final:
        z = z * lax.rsqrt(jnp.mean(z * z, axis=-1, keepdims=True) + NORM_EPS) * rest[0][...]
    o_ref[...] = z


def _gated_add(s, y, vecs, out_gain=None):
    B, L, D = s.shape
    tm = min(NORM_TM, L)
    tok = pl.BlockSpec((None, tm, D), lambda b, i: (b, i, 0))
    final = out_gain is not None
    specs = [tok, tok, pl.BlockSpec((None, 8, D), lambda b, i: (b, 0, 0))]
    args = [s, y, vecs]
    if final:
        specs.append(pl.BlockSpec((1, D), lambda b, i: (0, 0)))
        args.append(out_gain[None, :])
    return pl.pallas_call(
        functools.partial(_gated_add_kernel, final=final),
        grid=(B, L // tm), in_specs=specs, out_specs=tok,
        out_shape=jax.ShapeDtypeStruct((B, L, D), F32),
        compiler_params=_cparams(("parallel", "parallel")),
        name="gated_add",
    )(*args)


def _moe_kernel(cnt_ref, x_ref, rrow_ref, rcol_ref, gcol_ref, wg_ref, wu_ref, w2_ref, o_ref, xc_scr, y_scr,
                *, n_f, tm):
    i, e, f = pl.program_id(0), pl.program_id(1), pl.program_id(2)
    sub = MOE_SUB
    slab = min(MOE_SLAB, tm)
    nslab = tm // slab
    n_sub = (cnt_ref[i, e, nslab] + (sub - 1)) // sub

    def touches(s, j):
        return (cnt_ref[i, e, j] < (s + 1) * sub) & (cnt_ref[i, e, j + 1] > s * sub)

    @pl.when((e == 0) & (f == 0))
    def _zero():
        o_ref[...] = jnp.zeros_like(o_ref)

    @pl.when(f == 0)
    def _compact():
        def body(s, carry):
            rows = pl.ds(pl.multiple_of(s * sub, sub), sub)
            y_scr[rows, :] = jnp.zeros((sub, y_scr.shape[1]), F32)
            for j in range(nslab):
                cols = slice(j * slab, (j + 1) * slab)

                @pl.when(touches(s, j))
                def _add():
                    ridx = lax.broadcasted_iota(jnp.int32, (sub, slab), 0) + s * sub
                    onehot = jnp.where(rrow_ref[:, cols] == ridx, 1.0, 0.0).astype(BF16)
                    y_scr[rows, :] += _dot(onehot, x_ref[cols, :])
            xc_scr[rows, :] = y_scr[rows, :].astype(BF16)
            return carry
        lax.fori_loop(0, n_sub, body, 0)

    def expert(s, first):
        rows = pl.ds(pl.multiple_of(s * sub, sub), sub)
        xs = xc_scr[rows, :]
        gate = _dot(xs, wg_ref[...])
        up = _dot(xs, wu_ref[...])
        h = (gate * jax.nn.sigmoid(gate) * up).astype(BF16)
        part = _dot(h, w2_ref[...])
        y_scr[rows, :] = part if first else y_scr[rows, :] + part

    @pl.when(f == 0)
    def _first():
        lax.fori_loop(0, n_sub, lambda s, c: (expert(s, True), c)[1], 0)

    @pl.when(f > 0)
    def _rest():
        lax.fori_loop(0, n_sub, lambda s, c: (expert(s, False), c)[1], 0)

    @pl.when(f == n_f - 1)
    def _scatter():
        def body(s, carry):
            y = y_scr[pl.ds(pl.multiple_of(s * sub, sub), sub), :].astype(BF16)
            for j in range(nslab):
                rows = slice(j * slab, (j + 1) * slab)

                @pl.when(touches(s, j))
                def _add():
                    cidx = lax.broadcasted_iota(jnp.int32, (slab, sub), 1) + s * sub
                    onehot_t = jnp.where(rcol_ref[rows, :] == cidx, 1.0, 0.0).astype(BF16)
                    o_ref[rows, :] += gcol_ref[rows, :] * _dot(onehot_t, y)
            return carry
        lax.fori_loop(0, n_sub, body, 0)


def _moe(x, logits, w13, w2):
    T, D = x.shape
    E, _, F2 = w13.shape
    F = F2 // 2
    tm = min(MOE_TILE, T)
    nt = T // tm
    tf = FFN_TF
    n_f = F // tf
    top_val, top_idx = lax.top_k(logits, TOP_K)
    gates = jax.nn.softmax(top_val, axis=-1)
    onehot = jax.nn.one_hot(top_idx, E, dtype=F32)
    sel = jnp.sum(onehot, axis=1).astype(jnp.int32).reshape(nt, tm, E)
    gate_dense = jnp.sum(onehot * gates[..., None], axis=1).reshape(nt, tm, E)
    rank = jnp.where(sel > 0, jnp.cumsum(sel, axis=1) - sel, -1)
    rank = jnp.swapaxes(rank, 1, 2)
    slab = min(MOE_SLAB, tm)
    per_slab = jnp.sum(sel.reshape(nt, tm // slab, slab, E), axis=2)
    counts = jnp.concatenate([jnp.zeros((nt, 1, E), jnp.int32), jnp.cumsum(per_slab, axis=1)], axis=1)
    counts = jnp.swapaxes(counts, 1, 2)
    gcol = jnp.swapaxes(gate_dense, 1, 2)[..., None]
    tile = lambda shape, imap: pl.BlockSpec(shape, imap)
    return pl.pallas_call(
        functools.partial(_moe_kernel, n_f=n_f, tm=tm),
        grid_spec=pltpu.PrefetchScalarGridSpec(
            num_scalar_prefetch=1,
            grid=(nt, E, n_f),
            in_specs=[tile((tm, D), lambda i, e, f, c: (i, 0)),
                      tile((None, None, 1, tm), lambda i, e, f, c: (i, e, 0, 0)),
                      tile((None, None, tm, 1), lambda i, e, f, c: (i, e, 0, 0)),
                      tile((None, None, tm, 1), lambda i, e, f, c: (i, e, 0, 0)),
                      tile((None, D, tf), lambda i, e, f, c: (e, 0, f)),
                      tile((None, D, tf), lambda i, e, f, c: (e, 0, f + n_f)),
                      tile((None, tf, D), lambda i, e, f, c: (e, f, 0))],
            out_specs=tile((tm, D), lambda i, e, f, c: (i, 0)),
            scratch_shapes=[pltpu.VMEM((-(-tm // MOE_SUB) * MOE_SUB, D), BF16),
                            pltpu.VMEM((-(-tm // MOE_SUB) * MOE_SUB, D), F32)]),
        out_shape=jax.ShapeDtypeStruct((T, D), F32),
        compiler_params=_cparams(("parallel", "arbitrary", "arbitrary")),
        name="moe",
    )(counts, x, rank[:, :, None, :], rank[..., None], gcol, w13, w13, w2)


def _rwkv_scan_kernel(r_ref, lw_ref, k_ref, v_ref, a_ref, b_ref, h0_ref, y_ref, hT_ref, H_scr,
                      *, reverse, nblk, nchunk, npl):
    C = RW_CHUNK
    half = LANES // 2
    i = pl.program_id(2)

    @pl.when(i == 0)
    def _init():
        H_scr[...] = h0_ref[...]

    t_idx = lax.broadcasted_iota(jnp.int32, (C, LANES), 0)
    s_idx = lax.broadcasted_iota(jnp.int32, (C, LANES), 1) & (half - 1)
    tt = lax.broadcasted_iota(jnp.int32, (C, C), 0)
    ss = lax.broadcasted_iota(jnp.int32, (C, C), 1)
    if reverse:
        strict, incl, tri = s_idx > t_idx, s_idx >= t_idx, (ss >= tt).astype(BF16)
    else:
        strict, incl, tri = s_idx < t_idx, s_idx <= t_idx, (ss <= tt).astype(BF16)
    eye_lp = (s_idx == t_idx).astype(F32)
    lane = lax.broadcasted_iota(jnp.int32, (1, LANES), 1)
    m0 = (lane < half).astype(F32)
    m1 = 1.0 - m0
    rr = lax.broadcasted_iota(jnp.int32, (LANES, LANES), 0)
    cc = lax.broadcasted_iota(jnp.int32, (LANES, LANES), 1)
    mask_bd = ((rr < half) == (cc < half)).astype(F32)

    def bd(x):
        return jnp.concatenate([x * m0, x * m1], axis=0)

    order = list(range(nchunk - 1, -1, -1) if reverse else range(nchunk))
    units = [(slice(c * C, (c + 1) * C), slice(q * LANES, (q + 1) * LANES)) for c in order for q in range(npl)]
    ld = lambda ref: [ref[sl, ln] for sl, ln in units]
    r, lw, k, v, a, b = ld(r_ref), ld(lw_ref), ld(k_ref), ld(v_ref), ld(a_ref), ld(b_ref)
    G = [_cumsum_dot(tri, x) for x in lw]
    eG = [jnp.exp(g) for g in G]
    eGn = [jnp.exp(-g) for g in G]
    rt = [x * e for x, e in zip(r, eG)]
    at = [x * jnp.exp(g - l) for x, g, l in zip(a, G, lw)]
    kt = [x * e for x, e in zip(k, eGn)]
    bt = [x * e for x, e in zip(b, eGn)]
    M = [_bdot(jnp.concatenate([x, y], axis=0), jnp.concatenate([bd(z), bd(w)], axis=0), nt=True)
         for x, y, z, w in zip(at, rt, bt, kt)]
    Nm = [jnp.where(strict, m[:C, :LANES], 0.0) for m in M]
    Aak = [jnp.where(strict, m[:C, LANES:], 0.0) for m in M]
    Arb = [jnp.where(incl, m[C:, :LANES], 0.0) for m in M]
    Ark = [jnp.where(incl, m[C:, LANES:], 0.0) for m in M]
    T = None
    for lvl in range(1, int(math.log2(C)) + 1):
        same = (t_idx >> lvl) == (s_idx >> lvl)
        t_hi = ((t_idx >> (lvl - 1)) & 1) == 1
        s_hi = ((s_idx >> (lvl - 1)) & 1) == 1
        off = same & (s_hi & ~t_hi if reverse else t_hi & ~s_hi)
        Noff = [jnp.where(off, n, 0.0) for n in Nm]
        if T is None:
            T = [eye_lp + n for n in Noff]
        else:
            DN = [_bdot(t, bd(n)) for t, n in zip(T, Noff)]
            T = [t + _bdot(dn, bd(t)) for t, dn in zip(T, DN)]
    bdv = [bd(x) for x in v]
    X0 = [_bdot(x, y) for x, y in zip(Aak, bdv)]
    WU = [_bdot(t, jnp.concatenate([bd(x), bd(y)], axis=1)) for t, x, y in zip(T, at, X0)]
    W = [x[:, :LANES] for x in WU]
    U0 = [x[:, LANES:] for x in WU]
    RY = [_bdot(x, jnp.concatenate([bd(y), bd(z)], axis=1)) for x, y, z in zip(Arb, W, U0)]
    Rh = [x + y[:, :LANES] for x, y in zip(rt, RY)]
    Y0 = [y[:, LANES:] + _bdot(x, z) for y, x, z in zip(RY, Ark, bdv)]
    dPhi = [mask_bd * _bdot(x.T, y) for x, y in zip(W, bt)]
    Psi = [mask_bd * _bdot(jnp.concatenate([x, y], axis=0).T, jnp.concatenate([z, w], axis=0))
           for x, y, z, w in zip(U0, v, bt, kt)]
    e_end = [jnp.exp(g[0:1, :] if reverse else g[C - 1:C, :]) for g in G]

    Hs = [H_scr[q] for q in range(npl)]
    for u, (sl, ln) in enumerate(units):
        q = u % npl
        H = Hs[q]
        y_ref[sl, ln] = _bdot(Rh[u], H, nt=True) + Y0[u]
        Hs[q] = (H + _bdot(H, dPhi[u]) + Psi[u]) * e_end[u]
    for q in range(npl):
        H_scr[q] = Hs[q]

    @pl.when(i == nblk - 1)
    def _fin():
        for q in range(npl):
            hT_ref[q] = Hs[q]


def _pair_states(S):
    B, H, N, _ = S.shape
    S5 = S.reshape(B, H // 2, 2, N, N)
    eye = jnp.eye(2, dtype=S.dtype)
    return jnp.einsum('bpivk,ij->bpivjk', S5, eye).reshape(B, H // 2, 2 * N, 2 * N)


def _unpair_states(Sp):
    B, P, N2, _ = Sp.shape
    N = N2 // 2
    S6 = Sp.reshape(B, P, 2, N, 2, N)
    return jnp.stack([S6[:, :, 0, :, 0, :], S6[:, :, 1, :, 1, :]], axis=2).reshape(B, 2 * P, N, N)


def _rwkv_scan(r, lw, k, v, a, b, state0, reverse):
    B, L, D = r.shape
    bt = min(RW_BLOCK, L)
    nblk = L // bt
    npl = RW_PAIRS_PER_STEP
    npair = D // LANES
    blk = (lambda i: nblk - 1 - i) if reverse else (lambda i: i)
    tok = pl.BlockSpec((None, bt, npl * LANES), lambda bb, p, i: (bb, blk(i), p))
    st = pl.BlockSpec((None, npl, LANES, LANES), lambda bb, p, i: (bb, p, 0, 0))
    y, hT = pl.pallas_call(
        functools.partial(_rwkv_scan_kernel, reverse=reverse, nblk=nblk, nchunk=bt // RW_CHUNK, npl=npl),
        grid=(B, npair // npl, nblk),
        in_specs=[tok] * 6 + [st],
        out_specs=[tok, st],
        out_shape=[jax.ShapeDtypeStruct((B, L, D), F32),
                   jax.ShapeDtypeStruct((B, npair, LANES, LANES), F32)],
        scratch_shapes=[pltpu.VMEM((npl, LANES, LANES), F32)],
        compiler_params=_cparams(("parallel", "parallel", "arbitrary")),
        name="rwkv_scan_rev" if reverse else "rwkv_scan_fwd",
    )(r, lw, k, v, a, b, _pair_states(state0))
    return y, _unpair_states(hT)


def _gla_kernel(q_ref, f_ref, v_ref, lb_ref, h0_ref, o_ref, hT_ref, H_scr, *, reverse, nblk, npl):
    C = HG_CHUNK
    sh = int(math.log2(C))
    bt = q_ref.shape[0]
    nchunk = bt // C
    i = pl.program_id(2)

    @pl.when(i == 0)
    def _init():
        H_scr[...] = h0_ref[...]

    tt = lax.broadcasted_iota(jnp.int32, (bt, bt), 0)
    ss = lax.broadcasted_iota(jnp.int32, (bt, bt), 1)
    same = (tt >> sh) == (ss >> sh)
    mid = ((tt >> sh) << sh) + (C // 2 if reverse else C // 2 - 1)
    if reverse:
        incl, upto_mid = same & (ss >= tt), same & (ss >= mid)
    else:
        incl, upto_mid = same & (ss <= tt), same & (ss <= mid)
    one = lambda m: jnp.where(m, 1.0, 0.0)
    sums = jnp.concatenate([one(incl), one(incl) - one(upto_mid), one(same) - one(incl)], axis=0).astype(BF16)

    order = list(range(nchunk - 1, -1, -1) if reverse else range(nchunk))
    heads = [slice(p * LANES, (p + 1) * LANES) for p in range(npl)]
    q = [jax.nn.silu(q_ref[:, ln]) for ln in heads]
    fg = [lb_ref[:, ln] + (1.0 - lb_ref[:, ln]) * jax.nn.sigmoid(f_ref[:, ln]) for ln in heads]
    v = [v_ref[:, ln] for ln in heads]
    k = [1.0 - f for f in fg]
    Gs = [_cumsum_dot(sums, jnp.log(f)) for f in fg]
    Gabs = [g[:bt] for g in Gs]
    Grel = [g[bt:2 * bt] for g in Gs]
    Gend = [g[2 * bt:] for g in Gs]
    scores = [jnp.where(incl, _bdot(x * jnp.exp(g), y * jnp.exp(-g), nt=True), 0.0)
              for x, y, g in zip(q, k, Grel)]
    o_intra = [_bdot(s, x) for s, x in zip(scores, v)]
    qa = [x * jnp.exp(g) for x, g in zip(q, Gabs)]
    kend = [x * jnp.exp(g) for x, g in zip(k, Gend)]
    rows = [slice(c * C, (c + 1) * C) for c in range(nchunk)]
    KV = [[_bdot(x[r].T, y[r]) for r in rows] for x, y in zip(v, kend)]
    dec = [[jnp.exp(ga[r][0:1, :] + ge[r][0:1, :]) for r in rows] for ga, ge in zip(Gabs, Gend)]

    Hs = [H_scr[p] for p in range(npl)]
    for c in order:
        for p in range(npl):
            o_ref[rows[c], heads[p]] = o_intra[p][rows[c]] + _bdot(qa[p][rows[c]], Hs[p], nt=True)
            Hs[p] = Hs[p] * dec[p][c] + KV[p][c]
    for p in range(npl):
        H_scr[p] = Hs[p]

    @pl.when(i == nblk - 1)
    def _fin():
        for p in range(npl):
            hT_ref[p] = Hs[p]


def _gla_scan(proj, lb, state0, d):
    B, L, D5 = proj.shape
    D = D5 // 5
    reverse = d == 1
    bt = GLA_BLOCK
    nblk = L // bt
    npl = GLA_HEADS_PER_STEP
    nh = D // LANES
    ncb = nh // npl
    blk = (lambda i: nblk - 1 - i) if reverse else (lambda i: i)
    col = lambda off: pl.BlockSpec((None, bt, npl * LANES), lambda bb, p, i: (bb, blk(i), off * ncb + p))
    st = pl.BlockSpec((None, npl, LANES, LANES), lambda bb, p, i: (bb, p, 0, 0))
    o, hT = pl.pallas_call(
        functools.partial(_gla_kernel, reverse=reverse, nblk=nblk, npl=npl),
        grid=(B, ncb, nblk),
        in_specs=[col(0), col(1 + d), col(3), pl.BlockSpec((1, npl * LANES), lambda bb, p, i: (0, p)), st],
        out_specs=[col(0), st],
        out_shape=[jax.ShapeDtypeStruct((B, L, D), F32),
                   jax.ShapeDtypeStruct((B, nh, LANES, LANES), F32)],
        scratch_shapes=[pltpu.VMEM((npl, LANES, LANES), F32)],
        compiler_params=_cparams(("parallel", "parallel", "arbitrary")),
        name="gla_scan_rev" if reverse else "gla_scan_fwd",
    )(proj, proj, proj, lb[d][None, :], jnp.swapaxes(state0, 2, 3))
    return o, jnp.swapaxes(hT, 2, 3)


def _leftmul_kernel(fh_ref, fl_ref, x_ref, o_ref, xs_scr, *, nj):
    for j in range(nj):
        xs_scr[...] = x_ref[:, j, :]
        o_ref[:, j, :] = _x3dot(fh_ref[...], fl_ref[...], xs_scr[...])


def _leftmul_gate_kernel(fh_ref, fl_ref, x_ref, u_ref, s_ref, g_ref, o_ref, xs_scr, *, nj):
    for j in range(nj):
        xs_scr[...] = x_ref[:, j, :]
        y = _x3dot(fh_ref[...], fl_ref[...], xs_scr[...])
        o_ref[:, j, :] = (y + u_ref[:, j, :] * s_ref[...]) * g_ref[:, j, :]


def _leftmul(f, x, xoff=0, epilogue=None, D=D_MODEL):
    B, K, J, _ = x.shape
    M = f.shape[0]
    tj = 8 if J % 8 == 0 else J
    td = DFT_TD
    fh, fl = _split2(f)
    fs = pl.BlockSpec((M, K), lambda b, j, d: (0, 0))
    col = lambda rows, off: pl.BlockSpec((None, rows, tj, td), lambda b, j, d: (b, 0, j, off + d))
    if epilogue is None:
        kern, specs, args = _leftmul_kernel, [fs, fs, col(K, xoff)], (fh, fl, x)
    else:
        u, uoff, skip, gate, goff = epilogue
        kern = _leftmul_gate_kernel
        specs = [fs, fs, col(K, xoff), col(M, uoff), pl.BlockSpec((1, td), lambda b, j, d: (0, d)), col(M, goff)]
        args = (fh, fl, x, u, skip, gate)
    return pl.pallas_call(
        functools.partial(kern, nj=tj), grid=(B, J // tj, D // td), in_specs=specs, out_specs=col(M, 0),
        out_shape=jax.ShapeDtypeStruct((B, M, J, D), F32),
        scratch_shapes=[pltpu.VMEM((K, td), F32)],
        compiler_params=_cparams(("parallel", "parallel", "parallel")),
        name="dft_outer",
    )(*args)


def _spectral_fwd_kernel(fh_ref, fl_ref, y_ref, o_ref):
    o_ref[...] = _x3dot(fh_ref[...], fl_ref[...], y_ref[...])


def _spectral_conv_kernel(fh_ref, fl_ref, fih_ref, fil_ref, hf_ref, hb_ref, y_ref, o_ref):
    n2 = hf_ref.shape[0] // 2
    z = _x3dot(fh_ref[...], fl_ref[...], y_ref[...])
    zr, zi = z[:n2], z[n2:]
    kf = _combine_filter_spectrum(hf_ref[...], hb_ref[...])
    kr, ki = kf[:n2], kf[n2:]
    p = jnp.concatenate([zr * kr - zi * ki, zr * ki + zi * kr], axis=0)
    o_ref[...] = _x3dot(fih_ref[...], fil_ref[...], p)


def _spectral_mid(y, f_fwd, f_inv=None, spec=None, order=0):
    B, K1, R, D = y.shape
    ms = pl.BlockSpec((None, R, R), lambda k1, b: (k1, 0, 0))
    ys = pl.BlockSpec((None, None, R, D), lambda k1, b: (b, k1, 0, 0))
    if spec is None:
        kern, specs, args = _spectral_fwd_kernel, [ms, ms, ys], (*_split2(f_fwd), y)
    else:
        kern = _spectral_conv_kernel
        grp = lambda g: pl.BlockSpec((None, R, D), lambda k1, b: (k1, 0, g))
        specs = [ms, ms, ms, ms, grp(2 * order), grp(2 * order + 1), ys]
        args = (*_split2(f_fwd), *_split2(f_inv), spec, spec, y)
    return pl.pallas_call(
        kern, grid=(K1, B), in_specs=specs, out_specs=ys,
        out_shape=jax.ShapeDtypeStruct((B, K1, R, D), F32),
        compiler_params=_cparams(("parallel", "parallel")),
        name="dft_inner",
    )(*args)


def _dft_tables(L):
    n2 = HY_N2 if L % HY_N2 == 0 and L > HY_N2 * 2 else 1
    N = 2 * L
    n1 = N // n2
    k1h = n1 // 2
    ph = (jnp.arange(k1h, dtype=jnp.int32)[:, None] * 2 + 1) * jnp.arange(k1h, dtype=jnp.int32)[None, :]
    th = (ph % (2 * n1)).astype(F32) * (math.pi / n1)
    f_out = jnp.stack([jnp.cos(th), -jnp.sin(th)], axis=1).reshape(2 * k1h, k1h)
    f_out_inv = (2.0 / N) * f_out.T
    if n2 == 1:
        return n2, f_out, f_out_inv, None, None
    kk = jnp.arange(k1h, dtype=jnp.int32)[:, None, None] + n1 * jnp.arange(n2, dtype=jnp.int32)[None, :, None]
    ph = ((2 * kk + 1) * jnp.arange(n2, dtype=jnp.int32)[None, None, :]) % (2 * N)
    phi = ph.astype(F32) * (math.pi / N)
    c, s = jnp.cos(phi), jnp.sin(phi)
    f_in = jnp.concatenate([jnp.concatenate([c, s], axis=2), jnp.concatenate([-s, c], axis=2)], axis=1)
    ct, st = jnp.swapaxes(c, 1, 2), jnp.swapaxes(s, 1, 2)
    f_in_inv = jnp.concatenate([jnp.concatenate([ct, -st], axis=2), jnp.concatenate([st, ct], axis=2)], axis=1)
    return n2, f_out, f_out_inv, f_in, f_in_inv


def _long_conv_gated(u, ucol, spec, order, skip, gate, gcol, tables):
    n2, f_out, f_out_inv, f_in, f_in_inv = tables
    B, L, _ = u.shape
    D = D_MODEL
    k1h = L // n2
    per = D // DFT_TD
    u4 = u.reshape(B, k1h, n2, u.shape[-1])
    y = _leftmul(f_out, u4, xoff=ucol * per).reshape(B, k1h, 2 * n2, D)
    if f_in is None:
        cols = lambda g: spec[..., g * D:(g + 1) * D]
        kf = _combine_filter_spectrum(cols(2 * order), cols(2 * order + 1))
        yr, yi = y[:, :, 0], y[:, :, 1]
        kr, ki = kf[:, 0], kf[:, 1]
        q = jnp.stack([yr * kr - yi * ki, yr * ki + yi * kr], axis=2)
    else:
        q = _spectral_mid(y, f_in, f_in_inv, spec, order)
    out = _leftmul(f_out_inv, q.reshape(B, 2 * k1h, n2, D),
                   epilogue=(u4, ucol * per, skip[None, :], gate.reshape(B, k1h, n2, gate.shape[-1]), gcol * per))
    return out.reshape(B, L, D)


def _conv3_kernel(cur_ref, prev_ref, next_ref, w_ref, o_ref, *, nblk):
    i = pl.program_id(1)
    x = cur_ref[...]
    tm = x.shape[0]
    row = lax.broadcasted_iota(jnp.int32, x.shape, 0)
    before = jnp.where(i > 0, prev_ref[7:8, :], 0.0)
    after = jnp.where(i < nblk - 1, next_ref[0:1, :], 0.0)
    xp = jnp.where(row == 0, before, pltpu.roll(x, 1, axis=0))
    xn = jnp.where(row == tm - 1, after, pltpu.roll(x, tm - 1, axis=0))
    o_ref[...] = xp * w_ref[0:1, :] + x * w_ref[1:2, :] + xn * w_ref[2:3, :] + w_ref[3:4, :]


def _conv3(x, taps, bias):
    B, L, C = x.shape
    tm = min(NORM_TM, L)
    tc = CONV_TC
    nblk = L // tm
    r8 = tm // 8
    w = jnp.concatenate([taps, bias[None, :], jnp.zeros((4, C), F32)], axis=0)
    return pl.pallas_call(
        functools.partial(_conv3_kernel, nblk=nblk),
        grid=(B, nblk, C // tc),
        in_specs=[pl.BlockSpec((None, tm, tc), lambda b, i, j: (b, i, j)),
                  pl.BlockSpec((None, 8, tc), lambda b, i, j: (b, jnp.maximum(i * r8 - 1, 0), j)),
                  pl.BlockSpec((None, 8, tc), lambda b, i, j: (b, jnp.minimum((i + 1) * r8, L // 8 - 1), j)),
                  pl.BlockSpec((8, tc), lambda b, i, j: (0, j))],
        out_specs=pl.BlockSpec((None, tm, tc), lambda b, i, j: (b, i, j)),
        out_shape=jax.ShapeDtypeStruct((B, L, C), F32),
        compiler_params=_cparams(("parallel", "parallel", "parallel")),
        name="conv3",
    )(x, x, x, w)


def _hyena_filter_spectra(L, w1, b1, w2, b2, w3, b3, w4, freq, tables):
    pos = jnp.arange(L, dtype=F32)
    t = (pos / max(L - 1, 1))[:, None]
    ang = (2.0 * math.pi / L) * pos[:, None] * jnp.linspace(1e-4, HY_BANDS - 1, HY_BANDS, dtype=F32)[None, :]
    z = jnp.concatenate([t, jnp.cos(ang), -jnp.sin(ang)], axis=-1)
    z = jnp.pad(z, ((0, 0), (0, 40 - HY_EMB)))
    w1p = jnp.pad(w1, ((0, 40 - HY_EMB), (0, 0)))
    hid = jnp.sin(freq[0] * (_mm(z, w1p, exact=True) + b1))
    hid = jnp.sin(freq[1] * (_mm(hid, w2, exact=True) + b2))
    hid = jnp.sin(freq[2] * (_mm(hid, w3, exact=True) + b3))
    filt = _mm(hid, w4, exact=True)
    deltas = jnp.linspace(math.log(HY_TARGET) / HY_FAST_DECAY, math.log(HY_TARGET) / HY_SLOW_DECAY,
                          D_MODEL, dtype=F32)
    window = jnp.exp(-t * jnp.abs(deltas))
    lag0 = (jnp.arange(L) > 0).astype(F32)[:, None]
    scale = jnp.concatenate([window, window * lag0] * HY_ORDER, axis=1)
    filt = filt * scale
    n2 = tables[0]
    k1h = L // n2
    C = HY_ORDER * 2 * D_MODEL
    y = _leftmul(tables[1], filt.reshape(1, k1h, n2, C), D=C).reshape(1, k1h, 2 * n2, C)
    return (y if tables[3] is None else _spectral_mid(y, tables[3]))[0]


def _combine_filter_spectrum(hf, hb):
    n2 = hf.shape[-2] // 2
    return jnp.concatenate([hf[..., :n2, :] + hb[..., :n2, :], hf[..., n2:, :] - hb[..., n2:, :]], axis=-2)


def _hyena_stream(resid, vecs, fprm, w_in, b_in, conv_w, conv_b, skip, w_o, b_o):
    B, L, D = resid.shape
    tables = _dft_tables(L)
    spec = _hyena_filter_spectra(L, *fprm, tables)
    proj = _conv3(_mm_fused(resid, w_in, vecs=vecs, norm=True, bias=b_in), conv_w, conv_b)
    z = _long_conv_gated(proj, 2, spec, 0, skip[0], proj, 0, tables)
    y = _long_conv_gated(z, 0, spec, 1, skip[1], proj, 1, tables)
    return _mm_fused(y, w_o, vecs=vecs, bias=b_o, resid=resid)


def _headsum(x, ones):
    hi, lo = _split2(x)
    cols = []
    for j in range(x.shape[1] // LANES):
        ln = slice(j * LANES, (j + 1) * LANES)
        cols.append(_dot(hi[:, ln], ones) + _dot(lo[:, ln], ones))
    return jnp.concatenate(cols, axis=1)


def _token_shift(h, h_above, h_below, grid_w):
    tm, D = h.shape
    row = lax.broadcasted_iota(jnp.int32, (tm, D), 0)
    lane = lax.broadcasted_iota(jnp.int32, (tm, D), 1)
    before = pltpu.roll(h, 1, axis=0)
    after = pltpu.roll(h, tm - 1, axis=0)
    if grid_w is None:
        return jnp.where(lane < D // 2, jnp.where(row == 0, 0.0, before), jnp.where(row == tm - 1, 0.0, after))
    col = row & (grid_w - 1)
    left = jnp.where(col == 0, 0.0, before)
    right = jnp.where(col == grid_w - 1, 0.0, after)
    up = jnp.concatenate([h_above, h[:tm - grid_w]], axis=0)
    down = jnp.concatenate([h[grid_w:], h_below], axis=0)
    q = D // 4
    return jnp.where(lane < q, left, jnp.where(lane < 2 * q, right, jnp.where(lane < 3 * q, up, down)))


def _rwkv_pre_kernel(*refs, vres, grid_w, nblk):
    it = iter(refs)
    x_ref = next(it)
    xa_ref, xb_ref = (next(it), next(it)) if grid_w else (None, None)
    mod_ref = next(it)
    vf_ref = next(it) if vres else None
    vec_ref, wr_ref, wk_ref, wv_ref, w1_ref, w2_ref, a1_ref, a2_ref, g1_ref, g2_ref = (next(it) for _ in range(10))
    v1_ref, v2_ref = (next(it), next(it)) if vres else (None, None)
    ones_ref = next(it)
    r_o, v_o, nkk_o, lw0_o, lw1_o, kd0_o, kd1_o, b0_o, b1_o, bonus_o, g_o = it
    vec = lambda i: vec_ref[i:i + 1, :]
    i = pl.program_id(1)
    h = _norm_mod(x_ref[...], mod_ref)
    if grid_w:
        h_above = jnp.where(i > 0, _norm_mod(xa_ref[...], mod_ref), 0.0)
        h_below = jnp.where(i < nblk - 1, _norm_mod(xb_ref[...], mod_ref), 0.0)
    else:
        h_above = h_below = None
    xx = _token_shift(h, h_above, h_below, grid_w) - h
    xr, xw, xk, xv, xa, xg = ((h + xx * vec(j)).astype(BF16) for j in range(6))
    r = _dot(xr, wr_ref[...])
    k = _dot(xk, wk_ref[...])
    v = _dot(xv, wv_ref[...])
    if vres:
        lora = _dot(_dot(xv, v1_ref[...]).astype(BF16), v2_ref[...])
        v = v + (vf_ref[...] - v) * jax.nn.sigmoid(vec(13) + lora)
    ones = ones_ref[...]
    kk = k * vec(10)
    kk = kk / jnp.maximum(jnp.sqrt(_headsum(kk * kk, ones)), 1e-12)
    r_o[...] = r
    v_o[...] = v
    nkk_o[...] = -kk
    bonus = jnp.zeros_like(r)
    for d, (lw_o, kd_o, b_o) in enumerate(((lw0_o, kd0_o, b0_o), (lw1_o, kd1_o, b1_o))):
        wl = vec(6 + d) + _dot(jnp.tanh(_dot(xw, w1_ref[d])).astype(BF16), w2_ref[d])
        lw_o[...] = -jax.nn.sigmoid(wl) * math.exp(-0.5)
        a = jax.nn.sigmoid(vec(8 + d) + _dot(_dot(xa, a1_ref[d]).astype(BF16), a2_ref[d]))
        kd = k * (1.0 + (a - 1.0) * vec(11))
        kd_o[...] = kd
        b_o[...] = kk * a
        bonus = bonus + _headsum(r * kd * vec(12), ones) * v
    bonus_o[...] = bonus
    g_o[...] = _dot(jax.nn.sigmoid(_dot(xg, g1_ref[...])).astype(BF16), g2_ref[...])


def _rwkv_post_kernel(y0_ref, y1_ref, bonus_ref, g_ref, res_ref, mod_ref, ln_ref, wo_ref, ones_ref, o_ref):
    ones = ones_ref[...]
    y = y0_ref[...] + y1_ref[...]
    yc = y - _headsum(y, ones) * (1.0 / RW_HEAD)
    var = _headsum(yc * yc, ones) * (1.0 / RW_HEAD)
    yn = yc * lax.rsqrt(var + RW_LN_EPS) * ln_ref[0:1, :] + ln_ref[1:2, :] + bonus_ref[...]
    out = _dot((yn * g_ref[...]).astype(BF16), wo_ref[...])
    o_ref[...] = res_ref[...] + mod_ref[2:3, :] * out


def _full(a):
    nd = a.ndim
    return pl.BlockSpec(a.shape, lambda b, i: (0,) * nd)


def _rwkv7_stream(resid, mods, grid_w, state0, v_first, mu, w_rkv, w_o, w0, w1, w2, a0, a1, a2,
                  g1, g2, k_k, k_a, r_k, ln_w, ln_b, vres, need_out=True):
    B, L, D = resid.shape
    tm = min(RW_TM, L)
    nblk = L // tm
    assert grid_w is not None or nblk == 1
    vres_on = vres is not None
    rows = [mu[j] for j in range(6)] + [w0[0], w0[1], a0[0], a0[1], k_k, k_a, r_k.reshape(D)]
    rows.append(vres[0] if vres_on else jnp.zeros((D,), F32))
    vecs = jnp.stack(rows + [jnp.zeros((D,), F32)] * (16 - len(rows)))
    ones = jnp.kron(jnp.eye(LANES // RW_HEAD, dtype=F32), jnp.ones((RW_HEAD, RW_HEAD), F32)).astype(BF16)
    bf = lambda t: t.astype(BF16)
    tok = pl.BlockSpec((None, tm, D), lambda b, i: (b, i, 0))
    mod_spec = pl.BlockSpec((None, 8, D), lambda b, i: (b, 0, 0))
    args, specs = [resid], [tok]
    if grid_w:
        per = tm // grid_w
        last_row = L // grid_w - 1
        args += [resid, resid]
        specs += [pl.BlockSpec((None, grid_w, D), lambda b, i: (b, jnp.maximum(i * per - 1, 0), 0)),
                  pl.BlockSpec((None, grid_w, D), lambda b, i: (b, jnp.minimum((i + 1) * per, last_row), 0))]
    args.append(mods)
    specs.append(mod_spec)
    if vres_on:
        args.append(v_first)
        specs.append(tok)
    consts = [vecs, bf(w_rkv[0]), bf(w_rkv[1]), bf(w_rkv[2]), bf(w1), bf(w2), bf(a1), bf(a2), bf(g1), bf(g2)]
    consts += [bf(vres[1]), bf(vres[2])] if vres_on else []
    consts.append(ones)
    outs = pl.pallas_call(
        functools.partial(_rwkv_pre_kernel, vres=vres_on, grid_w=grid_w, nblk=nblk),
        grid=(B, nblk),
        in_specs=specs + [_full(c) for c in consts],
        out_specs=[tok] * 11,
        out_shape=[jax.ShapeDtypeStruct((B, L, D), F32)] * 11,
        compiler_params=_cparams(("parallel", "parallel")),
        name="rwkv_pre",
    )(*args, *consts)
    r, v, nkk, lw0, lw1, kd0, kd1, b0, b1, bonus, g = outs
    y0, s0 = _rwkv_scan(r, lw0, kd0, v, nkk, b0, state0[0], reverse=False)
    y1, s1 = _rwkv_scan(r, lw1, kd1, v, nkk, b1, state0[1], reverse=True)
    states = jnp.stack([s0, s1])
    if not need_out:
        return None, states, v
    ln = jnp.stack([ln_w, ln_b] + [jnp.zeros((D,), F32)] * 6)
    wo = bf(w_o)
    new_resid = pl.pallas_call(
        _rwkv_post_kernel,
        grid=(B, L // tm),
        in_specs=[tok] * 5 + [mod_spec, _full(ln), _full(wo), _full(ones)],
        out_specs=tok,
        out_shape=jax.ShapeDtypeStruct((B, L, D), F32),
        compiler_params=_cparams(("parallel", "parallel")),
        name="rwkv_post",
    )(y0, y1, bonus, g, resid, mods, ln, wo, ones)
    return new_resid, states, v


def _mm_fused_kernel(*refs, norm, bias, resid):
    it = iter(refs)
    x_ref = next(it)
    vec_ref = next(it) if (norm or resid) else None
    w_ref = next(it)
    b_ref = next(it) if bias else None
    res_ref = next(it) if resid else None
    o_ref, xs_scr = next(it), next(it)

    @pl.when(pl.program_id(2) == 0)
    def _prep():
        x = x_ref[...]
        xs_scr[...] = (_norm_mod(x, vec_ref) if norm else x).astype(BF16)

    acc = _dot(xs_scr[...], w_ref[...])
    if bias:
        acc = acc + b_ref[...]
    o_ref[...] = res_ref[...] + vec_ref[2:3, :] * acc if resid else acc


def _mm_fused(x, w, vecs=None, norm=False, bias=None, resid=None):
    B, L, K = x.shape
    N = w.shape[1]
    tm = min(MM_TM, L)
    tn = MM_TN if (N % MM_TN == 0 and resid is None) else N
    use_vec = norm or resid is not None
    specs = [pl.BlockSpec((None, tm, K), lambda b, i, j: (b, i, 0))]
    args = [x]
    if use_vec:
        specs.append(pl.BlockSpec((None, 8, vecs.shape[-1]), lambda b, i, j: (b, 0, 0)))
        args.append(vecs)
    specs.append(pl.BlockSpec((K, tn), lambda b, i, j: (0, j)))
    args.append(w.astype(BF16))
    if bias is not None:
        specs.append(pl.BlockSpec((1, tn), lambda b, i, j: (0, j)))
        args.append(bias[None, :])
    out_spec = pl.BlockSpec((None, tm, tn), lambda b, i, j: (b, i, j))
    if resid is not None:
        assert N == K == vecs.shape[-1] and tn == N
        specs.append(out_spec)
        args.append(resid)
    return pl.pallas_call(
        functools.partial(_mm_fused_kernel, norm=norm, bias=bias is not None, resid=resid is not None),
        grid=(B, L // tm, N // tn),
        in_specs=specs, out_specs=out_spec,
        out_shape=jax.ShapeDtypeStruct((B, L, N), F32),
        scratch_shapes=[pltpu.VMEM((tm, K), BF16)],
        compiler_params=_cparams(("parallel", "parallel", "arbitrary")),
        name="mm_fused",
    )(*args)


def _hgrn_post_kernel(o0_ref, o1_ref, g_ref, res_ref, vec_ref, gn_ref, wo_ref, ones_ref, o_ref):
    o = o0_ref[...] + o1_ref[...]
    ms = _headsum(o * o, ones_ref[...]) * (1.0 / HG_DK)
    on = o * lax.rsqrt(ms + NORM_EPS) * gn_ref[...]
    g = g_ref[...]
    z = (on * (g * jax.nn.sigmoid(g))).astype(BF16)
    o_ref[...] = res_ref[...] + vec_ref[2:3, :] * _dot(z, wo_ref[...])


def _hgrn2_stream(resid, vecs, state0, lb, w_in, gn, w_o, need_out=True):
    B, L, D = resid.shape
    proj = _mm_fused(resid, w_in, vecs=vecs, norm=True)
    o0, s0 = _gla_scan(proj, lb, state0[0], 0)
    o1, s1 = _gla_scan(proj, lb, state0[1], 1)
    states = jnp.stack([s0, s1])
    if not need_out:
        return None, states
    tm = min(RW_TM, L)
    tok = pl.BlockSpec((None, tm, D), lambda b, i: (b, i, 0))
    ones = jnp.ones((LANES, LANES), BF16)
    gn_row = jnp.tile(gn, D // gn.shape[0])[None, :]
    wo = w_o.astype(BF16)
    out = pl.pallas_call(
        _hgrn_post_kernel,
        grid=(B, L // tm),
        in_specs=[tok, tok, pl.BlockSpec((None, tm, D), lambda b, i: (b, i, 4)), tok,
                  pl.BlockSpec((None, 8, D), lambda b, i: (b, 0, 0)), _full(gn_row), _full(wo), _full(ones)],
        out_specs=tok,
        out_shape=jax.ShapeDtypeStruct((B, L, D), F32),
        compiler_params=_cparams(("parallel", "parallel")),
        name="hgrn_post",
    )(o0, o1, proj, resid, vecs, gn_row, wo, ones)
    return out, states


def kernel(x, c, ctx, c_ctx, norm_g, ada_w, ada_b, final_g,
           rw_mu, rw_wrkv, rw_wo, rw_w0, rw_w1, rw_w2, rw_a0, rw_a1, rw_a2,
           rw_v0, rw_v1, rw_v2, rw_g1, rw_g2, rw_kk, rw_ka, rw_rk, rw_lnw, rw_lnb,
           hy_win, hy_bin, hy_cw, hy_cb, hy_fw1, hy_fb1, hy_fw2, hy_fb2, hy_fw3, hy_fb3,
           hy_fw4, hy_freq, hy_skip, hy_wo, hy_bo,
           hg_win, hg_lb, hg_gn, hg_wo,
           ffn_w13, ffn_w2, moe_router, moe_w13, moe_w2):
    B = x.shape[0]
    depth = norm_g.shape[0]
    D = D_MODEL
    lat, cx = x, ctx
    v_first = None
    lbc = jnp.cumsum(jax.nn.softmax(hg_lb, axis=0), axis=0)
    lower = lbc - lbc[:1]
    cond = jnp.concatenate([jax.nn.silu(c), jax.nn.silu(c_ctx)[None, :]], axis=0)
    for i in range(depth):
        last = i == depth - 1
        if i % 2 == 0:
            mix_w13, mix_w2 = ffn_w13[i // 2].astype(BF16), ffn_w2[i // 2].astype(BF16)
        else:
            mix_w13, mix_w2 = moe_w13[i // 2].astype(BF16), moe_w2[i // 2].astype(BF16)
        mod = _mm(cond, ada_w[i]) + ada_b[i]
        mod_l = jnp.split(mod[:B, None, :], 6, axis=-1)
        mod_c = jnp.split(mod[B:, None, :], 6, axis=-1)
        kind, slot = i % N_MIXERS, i // N_MIXERS
        vec_l = _mod_rows(norm_g[i, 0], mod_l[0], mod_l[1], mod_l[2], B)
        vec_c = _mod_rows(norm_g[i, 0], mod_c[0], mod_c[1], mod_c[2], B)
        if kind == 0:
            vres = None if slot == 0 else (rw_v0[slot - 1], rw_v1[slot - 1], rw_v2[slot - 1])
            rw = (rw_mu[slot], rw_wrkv[slot], rw_wo[slot], rw_w0[slot], rw_w1[slot], rw_w2[slot],
                  rw_a0[slot], rw_a1[slot], rw_a2[slot], rw_g1[slot], rw_g2[slot], rw_kk[slot],
                  rw_ka[slot], rw_rk[slot], rw_lnw[slot], rw_lnb[slot], vres)
            zero = jnp.zeros((2, B, RW_H, RW_HEAD, RW_HEAD), F32)
            vf_c = None if v_first is None else v_first[0]
            vf_l = None if v_first is None else v_first[1]
            cx_new, s_ctx, v_c = _rwkv7_stream(cx, vec_c, None, zero, vf_c, *rw, need_out=not last)
            lat, _, v_l = _rwkv7_stream(lat, vec_l, GRID_W, s_ctx, vf_l, *rw)
            if not last:
                cx = cx_new
            if slot == 0:
                v_first = (v_c, v_l)
        elif kind == 1:
            fprm = (hy_fw1[slot], hy_fb1[slot], hy_fw2[slot], hy_fb2[slot], hy_fw3[slot],
                    hy_fb3[slot], hy_fw4[slot], hy_freq[slot])
            hprm = (hy_win[slot], hy_bin[slot], hy_cw[slot], hy_cb[slot], hy_skip[slot],
                    hy_wo[slot], hy_bo[slot])
            lat = _hyena_stream(lat, vec_l, fprm, *hprm)
            if not last:
                cx = _hyena_stream(cx, vec_c, fprm, *hprm)
        else:
            zero = jnp.zeros((2, B, HG_H, HG_DK, D // HG_H), F32)
            gprm = (lower[i], hg_win[slot], hg_gn[slot], hg_wo[slot])
            cx_new, s_ctx = _hgrn2_stream(cx, vec_c, zero, *gprm, need_out=not last)
            lat, _ = _hgrn2_stream(lat, vec_l, s_ctx, *gprm)
            if not last:
                cx = cx_new

        def channel_mix(s, mod, out_gain=None):
            vecs = _mod_rows(norm_g[i, 1], mod[3], mod[4], mod[5], B)
            if i % 2 == 0:
                return _ffn(s, vecs, mix_w13, mix_w2)
            xn, logits = _norm_route(s, vecs, moe_router[i // 2])
            out = _moe(xn.reshape(-1, D), logits.reshape(-1, N_EXPERTS), mix_w13, mix_w2)
            return _gated_add(s, out.reshape(s.shape), vecs, out_gain)

        assert depth % 2 == 0
        lat = channel_mix(lat, mod_l, final_g if last else None)
        if not last:
            cx = channel_mix(cx, mod_c)
    return lat
```

```python
import functools
import math

import jax
import jax.numpy as jnp
from jax import lax
from jax.experimental import pallas as pl
from jax.experimental.pallas import tpu as pltpu

F32 = jnp.float32
BF16 = jnp.bfloat16
HIGHEST = lax.Precision.HIGHEST

D_MODEL = 1024
GRID_W = 64
NORM_EPS = 1e-6
RW_HEAD = 64
RW_H = D_MODEL // RW_HEAD
RW_LN_EPS = 64e-5
HY_ORDER = 2
HY_EMB = 33
HY_BANDS = (HY_EMB - 1) // 2
HY_FAST_DECAY = 0.3
HY_SLOW_DECAY = 1.5
HY_TARGET = 1e-2
HG_DK = 128
HG_H = D_MODEL // HG_DK
HG_CHUNK = 32
N_EXPERTS = 8
TOP_K = 2
N_MIXERS = 3

LANES = 128
VMEM_LIMIT = 56 * 1024 * 1024

MM_TM, MM_TN = 1024, 512
FFN_TF = 512
NORM_TM = 512
HALO = 16
RW_TM = 256
RW_CHUNK = 64
RW_BLOCK = 512
RW_PAIRS_PER_STEP = 4
GLA_BLOCK = 256
GLA_HEADS_PER_STEP = 8
MOE_TILE = 2048
MOE_SUB = 256
MOE_SLAB = 512
HY_N2 = 128
DFT_TD = 512


def _cparams(sem):
    return pltpu.CompilerParams(dimension_semantics=sem, vmem_limit_bytes=VMEM_LIMIT)


def _dot(a, b, prec=None):
    return jnp.dot(a, b, preferred_element_type=F32, precision=prec)


def _split2(x):
    hi = x.astype(BF16)
    lo = (x - hi.astype(F32)).astype(BF16)
    return hi, lo


def _bdot(a, b, nt=False):
    dn = (((1,), (1 if nt else 0,)), ((), ()))
    return lax.dot_general(a.astype(BF16), b.astype(BF16), dn, preferred_element_type=F32)


def _x3dot(fh, fl, x):
    xh, xl = _split2(x)
    return _dot(fh, xh) + (_dot(fh, xl) + _dot(fl, xh))


def _cumsum_dot(tri, x):
    hi = x.astype(BF16)
    r1 = x - hi.astype(F32)
    mid = r1.astype(BF16)
    lo = (r1 - mid.astype(F32)).astype(BF16)
    n = x.shape[1]
    g = _dot(tri, jnp.concatenate([hi, mid, lo], axis=1))
    return g[:, :n] + (g[:, n:2 * n] + g[:, 2 * n:])


def _mm_kernel(a_ref, b_ref, o_ref, *, exact):
    if exact:
        o_ref[...] = _x3dot(*_split2(a_ref[...]), b_ref[...])
    else:
        o_ref[...] = _dot(a_ref[...], b_ref[...])


def _mm(a, b, *, exact=False):
    M, K = a.shape
    N = b.shape[1]
    dt = F32 if exact else BF16
    a = a.astype(dt)
    b = b.astype(dt)
    Mp = -(-M // 8) * 8
    if Mp != M:
        a = jnp.pad(a, ((0, Mp - M), (0, 0)))
    tm = MM_TM // 2 if Mp % (MM_TM // 2) == 0 else Mp
    tn = MM_TN if N % MM_TN == 0 else N
    out = pl.pallas_call(
        functools.partial(_mm_kernel, exact=exact),
        grid=(Mp // tm, N // tn),
        in_specs=[pl.BlockSpec((tm, K), lambda i, j: (i, 0)),
                  pl.BlockSpec((K, tn), lambda i, j: (0, j))],
        out_specs=pl.BlockSpec((tm, tn), lambda i, j: (i, j)),
        out_shape=jax.ShapeDtypeStruct((Mp, N), F32),
        compiler_params=_cparams(("parallel", "parallel")),
        name="mm",
    )(a, b)
    return out[:M] if Mp != M else out


def _norm_mod(x, vec_ref):
    xn = x * lax.rsqrt(jnp.mean(x * x, axis=-1, keepdims=True) + NORM_EPS)
    return xn * vec_ref[0:1, :] + vec_ref[1:2, :]


def _mod_rows(g, shift, scale, gate, B):
    D = g.shape[-1]
    rows = [jnp.broadcast_to(g * (1.0 + scale[:, 0]), (B, D)), jnp.broadcast_to(shift[:, 0], (B, D)),
            jnp.broadcast_to(gate[:, 0], (B, D))]
    return jnp.stack(rows + [jnp.zeros((B, D), F32)] * 5, axis=1)


def _ffn_kernel(x_ref, vec_ref, wg_ref, wu_ref, w2_ref, o_ref, xn_scr, acc_ref, *, n_f):
    f = pl.program_id(2)

    @pl.when(f == 0)
    def _first():
        xn_scr[...] = _norm_mod(x_ref[...], vec_ref).astype(BF16)
        acc_ref[...] = jnp.zeros_like(acc_ref)

    x = xn_scr[...]
    gate = _dot(x, wg_ref[...])
    up = _dot(x, wu_ref[...])
    h = (gate * jax.nn.sigmoid(gate) * up).astype(BF16)
    acc_ref[...] += _dot(h, w2_ref[...])

    @pl.when(f == n_f - 1)
    def _store():
        o_ref[...] = x_ref[...] + vec_ref[2:3, :] * acc_ref[...]


def _ffn(x, vecs, w13, w2):
    B, L, D = x.shape
    F = w13.shape[1] // 2
    tm = min(MM_TM, L)
    tf = FFN_TF if F % FFN_TF == 0 else FFN_TF // 2
    n_f = F // tf
    tok = pl.BlockSpec((None, tm, D), lambda b, i, f: (b, i, 0))
    return pl.pallas_call(
        functools.partial(_ffn_kernel, n_f=n_f),
        grid=(B, L // tm, n_f),
        in_specs=[tok,
                  pl.BlockSpec((None, 8, D), lambda b, i, f: (b, 0, 0)),
                  pl.BlockSpec((D, tf), lambda b, i, f: (0, f)),
                  pl.BlockSpec((D, tf), lambda b, i, f: (0, f + n_f)),
                  pl.BlockSpec((tf, D), lambda b, i, f: (f, 0))],
        out_specs=tok,
        out_shape=jax.ShapeDtypeStruct((B, L, D), F32),
        scratch_shapes=[pltpu.VMEM((tm, D), BF16), pltpu.VMEM((tm, D), F32)],
        compiler_params=_cparams(("parallel", "parallel", "arbitrary")),
        name="ffn",
    )(x, vecs, w13, w13, w2)


def _norm_route_kernel(x_ref, vec_ref, wr_ref, xn_ref, lg_ref):
    xn = _norm_mod(x_ref[...], vec_ref)
    xn_ref[...] = xn.astype(BF16)
    lg_ref[...] = _dot(xn, wr_ref[...], HIGHEST)


def _norm_route(x, vecs, w_router):
    B, L, D = x.shape
    E = w_router.shape[1]
    tm = min(NORM_TM, L)
    tok = pl.BlockSpec((None, tm, D), lambda b, i: (b, i, 0))
    return pl.pallas_call(
        _norm_route_kernel,
        grid=(B, L // tm),
        in_specs=[tok, pl.BlockSpec((None, 8, D), lambda b, i: (b, 0, 0)), pl.BlockSpec((D, E), lambda b, i: (0, 0))],
        out_specs=[tok, pl.BlockSpec((None, tm, E), lambda b, i: (b, i, 0))],
        out_shape=[jax.ShapeDtypeStruct((B, L, D), BF16), jax.ShapeDtypeStruct((B, L, E), F32)],
        compiler_params=_cparams(("parallel", "parallel")),
        name="norm_route",
    )(x, vecs, w_router)


def _gated_add_kernel(s_ref, y_ref, vec_ref, *rest, final):
    o_ref = rest[-1]
    z = s_ref[...] + vec_ref[2:3, :] * y_ref[...]
    if final:
        z = z * lax.rsqrt(jnp.mean(z * z, axis=-1, keepdims=True) + NORM_EPS) * rest[0][...]
    o_ref[...] = z


def _gated_add(s, y, vecs, out_gain=None):
    B, L, D = s.shape
    tm = min(NORM_TM, L)
    tok = pl.BlockSpec((None, tm, D), lambda b, i: (b, i, 0))
    final = out_gain is not None
    specs = [tok, tok, pl.BlockSpec((None, 8, D), lambda b, i: (b, 0, 0))]
    args = [s, y, vecs]
    if final:
        specs.append(pl.BlockSpec((1, D), lambda b, i: (0, 0)))
        args.append(out_gain[None, :])
    return pl.pallas_call(
        functools.partial(_gated_add_kernel, final=final),
        grid=(B, L // tm), in_specs=specs, out_specs=tok,
        out_shape=jax.ShapeDtypeStruct((B, L, D), F32),
        compiler_params=_cparams(("parallel", "parallel")),
        name="gated_add",
    )(*args)


def _moe_kernel(cnt_ref, x_ref, rrow_ref, rcol_ref, gcol_ref, wg_ref, wu_ref, w2_ref, o_ref, xc_scr, y_scr,
                *, n_f, tm):
    i, e, f = pl.program_id(0), pl.program_id(1), pl.program_id(2)
    sub = MOE_SUB
    slab = min(MOE_SLAB, tm)
    nslab = tm // slab
    n_sub = (cnt_ref[i, e, nslab] + (sub - 1)) // sub

    def touches(s, j):
        return (cnt_ref[i, e, j] < (s + 1) * sub) & (cnt_ref[i, e, j + 1] > s * sub)

    @pl.when((e == 0) & (f == 0))
    def _zero():
        o_ref[...] = jnp.zeros_like(o_ref)

    @pl.when(f == 0)
    def _compact():
        def body(s, carry):
            rows = pl.ds(pl.multiple_of(s * sub, sub), sub)
            y_scr[rows, :] = jnp.zeros((sub, y_scr.shape[1]), F32)
            for j in range(nslab):
                cols = slice(j * slab, (j + 1) * slab)

                @pl.when(touches(s, j))
                def _add():
                    ridx = lax.broadcasted_iota(jnp.int32, (sub, slab), 0) + s * sub
                    onehot = jnp.where(rrow_ref[:, cols] == ridx, 1.0, 0.0).astype(BF16)
                    y_scr[rows, :] += _dot(onehot, x_ref[cols, :])
            xc_scr[rows, :] = y_scr[rows, :].astype(BF16)
            return carry
        lax.fori_loop(0, n_sub, body, 0)

    def expert(s, first):
        rows = pl.ds(pl.multiple_of(s * sub, sub), sub)
        xs = xc_scr[rows, :]
        gate = _dot(xs, wg_ref[...])
        up = _dot(xs, wu_ref[...])
        h = (gate * jax.nn.sigmoid(gate) * up).astype(BF16)
        part = _dot(h, w2_ref[...])
        y_scr[rows, :] = part if first else y_scr[rows, :] + part

    def experts(first):
        def pair(p, carry):
            expert(2 * p, first)
            expert(2 * p + 1, first)
            return carry
        lax.fori_loop(0, n_sub // 2, pair, 0)

        @pl.when(n_sub % 2 == 1)
        def _tail():
            expert(n_sub - 1, first)

    @pl.when(f == 0)
    def _first():
        experts(True)

    @pl.when(f > 0)
    def _rest():
        experts(False)

    @pl.when(f == n_f - 1)
    def _scatter():
        def body(s, carry):
            y = y_scr[pl.ds(pl.multiple_of(s * sub, sub), sub), :].astype(BF16)
            for j in range(nslab):
                rows = slice(j * slab, (j + 1) * slab)

                @pl.when(touches(s, j))
                def _add():
                    cidx = lax.broadcasted_iota(jnp.int32, (slab, sub), 1) + s * sub
                    onehot_t = jnp.where(rcol_ref[rows, :] == cidx, 1.0, 0.0).astype(BF16)
                    o_ref[rows, :] += gcol_ref[rows, :] * _dot(onehot_t, y)
            return carry
        lax.fori_loop(0, n_sub, body, 0)


def _moe(x, logits, w13, w2):
    T, D = x.shape
    E, _, F2 = w13.shape
    F = F2 // 2
    tm = min(MOE_TILE, T)
    nt = T // tm
    tf = FFN_TF
    n_f = F // tf
    top_val, top_idx = lax.top_k(logits, TOP_K)
    gates = jax.nn.softmax(top_val, axis=-1)
    onehot = jax.nn.one_hot(top_idx, E, dtype=F32)
    sel = jnp.sum(onehot, axis=1).astype(jnp.int32).reshape(nt, tm, E)
    gate_dense = jnp.sum(onehot * gates[..., None], axis=1).reshape(nt, tm, E)
    rank = jnp.where(sel > 0, jnp.cumsum(sel, axis=1) - sel, -1)
    rank = jnp.swapaxes(rank, 1, 2)
    slab = min(MOE_SLAB, tm)
    per_slab = jnp.sum(sel.reshape(nt, tm // slab, slab, E), axis=2)
    counts = jnp.concatenate([jnp.zeros((nt, 1, E), jnp.int32), jnp.cumsum(per_slab, axis=1)], axis=1)
    counts = jnp.swapaxes(counts, 1, 2)
    gcol = jnp.swapaxes(gate_dense, 1, 2)[..., None]
    tile = lambda shape, imap: pl.BlockSpec(shape, imap)
    return pl.pallas_call(
        functools.partial(_moe_kernel, n_f=n_f, tm=tm),
        grid_spec=pltpu.PrefetchScalarGridSpec(
            num_scalar_prefetch=1,
            grid=(nt, E, n_f),
            in_specs=[tile((tm, D), lambda i, e, f, c: (i, 0)),
                      tile((None, None, 1, tm), lambda i, e, f, c: (i, e, 0, 0)),
                      tile((None, None, tm, 1), lambda i, e, f, c: (i, e, 0, 0)),
                      tile((None, None, tm, 1), lambda i, e, f, c: (i, e, 0, 0)),
                      tile((None, D, tf), lambda i, e, f, c: (e, 0, f)),
                      tile((None, D, tf), lambda i, e, f, c: (e, 0, f + n_f)),
                      tile((None, tf, D), lambda i, e, f, c: (e, f, 0))],
            out_specs=tile((tm, D), lambda i, e, f, c: (i, 0)),
            scratch_shapes=[pltpu.VMEM((-(-tm // MOE_SUB) * MOE_SUB, D), BF16),
                            pltpu.VMEM((-(-tm // MOE_SUB) * MOE_SUB, D), F32)]),
        out_shape=jax.ShapeDtypeStruct((T, D), F32),
        compiler_params=_cparams(("parallel", "arbitrary", "arbitrary")),
        name="moe",
    )(counts, x, rank[:, :, None, :], rank[..., None], gcol, w13, w13, w2)


def _rwkv_scan_kernel(r_ref, lw_ref, k_ref, v_ref, a_ref, b_ref, h0_ref, y_ref, hT_ref, H_scr,
                      *, reverse, nblk, nchunk, npl):
    C = RW_CHUNK
    half = LANES // 2
    i = pl.program_id(2)

    @pl.when(i == 0)
    def _init():
        H_scr[...] = h0_ref[...]

    t_idx = lax.broadcasted_iota(jnp.int32, (C, LANES), 0)
    s_idx = lax.broadcasted_iota(jnp.int32, (C, LANES), 1) & (half - 1)
    tt = lax.broadcasted_iota(jnp.int32, (C, C), 0)
    ss = lax.broadcasted_iota(jnp.int32, (C, C), 1)
    if reverse:
        strict, incl, tri = s_idx > t_idx, s_idx >= t_idx, (ss >= tt).astype(BF16)
    else:
        strict, incl, tri = s_idx < t_idx, s_idx <= t_idx, (ss <= tt).astype(BF16)
    eye_lp = (s_idx == t_idx).astype(F32)
    lane = lax.broadcasted_iota(jnp.int32, (1, LANES), 1)
    m0 = (lane < half).astype(F32)
    m1 = 1.0 - m0
    rr = lax.broadcasted_iota(jnp.int32, (LANES, LANES), 0)
    cc = lax.broadcasted_iota(jnp.int32, (LANES, LANES), 1)
    mask_bd = ((rr < half) == (cc < half)).astype(F32)

    def bd(x):
        return jnp.concatenate([x * m0, x * m1], axis=0)

    order = list(range(nchunk - 1, -1, -1) if reverse else range(nchunk))
    units = [(slice(c * C, (c + 1) * C), slice(q * LANES, (q + 1) * LANES)) for c in order for q in range(npl)]
    ld = lambda ref: [ref[sl, ln] for sl, ln in units]
    r, lw, k, v, a, b = ld(r_ref), ld(lw_ref), ld(k_ref), ld(v_ref), ld(a_ref), ld(b_ref)
    G = [_cumsum_dot(tri, x) for x in lw]
    eG = [jnp.exp(g) for g in G]
    eGn = [jnp.exp(-g) for g in G]
    rt = [x * e for x, e in zip(r, eG)]
    at = [x * jnp.exp(g - l) for x, g, l in zip(a, G, lw)]
    kt = [x * e for x, e in zip(k, eGn)]
    bt = [x * e for x, e in zip(b, eGn)]
    M = [_bdot(jnp.concatenate([x, y], axis=0), jnp.concatenate([bd(z), bd(w)], axis=0), nt=True)
         for x, y, z, w in zip(at, rt, bt, kt)]
    Nm = [jnp.where(strict, m[:C, :LANES], 0.0) for m in M]
    Aak = [jnp.where(strict, m[:C, LANES:], 0.0) for m in M]
    Arb = [jnp.where(incl, m[C:, :LANES], 0.0) for m in M]
    Ark = [jnp.where(incl, m[C:, LANES:], 0.0) for m in M]
    T = None
    for lvl in range(1, int(math.log2(C)) + 1):
        same = (t_idx >> lvl) == (s_idx >> lvl)
        t_hi = ((t_idx >> (lvl - 1)) & 1) == 1
        s_hi = ((s_idx >> (lvl - 1)) & 1) == 1
        off = same & (s_hi & ~t_hi if reverse else t_hi & ~s_hi)
        Noff = [jnp.where(off, n, 0.0) for n in Nm]
        if T is None:
            T = [eye_lp + n for n in Noff]
        else:
            DN = [_bdot(t, bd(n)) for t, n in zip(T, Noff)]
            T = [t + _bdot(dn, bd(t)) for t, dn in zip(T, DN)]
    bdv = [bd(x) for x in v]
    X0 = [_bdot(x, y) for x, y in zip(Aak, bdv)]
    WU = [_bdot(t, jnp.concatenate([bd(x), bd(y)], axis=1)) for t, x, y in zip(T, at, X0)]
    W = [x[:, :LANES] for x in WU]
    U0 = [x[:, LANES:] for x in WU]
    RY = [_bdot(x, jnp.concatenate([bd(y), bd(z)], axis=1)) for x, y, z in zip(Arb, W, U0)]
    Rh = [x + y[:, :LANES] for x, y in zip(rt, RY)]
    Y0 = [y[:, LANES:] + _bdot(x, z) for y, x, z in zip(RY, Ark, bdv)]
    dPhi = [mask_bd * _bdot(x.T, y) for x, y in zip(W, bt)]
    Psi = [mask_bd * _bdot(jnp.concatenate([x, y], axis=0).T, jnp.concatenate([z, w], axis=0))
           for x, y, z, w in zip(U0, v, bt, kt)]
    e_end = [jnp.exp(g[0:1, :] if reverse else g[C - 1:C, :]) for g in G]

    Hs = [H_scr[q] for q in range(npl)]
    for u, (sl, ln) in enumerate(units):
        q = u % npl
        H = Hs[q]
        y_ref[sl, ln] = _bdot(Rh[u], H, nt=True) + Y0[u]
        Hs[q] = (H + _bdot(H, dPhi[u]) + Psi[u]) * e_end[u]
    for q in range(npl):
        H_scr[q] = Hs[q]

    @pl.when(i == nblk - 1)
    def _fin():
        for q in range(npl):
            hT_ref[q] = Hs[q]


def _pair_states(S):
    B, H, N, _ = S.shape
    S5 = S.reshape(B, H // 2, 2, N, N)
    eye = jnp.eye(2, dtype=S.dtype)
    return jnp.einsum('bpivk,ij->bpivjk', S5, eye).reshape(B, H // 2, 2 * N, 2 * N)


def _unpair_states(Sp):
    B, P, N2, _ = Sp.shape
    N = N2 // 2
    S6 = Sp.reshape(B, P, 2, N, 2, N)
    return jnp.stack([S6[:, :, 0, :, 0, :], S6[:, :, 1, :, 1, :]], axis=2).reshape(B, 2 * P, N, N)


def _rwkv_scan(r, lw, k, v, a, b, state0, reverse):
    B, L, D = r.shape
    bt = min(RW_BLOCK, L)
    nblk = L // bt
    npl = RW_PAIRS_PER_STEP
    npair = D // LANES
    blk = (lambda i: nblk - 1 - i) if reverse else (lambda i: i)
    tok = pl.BlockSpec((None, bt, npl * LANES), lambda bb, p, i: (bb, blk(i), p))
    st = pl.BlockSpec((None, npl, LANES, LANES), lambda bb, p, i: (bb, p, 0, 0))
    y, hT = pl.pallas_call(
        functools.partial(_rwkv_scan_kernel, reverse=reverse, nblk=nblk, nchunk=bt // RW_CHUNK, npl=npl),
        grid=(B, npair // npl, nblk),
        in_specs=[tok] * 6 + [st],
        out_specs=[tok, st],
        out_shape=[jax.ShapeDtypeStruct((B, L, D), F32),
                   jax.ShapeDtypeStruct((B, npair, LANES, LANES), F32)],
        scratch_shapes=[pltpu.VMEM((npl, LANES, LANES), F32)],
        compiler_params=_cparams(("parallel", "parallel", "arbitrary")),
        name="rwkv_scan_rev" if reverse else "rwkv_scan_fwd",
    )(r, lw, k, v, a, b, _pair_states(state0))
    return y, _unpair_states(hT)


def _gla_kernel(q_ref, f_ref, v_ref, lb_ref, h0_ref, o_ref, hT_ref, H_scr, *, reverse, nblk, npl):
    C = HG_CHUNK
    sh = int(math.log2(C))
    bt = q_ref.shape[0]
    nchunk = bt // C
    i = pl.program_id(2)

    @pl.when(i == 0)
    def _init():
        H_scr[...] = h0_ref[...]

    tt = lax.broadcasted_iota(jnp.int32, (bt, bt), 0)
    ss = lax.broadcasted_iota(jnp.int32, (bt, bt), 1)
    same = (tt >> sh) == (ss >> sh)
    mid = ((tt >> sh) << sh) + (C // 2 if reverse else C // 2 - 1)
    if reverse:
        incl, upto_mid = same & (ss >= tt), same & (ss >= mid)
    else:
        incl, upto_mid = same & (ss <= tt), same & (ss <= mid)
    one = lambda m: jnp.where(m, 1.0, 0.0)
    sums = jnp.concatenate([one(incl), one(incl) - one(upto_mid), one(same) - one(incl)], axis=0).astype(BF16)

    order = list(range(nchunk - 1, -1, -1) if reverse else range(nchunk))
    heads = [slice(p * LANES, (p + 1) * LANES) for p in range(npl)]
    q = [jax.nn.silu(q_ref[:, ln]) for ln in heads]
    fg = [lb_ref[:, ln] + (1.0 - lb_ref[:, ln]) * jax.nn.sigmoid(f_ref[:, ln]) for ln in heads]
    v = [v_ref[:, ln] for ln in heads]
    k = [1.0 - f for f in fg]
    Gs = [_cumsum_dot(sums, jnp.log(f)) for f in fg]
    Gabs = [g[:bt] for g in Gs]
    Grel = [g[bt:2 * bt] for g in Gs]
    Gend = [g[2 * bt:] for g in Gs]
    scores = [jnp.where(incl, _bdot(x * jnp.exp(g), y * jnp.exp(-g), nt=True), 0.0)
              for x, y, g in zip(q, k, Grel)]
    o_intra = [_bdot(s, x) for s, x in zip(scores, v)]
    qa = [x * jnp.exp(g) for x, g in zip(q, Gabs)]
    kend = [x * jnp.exp(g) for x, g in zip(k, Gend)]
    rows = [slice(c * C, (c + 1) * C) for c in range(nchunk)]
    KV = [[_bdot(x[r].T, y[r]) for r in rows] for x, y in zip(v, kend)]
    dec = [[jnp.exp(ga[r][0:1, :] + ge[r][0:1, :]) for r in rows] for ga, ge in zip(Gabs, Gend)]

    Hs = [H_scr[p] for p in range(npl)]
    for c in order:
        for p in range(npl):
            o_ref[rows[c], heads[p]] = o_intra[p][rows[c]] + _bdot(qa[p][rows[c]], Hs[p], nt=True)
            Hs[p] = Hs[p] * dec[p][c] + KV[p][c]
    for p in range(npl):
        H_scr[p] = Hs[p]

    @pl.when(i == nblk - 1)
    def _fin():
        for p in range(npl):
            hT_ref[p] = Hs[p]


def _gla_scan(proj, lb, state0, d):
    B, L, D5 = proj.shape
    D = D5 // 5
    reverse = d == 1
    bt = GLA_BLOCK
    nblk = L // bt
    npl = GLA_HEADS_PER_STEP
    nh = D // LANES
    ncb = nh // npl
    blk = (lambda i: nblk - 1 - i) if reverse else (lambda i: i)
    col = lambda off: pl.BlockSpec((None, bt, npl * LANES), lambda bb, p, i: (bb, blk(i), off * ncb + p))
    st = pl.BlockSpec((None, npl, LANES, LANES), lambda bb, p, i: (bb, p, 0, 0))
    o, hT = pl.pallas_call(
        functools.partial(_gla_kernel, reverse=reverse, nblk=nblk, npl=npl),
        grid=(B, ncb, nblk),
        in_specs=[col(0), col(1 + d), col(3), pl.BlockSpec((1, npl * LANES), lambda bb, p, i: (0, p)), st],
        out_specs=[col(0), st],
        out_shape=[jax.ShapeDtypeStruct((B, L, D), F32),
                   jax.ShapeDtypeStruct((B, nh, LANES, LANES), F32)],
        scratch_shapes=[pltpu.VMEM((npl, LANES, LANES), F32)],
        compiler_params=_cparams(("parallel", "parallel", "arbitrary")),
        name="gla_scan_rev" if reverse else "gla_scan_fwd",
    )(proj, proj, proj, lb[d][None, :], jnp.swapaxes(state0, 2, 3))
    return o, jnp.swapaxes(hT, 2, 3)


def _leftmul_kernel(fh_ref, fl_ref, x_ref, o_ref, xs_scr, *, nj):
    for j in range(nj):
        xs_scr[...] = x_ref[:, j, :]
        o_ref[:, j, :] = _x3dot(fh_ref[...], fl_ref[...], xs_scr[...])


def _leftmul_gate_kernel(fh_ref, fl_ref, x_ref, u_ref, s_ref, g_ref, o_ref, xs_scr, *, nj):
    for j in range(nj):
        xs_scr[...] = x_ref[:, j, :]
        y = _x3dot(fh_ref[...], fl_ref[...], xs_scr[...])
        o_ref[:, j, :] = (y + u_ref[:, j, :] * s_ref[...]) * g_ref[:, j, :]


def _leftmul(f, x, xoff=0, epilogue=None, D=D_MODEL):
    B, K, J, _ = x.shape
    M = f.shape[0]
    tj = 8 if J % 8 == 0 else J
    td = DFT_TD
    fh, fl = _split2(f)
    fs = pl.BlockSpec((M, K), lambda b, j, d: (0, 0))
    col = lambda rows, off: pl.BlockSpec((None, rows, tj, td), lambda b, j, d: (b, 0, j, off + d))
    if epilogue is None:
        kern, specs, args = _leftmul_kernel, [fs, fs, col(K, xoff)], (fh, fl, x)
    else:
        u, uoff, skip, gate, goff = epilogue
        kern = _leftmul_gate_kernel
        specs = [fs, fs, col(K, xoff), col(M, uoff), pl.BlockSpec((1, td), lambda b, j, d: (0, d)), col(M, goff)]
        args = (fh, fl, x, u, skip, gate)
    return pl.pallas_call(
        functools.partial(kern, nj=tj), grid=(B, J // tj, D // td), in_specs=specs, out_specs=col(M, 0),
        out_shape=jax.ShapeDtypeStruct((B, M, J, D), F32),
        scratch_shapes=[pltpu.VMEM((K, td), F32)],
        compiler_params=_cparams(("parallel", "parallel", "parallel")),
        name="dft_outer",
    )(*args)


def _spectral_fwd_kernel(fh_ref, fl_ref, y_ref, o_ref):
    o_ref[...] = _x3dot(fh_ref[...], fl_ref[...], y_ref[...])


def _spectral_conv_kernel(fh_ref, fl_ref, fih_ref, fil_ref, hf_ref, hb_ref, y_ref, o_ref):
    n2 = hf_ref.shape[0] // 2
    z = _x3dot(fh_ref[...], fl_ref[...], y_ref[...])
    zr, zi = z[:n2], z[n2:]
    kf = _combine_filter_spectrum(hf_ref[...], hb_ref[...])
    kr, ki = kf[:n2], kf[n2:]
    p = jnp.concatenate([zr * kr - zi * ki, zr * ki + zi * kr], axis=0)
    o_ref[...] = _x3dot(fih_ref[...], fil_ref[...], p)


def _spectral_mid(y, f_fwd, f_inv=None, spec=None, order=0):
    B, K1, R, D = y.shape
    ms = pl.BlockSpec((None, R, R), lambda k1, b: (k1, 0, 0))
    ys = pl.BlockSpec((None, None, R, D), lambda k1, b: (b, k1, 0, 0))
    if spec is None:
        kern, specs, args = _spectral_fwd_kernel, [ms, ms, ys], (*_split2(f_fwd), y)
    else:
        kern = _spectral_conv_kernel
        grp = lambda g: pl.BlockSpec((None, R, D), lambda k1, b: (k1, 0, g))
        specs = [ms, ms, ms, ms, grp(2 * order), grp(2 * order + 1), ys]
        args = (*_split2(f_fwd), *_split2(f_inv), spec, spec, y)
    return pl.pallas_call(
        kern, grid=(K1, B), in_specs=specs, out_specs=ys,
        out_shape=jax.ShapeDtypeStruct((B, K1, R, D), F32),
        compiler_params=_cparams(("parallel", "parallel")),
        name="dft_inner",
    )(*args)


def _dft_tables(L):
    n2 = HY_N2 if L % HY_N2 == 0 and L > HY_N2 * 2 else 1
    N = 2 * L
    n1 = N // n2
    k1h = n1 // 2
    ph = (jnp.arange(k1h, dtype=jnp.int32)[:, None] * 2 + 1) * jnp.arange(k1h, dtype=jnp.int32)[None, :]
    th = (ph % (2 * n1)).astype(F32) * (math.pi / n1)
    f_out = jnp.stack([jnp.cos(th), -jnp.sin(th)], axis=1).reshape(2 * k1h, k1h)
    f_out_inv = (2.0 / N) * f_out.T
    if n2 == 1:
        return n2, f_out, f_out_inv, None, None
    kk = jnp.arange(k1h, dtype=jnp.int32)[:, None, None] + n1 * jnp.arange(n2, dtype=jnp.int32)[None, :, None]
    ph = ((2 * kk + 1) * jnp.arange(n2, dtype=jnp.int32)[None, None, :]) % (2 * N)
    phi = ph.astype(F32) * (math.pi / N)
    c, s = jnp.cos(phi), jnp.sin(phi)
    f_in = jnp.concatenate([jnp.concatenate([c, s], axis=2), jnp.concatenate([-s, c], axis=2)], axis=1)
    ct, st = jnp.swapaxes(c, 1, 2), jnp.swapaxes(s, 1, 2)
    f_in_inv = jnp.concatenate([jnp.concatenate([ct, -st], axis=2), jnp.concatenate([st, ct], axis=2)], axis=1)
    return n2, f_out, f_out_inv, f_in, f_in_inv


def _long_conv_gated(u, ucol, spec, order, skip, gate, gcol, tables):
    n2, f_out, f_out_inv, f_in, f_in_inv = tables
    B, L, _ = u.shape
    D = D_MODEL
    k1h = L // n2
    per = D // DFT_TD
    u4 = u.reshape(B, k1h, n2, u.shape[-1])
    y = _leftmul(f_out, u4, xoff=ucol * per).reshape(B, k1h, 2 * n2, D)
    if f_in is None:
        cols = lambda g: spec[..., g * D:(g + 1) * D]
        kf = _combine_filter_spectrum(cols(2 * order), cols(2 * order + 1))
        yr, yi = y[:, :, 0], y[:, :, 1]
        kr, ki = kf[:, 0], kf[:, 1]
        q = jnp.stack([yr * kr - yi * ki, yr * ki + yi * kr], axis=2)
    else:
        q = _spectral_mid(y, f_in, f_in_inv, spec, order)
    out = _leftmul(f_out_inv, q.reshape(B, 2 * k1h, n2, D),
                   epilogue=(u4, ucol * per, skip[None, :], gate.reshape(B, k1h, n2, gate.shape[-1]), gcol * per))
    return out.reshape(B, L, D)


def _hyena_filter_spectra(L, w1, b1, w2, b2, w3, b3, w4, freq, tables):
    pos = jnp.arange(L, dtype=F32)
    t = (pos / max(L - 1, 1))[:, None]
    ang = (2.0 * math.pi / L) * pos[:, None] * jnp.linspace(1e-4, HY_BANDS - 1, HY_BANDS, dtype=F32)[None, :]
    z = jnp.concatenate([t, jnp.cos(ang), -jnp.sin(ang)], axis=-1)
    z = jnp.pad(z, ((0, 0), (0, 40 - HY_EMB)))
    w1p = jnp.pad(w1, ((0, 40 - HY_EMB), (0, 0)))
    hid = jnp.sin(freq[0] * (_mm(z, w1p, exact=True) + b1))
    hid = jnp.sin(freq[1] * (_mm(hid, w2, exact=True) + b2))
    hid = jnp.sin(freq[2] * (_mm(hid, w3, exact=True) + b3))
    filt = _mm(hid, w4, exact=True)
    deltas = jnp.linspace(math.log(HY_TARGET) / HY_FAST_DECAY, math.log(HY_TARGET) / HY_SLOW_DECAY,
                          D_MODEL, dtype=F32)
    window = jnp.exp(-t * jnp.abs(deltas))
    lag0 = (jnp.arange(L) > 0).astype(F32)[:, None]
    scale = jnp.concatenate([window, window * lag0] * HY_ORDER, axis=1)
    filt = filt * scale
    n2 = tables[0]
    k1h = L // n2
    C = HY_ORDER * 2 * D_MODEL
    y = _leftmul(tables[1], filt.reshape(1, k1h, n2, C), D=C).reshape(1, k1h, 2 * n2, C)
    return (y if tables[3] is None else _spectral_mid(y, tables[3]))[0]


def _combine_filter_spectrum(hf, hb):
    n2 = hf.shape[-2] // 2
    return jnp.concatenate([hf[..., :n2, :] + hb[..., :n2, :], hf[..., n2:, :] - hb[..., n2:, :]], axis=-2)


def _hyena_stream(resid, vecs, fprm, w_in, b_in, conv_w, conv_b, skip, w_o, b_o):
    B, L, D = resid.shape
    tables = _dft_tables(L)
    spec = _hyena_filter_spectra(L, *fprm, tables)
    proj = _proj_conv3(resid, vecs, w_in, b_in, conv_w, conv_b)
    z = _long_conv_gated(proj, 2, spec, 0, skip[0], proj, 0, tables)
    y = _long_conv_gated(z, 0, spec, 1, skip[1], proj, 1, tables)
    return _mm_fused(y, w_o, vecs=vecs, bias=b_o, resid=resid)


def _headsum(x, ones):
    hi, lo = _split2(x)
    cols = []
    for j in range(x.shape[1] // LANES):
        ln = slice(j * LANES, (j + 1) * LANES)
        cols.append(_dot(hi[:, ln], ones) + _dot(lo[:, ln], ones))
    return jnp.concatenate(cols, axis=1)


def _token_shift(h, h_above, h_below, grid_w):
    tm, D = h.shape
    row = lax.broadcasted_iota(jnp.int32, (tm, D), 0)
    lane = lax.broadcasted_iota(jnp.int32, (tm, D), 1)
    before = pltpu.roll(h, 1, axis=0)
    after = pltpu.roll(h, tm - 1, axis=0)
    if grid_w is None:
        return jnp.where(lane < D // 2, jnp.where(row == 0, 0.0, before), jnp.where(row == tm - 1, 0.0, after))
    col = row & (grid_w - 1)
    left = jnp.where(col == 0, 0.0, before)
    right = jnp.where(col == grid_w - 1, 0.0, after)
    up = jnp.concatenate([h_above, h[:tm - grid_w]], axis=0)
    down = jnp.concatenate([h[grid_w:], h_below], axis=0)
    q = D // 4
    return jnp.where(lane < q, left, jnp.where(lane < 2 * q, right, jnp.where(lane < 3 * q, up, down)))


def _rwkv_pre_kernel(*refs, vres, grid_w, nblk):
    it = iter(refs)
    x_ref = next(it)
    xa_ref, xb_ref = (next(it), next(it)) if grid_w else (None, None)
    mod_ref = next(it)
    vf_ref = next(it) if vres else None
    vec_ref, wr_ref, wk_ref, wv_ref, w1_ref, w2_ref, a1_ref, a2_ref, g1_ref, g2_ref = (next(it) for _ in range(10))
    v1_ref, v2_ref = (next(it), next(it)) if vres else (None, None)
    ones_ref = next(it)
    r_o, v_o, nkk_o, lw0_o, lw1_o, kd0_o, kd1_o, b0_o, b1_o, bonus_o, g_o = it
    vec = lambda i: vec_ref[i:i + 1, :]
    i = pl.program_id(1)
    h = _norm_mod(x_ref[...], mod_ref)
    if grid_w:
        h_above = jnp.where(i > 0, _norm_mod(xa_ref[...], mod_ref), 0.0)
        h_below = jnp.where(i < nblk - 1, _norm_mod(xb_ref[...], mod_ref), 0.0)
    else:
        h_above = h_below = None
    xx = _token_shift(h, h_above, h_below, grid_w) - h
    xr, xw, xk, xv, xa, xg = ((h + xx * vec(j)).astype(BF16) for j in range(6))
    r = _dot(xr, wr_ref[...])
    k = _dot(xk, wk_ref[...])
    v = _dot(xv, wv_ref[...])
    if vres:
        lora = _dot(_dot(xv, v1_ref[...]).astype(BF16), v2_ref[...])
        v = v + (vf_ref[...] - v) * jax.nn.sigmoid(vec(13) + lora)
    ones = ones_ref[...]
    kk = k * vec(10)
    kk = kk / jnp.maximum(jnp.sqrt(_headsum(kk * kk, ones)), 1e-12)
    r_o[...] = r
    v_o[...] = v
    nkk_o[...] = -kk
    bonus = jnp.zeros_like(r)
    for d, (lw_o, kd_o, b_o) in enumerate(((lw0_o, kd0_o, b0_o), (lw1_o, kd1_o, b1_o))):
        wl = vec(6 + d) + _dot(jnp.tanh(_dot(xw, w1_ref[d])).astype(BF16), w2_ref[d])
        lw_o[...] = -jax.nn.sigmoid(wl) * math.exp(-0.5)
        a = jax.nn.sigmoid(vec(8 + d) + _dot(_dot(xa, a1_ref[d]).astype(BF16), a2_ref[d]))
        kd = k * (1.0 + (a - 1.0) * vec(11))
        kd_o[...] = kd
        b_o[...] = kk * a
        bonus = bonus + _headsum(r * kd * vec(12), ones) * v
    bonus_o[...] = bonus
    g_o[...] = _dot(jax.nn.sigmoid(_dot(xg, g1_ref[...])).astype(BF16), g2_ref[...])


def _rwkv_post_kernel(y0_ref, y1_ref, bonus_ref, g_ref, res_ref, mod_ref, ln_ref, wo_ref, ones_ref, o_ref):
    ones = ones_ref[...]
    y = y0_ref[...] + y1_ref[...]
    yc = y - _headsum(y, ones) * (1.0 / RW_HEAD)
    var = _headsum(yc * yc, ones) * (1.0 / RW_HEAD)
    yn = yc * lax.rsqrt(var + RW_LN_EPS) * ln_ref[0:1, :] + ln_ref[1:2, :] + bonus_ref[...]
    out = _dot((yn * g_ref[...]).astype(BF16), wo_ref[...])
    o_ref[...] = res_ref[...] + mod_ref[2:3, :] * out


def _full(a):
    nd = a.ndim
    return pl.BlockSpec(a.shape, lambda b, i: (0,) * nd)


def _rwkv7_stream(resid, mods, grid_w, state0, v_first, mu, w_rkv, w_o, w0, w1, w2, a0, a1, a2,
                  g1, g2, k_k, k_a, r_k, ln_w, ln_b, vres, need_out=True):
    B, L, D = resid.shape
    tm = min(RW_TM, L)
    nblk = L // tm
    assert grid_w is not None or nblk == 1
    vres_on = vres is not None
    rows = [mu[j] for j in range(6)] + [w0[0], w0[1], a0[0], a0[1], k_k, k_a, r_k.reshape(D)]
    rows.append(vres[0] if vres_on else jnp.zeros((D,), F32))
    vecs = jnp.stack(rows + [jnp.zeros((D,), F32)] * (16 - len(rows)))
    ones = jnp.kron(jnp.eye(LANES // RW_HEAD, dtype=F32), jnp.ones((RW_HEAD, RW_HEAD), F32)).astype(BF16)
    bf = lambda t: t.astype(BF16)
    tok = pl.BlockSpec((None, tm, D), lambda b, i: (b, i, 0))
    mod_spec = pl.BlockSpec((None, 8, D), lambda b, i: (b, 0, 0))
    args, specs = [resid], [tok]
    if grid_w:
        per = tm // grid_w
        last_row = L // grid_w - 1
        args += [resid, resid]
        specs += [pl.BlockSpec((None, grid_w, D), lambda b, i: (b, jnp.maximum(i * per - 1, 0), 0)),
                  pl.BlockSpec((None, grid_w, D), lambda b, i: (b, jnp.minimum((i + 1) * per, last_row), 0))]
    args.append(mods)
    specs.append(mod_spec)
    if vres_on:
        args.append(v_first)
        specs.append(tok)
    consts = [vecs, bf(w_rkv[0]), bf(w_rkv[1]), bf(w_rkv[2]), bf(w1), bf(w2), bf(a1), bf(a2), bf(g1), bf(g2)]
    consts += [bf(vres[1]), bf(vres[2])] if vres_on else []
    consts.append(ones)
    outs = pl.pallas_call(
        functools.partial(_rwkv_pre_kernel, vres=vres_on, grid_w=grid_w, nblk=nblk),
        grid=(B, nblk),
        in_specs=specs + [_full(c) for c in consts],
        out_specs=[tok] * 11,
        out_shape=[jax.ShapeDtypeStruct((B, L, D), F32)] * 11,
        compiler_params=_cparams(("parallel", "parallel")),
        name="rwkv_pre",
    )(*args, *consts)
    r, v, nkk, lw0, lw1, kd0, kd1, b0, b1, bonus, g = outs
    y0, s0 = _rwkv_scan(r, lw0, kd0, v, nkk, b0, state0[0], reverse=False)
    y1, s1 = _rwkv_scan(r, lw1, kd1, v, nkk, b1, state0[1], reverse=True)
    states = jnp.stack([s0, s1])
    if not need_out:
        return None, states, v
    ln = jnp.stack([ln_w, ln_b] + [jnp.zeros((D,), F32)] * 6)
    wo = bf(w_o)
    new_resid = pl.pallas_call(
        _rwkv_post_kernel,
        grid=(B, L // tm),
        in_specs=[tok] * 5 + [mod_spec, _full(ln), _full(wo), _full(ones)],
        out_specs=tok,
        out_shape=jax.ShapeDtypeStruct((B, L, D), F32),
        compiler_params=_cparams(("parallel", "parallel")),
        name="rwkv_post",
    )(y0, y1, bonus, g, resid, mods, ln, wo, ones)
    return new_resid, states, v


def _mm_fused_kernel(*refs, norm, bias, resid):
    it = iter(refs)
    x_ref = next(it)
    vec_ref = next(it) if (norm or resid) else None
    w_ref = next(it)
    b_ref = next(it) if bias else None
    res_ref = next(it) if resid else None
    o_ref, xs_scr = next(it), next(it)

    @pl.when(pl.program_id(2) == 0)
    def _prep():
        x = x_ref[...]
        xs_scr[...] = (_norm_mod(x, vec_ref) if norm else x).astype(BF16)

    acc = _dot(xs_scr[...], w_ref[...])
    if bias:
        acc = acc + b_ref[...]
    o_ref[...] = res_ref[...] + vec_ref[2:3, :] * acc if resid else acc


def _mm_fused(x, w, vecs=None, norm=False, bias=None, resid=None):
    B, L, K = x.shape
    N = w.shape[1]
    tm = min(MM_TM, L)
    tn = MM_TN if (N % MM_TN == 0 and resid is None) else N
    use_vec = norm or resid is not None
    specs = [pl.BlockSpec((None, tm, K), lambda b, i, j: (b, i, 0))]
    args = [x]
    if use_vec:
        specs.append(pl.BlockSpec((None, 8, vecs.shape[-1]), lambda b, i, j: (b, 0, 0)))
        args.append(vecs)
    specs.append(pl.BlockSpec((K, tn), lambda b, i, j: (0, j)))
    args.append(w.astype(BF16))
    if bias is not None:
        specs.append(pl.BlockSpec((1, tn), lambda b, i, j: (0, j)))
        args.append(bias[None, :])
    out_spec = pl.BlockSpec((None, tm, tn), lambda b, i, j: (b, i, j))
    if resid is not None:
        assert N == K == vecs.shape[-1] and tn == N
        specs.append(out_spec)
        args.append(resid)
    return pl.pallas_call(
        functools.partial(_mm_fused_kernel, norm=norm, bias=bias is not None, resid=resid is not None),
        grid=(B, L // tm, N // tn),
        in_specs=specs, out_specs=out_spec,
        out_shape=jax.ShapeDtypeStruct((B, L, N), F32),
        scratch_shapes=[pltpu.VMEM((tm, K), BF16)],
        compiler_params=_cparams(("parallel", "parallel", "arbitrary")),
        name="mm_fused",
    )(*args)


def _proj_conv3_kernel(x_ref, xa_ref, xb_ref, vec_ref, w_ref, b_ref, cw_ref, o_ref, xs_scr, acc_scr, *, nblk):
    i = pl.program_id(1)
    tm = x_ref.shape[0]

    @pl.when(pl.program_id(2) == 0)
    def _prep():
        xs_scr[0:HALO, :] = _norm_mod(xa_ref[...], vec_ref).astype(BF16)
        xs_scr[HALO:HALO + tm, :] = _norm_mod(x_ref[...], vec_ref).astype(BF16)
        xs_scr[HALO + tm:, :] = _norm_mod(xb_ref[...], vec_ref).astype(BF16)

    acc_scr[...] = _dot(xs_scr[...], w_ref[...]) + b_ref[...]
    row = lax.broadcasted_iota(jnp.int32, o_ref.shape, 0)
    before = jnp.where((row == 0) & (i == 0), 0.0, acc_scr[HALO - 1:HALO - 1 + tm, :])
    after = jnp.where((row == tm - 1) & (i == nblk - 1), 0.0, acc_scr[HALO + 1:HALO + 1 + tm, :])
    o_ref[...] = (before * cw_ref[0:1, :] + acc_scr[HALO:HALO + tm, :] * cw_ref[1:2, :]
                  + after * cw_ref[2:3, :] + cw_ref[3:4, :])


def _proj_conv3(x, vecs, w, bias, taps, conv_bias):
    B, L, K = x.shape
    N = w.shape[1]
    tm = min(MM_TM, L)
    tn = MM_TN
    nblk = L // tm
    per = tm // HALO
    cw = jnp.concatenate([taps, conv_bias[None, :], jnp.zeros((4, N), F32)], axis=0)
    return pl.pallas_call(
        functools.partial(_proj_conv3_kernel, nblk=nblk),
        grid=(B, nblk, N // tn),
        in_specs=[pl.BlockSpec((None, tm, K), lambda b, i, j: (b, i, 0)),
                  pl.BlockSpec((None, HALO, K), lambda b, i, j: (b, jnp.maximum(i * per - 1, 0), 0)),
                  pl.BlockSpec((None, HALO, K), lambda b, i, j: (b, jnp.minimum((i + 1) * per, L // HALO - 1), 0)),
                  pl.BlockSpec((None, 8, K), lambda b, i, j: (b, 0, 0)),
                  pl.BlockSpec((K, tn), lambda b, i, j: (0, j)),
                  pl.BlockSpec((1, tn), lambda b, i, j: (0, j)),
                  pl.BlockSpec((8, tn), lambda b, i, j: (0, j))],
        out_specs=pl.BlockSpec((None, tm, tn), lambda b, i, j: (b, i, j)),
        out_shape=jax.ShapeDtypeStruct((B, L, N), F32),
        scratch_shapes=[pltpu.VMEM((tm + 2 * HALO, K), BF16), pltpu.VMEM((tm + 2 * HALO, tn), F32)],
        compiler_params=_cparams(("parallel", "parallel", "arbitrary")),
        name="proj_conv3",
    )(x, x, x, vecs, w.astype(BF16), bias[None, :], cw)


def _hgrn_post_kernel(o0_ref, o1_ref, g_ref, res_ref, vec_ref, gn_ref, wo_ref, ones_ref, o_ref):
    o = o0_ref[...] + o1_ref[...]
    ms = _headsum(o * o, ones_ref[...]) * (1.0 / HG_DK)
    on = o * lax.rsqrt(ms + NORM_EPS) * gn_ref[...]
    g = g_ref[...]
    z = (on * (g * jax.nn.sigmoid(g))).astype(BF16)
    o_ref[...] = res_ref[...] + vec_ref[2:3, :] * _dot(z, wo_ref[...])


def _hgrn2_stream(resid, vecs, state0, lb, w_in, gn, w_o, need_out=True):
    B, L, D = resid.shape
    proj = _mm_fused(resid, w_in, vecs=vecs, norm=True)
    o0, s0 = _gla_scan(proj, lb, state0[0], 0)
    o1, s1 = _gla_scan(proj, lb, state0[1], 1)
    states = jnp.stack([s0, s1])
    if not need_out:
        return None, states
    tm = min(RW_TM, L)
    tok = pl.BlockSpec((None, tm, D), lambda b, i: (b, i, 0))
    ones = jnp.ones((LANES, LANES), BF16)
    gn_row = jnp.tile(gn, D // gn.shape[0])[None, :]
    wo = w_o.astype(BF16)
    out = pl.pallas_call(
        _hgrn_post_kernel,
        grid=(B, L // tm),
        in_specs=[tok, tok, pl.BlockSpec((None, tm, D), lambda b, i: (b, i, 4)), tok,
                  pl.BlockSpec((None, 8, D), lambda b, i: (b, 0, 0)), _full(gn_row), _full(wo), _full(ones)],
        out_specs=tok,
        out_shape=jax.ShapeDtypeStruct((B, L, D), F32),
        compiler_params=_cparams(("parallel", "parallel")),
        name="hgrn_post",
    )(o0, o1, proj, resid, vecs, gn_row, wo, ones)
    return out, states


def kernel(x, c, ctx, c_ctx, norm_g, ada_w, ada_b, final_g,
           rw_mu, rw_wrkv, rw_wo, rw_w0, rw_w1, rw_w2, rw_a0, rw_a1, rw_a2,
           rw_v0, rw_v1, rw_v2, rw_g1, rw_g2, rw_kk, rw_ka, rw_rk, rw_lnw, rw_lnb,
           hy_win, hy_bin, hy_cw, hy_cb, hy_fw1, hy_fb1, hy_fw2, hy_fb2, hy_fw3, hy_fb3,
           hy_fw4, hy_freq, hy_skip, hy_wo, hy_bo,
           hg_win, hg_lb, hg_gn, hg_wo,
           ffn_w13, ffn_w2, moe_router, moe_w13, moe_w2):
    B = x.shape[0]
    depth = norm_g.shape[0]
    D = D_MODEL
    lat, cx = x, ctx
    v_first = None
    lbc = jnp.cumsum(jax.nn.softmax(hg_lb, axis=0), axis=0)
    lower = lbc - lbc[:1]
    cond = jnp.concatenate([jax.nn.silu(c), jax.nn.silu(c_ctx)[None, :]], axis=0)
    for i in range(depth):
        last = i == depth - 1
        if i % 2 == 0:
            mix_w13, mix_w2 = ffn_w13[i // 2].astype(BF16), ffn_w2[i // 2].astype(BF16)
        else:
            mix_w13, mix_w2 = moe_w13[i // 2].astype(BF16), moe_w2[i // 2].astype(BF16)
        mod = _mm(cond, ada_w[i]) + ada_b[i]
        mod_l = jnp.split(mod[:B, None, :], 6, axis=-1)
        mod_c = jnp.split(mod[B:, None, :], 6, axis=-1)
        kind, slot = i % N_MIXERS, i // N_MIXERS
        vec_l = _mod_rows(norm_g[i, 0], mod_l[0], mod_l[1], mod_l[2], B)
        vec_c = _mod_rows(norm_g[i, 0], mod_c[0], mod_c[1], mod_c[2], B)
        if kind == 0:
            vres = None if slot == 0 else (rw_v0[slot - 1], rw_v1[slot - 1], rw_v2[slot - 1])
            rw = (rw_mu[slot], rw_wrkv[slot], rw_wo[slot], rw_w0[slot], rw_w1[slot], rw_w2[slot],
                  rw_a0[slot], rw_a1[slot], rw_a2[slot], rw_g1[slot], rw_g2[slot], rw_kk[slot],
                  rw_ka[slot], rw_rk[slot], rw_lnw[slot], rw_lnb[slot], vres)
            zero = jnp.zeros((2, B, RW_H, RW_HEAD, RW_HEAD), F32)
            vf_c = None if v_first is None else v_first[0]
            vf_l = None if v_first is None else v_first[1]
            cx_new, s_ctx, v_c = _rwkv7_stream(cx, vec_c, None, zero, vf_c, *rw, need_out=not last)
            lat, _, v_l = _rwkv7_stream(lat, vec_l, GRID_W, s_ctx, vf_l, *rw)
            if not last:
                cx = cx_new
            if slot == 0:
                v_first = (v_c, v_l)
        elif kind == 1:
            fprm = (hy_fw1[slot], hy_fb1[slot], hy_fw2[slot], hy_fb2[slot], hy_fw3[slot],
                    hy_fb3[slot], hy_fw4[slot], hy_freq[slot])
            hprm = (hy_win[slot], hy_bin[slot], hy_cw[slot], hy_cb[slot], hy_skip[slot],
                    hy_wo[slot], hy_bo[slot])
            lat = _hyena_stream(lat, vec_l, fprm, *hprm)
            if not last:
                cx = _hyena_stream(cx, vec_c, fprm, *hprm)
        else:
            zero = jnp.zeros((2, B, HG_H, HG_DK, D // HG_H), F32)
            gprm = (lower[i], hg_win[slot], hg_gn[slot], hg_wo[slot])
            cx_new, s_ctx = _hgrn2_stream(cx, vec_c, zero, *gprm, need_out=not last)
            lat, _ = _hgrn2_stream(lat, vec_l, s_ctx, *gprm)
            if not last:
                cx = cx_new

        def channel_mix(s, mod, out_gain=None):
            vecs = _mod_rows(norm_g[i, 1], mod[3], mod[4], mod[5], B)
            if i % 2 == 0:
                return _ffn(s, vecs, mix_w13, mix_w2)
            xn, logits = _norm_route(s, vecs, moe_router[i // 2])
            out = _moe(xn.reshape(-1, D), logits.reshape(-1, N_EXPERTS), mix_w13, mix_w2)
            return _gated_add(s, out.reshape(s.shape), vecs, out_gain)

        assert depth % 2 == 0
        lat = channel_mix(lat, mod_l, final_g if last else None)
        if not last:
            cx = channel_mix(cx, mod_c)
    return lat
```

```python
import functools
import math

import jax
import jax.numpy as jnp
from jax import lax
from jax.experimental import pallas as pl
from jax.experimental.pallas import tpu as pltpu

F32 = jnp.float32
BF16 = jnp.bfloat16
HIGHEST = lax.Precision.HIGHEST

D_MODEL = 1024
GRID_W = 64
NORM_EPS = 1e-6
RW_HEAD = 64
RW_H = D_MODEL // RW_HEAD
RW_LN_EPS = 64e-5
HY_ORDER = 2
HY_EMB = 33
HY_BANDS = (HY_EMB - 1) // 2
HY_FAST_DECAY = 0.3
HY_SLOW_DECAY = 1.5
HY_TARGET = 1e-2
HG_DK = 128
HG_H = D_MODEL // HG_DK
HG_CHUNK = 32
N_EXPERTS = 8
TOP_K = 2
N_MIXERS = 3

LANES = 128
VMEM_LIMIT = 56 * 1024 * 1024

MM_TM, MM_TN = 1024, 512
FFN_TF = 512
NORM_TM = 1024
POST_TM = 512
HALO = 16
RW_TM = 256
RW_CHUNK = 64
RW_BLOCK = 512
RW_PAIRS_PER_STEP = 4
GLA_BLOCK = 256
GLA_HEADS_PER_STEP = 8
MOE_TILE = 2048
MOE_SUB = 256
MOE_SLAB = 512
HY_N2 = 128
DFT_TD = 512


def _cparams(sem):
    return pltpu.CompilerParams(dimension_semantics=sem, vmem_limit_bytes=VMEM_LIMIT)


def _dot(a, b, prec=None):
    return jnp.dot(a, b, preferred_element_type=F32, precision=prec)


def _split2(x):
    hi = x.astype(BF16)
    lo = (x - hi.astype(F32)).astype(BF16)
    return hi, lo


def _bdot(a, b, nt=False):
    dn = (((1,), (1 if nt else 0,)), ((), ()))
    return lax.dot_general(a.astype(BF16), b.astype(BF16), dn, preferred_element_type=F32)


def _x3dot(fh, fl, x):
    xh, xl = _split2(x)
    return _dot(fh, xh) + (_dot(fh, xl) + _dot(fl, xh))


def _cumsum_dot(tri, x):
    hi = x.astype(BF16)
    r1 = x - hi.astype(F32)
    mid = r1.astype(BF16)
    lo = (r1 - mid.astype(F32)).astype(BF16)
    n = x.shape[1]
    g = _dot(tri, jnp.concatenate([hi, mid, lo], axis=1))
    return g[:, :n] + (g[:, n:2 * n] + g[:, 2 * n:])


def _mm_kernel(a_ref, b_ref, o_ref, *, exact):
    if exact:
        o_ref[...] = _x3dot(*_split2(a_ref[...]), b_ref[...])
    else:
        o_ref[...] = _dot(a_ref[...], b_ref[...])


def _mm(a, b, *, exact=False):
    M, K = a.shape
    N = b.shape[1]
    dt = F32 if exact else BF16
    a = a.astype(dt)
    b = b.astype(dt)
    Mp = -(-M // 8) * 8
    if Mp != M:
        a = jnp.pad(a, ((0, Mp - M), (0, 0)))
    tm = MM_TM // 2 if Mp % (MM_TM // 2) == 0 else Mp
    tn = MM_TN if N % MM_TN == 0 else N
    out = pl.pallas_call(
        functools.partial(_mm_kernel, exact=exact),
        grid=(Mp // tm, N // tn),
        in_specs=[pl.BlockSpec((tm, K), lambda i, j: (i, 0)),
                  pl.BlockSpec((K, tn), lambda i, j: (0, j))],
        out_specs=pl.BlockSpec((tm, tn), lambda i, j: (i, j)),
        out_shape=jax.ShapeDtypeStruct((Mp, N), F32),
        compiler_params=_cparams(("parallel", "parallel")),
        name="mm",
    )(a, b)
    return out[:M] if Mp != M else out


def _norm_mod(x, vec_ref):
    xn = x * lax.rsqrt(jnp.mean(x * x, axis=-1, keepdims=True) + NORM_EPS)
    return xn * vec_ref[0:1, :] + vec_ref[1:2, :]


def _mod_rows(g, shift, scale, gate, B):
    D = g.shape[-1]
    rows = [jnp.broadcast_to(g * (1.0 + scale[:, 0]), (B, D)), jnp.broadcast_to(shift[:, 0], (B, D)),
            jnp.broadcast_to(gate[:, 0], (B, D))]
    return jnp.stack(rows + [jnp.zeros((B, D), F32)] * 5, axis=1)


def _ffn_kernel(x_ref, vec_ref, wg_ref, wu_ref, w2_ref, o_ref, xn_scr, acc_ref, *, n_f):
    f = pl.program_id(2)

    @pl.when(f == 0)
    def _first():
        xn_scr[...] = _norm_mod(x_ref[...], vec_ref).astype(BF16)
        acc_ref[...] = jnp.zeros_like(acc_ref)

    x = xn_scr[...]
    gate = _dot(x, wg_ref[...])
    up = _dot(x, wu_ref[...])
    h = (gate * jax.nn.sigmoid(gate) * up).astype(BF16)
    acc_ref[...] += _dot(h, w2_ref[...])

    @pl.when(f == n_f - 1)
    def _store():
        o_ref[...] = x_ref[...] + vec_ref[2:3, :] * acc_ref[...]


def _ffn(x, vecs, w13, w2):
    B, L, D = x.shape
    F = w13.shape[1] // 2
    tm = min(MM_TM, L)
    tf = FFN_TF if F % FFN_TF == 0 else FFN_TF // 2
    n_f = F // tf
    tok = pl.BlockSpec((None, tm, D), lambda b, i, f: (b, i, 0))
    return pl.pallas_call(
        functools.partial(_ffn_kernel, n_f=n_f),
        grid=(B, L // tm, n_f),
        in_specs=[tok,
                  pl.BlockSpec((None, 8, D), lambda b, i, f: (b, 0, 0)),
                  pl.BlockSpec((D, tf), lambda b, i, f: (0, f)),
                  pl.BlockSpec((D, tf), lambda b, i, f: (0, f + n_f)),
                  pl.BlockSpec((tf, D), lambda b, i, f: (f, 0))],
        out_specs=tok,
        out_shape=jax.ShapeDtypeStruct((B, L, D), F32),
        scratch_shapes=[pltpu.VMEM((tm, D), BF16), pltpu.VMEM((tm, D), F32)],
        compiler_params=_cparams(("parallel", "parallel", "arbitrary")),
        name="ffn",
    )(x, vecs, w13, w13, w2)


def _norm_route_kernel(x_ref, vec_ref, wr_ref, xn_ref, lg_ref):
    xn = _norm_mod(x_ref[...], vec_ref)
    xn_ref[...] = xn.astype(BF16)
    lg_ref[...] = _dot(xn, wr_ref[...], HIGHEST)


def _norm_route(x, vecs, w_router):
    B, L, D = x.shape
    E = w_router.shape[1]
    tm = min(NORM_TM, L)
    tok = pl.BlockSpec((None, tm, D), lambda b, i: (b, i, 0))
    return pl.pallas_call(
        _norm_route_kernel,
        grid=(B, L // tm),
        in_specs=[tok, pl.BlockSpec((None, 8, D), lambda b, i: (b, 0, 0)), pl.BlockSpec((D, E), lambda b, i: (0, 0))],
        out_specs=[tok, pl.BlockSpec((None, tm, E), lambda b, i: (b, i, 0))],
        out_shape=[jax.ShapeDtypeStruct((B, L, D), BF16), jax.ShapeDtypeStruct((B, L, E), F32)],
        compiler_params=_cparams(("parallel", "parallel")),
        name="norm_route",
    )(x, vecs, w_router)


def _gated_add_kernel(s_ref, y_ref, vec_ref, *rest, final):
    o_ref = rest[-1]
    z = s_ref[...] + vec_ref[2:3, :] * y_ref[...]
    if final:
        z = z * lax.rsqrt(jnp.mean(z * z, axis=-1, keepdims=True) + NORM_EPS) * rest[0][...]
    o_ref[...] = z


def _gated_add(s, y, vecs, out_gain=None):
    B, L, D = s.shape
    tm = min(NORM_TM, L)
    tok = pl.BlockSpec((None, tm, D), lambda b, i: (b, i, 0))
    final = out_gain is not None
    specs = [tok, tok, pl.BlockSpec((None, 8, D), lambda b, i: (b, 0, 0))]
    args = [s, y, vecs]
    if final:
        specs.append(pl.BlockSpec((1, D), lambda b, i: (0, 0)))
        args.append(out_gain[None, :])
    return pl.pallas_call(
        functools.partial(_gated_add_kernel, final=final),
        grid=(B, L // tm), in_specs=specs, out_specs=tok,
        out_shape=jax.ShapeDtypeStruct((B, L, D), F32),
        compiler_params=_cparams(("parallel", "parallel")),
        name="gated_add",
    )(*args)


def _moe_kernel(cnt_ref, x_ref, rrow_ref, rcol_ref, gcol_ref, wg_ref, wu_ref, w2_ref, o_ref, xc_scr, y_scr,
                *, n_f, tm):
    i, e, f = pl.program_id(0), pl.program_id(1), pl.program_id(2)
    sub = MOE_SUB
    slab = min(MOE_SLAB, tm)
    nslab = tm // slab
    n_sub = (cnt_ref[i, e, nslab] + (sub - 1)) // sub

    def touches(s, j):
        return (cnt_ref[i, e, j] < (s + 1) * sub) & (cnt_ref[i, e, j + 1] > s * sub)

    @pl.when((e == 0) & (f == 0))
    def _zero():
        o_ref[...] = jnp.zeros_like(o_ref)

    @pl.when(f == 0)
    def _compact():
        def body(s, carry):
            rows = pl.ds(pl.multiple_of(s * sub, sub), sub)
            y_scr[rows, :] = jnp.zeros((sub, y_scr.shape[1]), F32)
            for j in range(nslab):
                cols = slice(j * slab, (j + 1) * slab)

                @pl.when(touches(s, j))
                def _add():
                    ridx = lax.broadcasted_iota(jnp.int32, (sub, slab), 0) + s * sub
                    onehot = jnp.where(rrow_ref[:, cols] == ridx, 1.0, 0.0).astype(BF16)
                    y_scr[rows, :] += _dot(onehot, x_ref[cols, :])
            xc_scr[rows, :] = y_scr[rows, :].astype(BF16)
            return carry
        lax.fori_loop(0, n_sub, body, 0)

    def expert(s, first):
        rows = pl.ds(pl.multiple_of(s * sub, sub), sub)
        xs = xc_scr[rows, :]
        gate = _dot(xs, wg_ref[...])
        up = _dot(xs, wu_ref[...])
        h = (gate * jax.nn.sigmoid(gate) * up).astype(BF16)
        part = _dot(h, w2_ref[...])
        y_scr[rows, :] = part if first else y_scr[rows, :] + part

    def experts(first):
        def pair(p, carry):
            expert(2 * p, first)
            expert(2 * p + 1, first)
            return carry
        lax.fori_loop(0, n_sub // 2, pair, 0)

        @pl.when(n_sub % 2 == 1)
        def _tail():
            expert(n_sub - 1, first)

    @pl.when(f == 0)
    def _first():
        experts(True)

    @pl.when(f > 0)
    def _rest():
        experts(False)

    @pl.when(f == n_f - 1)
    def _scatter():
        def body(s, carry):
            y = y_scr[pl.ds(pl.multiple_of(s * sub, sub), sub), :].astype(BF16)
            for j in range(nslab):
                rows = slice(j * slab, (j + 1) * slab)

                @pl.when(touches(s, j))
                def _add():
                    cidx = lax.broadcasted_iota(jnp.int32, (slab, sub), 1) + s * sub
                    onehot_t = jnp.where(rcol_ref[rows, :] == cidx, 1.0, 0.0).astype(BF16)
                    o_ref[rows, :] += gcol_ref[rows, :] * _dot(onehot_t, y)
            return carry
        lax.fori_loop(0, n_sub, body, 0)


def _moe(x, logits, w13, w2):
    T, D = x.shape
    E, _, F2 = w13.shape
    F = F2 // 2
    tm = min(MOE_TILE, T)
    nt = T // tm
    tf = FFN_TF
    n_f = F // tf
    top_val, top_idx = lax.top_k(logits, TOP_K)
    gates = jax.nn.softmax(top_val, axis=-1)
    onehot = jax.nn.one_hot(top_idx, E, dtype=F32)
    sel = jnp.sum(onehot, axis=1).astype(jnp.int32).reshape(nt, tm, E)
    gate_dense = jnp.sum(onehot * gates[..., None], axis=1).reshape(nt, tm, E)
    rank = jnp.where(sel > 0, jnp.cumsum(sel, axis=1) - sel, -1)
    rank = jnp.swapaxes(rank, 1, 2)
    slab = min(MOE_SLAB, tm)
    per_slab = jnp.sum(sel.reshape(nt, tm // slab, slab, E), axis=2)
    counts = jnp.concatenate([jnp.zeros((nt, 1, E), jnp.int32), jnp.cumsum(per_slab, axis=1)], axis=1)
    counts = jnp.swapaxes(counts, 1, 2)
    gcol = jnp.swapaxes(gate_dense, 1, 2)[..., None]
    tile = lambda shape, imap: pl.BlockSpec(shape, imap)
    return pl.pallas_call(
        functools.partial(_moe_kernel, n_f=n_f, tm=tm),
        grid_spec=pltpu.PrefetchScalarGridSpec(
            num_scalar_prefetch=1,
            grid=(nt, E, n_f),
            in_specs=[tile((tm, D), lambda i, e, f, c: (i, 0)),
                      tile((None, None, 1, tm), lambda i, e, f, c: (i, e, 0, 0)),
                      tile((None, None, tm, 1), lambda i, e, f, c: (i, e, 0, 0)),
                      tile((None, None, tm, 1), lambda i, e, f, c: (i, e, 0, 0)),
                      tile((None, D, tf), lambda i, e, f, c: (e, 0, f)),
                      tile((None, D, tf), lambda i, e, f, c: (e, 0, f + n_f)),
                      tile((None, tf, D), lambda i, e, f, c: (e, f, 0))],
            out_specs=tile((tm, D), lambda i, e, f, c: (i, 0)),
            scratch_shapes=[pltpu.VMEM((-(-tm // MOE_SUB) * MOE_SUB, D), BF16),
                            pltpu.VMEM((-(-tm // MOE_SUB) * MOE_SUB, D), F32)]),
        out_shape=jax.ShapeDtypeStruct((T, D), F32),
        compiler_params=_cparams(("parallel", "arbitrary", "arbitrary")),
        name="moe",
    )(counts, x, rank[:, :, None, :], rank[..., None], gcol, w13, w13, w2)


def _rwkv_scan_kernel(r_ref, lw_ref, k_ref, v_ref, a_ref, b_ref, h0_ref, y_ref, hT_ref, H_scr,
                      *, reverse, nblk, nchunk, npl):
    C = RW_CHUNK
    half = LANES // 2
    i = pl.program_id(2)

    @pl.when(i == 0)
    def _init():
        H_scr[...] = h0_ref[...]

    t_idx = lax.broadcasted_iota(jnp.int32, (C, LANES), 0)
    s_idx = lax.broadcasted_iota(jnp.int32, (C, LANES), 1) & (half - 1)
    tt = lax.broadcasted_iota(jnp.int32, (C, C), 0)
    ss = lax.broadcasted_iota(jnp.int32, (C, C), 1)
    if reverse:
        strict, incl, tri = s_idx > t_idx, s_idx >= t_idx, (ss >= tt).astype(BF16)
    else:
        strict, incl, tri = s_idx < t_idx, s_idx <= t_idx, (ss <= tt).astype(BF16)
    eye_lp = (s_idx == t_idx).astype(F32)
    lane = lax.broadcasted_iota(jnp.int32, (1, LANES), 1)
    m0 = (lane < half).astype(F32)
    m1 = 1.0 - m0
    rr = lax.broadcasted_iota(jnp.int32, (LANES, LANES), 0)
    cc = lax.broadcasted_iota(jnp.int32, (LANES, LANES), 1)
    mask_bd = ((rr < half) == (cc < half)).astype(F32)

    def bd(x):
        return jnp.concatenate([x * m0, x * m1], axis=0)

    order = list(range(nchunk - 1, -1, -1) if reverse else range(nchunk))
    units = [(slice(c * C, (c + 1) * C), slice(q * LANES, (q + 1) * LANES)) for c in order for q in range(npl)]
    ld = lambda ref: [ref[sl, ln] for sl, ln in units]
    r, lw, k, v, a, b = ld(r_ref), ld(lw_ref), ld(k_ref), ld(v_ref), ld(a_ref), ld(b_ref)
    G = [_cumsum_dot(tri, x) for x in lw]
    eG = [jnp.exp(g) for g in G]
    eGn = [jnp.exp(-g) for g in G]
    rt = [x * e for x, e in zip(r, eG)]
    at = [x * jnp.exp(g - l) for x, g, l in zip(a, G, lw)]
    kt = [x * e for x, e in zip(k, eGn)]
    bt = [x * e for x, e in zip(b, eGn)]
    M = [_bdot(jnp.concatenate([x, y], axis=0), jnp.concatenate([bd(z), bd(w)], axis=0), nt=True)
         for x, y, z, w in zip(at, rt, bt, kt)]
    Nm = [jnp.where(strict, m[:C, :LANES], 0.0) for m in M]
    Aak = [jnp.where(strict, m[:C, LANES:], 0.0) for m in M]
    Arb = [jnp.where(incl, m[C:, :LANES], 0.0) for m in M]
    Ark = [jnp.where(incl, m[C:, LANES:], 0.0) for m in M]
    T = None
    for lvl in range(1, int(math.log2(C)) + 1):
        same = (t_idx >> lvl) == (s_idx >> lvl)
        t_hi = ((t_idx >> (lvl - 1)) & 1) == 1
        s_hi = ((s_idx >> (lvl - 1)) & 1) == 1
        off = same & (s_hi & ~t_hi if reverse else t_hi & ~s_hi)
        Noff = [jnp.where(off, n, 0.0) for n in Nm]
        if T is None:
            T = [eye_lp + n for n in Noff]
        else:
            DN = [_bdot(t, bd(n)) for t, n in zip(T, Noff)]
            T = [t + _bdot(dn, bd(t)) for t, dn in zip(T, DN)]
    bdv = [bd(x) for x in v]
    X0 = [_bdot(x, y) for x, y in zip(Aak, bdv)]
    WU = [_bdot(t, jnp.concatenate([bd(x), bd(y)], axis=1)) for t, x, y in zip(T, at, X0)]
    W = [x[:, :LANES] for x in WU]
    U0 = [x[:, LANES:] for x in WU]
    RY = [_bdot(x, jnp.concatenate([bd(y), bd(z)], axis=1)) for x, y, z in zip(Arb, W, U0)]
    Rh = [x + y[:, :LANES] for x, y in zip(rt, RY)]
    Y0 = [y[:, LANES:] + _bdot(x, z) for y, x, z in zip(RY, Ark, bdv)]
    dPhi = [mask_bd * _bdot(x.T, y) for x, y in zip(W, bt)]
    Psi = [mask_bd * _bdot(jnp.concatenate([x, y], axis=0).T, jnp.concatenate([z, w], axis=0))
           for x, y, z, w in zip(U0, v, bt, kt)]
    e_end = [jnp.exp(g[0:1, :] if reverse else g[C - 1:C, :]) for g in G]

    Hs = [H_scr[q] for q in range(npl)]
    for u, (sl, ln) in enumerate(units):
        q = u % npl
        H = Hs[q]
        y_ref[sl, ln] = _bdot(Rh[u], H, nt=True) + Y0[u]
        Hs[q] = (H + _bdot(H, dPhi[u]) + Psi[u]) * e_end[u]
    for q in range(npl):
        H_scr[q] = Hs[q]

    @pl.when(i == nblk - 1)
    def _fin():
        for q in range(npl):
            hT_ref[q] = Hs[q]


def _pair_states(S):
    B, H, N, _ = S.shape
    S5 = S.reshape(B, H // 2, 2, N, N)
    eye = jnp.eye(2, dtype=S.dtype)
    return jnp.einsum('bpivk,ij->bpivjk', S5, eye).reshape(B, H // 2, 2 * N, 2 * N)


def _unpair_states(Sp):
    B, P, N2, _ = Sp.shape
    N = N2 // 2
    S6 = Sp.reshape(B, P, 2, N, 2, N)
    return jnp.stack([S6[:, :, 0, :, 0, :], S6[:, :, 1, :, 1, :]], axis=2).reshape(B, 2 * P, N, N)


def _rwkv_scan(r, lw, k, v, a, b, state0, reverse):
    B, L, D = r.shape
    bt = min(RW_BLOCK, L)
    nblk = L // bt
    npl = RW_PAIRS_PER_STEP
    npair = D // LANES
    blk = (lambda i: nblk - 1 - i) if reverse else (lambda i: i)
    tok = pl.BlockSpec((None, bt, npl * LANES), lambda bb, p, i: (bb, blk(i), p))
    st = pl.BlockSpec((None, npl, LANES, LANES), lambda bb, p, i: (bb, p, 0, 0))
    y, hT = pl.pallas_call(
        functools.partial(_rwkv_scan_kernel, reverse=reverse, nblk=nblk, nchunk=bt // RW_CHUNK, npl=npl),
        grid=(B, npair // npl, nblk),
        in_specs=[tok] * 6 + [st],
        out_specs=[tok, st],
        out_shape=[jax.ShapeDtypeStruct((B, L, D), F32),
                   jax.ShapeDtypeStruct((B, npair, LANES, LANES), F32)],
        scratch_shapes=[pltpu.VMEM((npl, LANES, LANES), F32)],
        compiler_params=_cparams(("parallel", "parallel", "arbitrary")),
        name="rwkv_scan_rev" if reverse else "rwkv_scan_fwd",
    )(r, lw, k, v, a, b, _pair_states(state0))
    return y, _unpair_states(hT)


def _gla_kernel(q_ref, f_ref, v_ref, lb_ref, h0_ref, o_ref, hT_ref, H_scr, *, reverse, nblk, npl):
    C = HG_CHUNK
    sh = int(math.log2(C))
    bt = q_ref.shape[0]
    nchunk = bt // C
    i = pl.program_id(2)

    @pl.when(i == 0)
    def _init():
        H_scr[...] = h0_ref[...]

    tt = lax.broadcasted_iota(jnp.int32, (bt, bt), 0)
    ss = lax.broadcasted_iota(jnp.int32, (bt, bt), 1)
    same = (tt >> sh) == (ss >> sh)
    mid = ((tt >> sh) << sh) + (C // 2 if reverse else C // 2 - 1)
    if reverse:
        incl, upto_mid = same & (ss >= tt), same & (ss >= mid)
    else:
        incl, upto_mid = same & (ss <= tt), same & (ss <= mid)
    one = lambda m: jnp.where(m, 1.0, 0.0)
    sums = jnp.concatenate([one(incl), one(incl) - one(upto_mid), one(same) - one(incl)], axis=0).astype(BF16)

    order = list(range(nchunk - 1, -1, -1) if reverse else range(nchunk))
    heads = [slice(p * LANES, (p + 1) * LANES) for p in range(npl)]
    q = [jax.nn.silu(q_ref[:, ln]) for ln in heads]
    fg = [lb_ref[:, ln] + (1.0 - lb_ref[:, ln]) * jax.nn.sigmoid(f_ref[:, ln]) for ln in heads]
    v = [v_ref[:, ln] for ln in heads]
    k = [1.0 - f for f in fg]
    Gs = [_cumsum_dot(sums, jnp.log(f)) for f in fg]
    Gabs = [g[:bt] for g in Gs]
    Grel = [g[bt:2 * bt] for g in Gs]
    Gend = [g[2 * bt:] for g in Gs]
    scores = [jnp.where(incl, _bdot(x * jnp.exp(g), y * jnp.exp(-g), nt=True), 0.0)
              for x, y, g in zip(q, k, Grel)]
    o_intra = [_bdot(s, x) for s, x in zip(scores, v)]
    qa = [x * jnp.exp(g) for x, g in zip(q, Gabs)]
    kend = [x * jnp.exp(g) for x, g in zip(k, Gend)]
    rows = [slice(c * C, (c + 1) * C) for c in range(nchunk)]
    KV = [[_bdot(x[r].T, y[r]) for r in rows] for x, y in zip(v, kend)]
    dec = [[jnp.exp(ga[r][0:1, :] + ge[r][0:1, :]) for r in rows] for ga, ge in zip(Gabs, Gend)]

    Hs = [H_scr[p] for p in range(npl)]
    for c in order:
        for p in range(npl):
            o_ref[rows[c], heads[p]] = o_intra[p][rows[c]] + _bdot(qa[p][rows[c]], Hs[p], nt=True)
            Hs[p] = Hs[p] * dec[p][c] + KV[p][c]
    for p in range(npl):
        H_scr[p] = Hs[p]

    @pl.when(i == nblk - 1)
    def _fin():
        for p in range(npl):
            hT_ref[p] = Hs[p]


def _gla_scan(proj, lb, state0, d):
    B, L, D5 = proj.shape
    D = D5 // 5
    reverse = d == 1
    bt = GLA_BLOCK
    nblk = L // bt
    npl = GLA_HEADS_PER_STEP
    nh = D // LANES
    ncb = nh // npl
    blk = (lambda i: nblk - 1 - i) if reverse else (lambda i: i)
    col = lambda off: pl.BlockSpec((None, bt, npl * LANES), lambda bb, p, i: (bb, blk(i), off * ncb + p))
    st = pl.BlockSpec((None, npl, LANES, LANES), lambda bb, p, i: (bb, p, 0, 0))
    o, hT = pl.pallas_call(
        functools.partial(_gla_kernel, reverse=reverse, nblk=nblk, npl=npl),
        grid=(B, ncb, nblk),
        in_specs=[col(0), col(1 + d), col(3), pl.BlockSpec((1, npl * LANES), lambda bb, p, i: (0, p)), st],
        out_specs=[col(0), st],
        out_shape=[jax.ShapeDtypeStruct((B, L, D), F32),
                   jax.ShapeDtypeStruct((B, nh, LANES, LANES), F32)],
        scratch_shapes=[pltpu.VMEM((npl, LANES, LANES), F32)],
        compiler_params=_cparams(("parallel", "parallel", "arbitrary")),
        name="gla_scan_rev" if reverse else "gla_scan_fwd",
    )(proj, proj, proj, lb[d][None, :], jnp.swapaxes(state0, 2, 3))
    return o, jnp.swapaxes(hT, 2, 3)


def _leftmul_kernel(fh_ref, fl_ref, x_ref, o_ref, xs_scr, *, nj):
    for j in range(nj):
        xs_scr[...] = x_ref[:, j, :]
        o_ref[:, j, :] = _x3dot(fh_ref[...], fl_ref[...], xs_scr[...])


def _leftmul_gate_kernel(fh_ref, fl_ref, x_ref, u_ref, s_ref, g_ref, o_ref, xs_scr, *, nj):
    for j in range(nj):
        xs_scr[...] = x_ref[:, j, :]
        y = _x3dot(fh_ref[...], fl_ref[...], xs_scr[...])
        o_ref[:, j, :] = (y + u_ref[:, j, :] * s_ref[...]) * g_ref[:, j, :]


def _leftmul(f, x, xoff=0, epilogue=None, D=D_MODEL):
    B, K, J, _ = x.shape
    M = f.shape[0]
    tj = 8 if J % 8 == 0 else J
    td = DFT_TD
    fh, fl = _split2(f)
    fs = pl.BlockSpec((M, K), lambda b, j, d: (0, 0))
    col = lambda rows, off: pl.BlockSpec((None, rows, tj, td), lambda b, j, d: (b, 0, j, off + d))
    if epilogue is None:
        kern, specs, args = _leftmul_kernel, [fs, fs, col(K, xoff)], (fh, fl, x)
    else:
        u, uoff, skip, gate, goff = epilogue
        kern = _leftmul_gate_kernel
        specs = [fs, fs, col(K, xoff), col(M, uoff), pl.BlockSpec((1, td), lambda b, j, d: (0, d)), col(M, goff)]
        args = (fh, fl, x, u, skip, gate)
    return pl.pallas_call(
        functools.partial(kern, nj=tj), grid=(B, J // tj, D // td), in_specs=specs, out_specs=col(M, 0),
        out_shape=jax.ShapeDtypeStruct((B, M, J, D), F32),
        scratch_shapes=[pltpu.VMEM((K, td), F32)],
        compiler_params=_cparams(("parallel", "parallel", "parallel")),
        name="dft_outer",
    )(*args)


def _spectral_fwd_kernel(fh_ref, fl_ref, y_ref, o_ref):
    o_ref[...] = _x3dot(fh_ref[...], fl_ref[...], y_ref[...])


def _spectral_conv_kernel(fh_ref, fl_ref, fih_ref, fil_ref, hf_ref, hb_ref, y_ref, o_ref):
    n2 = hf_ref.shape[0] // 2
    z = _x3dot(fh_ref[...], fl_ref[...], y_ref[...])
    zr, zi = z[:n2], z[n2:]
    kf = _combine_filter_spectrum(hf_ref[...], hb_ref[...])
    kr, ki = kf[:n2], kf[n2:]
    p = jnp.concatenate([zr * kr - zi * ki, zr * ki + zi * kr], axis=0)
    o_ref[...] = _x3dot(fih_ref[...], fil_ref[...], p)


def _spectral_mid(y, f_fwd, f_inv=None, spec=None, order=0):
    B, K1, R, D = y.shape
    ms = pl.BlockSpec((None, R, R), lambda k1, b: (k1, 0, 0))
    ys = pl.BlockSpec((None, None, R, D), lambda k1, b: (b, k1, 0, 0))
    if spec is None:
        kern, specs, args = _spectral_fwd_kernel, [ms, ms, ys], (*_split2(f_fwd), y)
    else:
        kern = _spectral_conv_kernel
        grp = lambda g: pl.BlockSpec((None, R, D), lambda k1, b: (k1, 0, g))
        specs = [ms, ms, ms, ms, grp(2 * order), grp(2 * order + 1), ys]
        args = (*_split2(f_fwd), *_split2(f_inv), spec, spec, y)
    return pl.pallas_call(
        kern, grid=(K1, B), in_specs=specs, out_specs=ys,
        out_shape=jax.ShapeDtypeStruct((B, K1, R, D), F32),
        compiler_params=_cparams(("parallel", "parallel")),
        name="dft_inner",
    )(*args)


def _dft_tables(L):
    n2 = HY_N2 if L % HY_N2 == 0 and L > HY_N2 * 2 else 1
    N = 2 * L
    n1 = N // n2
    k1h = n1 // 2
    ph = (jnp.arange(k1h, dtype=jnp.int32)[:, None] * 2 + 1) * jnp.arange(k1h, dtype=jnp.int32)[None, :]
    th = (ph % (2 * n1)).astype(F32) * (math.pi / n1)
    f_out = jnp.stack([jnp.cos(th), -jnp.sin(th)], axis=1).reshape(2 * k1h, k1h)
    f_out_inv = (2.0 / N) * f_out.T
    if n2 == 1:
        return n2, f_out, f_out_inv, None, None
    kk = jnp.arange(k1h, dtype=jnp.int32)[:, None, None] + n1 * jnp.arange(n2, dtype=jnp.int32)[None, :, None]
    ph = ((2 * kk + 1) * jnp.arange(n2, dtype=jnp.int32)[None, None, :]) % (2 * N)
    phi = ph.astype(F32) * (math.pi / N)
    c, s = jnp.cos(phi), jnp.sin(phi)
    f_in = jnp.concatenate([jnp.concatenate([c, s], axis=2), jnp.concatenate([-s, c], axis=2)], axis=1)
    ct, st = jnp.swapaxes(c, 1, 2), jnp.swapaxes(s, 1, 2)
    f_in_inv = jnp.concatenate([jnp.concatenate([ct, -st], axis=2), jnp.concatenate([st, ct], axis=2)], axis=1)
    return n2, f_out, f_out_inv, f_in, f_in_inv


def _long_conv_gated(u, ucol, spec, order, skip, gate, gcol, tables):
    n2, f_out, f_out_inv, f_in, f_in_inv = tables
    B, L, _ = u.shape
    D = D_MODEL
    k1h = L // n2
    per = D // DFT_TD
    u4 = u.reshape(B, k1h, n2, u.shape[-1])
    y = _leftmul(f_out, u4, xoff=ucol * per).reshape(B, k1h, 2 * n2, D)
    if f_in is None:
        cols = lambda g: spec[..., g * D:(g + 1) * D]
        kf = _combine_filter_spectrum(cols(2 * order), cols(2 * order + 1))
        yr, yi = y[:, :, 0], y[:, :, 1]
        kr, ki = kf[:, 0], kf[:, 1]
        q = jnp.stack([yr * kr - yi * ki, yr * ki + yi * kr], axis=2)
    else:
        q = _spectral_mid(y, f_in, f_in_inv, spec, order)
    out = _leftmul(f_out_inv, q.reshape(B, 2 * k1h, n2, D),
                   epilogue=(u4, ucol * per, skip[None, :], gate.reshape(B, k1h, n2, gate.shape[-1]), gcol * per))
    return out.reshape(B, L, D)


def _hyena_filter_spectra(L, w1, b1, w2, b2, w3, b3, w4, freq, tables):
    pos = jnp.arange(L, dtype=F32)
    t = (pos / max(L - 1, 1))[:, None]
    ang = (2.0 * math.pi / L) * pos[:, None] * jnp.linspace(1e-4, HY_BANDS - 1, HY_BANDS, dtype=F32)[None, :]
    z = jnp.concatenate([t, jnp.cos(ang), -jnp.sin(ang)], axis=-1)
    z = jnp.pad(z, ((0, 0), (0, 40 - HY_EMB)))
    w1p = jnp.pad(w1, ((0, 40 - HY_EMB), (0, 0)))
    hid = jnp.sin(freq[0] * (_mm(z, w1p, exact=True) + b1))
    hid = jnp.sin(freq[1] * (_mm(hid, w2, exact=True) + b2))
    hid = jnp.sin(freq[2] * (_mm(hid, w3, exact=True) + b3))
    filt = _mm(hid, w4, exact=True)
    deltas = jnp.linspace(math.log(HY_TARGET) / HY_FAST_DECAY, math.log(HY_TARGET) / HY_SLOW_DECAY,
                          D_MODEL, dtype=F32)
    window = jnp.exp(-t * jnp.abs(deltas))
    lag0 = (jnp.arange(L) > 0).astype(F32)[:, None]
    scale = jnp.concatenate([window, window * lag0] * HY_ORDER, axis=1)
    filt = filt * scale
    n2 = tables[0]
    k1h = L // n2
    C = HY_ORDER * 2 * D_MODEL
    y = _leftmul(tables[1], filt.reshape(1, k1h, n2, C), D=C).reshape(1, k1h, 2 * n2, C)
    return (y if tables[3] is None else _spectral_mid(y, tables[3]))[0]


def _combine_filter_spectrum(hf, hb):
    n2 = hf.shape[-2] // 2
    return jnp.concatenate([hf[..., :n2, :] + hb[..., :n2, :], hf[..., n2:, :] - hb[..., n2:, :]], axis=-2)


def _hyena_stream(resid, vecs, fprm, w_in, b_in, conv_w, conv_b, skip, w_o, b_o):
    B, L, D = resid.shape
    tables = _dft_tables(L)
    spec = _hyena_filter_spectra(L, *fprm, tables)
    proj = _proj_conv3(resid, vecs, w_in, b_in, conv_w, conv_b)
    z = _long_conv_gated(proj, 2, spec, 0, skip[0], proj, 0, tables)
    y = _long_conv_gated(z, 0, spec, 1, skip[1], proj, 1, tables)
    return _mm_fused(y, w_o, vecs=vecs, bias=b_o, resid=resid)


def _headsum(x, ones):
    hi, lo = _split2(x)
    cols = []
    for j in range(x.shape[1] // LANES):
        ln = slice(j * LANES, (j + 1) * LANES)
        cols.append(_dot(hi[:, ln], ones) + _dot(lo[:, ln], ones))
    return jnp.concatenate(cols, axis=1)


def _token_shift(h, h_above, h_below, grid_w):
    tm, D = h.shape
    row = lax.broadcasted_iota(jnp.int32, (tm, D), 0)
    lane = lax.broadcasted_iota(jnp.int32, (tm, D), 1)
    before = pltpu.roll(h, 1, axis=0)
    after = pltpu.roll(h, tm - 1, axis=0)
    if grid_w is None:
        return jnp.where(lane < D // 2, jnp.where(row == 0, 0.0, before), jnp.where(row == tm - 1, 0.0, after))
    col = row & (grid_w - 1)
    left = jnp.where(col == 0, 0.0, before)
    right = jnp.where(col == grid_w - 1, 0.0, after)
    up = jnp.concatenate([h_above, h[:tm - grid_w]], axis=0)
    down = jnp.concatenate([h[grid_w:], h_below], axis=0)
    q = D // 4
    return jnp.where(lane < q, left, jnp.where(lane < 2 * q, right, jnp.where(lane < 3 * q, up, down)))


def _rwkv_pre_kernel(*refs, vres, grid_w, nblk):
    it = iter(refs)
    x_ref = next(it)
    xa_ref, xb_ref = (next(it), next(it)) if grid_w else (None, None)
    mod_ref = next(it)
    vf_ref = next(it) if vres else None
    vec_ref, wr_ref, wk_ref, wv_ref, w1_ref, w2_ref, a1_ref, a2_ref, g1_ref, g2_ref = (next(it) for _ in range(10))
    v1_ref, v2_ref = (next(it), next(it)) if vres else (None, None)
    ones_ref = next(it)
    r_o, v_o, nkk_o, lw0_o, lw1_o, kd0_o, kd1_o, b0_o, b1_o, bonus_o, g_o = it
    vec = lambda i: vec_ref[i:i + 1, :]
    i = pl.program_id(1)
    h = _norm_mod(x_ref[...], mod_ref)
    if grid_w:
        h_above = jnp.where(i > 0, _norm_mod(xa_ref[...], mod_ref), 0.0)
        h_below = jnp.where(i < nblk - 1, _norm_mod(xb_ref[...], mod_ref), 0.0)
    else:
        h_above = h_below = None
    xx = _token_shift(h, h_above, h_below, grid_w) - h
    xr, xw, xk, xv, xa, xg = ((h + xx * vec(j)).astype(BF16) for j in range(6))
    r = _dot(xr, wr_ref[...])
    k = _dot(xk, wk_ref[...])
    v = _dot(xv, wv_ref[...])
    if vres:
        lora = _dot(_dot(xv, v1_ref[...]).astype(BF16), v2_ref[...])
        v = v + (vf_ref[...] - v) * jax.nn.sigmoid(vec(13) + lora)
    ones = ones_ref[...]
    kk = k * vec(10)
    kk = kk / jnp.maximum(jnp.sqrt(_headsum(kk * kk, ones)), 1e-12)
    r_o[...] = r
    v_o[...] = v
    nkk_o[...] = -kk
    bonus = jnp.zeros_like(r)
    for d, (lw_o, kd_o, b_o) in enumerate(((lw0_o, kd0_o, b0_o), (lw1_o, kd1_o, b1_o))):
        wl = vec(6 + d) + _dot(jnp.tanh(_dot(xw, w1_ref[d])).astype(BF16), w2_ref[d])
        lw_o[...] = -jax.nn.sigmoid(wl) * math.exp(-0.5)
        a = jax.nn.sigmoid(vec(8 + d) + _dot(_dot(xa, a1_ref[d]).astype(BF16), a2_ref[d]))
        kd = k * (1.0 + (a - 1.0) * vec(11))
        kd_o[...] = kd
        b_o[...] = kk * a
        bonus = bonus + _headsum(r * kd * vec(12), ones) * v
    bonus_o[...] = bonus
    g_o[...] = _dot(jax.nn.sigmoid(_dot(xg, g1_ref[...])).astype(BF16), g2_ref[...])


def _rwkv_post_kernel(y0_ref, y1_ref, bonus_ref, g_ref, res_ref, mod_ref, ln_ref, wo_ref, ones_ref, o_ref):
    ones = ones_ref[...]
    y = y0_ref[...] + y1_ref[...]
    yc = y - _headsum(y, ones) * (1.0 / RW_HEAD)
    var = _headsum(yc * yc, ones) * (1.0 / RW_HEAD)
    yn = yc * lax.rsqrt(var + RW_LN_EPS) * ln_ref[0:1, :] + ln_ref[1:2, :] + bonus_ref[...]
    out = _dot((yn * g_ref[...]).astype(BF16), wo_ref[...])
    o_ref[...] = res_ref[...] + mod_ref[2:3, :] * out


def _full(a):
    nd = a.ndim
    return pl.BlockSpec(a.shape, lambda b, i: (0,) * nd)


def _rwkv7_stream(resid, mods, grid_w, state0, v_first, mu, w_rkv, w_o, w0, w1, w2, a0, a1, a2,
                  g1, g2, k_k, k_a, r_k, ln_w, ln_b, vres, need_out=True):
    B, L, D = resid.shape
    tm = min(RW_TM, L)
    nblk = L // tm
    assert grid_w is not None or nblk == 1
    vres_on = vres is not None
    rows = [mu[j] for j in range(6)] + [w0[0], w0[1], a0[0], a0[1], k_k, k_a, r_k.reshape(D)]
    rows.append(vres[0] if vres_on else jnp.zeros((D,), F32))
    vecs = jnp.stack(rows + [jnp.zeros((D,), F32)] * (16 - len(rows)))
    ones = jnp.kron(jnp.eye(LANES // RW_HEAD, dtype=F32), jnp.ones((RW_HEAD, RW_HEAD), F32)).astype(BF16)
    bf = lambda t: t.astype(BF16)
    tok = pl.BlockSpec((None, tm, D), lambda b, i: (b, i, 0))
    mod_spec = pl.BlockSpec((None, 8, D), lambda b, i: (b, 0, 0))
    args, specs = [resid], [tok]
    if grid_w:
        per = tm // grid_w
        last_row = L // grid_w - 1
        args += [resid, resid]
        specs += [pl.BlockSpec((None, grid_w, D), lambda b, i: (b, jnp.maximum(i * per - 1, 0), 0)),
                  pl.BlockSpec((None, grid_w, D), lambda b, i: (b, jnp.minimum((i + 1) * per, last_row), 0))]
    args.append(mods)
    specs.append(mod_spec)
    if vres_on:
        args.append(v_first)
        specs.append(tok)
    consts = [vecs, bf(w_rkv[0]), bf(w_rkv[1]), bf(w_rkv[2]), bf(w1), bf(w2), bf(a1), bf(a2), bf(g1), bf(g2)]
    consts += [bf(vres[1]), bf(vres[2])] if vres_on else []
    consts.append(ones)
    outs = pl.pallas_call(
        functools.partial(_rwkv_pre_kernel, vres=vres_on, grid_w=grid_w, nblk=nblk),
        grid=(B, nblk),
        in_specs=specs + [_full(c) for c in consts],
        out_specs=[tok] * 11,
        out_shape=[jax.ShapeDtypeStruct((B, L, D), F32)] * 11,
        compiler_params=_cparams(("parallel", "parallel")),
        name="rwkv_pre",
    )(*args, *consts)
    r, v, nkk, lw0, lw1, kd0, kd1, b0, b1, bonus, g = outs
    y0, s0 = _rwkv_scan(r, lw0, kd0, v, nkk, b0, state0[0], reverse=False)
    y1, s1 = _rwkv_scan(r, lw1, kd1, v, nkk, b1, state0[1], reverse=True)
    states = jnp.stack([s0, s1])
    if not need_out:
        return None, states, v
    ln = jnp.stack([ln_w, ln_b] + [jnp.zeros((D,), F32)] * 6)
    wo = bf(w_o)
    tp = min(POST_TM, L)
    tok = pl.BlockSpec((None, tp, D), lambda b, i: (b, i, 0))
    new_resid = pl.pallas_call(
        _rwkv_post_kernel,
        grid=(B, L // tp),
        in_specs=[tok] * 5 + [mod_spec, _full(ln), _full(wo), _full(ones)],
        out_specs=tok,
        out_shape=jax.ShapeDtypeStruct((B, L, D), F32),
        compiler_params=_cparams(("parallel", "parallel")),
        name="rwkv_post",
    )(y0, y1, bonus, g, resid, mods, ln, wo, ones)
    return new_resid, states, v


def _mm_fused_kernel(*refs, norm, bias, resid):
    it = iter(refs)
    x_ref = next(it)
    vec_ref = next(it) if (norm or resid) else None
    w_ref = next(it)
    b_ref = next(it) if bias else None
    res_ref = next(it) if resid else None
    o_ref, xs_scr = next(it), next(it)

    @pl.when(pl.program_id(2) == 0)
    def _prep():
        x = x_ref[...]
        xs_scr[...] = (_norm_mod(x, vec_ref) if norm else x).astype(BF16)

    acc = _dot(xs_scr[...], w_ref[...])
    if bias:
        acc = acc + b_ref[...]
    o_ref[...] = res_ref[...] + vec_ref[2:3, :] * acc if resid else acc


def _mm_fused(x, w, vecs=None, norm=False, bias=None, resid=None):
    B, L, K = x.shape
    N = w.shape[1]
    tm = min(MM_TM, L)
    tn = MM_TN if (N % MM_TN == 0 and resid is None) else N
    use_vec = norm or resid is not None
    specs = [pl.BlockSpec((None, tm, K), lambda b, i, j: (b, i, 0))]
    args = [x]
    if use_vec:
        specs.append(pl.BlockSpec((None, 8, vecs.shape[-1]), lambda b, i, j: (b, 0, 0)))
        args.append(vecs)
    specs.append(pl.BlockSpec((K, tn), lambda b, i, j: (0, j)))
    args.append(w.astype(BF16))
    if bias is not None:
        specs.append(pl.BlockSpec((1, tn), lambda b, i, j: (0, j)))
        args.append(bias[None, :])
    out_spec = pl.BlockSpec((None, tm, tn), lambda b, i, j: (b, i, j))
    if resid is not None:
        assert N == K == vecs.shape[-1] and tn == N
        specs.append(out_spec)
        args.append(resid)
    return pl.pallas_call(
        functools.partial(_mm_fused_kernel, norm=norm, bias=bias is not None, resid=resid is not None),
        grid=(B, L // tm, N // tn),
        in_specs=specs, out_specs=out_spec,
        out_shape=jax.ShapeDtypeStruct((B, L, N), F32),
        scratch_shapes=[pltpu.VMEM((tm, K), BF16)],
        compiler_params=_cparams(("parallel", "parallel", "arbitrary")),
        name="mm_fused",
    )(*args)


def _proj_conv3_kernel(x_ref, xa_ref, xb_ref, vec_ref, w_ref, b_ref, cw_ref, o_ref, xs_scr, acc_scr, *, nblk):
    i = pl.program_id(1)
    tm = x_ref.shape[0]

    @pl.when(pl.program_id(2) == 0)
    def _prep():
        xs_scr[0:HALO, :] = _norm_mod(xa_ref[...], vec_ref).astype(BF16)
        xs_scr[HALO:HALO + tm, :] = _norm_mod(x_ref[...], vec_ref).astype(BF16)
        xs_scr[HALO + tm:, :] = _norm_mod(xb_ref[...], vec_ref).astype(BF16)

    acc_scr[...] = _dot(xs_scr[...], w_ref[...]) + b_ref[...]
    row = lax.broadcasted_iota(jnp.int32, o_ref.shape, 0)
    before = jnp.where((row == 0) & (i == 0), 0.0, acc_scr[HALO - 1:HALO - 1 + tm, :])
    after = jnp.where((row == tm - 1) & (i == nblk - 1), 0.0, acc_scr[HALO + 1:HALO + 1 + tm, :])
    o_ref[...] = (before * cw_ref[0:1, :] + acc_scr[HALO:HALO + tm, :] * cw_ref[1:2, :]
                  + after * cw_ref[2:3, :] + cw_ref[3:4, :])


def _proj_conv3(x, vecs, w, bias, taps, conv_bias):
    B, L, K = x.shape
    N = w.shape[1]
    tm = min(MM_TM, L)
    tn = MM_TN
    nblk = L // tm
    per = tm // HALO
    cw = jnp.concatenate([taps, conv_bias[None, :], jnp.zeros((4, N), F32)], axis=0)
    return pl.pallas_call(
        functools.partial(_proj_conv3_kernel, nblk=nblk),
        grid=(B, nblk, N // tn),
        in_specs=[pl.BlockSpec((None, tm, K), lambda b, i, j: (b, i, 0)),
                  pl.BlockSpec((None, HALO, K), lambda b, i, j: (b, jnp.maximum(i * per - 1, 0), 0)),
                  pl.BlockSpec((None, HALO, K), lambda b, i, j: (b, jnp.minimum((i + 1) * per, L // HALO - 1), 0)),
                  pl.BlockSpec((None, 8, K), lambda b, i, j: (b, 0, 0)),
                  pl.BlockSpec((K, tn), lambda b, i, j: (0, j)),
                  pl.BlockSpec((1, tn), lambda b, i, j: (0, j)),
                  pl.BlockSpec((8, tn), lambda b, i, j: (0, j))],
        out_specs=pl.BlockSpec((None, tm, tn), lambda b, i, j: (b, i, j)),
        out_shape=jax.ShapeDtypeStruct((B, L, N), F32),
        scratch_shapes=[pltpu.VMEM((tm + 2 * HALO, K), BF16), pltpu.VMEM((tm + 2 * HALO, tn), F32)],
        compiler_params=_cparams(("parallel", "parallel", "arbitrary")),
        name="proj_conv3",
    )(x, x, x, vecs, w.astype(BF16), bias[None, :], cw)


def _hgrn_post_kernel(o0_ref, o1_ref, g_ref, res_ref, vec_ref, gn_ref, wo_ref, ones_ref, o_ref):
    o = o0_ref[...] + o1_ref[...]
    ms = _headsum(o * o, ones_ref[...]) * (1.0 / HG_DK)
    on = o * lax.rsqrt(ms + NORM_EPS) * gn_ref[...]
    g = g_ref[...]
    z = (on * (g * jax.nn.sigmoid(g))).astype(BF16)
    o_ref[...] = res_ref[...] + vec_ref[2:3, :] * _dot(z, wo_ref[...])


def _hgrn2_stream(resid, vecs, state0, lb, w_in, gn, w_o, need_out=True):
    B, L, D = resid.shape
    proj = _mm_fused(resid, w_in, vecs=vecs, norm=True)
    o0, s0 = _gla_scan(proj, lb, state0[0], 0)
    o1, s1 = _gla_scan(proj, lb, state0[1], 1)
    states = jnp.stack([s0, s1])
    if not need_out:
        return None, states
    tm = min(POST_TM, L)
    tok = pl.BlockSpec((None, tm, D), lambda b, i: (b, i, 0))
    ones = jnp.ones((LANES, LANES), BF16)
    gn_row = jnp.tile(gn, D // gn.shape[0])[None, :]
    wo = w_o.astype(BF16)
    out = pl.pallas_call(
        _hgrn_post_kernel,
        grid=(B, L // tm),
        in_specs=[tok, tok, pl.BlockSpec((None, tm, D), lambda b, i: (b, i, 4)), tok,
                  pl.BlockSpec((None, 8, D), lambda b, i: (b, 0, 0)), _full(gn_row), _full(wo), _full(ones)],
        out_specs=tok,
        out_shape=jax.ShapeDtypeStruct((B, L, D), F32),
        compiler_params=_cparams(("parallel", "parallel")),
        name="hgrn_post",
    )(o0, o1, proj, resid, vecs, gn_row, wo, ones)
    return out, states


def kernel(x, c, ctx, c_ctx, norm_g, ada_w, ada_b, final_g,
           rw_mu, rw_wrkv, rw_wo, rw_w0, rw_w1, rw_w2, rw_a0, rw_a1, rw_a2,
           rw_v0, rw_v1, rw_v2, rw_g1, rw_g2, rw_kk, rw_ka, rw_rk, rw_lnw, rw_lnb,
           hy_win, hy_bin, hy_cw, hy_cb, hy_fw1, hy_fb1, hy_fw2, hy_fb2, hy_fw3, hy_fb3,
           hy_fw4, hy_freq, hy_skip, hy_wo, hy_bo,
           hg_win, hg_lb, hg_gn, hg_wo,
           ffn_w13, ffn_w2, moe_router, moe_w13, moe_w2):
    B = x.shape[0]
    depth = norm_g.shape[0]
    D = D_MODEL
    lat, cx = x, ctx
    v_first = None
    lbc = jnp.cumsum(jax.nn.softmax(hg_lb, axis=0), axis=0)
    lower = lbc - lbc[:1]
    cond = jnp.concatenate([jax.nn.silu(c), jax.nn.silu(c_ctx)[None, :]], axis=0)
    for i in range(depth):
        last = i == depth - 1
        if i % 2 == 0:
            mix_w13, mix_w2 = ffn_w13[i // 2].astype(BF16), ffn_w2[i // 2].astype(BF16)
        else:
            mix_w13, mix_w2 = moe_w13[i // 2].astype(BF16), moe_w2[i // 2].astype(BF16)
        mod = _mm(cond, ada_w[i]) + ada_b[i]
        mod_l = jnp.split(mod[:B, None, :], 6, axis=-1)
        mod_c = jnp.split(mod[B:, None, :], 6, axis=-1)
        kind, slot = i % N_MIXERS, i // N_MIXERS
        vec_l = _mod_rows(norm_g[i, 0], mod_l[0], mod_l[1], mod_l[2], B)
        vec_c = _mod_rows(norm_g[i, 0], mod_c[0], mod_c[1], mod_c[2], B)
        if kind == 0:
            vres = None if slot == 0 else (rw_v0[slot - 1], rw_v1[slot - 1], rw_v2[slot - 1])
            rw = (rw_mu[slot], rw_wrkv[slot], rw_wo[slot], rw_w0[slot], rw_w1[slot], rw_w2[slot],
                  rw_a0[slot], rw_a1[slot], rw_a2[slot], rw_g1[slot], rw_g2[slot], rw_kk[slot],
                  rw_ka[slot], rw_rk[slot], rw_lnw[slot], rw_lnb[slot], vres)
            zero = jnp.zeros((2, B, RW_H, RW_HEAD, RW_HEAD), F32)
            vf_c = None if v_first is None else v_first[0]
            vf_l = None if v_first is None else v_first[1]
            cx_new, s_ctx, v_c = _rwkv7_stream(cx, vec_c, None, zero, vf_c, *rw, need_out=not last)
            lat, _, v_l = _rwkv7_stream(lat, vec_l, GRID_W, s_ctx, vf_l, *rw)
            if not last:
                cx = cx_new
            if slot == 0:
                v_first = (v_c, v_l)
        elif kind == 1:
            fprm = (hy_fw1[slot], hy_fb1[slot], hy_fw2[slot], hy_fb2[slot], hy_fw3[slot],
                    hy_fb3[slot], hy_fw4[slot], hy_freq[slot])
            hprm = (hy_win[slot], hy_bin[slot], hy_cw[slot], hy_cb[slot], hy_skip[slot],
                    hy_wo[slot], hy_bo[slot])
            lat = _hyena_stream(lat, vec_l, fprm, *hprm)
            if not last:
                cx = _hyena_stream(cx, vec_c, fprm, *hprm)
        else:
            zero = jnp.zeros((2, B, HG_H, HG_DK, D // HG_H), F32)
            gprm = (lower[i], hg_win[slot], hg_gn[slot], hg_wo[slot])
            cx_new, s_ctx = _hgrn2_stream(cx, vec_c, zero, *gprm, need_out=not last)
            lat, _ = _hgrn2_stream(lat, vec_l, s_ctx, *gprm)
            if not last:
                cx = cx_new

        def channel_mix(s, mod, out_gain=None):
            vecs = _mod_rows(norm_g[i, 1], mod[3], mod[4], mod[5], B)
            if i % 2 == 0:
                return _ffn(s, vecs, mix_w13, mix_w2)
            xn, logits = _norm_route(s, vecs, moe_router[i // 2])
            out = _moe(xn.reshape(-1, D), logits.reshape(-1, N_EXPERTS), mix_w13, mix_w2)
            return _gated_add(s, out.reshape(s.shape), vecs, out_gain)

        assert depth % 2 == 0
        lat = channel_mix(lat, mod_l, final_g if last else None)
        if not last:
            cx = channel_mix(cx, mod_c)
    return lat
```

```python
import functools
import math

import jax
import jax.numpy as jnp
from jax import lax
from jax.experimental import pallas as pl
from jax.experimental.pallas import tpu as pltpu

F32 = jnp.float32
BF16 = jnp.bfloat16
HIGHEST = lax.Precision.HIGHEST

D_MODEL = 1024
GRID_W = 64
NORM_EPS = 1e-6
RW_HEAD = 64
RW_H = D_MODEL // RW_HEAD
RW_LN_EPS = 64e-5
HY_ORDER = 2
HY_EMB = 33
HY_BANDS = (HY_EMB - 1) // 2
HY_FAST_DECAY = 0.3
HY_SLOW_DECAY = 1.5
HY_TARGET = 1e-2
HG_DK = 128
HG_H = D_MODEL // HG_DK
HG_CHUNK = 32
N_EXPERTS = 8
TOP_K = 2
N_MIXERS = 3

LANES = 128
VMEM_LIMIT = 56 * 1024 * 1024

MM_TM, MM_TN = 1024, 512
FFN_TF = 512
NORM_TM = 1024
POST_TM = 512
HALO = 16
RW_TM = 256
RW_CHUNK = 64
RW_BLOCK = 512
RW_PAIRS_PER_STEP = 4
GLA_BLOCK = 256
GLA_HEADS_PER_STEP = 8
MOE_TILE = 2048
MOE_SUB = 256
MOE_SLAB = 512
HY_N2 = 128
DFT_TD = 512


def _cparams(sem):
    return pltpu.CompilerParams(dimension_semantics=sem, vmem_limit_bytes=VMEM_LIMIT)


def _dot(a, b, prec=None):
    return jnp.dot(a, b, preferred_element_type=F32, precision=prec)


def _split2(x):
    hi = x.astype(BF16)
    lo = (x - hi.astype(F32)).astype(BF16)
    return hi, lo


def _bdot(a, b, nt=False):
    dn = (((1,), (1 if nt else 0,)), ((), ()))
    return lax.dot_general(a.astype(BF16), b.astype(BF16), dn, preferred_element_type=F32)


def _x3dot(fh, fl, x):
    xh, xl = _split2(x)
    return _dot(fh, xh) + (_dot(fh, xl) + _dot(fl, xh))


def _cumsum_dot(tri, x):
    hi = x.astype(BF16)
    r1 = x - hi.astype(F32)
    mid = r1.astype(BF16)
    lo = (r1 - mid.astype(F32)).astype(BF16)
    n = x.shape[1]
    g = _dot(tri, jnp.concatenate([hi, mid, lo], axis=1))
    return g[:, :n] + (g[:, n:2 * n] + g[:, 2 * n:])


def _mm_kernel(a_ref, b_ref, o_ref, *, exact):
    if exact:
        o_ref[...] = _x3dot(*_split2(a_ref[...]), b_ref[...])
    else:
        o_ref[...] = _dot(a_ref[...], b_ref[...])


def _mm(a, b, *, exact=False):
    M, K = a.shape
    N = b.shape[1]
    dt = F32 if exact else BF16
    a = a.astype(dt)
    b = b.astype(dt)
    Mp = -(-M // 8) * 8
    if Mp != M:
        a = jnp.pad(a, ((0, Mp - M), (0, 0)))
    tm = MM_TM // 2 if Mp % (MM_TM // 2) == 0 else Mp
    tn = MM_TN if N % MM_TN == 0 else N
    out = pl.pallas_call(
        functools.partial(_mm_kernel, exact=exact),
        grid=(Mp // tm, N // tn),
        in_specs=[pl.BlockSpec((tm, K), lambda i, j: (i, 0)),
                  pl.BlockSpec((K, tn), lambda i, j: (0, j))],
        out_specs=pl.BlockSpec((tm, tn), lambda i, j: (i, j)),
        out_shape=jax.ShapeDtypeStruct((Mp, N), F32),
        compiler_params=_cparams(("parallel", "parallel")),
        name="mm",
    )(a, b)
    return out[:M] if Mp != M else out


def _norm_mod(x, vec_ref):
    xn = x * lax.rsqrt(jnp.mean(x * x, axis=-1, keepdims=True) + NORM_EPS)
    return xn * vec_ref[0:1, :] + vec_ref[1:2, :]


def _mod_rows(g, shift, scale, gate, B):
    D = g.shape[-1]
    rows = [jnp.broadcast_to(g * (1.0 + scale[:, 0]), (B, D)), jnp.broadcast_to(shift[:, 0], (B, D)),
            jnp.broadcast_to(gate[:, 0], (B, D))]
    return jnp.stack(rows + [jnp.zeros((B, D), F32)] * 5, axis=1)


def _ffn_kernel(x_ref, vec_ref, wg_ref, wu_ref, w2_ref, o_ref, xn_scr, acc_ref, *, n_f):
    f = pl.program_id(2)

    @pl.when(f == 0)
    def _first():
        xn_scr[...] = _norm_mod(x_ref[...], vec_ref).astype(BF16)
        acc_ref[...] = jnp.zeros_like(acc_ref)

    x = xn_scr[...]
    gate = _dot(x, wg_ref[...])
    up = _dot(x, wu_ref[...])
    h = (gate * jax.nn.sigmoid(gate) * up).astype(BF16)
    acc_ref[...] += _dot(h, w2_ref[...])

    @pl.when(f == n_f - 1)
    def _store():
        o_ref[...] = x_ref[...] + vec_ref[2:3, :] * acc_ref[...]


def _ffn(x, vecs, w13, w2):
    B, L, D = x.shape
    F = w13.shape[1] // 2
    tm = min(MM_TM, L)
    tf = FFN_TF if F % FFN_TF == 0 else FFN_TF // 2
    n_f = F // tf
    tok = pl.BlockSpec((None, tm, D), lambda b, i, f: (b, i, 0))
    return pl.pallas_call(
        functools.partial(_ffn_kernel, n_f=n_f),
        grid=(B, L // tm, n_f),
        in_specs=[tok,
                  pl.BlockSpec((None, 8, D), lambda b, i, f: (b, 0, 0)),
                  pl.BlockSpec((D, tf), lambda b, i, f: (0, f)),
                  pl.BlockSpec((D, tf), lambda b, i, f: (0, f + n_f)),
                  pl.BlockSpec((tf, D), lambda b, i, f: (f, 0))],
        out_specs=tok,
        out_shape=jax.ShapeDtypeStruct((B, L, D), F32),
        scratch_shapes=[pltpu.VMEM((tm, D), BF16), pltpu.VMEM((tm, D), F32)],
        compiler_params=_cparams(("parallel", "parallel", "arbitrary")),
        name="ffn",
    )(x, vecs, w13, w13, w2)


def _norm_route_kernel(x_ref, vec_ref, wr_ref, xn_ref, lg_ref):
    xn = _norm_mod(x_ref[...], vec_ref)
    xn_ref[...] = xn.astype(BF16)
    lg_ref[...] = _dot(xn, wr_ref[...], HIGHEST)


def _norm_route(x, vecs, w_router):
    B, L, D = x.shape
    E = w_router.shape[1]
    tm = min(NORM_TM, L)
    tok = pl.BlockSpec((None, tm, D), lambda b, i: (b, i, 0))
    return pl.pallas_call(
        _norm_route_kernel,
        grid=(B, L // tm),
        in_specs=[tok, pl.BlockSpec((None, 8, D), lambda b, i: (b, 0, 0)), pl.BlockSpec((D, E), lambda b, i: (0, 0))],
        out_specs=[tok, pl.BlockSpec((None, tm, E), lambda b, i: (b, i, 0))],
        out_shape=[jax.ShapeDtypeStruct((B, L, D), BF16), jax.ShapeDtypeStruct((B, L, E), F32)],
        compiler_params=_cparams(("parallel", "parallel")),
        name="norm_route",
    )(x, vecs, w_router)


def _gated_add_kernel(s_ref, y_ref, vec_ref, *rest, final):
    o_ref = rest[-1]
    z = s_ref[...] + vec_ref[2:3, :] * y_ref[...]
    if final:
        z = z * lax.rsqrt(jnp.mean(z * z, axis=-1, keepdims=True) + NORM_EPS) * rest[0][...]
    o_ref[...] = z


def _gated_add(s, y, vecs, out_gain=None):
    B, L, D = s.shape
    tm = min(NORM_TM, L)
    tok = pl.BlockSpec((None, tm, D), lambda b, i: (b, i, 0))
    final = out_gain is not None
    specs = [tok, tok, pl.BlockSpec((None, 8, D), lambda b, i: (b, 0, 0))]
    args = [s, y, vecs]
    if final:
        specs.append(pl.BlockSpec((1, D), lambda b, i: (0, 0)))
        args.append(out_gain[None, :])
    return pl.pallas_call(
        functools.partial(_gated_add_kernel, final=final),
        grid=(B, L // tm), in_specs=specs, out_specs=tok,
        out_shape=jax.ShapeDtypeStruct((B, L, D), F32),
        compiler_params=_cparams(("parallel", "parallel")),
        name="gated_add",
    )(*args)


def _moe_kernel(cnt_ref, x_ref, rrow_ref, rcol_ref, gcol_ref, wg_ref, wu_ref, w2_ref, o_ref, xc_scr, y_scr,
                *, n_f, tm):
    i, e, f = pl.program_id(0), pl.program_id(1), pl.program_id(2)
    sub = MOE_SUB
    slab = min(MOE_SLAB, tm)
    nslab = tm // slab
    n_sub = (cnt_ref[i, e, nslab] + (sub - 1)) // sub

    def touches(s, j):
        return (cnt_ref[i, e, j] < (s + 1) * sub) & (cnt_ref[i, e, j + 1] > s * sub)

    @pl.when((e == 0) & (f == 0))
    def _zero():
        o_ref[...] = jnp.zeros_like(o_ref)

    @pl.when(f == 0)
    def _compact():
        def body(s, carry):
            rows = pl.ds(pl.multiple_of(s * sub, sub), sub)
            y_scr[rows, :] = jnp.zeros((sub, y_scr.shape[1]), F32)
            for j in range(nslab):
                cols = slice(j * slab, (j + 1) * slab)

                @pl.when(touches(s, j))
                def _add():
                    ridx = lax.broadcasted_iota(jnp.int32, (sub, slab), 0) + s * sub
                    onehot = jnp.where(rrow_ref[:, cols] == ridx, 1.0, 0.0).astype(BF16)
                    y_scr[rows, :] += _dot(onehot, x_ref[cols, :])
            xc_scr[rows, :] = y_scr[rows, :].astype(BF16)
            return carry
        lax.fori_loop(0, n_sub, body, 0)

    def expert(s, first):
        rows = pl.ds(pl.multiple_of(s * sub, sub), sub)
        xs = xc_scr[rows, :]
        gate = _dot(xs, wg_ref[...])
        up = _dot(xs, wu_ref[...])
        h = (gate * jax.nn.sigmoid(gate) * up).astype(BF16)
        part = _dot(h, w2_ref[...])
        y_scr[rows, :] = part if first else y_scr[rows, :] + part

    def experts(first):
        def pair(p, carry):
            expert(2 * p, first)
            expert(2 * p + 1, first)
            return carry
        lax.fori_loop(0, n_sub // 2, pair, 0)

        @pl.when(n_sub % 2 == 1)
        def _tail():
            expert(n_sub - 1, first)

    @pl.when(f == 0)
    def _first():
        experts(True)

    @pl.when(f > 0)
    def _rest():
        experts(False)

    @pl.when(f == n_f - 1)
    def _scatter():
        def body(s, carry):
            y = y_scr[pl.ds(pl.multiple_of(s * sub, sub), sub), :].astype(BF16)
            for j in range(nslab):
                rows = slice(j * slab, (j + 1) * slab)

                @pl.when(touches(s, j))
                def _add():
                    cidx = lax.broadcasted_iota(jnp.int32, (slab, sub), 1) + s * sub
                    onehot_t = jnp.where(rcol_ref[rows, :] == cidx, 1.0, 0.0).astype(BF16)
                    o_ref[rows, :] += gcol_ref[rows, :] * _dot(onehot_t, y)
            return carry
        lax.fori_loop(0, n_sub, body, 0)


def _moe(x, logits, w13, w2):
    T, D = x.shape
    E, _, F2 = w13.shape
    F = F2 // 2
    tm = min(MOE_TILE, T)
    nt = T // tm
    tf = FFN_TF
    n_f = F // tf
    top_val, top_idx = lax.top_k(logits, TOP_K)
    gates = jax.nn.softmax(top_val, axis=-1)
    onehot = jax.nn.one_hot(top_idx, E, dtype=F32)
    sel = jnp.sum(onehot, axis=1).astype(jnp.int32).reshape(nt, tm, E)
    gate_dense = jnp.sum(onehot * gates[..., None], axis=1).reshape(nt, tm, E)
    rank = jnp.where(sel > 0, jnp.cumsum(sel, axis=1) - sel, -1)
    rank = jnp.swapaxes(rank, 1, 2)
    slab = min(MOE_SLAB, tm)
    per_slab = jnp.sum(sel.reshape(nt, tm // slab, slab, E), axis=2)
    counts = jnp.concatenate([jnp.zeros((nt, 1, E), jnp.int32), jnp.cumsum(per_slab, axis=1)], axis=1)
    counts = jnp.swapaxes(counts, 1, 2)
    gcol = jnp.swapaxes(gate_dense, 1, 2)[..., None]
    tile = lambda shape, imap: pl.BlockSpec(shape, imap)
    return pl.pallas_call(
        functools.partial(_moe_kernel, n_f=n_f, tm=tm),
        grid_spec=pltpu.PrefetchScalarGridSpec(
            num_scalar_prefetch=1,
            grid=(nt, E, n_f),
            in_specs=[tile((tm, D), lambda i, e, f, c: (i, 0)),
                      tile((None, None, 1, tm), lambda i, e, f, c: (i, e, 0, 0)),
                      tile((None, None, tm, 1), lambda i, e, f, c: (i, e, 0, 0)),
                      tile((None, None, tm, 1), lambda i, e, f, c: (i, e, 0, 0)),
                      tile((None, D, tf), lambda i, e, f, c: (e, 0, f)),
                      tile((None, D, tf), lambda i, e, f, c: (e, 0, f + n_f)),
                      tile((None, tf, D), lambda i, e, f, c: (e, f, 0))],
            out_specs=tile((tm, D), lambda i, e, f, c: (i, 0)),
            scratch_shapes=[pltpu.VMEM((-(-tm // MOE_SUB) * MOE_SUB, D), BF16),
                            pltpu.VMEM((-(-tm // MOE_SUB) * MOE_SUB, D), F32)]),
        out_shape=jax.ShapeDtypeStruct((T, D), F32),
        compiler_params=_cparams(("parallel", "arbitrary", "arbitrary")),
        name="moe",
    )(counts, x, rank[:, :, None, :], rank[..., None], gcol, w13, w13, w2)


def _rwkv_scan_kernel(r_ref, lw_ref, k_ref, v_ref, a_ref, b_ref, h0_ref, y_ref, hT_ref, H_scr,
                      *, reverse, nblk, nchunk, npl):
    C = RW_CHUNK
    half = LANES // 2
    i = pl.program_id(2)

    @pl.when(i == 0)
    def _init():
        H_scr[...] = h0_ref[...]

    t_idx = lax.broadcasted_iota(jnp.int32, (C, LANES), 0)
    s_idx = lax.broadcasted_iota(jnp.int32, (C, LANES), 1) & (half - 1)
    tt = lax.broadcasted_iota(jnp.int32, (C, C), 0)
    ss = lax.broadcasted_iota(jnp.int32, (C, C), 1)
    if reverse:
        strict, incl, tri = s_idx > t_idx, s_idx >= t_idx, (ss >= tt).astype(BF16)
    else:
        strict, incl, tri = s_idx < t_idx, s_idx <= t_idx, (ss <= tt).astype(BF16)
    eye_lp = (s_idx == t_idx).astype(F32)
    lane = lax.broadcasted_iota(jnp.int32, (1, LANES), 1)
    m0 = (lane < half).astype(F32)
    m1 = 1.0 - m0
    rr = lax.broadcasted_iota(jnp.int32, (LANES, LANES), 0)
    cc = lax.broadcasted_iota(jnp.int32, (LANES, LANES), 1)
    mask_bd = ((rr < half) == (cc < half)).astype(F32)

    def bd(x):
        return jnp.concatenate([x * m0, x * m1], axis=0)

    order = list(range(nchunk - 1, -1, -1) if reverse else range(nchunk))
    units = [(slice(c * C, (c + 1) * C), slice(q * LANES, (q + 1) * LANES)) for c in order for q in range(npl)]
    ld = lambda ref: [ref[sl, ln] for sl, ln in units]
    r, lw, k, v, a, b = ld(r_ref), ld(lw_ref), ld(k_ref), ld(v_ref), ld(a_ref), ld(b_ref)
    G = [_cumsum_dot(tri, x) for x in lw]
    eG = [jnp.exp(g) for g in G]
    eGn = [jnp.exp(-g) for g in G]
    rt = [x * e for x, e in zip(r, eG)]
    at = [x * jnp.exp(g - l) for x, g, l in zip(a, G, lw)]
    kt = [x * e for x, e in zip(k, eGn)]
    bt = [x * e for x, e in zip(b, eGn)]
    M = [_bdot(jnp.concatenate([x, y], axis=0), jnp.concatenate([bd(z), bd(w)], axis=0), nt=True)
         for x, y, z, w in zip(at, rt, bt, kt)]
    Nm = [jnp.where(strict, m[:C, :LANES], 0.0) for m in M]
    Aak = [jnp.where(strict, m[:C, LANES:], 0.0) for m in M]
    Arb = [jnp.where(incl, m[C:, :LANES], 0.0) for m in M]
    Ark = [jnp.where(incl, m[C:, LANES:], 0.0) for m in M]
    T = None
    for lvl in range(1, int(math.log2(C)) + 1):
        same = (t_idx >> lvl) == (s_idx >> lvl)
        t_hi = ((t_idx >> (lvl - 1)) & 1) == 1
        s_hi = ((s_idx >> (lvl - 1)) & 1) == 1
        off = same & (s_hi & ~t_hi if reverse else t_hi & ~s_hi)
        Noff = [jnp.where(off, n, 0.0) for n in Nm]
        if T is None:
            T = [eye_lp + n for n in Noff]
        else:
            DN = [_bdot(t, bd(n)) for t, n in zip(T, Noff)]
            T = [t + _bdot(dn, bd(t)) for t, dn in zip(T, DN)]
    bdv = [bd(x) for x in v]
    X0 = [_bdot(x, y) for x, y in zip(Aak, bdv)]
    WU = [_bdot(t, jnp.concatenate([bd(x), bd(y)], axis=1)) for t, x, y in zip(T, at, X0)]
    W = [x[:, :LANES] for x in WU]
    U0 = [x[:, LANES:] for x in WU]
    RY = [_bdot(x, jnp.concatenate([bd(y), bd(z)], axis=1)) for x, y, z in zip(Arb, W, U0)]
    Rh = [x + y[:, :LANES] for x, y in zip(rt, RY)]
    Y0 = [y[:, LANES:] + _bdot(x, z) for y, x, z in zip(RY, Ark, bdv)]
    dPhi = [mask_bd * _bdot(x.T, y) for x, y in zip(W, bt)]
    Psi = [mask_bd * _bdot(jnp.concatenate([x, y], axis=0).T, jnp.concatenate([z, w], axis=0))
           for x, y, z, w in zip(U0, v, bt, kt)]
    e_end = [jnp.exp(g[0:1, :] if reverse else g[C - 1:C, :]) for g in G]

    Hs = [H_scr[q] for q in range(npl)]
    for u, (sl, ln) in enumerate(units):
        q = u % npl
        H = Hs[q]
        y_ref[sl, ln] = _bdot(Rh[u], H, nt=True) + Y0[u]
        Hs[q] = (H + _bdot(H, dPhi[u]) + Psi[u]) * e_end[u]
    for q in range(npl):
        H_scr[q] = Hs[q]

    @pl.when(i == nblk - 1)
    def _fin():
        for q in range(npl):
            hT_ref[q] = Hs[q]


def _pair_states(S):
    B, H, N, _ = S.shape
    S5 = S.reshape(B, H // 2, 2, N, N)
    eye = jnp.eye(2, dtype=S.dtype)
    return jnp.einsum('bpivk,ij->bpivjk', S5, eye).reshape(B, H // 2, 2 * N, 2 * N)


def _unpair_states(Sp):
    B, P, N2, _ = Sp.shape
    N = N2 // 2
    S6 = Sp.reshape(B, P, 2, N, 2, N)
    return jnp.stack([S6[:, :, 0, :, 0, :], S6[:, :, 1, :, 1, :]], axis=2).reshape(B, 2 * P, N, N)


def _rwkv_scan(r, lw, k, v, a, b, state0, reverse):
    B, L, D = r.shape
    bt = min(RW_BLOCK, L)
    nblk = L // bt
    npl = RW_PAIRS_PER_STEP
    npair = D // LANES
    blk = (lambda i: nblk - 1 - i) if reverse else (lambda i: i)
    tok = pl.BlockSpec((None, bt, npl * LANES), lambda bb, p, i: (bb, blk(i), p))
    st = pl.BlockSpec((None, npl, LANES, LANES), lambda bb, p, i: (bb, p, 0, 0))
    y, hT = pl.pallas_call(
        functools.partial(_rwkv_scan_kernel, reverse=reverse, nblk=nblk, nchunk=bt // RW_CHUNK, npl=npl),
        grid=(B, npair // npl, nblk),
        in_specs=[tok] * 6 + [st],
        out_specs=[tok, st],
        out_shape=[jax.ShapeDtypeStruct((B, L, D), F32),
                   jax.ShapeDtypeStruct((B, npair, LANES, LANES), F32)],
        scratch_shapes=[pltpu.VMEM((npl, LANES, LANES), F32)],
        compiler_params=_cparams(("parallel", "parallel", "arbitrary")),
        name="rwkv_scan_rev" if reverse else "rwkv_scan_fwd",
    )(r, lw, k, v, a, b, _pair_states(state0))
    return y, _unpair_states(hT)


def _gla_kernel(q_ref, f_ref, v_ref, lb_ref, h0_ref, o_ref, hT_ref, H_scr, *, reverse, nblk, npl):
    C = HG_CHUNK
    sh = int(math.log2(C))
    bt = q_ref.shape[0]
    nchunk = bt // C
    i = pl.program_id(2)

    @pl.when(i == 0)
    def _init():
        H_scr[...] = h0_ref[...]

    tt = lax.broadcasted_iota(jnp.int32, (bt, bt), 0)
    ss = lax.broadcasted_iota(jnp.int32, (bt, bt), 1)
    same = (tt >> sh) == (ss >> sh)
    mid = ((tt >> sh) << sh) + (C // 2 if reverse else C // 2 - 1)
    if reverse:
        incl, upto_mid = same & (ss >= tt), same & (ss >= mid)
    else:
        incl, upto_mid = same & (ss <= tt), same & (ss <= mid)
    one = lambda m: jnp.where(m, 1.0, 0.0)
    sums = jnp.concatenate([one(incl), one(incl) - one(upto_mid), one(same) - one(incl)], axis=0).astype(BF16)

    order = list(range(nchunk - 1, -1, -1) if reverse else range(nchunk))
    heads = [slice(p * LANES, (p + 1) * LANES) for p in range(npl)]
    q = [jax.nn.silu(q_ref[:, ln]) for ln in heads]
    fg = [lb_ref[:, ln] + (1.0 - lb_ref[:, ln]) * jax.nn.sigmoid(f_ref[:, ln]) for ln in heads]
    v = [v_ref[:, ln] for ln in heads]
    k = [1.0 - f for f in fg]
    Gs = [_cumsum_dot(sums, jnp.log(f)) for f in fg]
    Gabs = [g[:bt] for g in Gs]
    Grel = [g[bt:2 * bt] for g in Gs]
    Gend = [g[2 * bt:] for g in Gs]
    scores = [jnp.where(incl, _bdot(x * jnp.exp(g), y * jnp.exp(-g), nt=True), 0.0)
              for x, y, g in zip(q, k, Grel)]
    o_intra = [_bdot(s, x) for s, x in zip(scores, v)]
    qa = [x * jnp.exp(g) for x, g in zip(q, Gabs)]
    kend = [x * jnp.exp(g) for x, g in zip(k, Gend)]
    rows = [slice(c * C, (c + 1) * C) for c in range(nchunk)]
    KV = [[_bdot(x[r].T, y[r]) for r in rows] for x, y in zip(v, kend)]
    dec = [[jnp.exp(ga[r][0:1, :] + ge[r][0:1, :]) for r in rows] for ga, ge in zip(Gabs, Gend)]

    Hs = [H_scr[p] for p in range(npl)]
    for c in order:
        for p in range(npl):
            o_ref[rows[c], heads[p]] = o_intra[p][rows[c]] + _bdot(qa[p][rows[c]], Hs[p], nt=True)
            Hs[p] = Hs[p] * dec[p][c] + KV[p][c]
    for p in range(npl):
        H_scr[p] = Hs[p]

    @pl.when(i == nblk - 1)
    def _fin():
        for p in range(npl):
            hT_ref[p] = Hs[p]


def _gla_scan(proj, lb, state0, d):
    B, L, D5 = proj.shape
    D = D5 // 5
    reverse = d == 1
    bt = GLA_BLOCK
    nblk = L // bt
    npl = GLA_HEADS_PER_STEP
    nh = D // LANES
    ncb = nh // npl
    blk = (lambda i: nblk - 1 - i) if reverse else (lambda i: i)
    col = lambda off: pl.BlockSpec((None, bt, npl * LANES), lambda bb, p, i: (bb, blk(i), off * ncb + p))
    st = pl.BlockSpec((None, npl, LANES, LANES), lambda bb, p, i: (bb, p, 0, 0))
    o, hT = pl.pallas_call(
        functools.partial(_gla_kernel, reverse=reverse, nblk=nblk, npl=npl),
        grid=(B, ncb, nblk),
        in_specs=[col(0), col(1 + d), col(3), pl.BlockSpec((1, npl * LANES), lambda bb, p, i: (0, p)), st],
        out_specs=[col(0), st],
        out_shape=[jax.ShapeDtypeStruct((B, L, D), F32),
                   jax.ShapeDtypeStruct((B, nh, LANES, LANES), F32)],
        scratch_shapes=[pltpu.VMEM((npl, LANES, LANES), F32)],
        compiler_params=_cparams(("parallel", "parallel", "arbitrary")),
        name="gla_scan_rev" if reverse else "gla_scan_fwd",
    )(proj, proj, proj, lb[d][None, :], jnp.swapaxes(state0, 2, 3))
    return o, jnp.swapaxes(hT, 2, 3)


def _leftmul_kernel(fh_ref, fl_ref, x_ref, o_ref, xs_scr, *, nj):
    for j in range(nj):
        xs_scr[...] = x_ref[:, j, :]
        o_ref[:, j, :] = _x3dot(fh_ref[...], fl_ref[...], xs_scr[...])


def _leftmul_gate_kernel(fh_ref, fl_ref, x_ref, u_ref, s_ref, g_ref, o_ref, xs_scr, *, nj):
    for j in range(nj):
        xs_scr[...] = x_ref[:, j, :]
        y = _x3dot(fh_ref[...], fl_ref[...], xs_scr[...])
        o_ref[:, j, :] = (y + u_ref[:, j, :] * s_ref[...]) * g_ref[:, j, :]


def _leftmul(f, x, xoff=0, epilogue=None, D=D_MODEL):
    B, K, J, _ = x.shape
    M = f.shape[0]
    tj = 8 if J % 8 == 0 else J
    td = DFT_TD
    fh, fl = _split2(f)
    fs = pl.BlockSpec((M, K), lambda b, j, d: (0, 0))
    col = lambda rows, off: pl.BlockSpec((None, rows, tj, td), lambda b, j, d: (b, 0, j, off + d))
    if epilogue is None:
        kern, specs, args = _leftmul_kernel, [fs, fs, col(K, xoff)], (fh, fl, x)
    else:
        u, uoff, skip, gate, goff = epilogue
        kern = _leftmul_gate_kernel
        specs = [fs, fs, col(K, xoff), col(M, uoff), pl.BlockSpec((1, td), lambda b, j, d: (0, d)), col(M, goff)]
        args = (fh, fl, x, u, skip, gate)
    return pl.pallas_call(
        functools.partial(kern, nj=tj), grid=(B, J // tj, D // td), in_specs=specs, out_specs=col(M, 0),
        out_shape=jax.ShapeDtypeStruct((B, M, J, D), F32),
        scratch_shapes=[pltpu.VMEM((K, td), F32)],
        compiler_params=_cparams(("parallel", "parallel", "parallel")),
        name="dft_outer",
    )(*args)


def _spectral_fwd_kernel(fh_ref, fl_ref, y_ref, o_ref):
    z = _x3dot(fh_ref[...], fl_ref[...], y_ref[...])
    D = o_ref.shape[1] // HY_ORDER
    for o in range(HY_ORDER):
        o_ref[:, o * D:(o + 1) * D] = _combine_filter_spectrum(z[:, 2 * o * D:(2 * o + 1) * D],
                                                               z[:, (2 * o + 1) * D:(2 * o + 2) * D])


def _spectral_conv_kernel(fh_ref, fl_ref, fih_ref, fil_ref, kf_ref, y_ref, o_ref):
    n2 = kf_ref.shape[0] // 2
    z = _x3dot(fh_ref[...], fl_ref[...], y_ref[...])
    zr, zi = z[:n2], z[n2:]
    kr, ki = kf_ref[:n2, :], kf_ref[n2:, :]
    p = jnp.concatenate([zr * kr - zi * ki, zr * ki + zi * kr], axis=0)
    o_ref[...] = _x3dot(fih_ref[...], fil_ref[...], p)


def _spectral_mid(y, f_fwd, f_inv=None, spec=None, order=0):
    B, K1, R, D = y.shape
    ms = pl.BlockSpec((None, R, R), lambda k1, b: (k1, 0, 0))
    ys = pl.BlockSpec((None, None, R, D), lambda k1, b: (b, k1, 0, 0))
    Do = D
    os_ = ys
    if spec is None:
        kern, specs, args = _spectral_fwd_kernel, [ms, ms, ys], (*_split2(f_fwd), y)
        Do = D // 2
        os_ = pl.BlockSpec((None, None, R, Do), lambda k1, b: (b, k1, 0, 0))
    else:
        kern = _spectral_conv_kernel
        specs = [ms, ms, ms, ms, pl.BlockSpec((None, R, D), lambda k1, b: (k1, 0, order)), ys]
        args = (*_split2(f_fwd), *_split2(f_inv), spec, y)
    return pl.pallas_call(
        kern, grid=(K1, B), in_specs=specs, out_specs=os_,
        out_shape=jax.ShapeDtypeStruct((B, K1, R, Do), F32),
        compiler_params=_cparams(("parallel", "parallel")),
        name="dft_inner",
    )(*args)


def _dft_tables(L):
    n2 = HY_N2 if L % HY_N2 == 0 and L > HY_N2 * 2 else 1
    N = 2 * L
    n1 = N // n2
    k1h = n1 // 2
    ph = (jnp.arange(k1h, dtype=jnp.int32)[:, None] * 2 + 1) * jnp.arange(k1h, dtype=jnp.int32)[None, :]
    th = (ph % (2 * n1)).astype(F32) * (math.pi / n1)
    f_out = jnp.stack([jnp.cos(th), -jnp.sin(th)], axis=1).reshape(2 * k1h, k1h)
    f_out_inv = (2.0 / N) * f_out.T
    if n2 == 1:
        return n2, f_out, f_out_inv, None, None
    kk = jnp.arange(k1h, dtype=jnp.int32)[:, None, None] + n1 * jnp.arange(n2, dtype=jnp.int32)[None, :, None]
    ph = ((2 * kk + 1) * jnp.arange(n2, dtype=jnp.int32)[None, None, :]) % (2 * N)
    phi = ph.astype(F32) * (math.pi / N)
    c, s = jnp.cos(phi), jnp.sin(phi)
    f_in = jnp.concatenate([jnp.concatenate([c, s], axis=2), jnp.concatenate([-s, c], axis=2)], axis=1)
    ct, st = jnp.swapaxes(c, 1, 2), jnp.swapaxes(s, 1, 2)
    f_in_inv = jnp.concatenate([jnp.concatenate([ct, -st], axis=2), jnp.concatenate([st, ct], axis=2)], axis=1)
    return n2, f_out, f_out_inv, f_in, f_in_inv


def _long_conv_gated(u, ucol, spec, order, skip, gate, gcol, tables):
    n2, f_out, f_out_inv, f_in, f_in_inv = tables
    B, L, _ = u.shape
    D = D_MODEL
    k1h = L // n2
    per = D // DFT_TD
    u4 = u.reshape(B, k1h, n2, u.shape[-1])
    y = _leftmul(f_out, u4, xoff=ucol * per).reshape(B, k1h, 2 * n2, D)
    if f_in is None:
        kf = spec[..., order * D:(order + 1) * D]
        yr, yi = y[:, :, 0], y[:, :, 1]
        kr, ki = kf[:, 0], kf[:, 1]
        q = jnp.stack([yr * kr - yi * ki, yr * ki + yi * kr], axis=2)
    else:
        q = _spectral_mid(y, f_in, f_in_inv, spec, order)
    out = _leftmul(f_out_inv, q.reshape(B, 2 * k1h, n2, D),
                   epilogue=(u4, ucol * per, skip[None, :], gate.reshape(B, k1h, n2, gate.shape[-1]), gcol * per))
    return out.reshape(B, L, D)


def _hyena_filter_spectra(L, w1, b1, w2, b2, w3, b3, w4, freq, tables):
    pos = jnp.arange(L, dtype=F32)
    t = (pos / max(L - 1, 1))[:, None]
    ang = (2.0 * math.pi / L) * pos[:, None] * jnp.linspace(1e-4, HY_BANDS - 1, HY_BANDS, dtype=F32)[None, :]
    z = jnp.concatenate([t, jnp.cos(ang), -jnp.sin(ang)], axis=-1)
    z = jnp.pad(z, ((0, 0), (0, 40 - HY_EMB)))
    w1p = jnp.pad(w1, ((0, 40 - HY_EMB), (0, 0)))
    hid = jnp.sin(freq[0] * (_mm(z, w1p, exact=True) + b1))
    hid = jnp.sin(freq[1] * (_mm(hid, w2, exact=True) + b2))
    hid = jnp.sin(freq[2] * (_mm(hid, w3, exact=True) + b3))
    filt = _mm(hid, w4, exact=True)
    deltas = jnp.linspace(math.log(HY_TARGET) / HY_FAST_DECAY, math.log(HY_TARGET) / HY_SLOW_DECAY,
                          D_MODEL, dtype=F32)
    window = jnp.exp(-t * jnp.abs(deltas))
    lag0 = (jnp.arange(L) > 0).astype(F32)[:, None]
    scale = jnp.concatenate([window, window * lag0] * HY_ORDER, axis=1)
    filt = filt * scale
    n2 = tables[0]
    k1h = L // n2
    C = HY_ORDER * 2 * D_MODEL
    y = _leftmul(tables[1], filt.reshape(1, k1h, n2, C), D=C).reshape(1, k1h, 2 * n2, C)
    if tables[3] is not None:
        return _spectral_mid(y, tables[3])[0]
    cols = lambda g: y[0][..., g * D_MODEL:(g + 1) * D_MODEL]
    return jnp.concatenate([_combine_filter_spectrum(cols(2 * o), cols(2 * o + 1)) for o in range(HY_ORDER)], axis=-1)


def _combine_filter_spectrum(hf, hb):
    n2 = hf.shape[-2] // 2
    return jnp.concatenate([hf[..., :n2, :] + hb[..., :n2, :], hf[..., n2:, :] - hb[..., n2:, :]], axis=-2)


def _hyena_stream(resid, vecs, fprm, w_in, b_in, conv_w, conv_b, skip, w_o, b_o):
    B, L, D = resid.shape
    tables = _dft_tables(L)
    spec = _hyena_filter_spectra(L, *fprm, tables)
    proj = _proj_conv3(resid, vecs, w_in, b_in, conv_w, conv_b)
    z = _long_conv_gated(proj, 2, spec, 0, skip[0], proj, 0, tables)
    y = _long_conv_gated(z, 0, spec, 1, skip[1], proj, 1, tables)
    return _mm_fused(y, w_o, vecs=vecs, bias=b_o, resid=resid)


def _headsum(x, ones):
    hi, lo = _split2(x)
    cols = []
    for j in range(x.shape[1] // LANES):
        ln = slice(j * LANES, (j + 1) * LANES)
        cols.append(_dot(hi[:, ln], ones) + _dot(lo[:, ln], ones))
    return jnp.concatenate(cols, axis=1)


def _token_shift(h, h_above, h_below, grid_w):
    tm, D = h.shape
    row = lax.broadcasted_iota(jnp.int32, (tm, D), 0)
    lane = lax.broadcasted_iota(jnp.int32, (tm, D), 1)
    before = pltpu.roll(h, 1, axis=0)
    after = pltpu.roll(h, tm - 1, axis=0)
    if grid_w is None:
        return jnp.where(lane < D // 2, jnp.where(row == 0, 0.0, before), jnp.where(row == tm - 1, 0.0, after))
    col = row & (grid_w - 1)
    left = jnp.where(col == 0, 0.0, before)
    right = jnp.where(col == grid_w - 1, 0.0, after)
    up = jnp.concatenate([h_above, h[:tm - grid_w]], axis=0)
    down = jnp.concatenate([h[grid_w:], h_below], axis=0)
    q = D // 4
    return jnp.where(lane < q, left, jnp.where(lane < 2 * q, right, jnp.where(lane < 3 * q, up, down)))


def _rwkv_pre_kernel(*refs, vres, grid_w, nblk):
    it = iter(refs)
    x_ref = next(it)
    xa_ref, xb_ref = (next(it), next(it)) if grid_w else (None, None)
    mod_ref = next(it)
    vf_ref = next(it) if vres else None
    vec_ref, wr_ref, wk_ref, wv_ref, w1_ref, w2_ref, a1_ref, a2_ref, g1_ref, g2_ref = (next(it) for _ in range(10))
    v1_ref, v2_ref = (next(it), next(it)) if vres else (None, None)
    ones_ref = next(it)
    r_o, v_o, nkk_o, lw0_o, lw1_o, kd0_o, kd1_o, b0_o, b1_o, bonus_o, g_o = it
    vec = lambda i: vec_ref[i:i + 1, :]
    i = pl.program_id(1)
    h = _norm_mod(x_ref[...], mod_ref)
    if grid_w:
        h_above = jnp.where(i > 0, _norm_mod(xa_ref[...], mod_ref), 0.0)
        h_below = jnp.where(i < nblk - 1, _norm_mod(xb_ref[...], mod_ref), 0.0)
    else:
        h_above = h_below = None
    xx = _token_shift(h, h_above, h_below, grid_w) - h
    xr, xw, xk, xv, xa, xg = ((h + xx * vec(j)).astype(BF16) for j in range(6))
    r = _dot(xr, wr_ref[...])
    k = _dot(xk, wk_ref[...])
    v = _dot(xv, wv_ref[...])
    if vres:
        lora = _dot(_dot(xv, v1_ref[...]).astype(BF16), v2_ref[...])
        v = v + (vf_ref[...] - v) * jax.nn.sigmoid(vec(13) + lora)
    ones = ones_ref[...]
    kk = k * vec(10)
    kk = kk / jnp.maximum(jnp.sqrt(_headsum(kk * kk, ones)), 1e-12)
    r_o[...] = r
    v_o[...] = v
    nkk_o[...] = -kk
    bonus = jnp.zeros_like(r)
    for d, (lw_o, kd_o, b_o) in enumerate(((lw0_o, kd0_o, b0_o), (lw1_o, kd1_o, b1_o))):
        wl = vec(6 + d) + _dot(jnp.tanh(_dot(xw, w1_ref[d])).astype(BF16), w2_ref[d])
        lw_o[...] = -jax.nn.sigmoid(wl) * math.exp(-0.5)
        a = jax.nn.sigmoid(vec(8 + d) + _dot(_dot(xa, a1_ref[d]).astype(BF16), a2_ref[d]))
        kd = k * (1.0 + (a - 1.0) * vec(11))
        kd_o[...] = kd
        b_o[...] = kk * a
        bonus = bonus + _headsum(r * kd * vec(12), ones) * v
    bonus_o[...] = bonus
    g_o[...] = _dot(jax.nn.sigmoid(_dot(xg, g1_ref[...])).astype(BF16), g2_ref[...])


def _rwkv_post_kernel(y0_ref, y1_ref, bonus_ref, g_ref, res_ref, mod_ref, ln_ref, wo_ref, ones_ref, o_ref):
    ones = ones_ref[...]
    y = y0_ref[...] + y1_ref[...]
    yc = y - _headsum(y, ones) * (1.0 / RW_HEAD)
    var = _headsum(yc * yc, ones) * (1.0 / RW_HEAD)
    yn = yc * lax.rsqrt(var + RW_LN_EPS) * ln_ref[0:1, :] + ln_ref[1:2, :] + bonus_ref[...]
    out = _dot((yn * g_ref[...]).astype(BF16), wo_ref[...])
    o_ref[...] = res_ref[...] + mod_ref[2:3, :] * out


def _full(a):
    nd = a.ndim
    return pl.BlockSpec(a.shape, lambda b, i: (0,) * nd)


def _rwkv7_stream(resid, mods, grid_w, state0, v_first, mu, w_rkv, w_o, w0, w1, w2, a0, a1, a2,
                  g1, g2, k_k, k_a, r_k, ln_w, ln_b, vres, need_out=True):
    B, L, D = resid.shape
    tm = min(RW_TM, L)
    nblk = L // tm
    assert grid_w is not None or nblk == 1
    vres_on = vres is not None
    rows = [mu[j] for j in range(6)] + [w0[0], w0[1], a0[0], a0[1], k_k, k_a, r_k.reshape(D)]
    rows.append(vres[0] if vres_on else jnp.zeros((D,), F32))
    vecs = jnp.stack(rows + [jnp.zeros((D,), F32)] * (16 - len(rows)))
    ones = jnp.kron(jnp.eye(LANES // RW_HEAD, dtype=F32), jnp.ones((RW_HEAD, RW_HEAD), F32)).astype(BF16)
    bf = lambda t: t.astype(BF16)
    tok = pl.BlockSpec((None, tm, D), lambda b, i: (b, i, 0))
    mod_spec = pl.BlockSpec((None, 8, D), lambda b, i: (b, 0, 0))
    args, specs = [resid], [tok]
    if grid_w:
        per = tm // grid_w
        last_row = L // grid_w - 1
        args += [resid, resid]
        specs += [pl.BlockSpec((None, grid_w, D), lambda b, i: (b, jnp.maximum(i * per - 1, 0), 0)),
                  pl.BlockSpec((None, grid_w, D), lambda b, i: (b, jnp.minimum((i + 1) * per, last_row), 0))]
    args.append(mods)
    specs.append(mod_spec)
    if vres_on:
        args.append(v_first)
        specs.append(tok)
    consts = [vecs, bf(w_rkv[0]), bf(w_rkv[1]), bf(w_rkv[2]), bf(w1), bf(w2), bf(a1), bf(a2), bf(g1), bf(g2)]
    consts += [bf(vres[1]), bf(vres[2])] if vres_on else []
    consts.append(ones)
    outs = pl.pallas_call(
        functools.partial(_rwkv_pre_kernel, vres=vres_on, grid_w=grid_w, nblk=nblk),
        grid=(B, nblk),
        in_specs=specs + [_full(c) for c in consts],
        out_specs=[tok] * 11,
        out_shape=[jax.ShapeDtypeStruct((B, L, D), F32)] * 11,
        compiler_params=_cparams(("parallel", "parallel")),
        name="rwkv_pre",
    )(*args, *consts)
    r, v, nkk, lw0, lw1, kd0, kd1, b0, b1, bonus, g = outs
    y0, s0 = _rwkv_scan(r, lw0, kd0, v, nkk, b0, state0[0], reverse=False)
    y1, s1 = _rwkv_scan(r, lw1, kd1, v, nkk, b1, state0[1], reverse=True)
    states = jnp.stack([s0, s1])
    if not need_out:
        return None, states, v
    ln = jnp.stack([ln_w, ln_b] + [jnp.zeros((D,), F32)] * 6)
    wo = bf(w_o)
    tp = min(POST_TM, L)
    tok = pl.BlockSpec((None, tp, D), lambda b, i: (b, i, 0))
    new_resid = pl.pallas_call(
        _rwkv_post_kernel,
        grid=(B, L // tp),
        in_specs=[tok] * 5 + [mod_spec, _full(ln), _full(wo), _full(ones)],
        out_specs=tok,
        out_shape=jax.ShapeDtypeStruct((B, L, D), F32),
        compiler_params=_cparams(("parallel", "parallel")),
        name="rwkv_post",
    )(y0, y1, bonus, g, resid, mods, ln, wo, ones)
    return new_resid, states, v


def _mm_fused_kernel(*refs, norm, bias, resid):
    it = iter(refs)
    x_ref = next(it)
    vec_ref = next(it) if (norm or resid) else None
    w_ref = next(it)
    b_ref = next(it) if bias else None
    res_ref = next(it) if resid else None
    o_ref, xs_scr = next(it), next(it)

    @pl.when(pl.program_id(2) == 0)
    def _prep():
        x = x_ref[...]
        xs_scr[...] = (_norm_mod(x, vec_ref) if norm else x).astype(BF16)

    acc = _dot(xs_scr[...], w_ref[...])
    if bias:
        acc = acc + b_ref[...]
    o_ref[...] = res_ref[...] + vec_ref[2:3, :] * acc if resid else acc


def _mm_fused(x, w, vecs=None, norm=False, bias=None, resid=None):
    B, L, K = x.shape
    N = w.shape[1]
    tm = min(MM_TM, L)
    tn = MM_TN if (N % MM_TN == 0 and resid is None) else N
    use_vec = norm or resid is not None
    specs = [pl.BlockSpec((None, tm, K), lambda b, i, j: (b, i, 0))]
    args = [x]
    if use_vec:
        specs.append(pl.BlockSpec((None, 8, vecs.shape[-1]), lambda b, i, j: (b, 0, 0)))
        args.append(vecs)
    specs.append(pl.BlockSpec((K, tn), lambda b, i, j: (0, j)))
    args.append(w.astype(BF16))
    if bias is not None:
        specs.append(pl.BlockSpec((1, tn), lambda b, i, j: (0, j)))
        args.append(bias[None, :])
    out_spec = pl.BlockSpec((None, tm, tn), lambda b, i, j: (b, i, j))
    if resid is not None:
        assert N == K == vecs.shape[-1] and tn == N
        specs.append(out_spec)
        args.append(resid)
    return pl.pallas_call(
        functools.partial(_mm_fused_kernel, norm=norm, bias=bias is not None, resid=resid is not None),
        grid=(B, L // tm, N // tn),
        in_specs=specs, out_specs=out_spec,
        out_shape=jax.ShapeDtypeStruct((B, L, N), F32),
        scratch_shapes=[pltpu.VMEM((tm, K), BF16)],
        compiler_params=_cparams(("parallel", "parallel", "arbitrary")),
        name="mm_fused",
    )(*args)


def _proj_conv3_kernel(x_ref, xa_ref, xb_ref, vec_ref, w_ref, b_ref, cw_ref, o_ref, xs_scr, acc_scr, *, nblk):
    i = pl.program_id(1)
    tm = x_ref.shape[0]

    @pl.when(pl.program_id(2) == 0)
    def _prep():
        xs_scr[0:HALO, :] = _norm_mod(xa_ref[...], vec_ref).astype(BF16)
        xs_scr[HALO:HALO + tm, :] = _norm_mod(x_ref[...], vec_ref).astype(BF16)
        xs_scr[HALO + tm:, :] = _norm_mod(xb_ref[...], vec_ref).astype(BF16)

    acc_scr[...] = _dot(xs_scr[...], w_ref[...]) + b_ref[...]
    row = lax.broadcasted_iota(jnp.int32, o_ref.shape, 0)
    before = jnp.where((row == 0) & (i == 0), 0.0, acc_scr[HALO - 1:HALO - 1 + tm, :])
    after = jnp.where((row == tm - 1) & (i == nblk - 1), 0.0, acc_scr[HALO + 1:HALO + 1 + tm, :])
    o_ref[...] = (before * cw_ref[0:1, :] + acc_scr[HALO:HALO + tm, :] * cw_ref[1:2, :]
                  + after * cw_ref[2:3, :] + cw_ref[3:4, :])


def _proj_conv3(x, vecs, w, bias, taps, conv_bias):
    B, L, K = x.shape
    N = w.shape[1]
    tm = min(MM_TM, L)
    tn = MM_TN
    nblk = L // tm
    per = tm // HALO
    cw = jnp.concatenate([taps, conv_bias[None, :], jnp.zeros((4, N), F32)], axis=0)
    return pl.pallas_call(
        functools.partial(_proj_conv3_kernel, nblk=nblk),
        grid=(B, nblk, N // tn),
        in_specs=[pl.BlockSpec((None, tm, K), lambda b, i, j: (b, i, 0)),
                  pl.BlockSpec((None, HALO, K), lambda b, i, j: (b, jnp.maximum(i * per - 1, 0), 0)),
                  pl.BlockSpec((None, HALO, K), lambda b, i, j: (b, jnp.minimum((i + 1) * per, L // HALO - 1), 0)),
                  pl.BlockSpec((None, 8, K), lambda b, i, j: (b, 0, 0)),
                  pl.BlockSpec((K, tn), lambda b, i, j: (0, j)),
                  pl.BlockSpec((1, tn), lambda b, i, j: (0, j)),
                  pl.BlockSpec((8, tn), lambda b, i, j: (0, j))],
        out_specs=pl.BlockSpec((None, tm, tn), lambda b, i, j: (b, i, j)),
        out_shape=jax.ShapeDtypeStruct((B, L, N), F32),
        scratch_shapes=[pltpu.VMEM((tm + 2 * HALO, K), BF16), pltpu.VMEM((tm + 2 * HALO, tn), F32)],
        compiler_params=_cparams(("parallel", "parallel", "arbitrary")),
        name="proj_conv3",
    )(x, x, x, vecs, w.astype(BF16), bias[None, :], cw)


def _hgrn_post_kernel(o0_ref, o1_ref, g_ref, res_ref, vec_ref, gn_ref, wo_ref, ones_ref, o_ref):
    o = o0_ref[...] + o1_ref[...]
    ms = _headsum(o * o, ones_ref[...]) * (1.0 / HG_DK)
    on = o * lax.rsqrt(ms + NORM_EPS) * gn_ref[...]
    g = g_ref[...]
    z = (on * (g * jax.nn.sigmoid(g))).astype(BF16)
    o_ref[...] = res_ref[...] + vec_ref[2:3, :] * _dot(z, wo_ref[...])


def _hgrn2_stream(resid, vecs, state0, lb, w_in, gn, w_o, need_out=True):
    B, L, D = resid.shape
    proj = _mm_fused(resid, w_in, vecs=vecs, norm=True)
    o0, s0 = _gla_scan(proj, lb, state0[0], 0)
    o1, s1 = _gla_scan(proj, lb, state0[1], 1)
    states = jnp.stack([s0, s1])
    if not need_out:
        return None, states
    tm = min(POST_TM, L)
    tok = pl.BlockSpec((None, tm, D), lambda b, i: (b, i, 0))
    ones = jnp.ones((LANES, LANES), BF16)
    gn_row = jnp.tile(gn, D // gn.shape[0])[None, :]
    wo = w_o.astype(BF16)
    out = pl.pallas_call(
        _hgrn_post_kernel,
        grid=(B, L // tm),
        in_specs=[tok, tok, pl.BlockSpec((None, tm, D), lambda b, i: (b, i, 4)), tok,
                  pl.BlockSpec((None, 8, D), lambda b, i: (b, 0, 0)), _full(gn_row), _full(wo), _full(ones)],
        out_specs=tok,
        out_shape=jax.ShapeDtypeStruct((B, L, D), F32),
        compiler_params=_cparams(("parallel", "parallel")),
        name="hgrn_post",
    )(o0, o1, proj, resid, vecs, gn_row, wo, ones)
    return out, states


def kernel(x, c, ctx, c_ctx, norm_g, ada_w, ada_b, final_g,
           rw_mu, rw_wrkv, rw_wo, rw_w0, rw_w1, rw_w2, rw_a0, rw_a1, rw_a2,
           rw_v0, rw_v1, rw_v2, rw_g1, rw_g2, rw_kk, rw_ka, rw_rk, rw_lnw, rw_lnb,
           hy_win, hy_bin, hy_cw, hy_cb, hy_fw1, hy_fb1, hy_fw2, hy_fb2, hy_fw3, hy_fb3,
           hy_fw4, hy_freq, hy_skip, hy_wo, hy_bo,
           hg_win, hg_lb, hg_gn, hg_wo,
           ffn_w13, ffn_w2, moe_router, moe_w13, moe_w2):
    B = x.shape[0]
    depth = norm_g.shape[0]
    D = D_MODEL
    lat, cx = x, ctx
    v_first = None
    lbc = jnp.cumsum(jax.nn.softmax(hg_lb, axis=0), axis=0)
    lower = lbc - lbc[:1]
    cond = jnp.concatenate([jax.nn.silu(c), jax.nn.silu(c_ctx)[None, :]], axis=0)
    for i in range(depth):
        last = i == depth - 1
        if i % 2 == 0:
            mix_w13, mix_w2 = ffn_w13[i // 2].astype(BF16), ffn_w2[i // 2].astype(BF16)
        else:
            mix_w13, mix_w2 = moe_w13[i // 2].astype(BF16), moe_w2[i // 2].astype(BF16)
        mod = _mm(cond, ada_w[i]) + ada_b[i]
        mod_l = jnp.split(mod[:B, None, :], 6, axis=-1)
        mod_c = jnp.split(mod[B:, None, :], 6, axis=-1)
        kind, slot = i % N_MIXERS, i // N_MIXERS
        vec_l = _mod_rows(norm_g[i, 0], mod_l[0], mod_l[1], mod_l[2], B)
        vec_c = _mod_rows(norm_g[i, 0], mod_c[0], mod_c[1], mod_c[2], B)
        if kind == 0:
            vres = None if slot == 0 else (rw_v0[slot - 1], rw_v1[slot - 1], rw_v2[slot - 1])
            rw = (rw_mu[slot], rw_wrkv[slot], rw_wo[slot], rw_w0[slot], rw_w1[slot], rw_w2[slot],
                  rw_a0[slot], rw_a1[slot], rw_a2[slot], rw_g1[slot], rw_g2[slot], rw_kk[slot],
                  rw_ka[slot], rw_rk[slot], rw_lnw[slot], rw_lnb[slot], vres)
            zero = jnp.zeros((2, B, RW_H, RW_HEAD, RW_HEAD), F32)
            vf_c = None if v_first is None else v_first[0]
            vf_l = None if v_first is None else v_first[1]
            cx_new, s_ctx, v_c = _rwkv7_stream(cx, vec_c, None, zero, vf_c, *rw, need_out=not last)
            lat, _, v_l = _rwkv7_stream(lat, vec_l, GRID_W, s_ctx, vf_l, *rw)
            if not last:
                cx = cx_new
            if slot == 0:
                v_first = (v_c, v_l)
        elif kind == 1:
            fprm = (hy_fw1[slot], hy_fb1[slot], hy_fw2[slot], hy_fb2[slot], hy_fw3[slot],
                    hy_fb3[slot], hy_fw4[slot], hy_freq[slot])
            hprm = (hy_win[slot], hy_bin[slot], hy_cw[slot], hy_cb[slot], hy_skip[slot],
                    hy_wo[slot], hy_bo[slot])
            lat = _hyena_stream(lat, vec_l, fprm, *hprm)
            if not last:
                cx = _hyena_stream(cx, vec_c, fprm, *hprm)
        else:
            zero = jnp.zeros((2, B, HG_H, HG_DK, D // HG_H), F32)
            gprm = (lower[i], hg_win[slot], hg_gn[slot], hg_wo[slot])
            cx_new, s_ctx = _hgrn2_stream(cx, vec_c, zero, *gprm, need_out=not last)
            lat, _ = _hgrn2_stream(lat, vec_l, s_ctx, *gprm)
            if not last:
                cx = cx_new

        def channel_mix(s, mod, out_gain=None):
            vecs = _mod_rows(norm_g[i, 1], mod[3], mod[4], mod[5], B)
            if i % 2 == 0:
                return _ffn(s, vecs, mix_w13, mix_w2)
            xn, logits = _norm_route(s, vecs, moe_router[i // 2])
            out = _moe(xn.reshape(-1, D), logits.reshape(-1, N_EXPERTS), mix_w13, mix_w2)
            return _gated_add(s, out.reshape(s.shape), vecs, out_gain)

        assert depth % 2 == 0
        lat = channel_mix(lat, mod_l, final_g if last else None)
        if not last:
            cx = channel_mix(cx, mod_c)
    return lat
```
